```python
import numpy as np
import jax, jax.numpy as jnp
from jax import lax

D_MODEL = 1024
BATCH = 2
SEQ = 8192
DEPTH = 1

GRID_W = 64
D_MIX = D_MODEL
D_ATT = D_MIX // 2
D_HG = D_MIX - D_ATT
ATT_HEADS = 8
ATT_HEAD_DIM = D_ATT // ATT_HEADS
HG_HEADS = 4
HG_HEAD_DIM = D_HG // HG_HEADS
WIN_ROWS_MAX = 8
WIN_COLS = 16
Q_BLOCK_W = 16
K_BLOCK_W = 32
HG_CHUNK = 64
IN_WIDTHS = (D_ATT, D_ATT, D_ATT, D_ATT, D_HG, D_HG, D_HG, D_HG, D_HG)
D_IN = sum(IN_WIDTHS)
LN_EPS = 1e-5
RMS_EPS = 1e-6
DEEPNORM_ALPHA = (2.0 * DEPTH) ** 0.25
DEEPNORM_BETA = (8.0 * DEPTH) ** -0.25

kernel_name = "hymba_natten_hgrn2_deepnorm_encoder"


def neighbourhood_attention(q, k, v, rpb):
    B, T, H, dh = q.shape
    rows = T // GRID_W
    kr = min(WIN_ROWS_MAX, rows)
    q = q.reshape(B, rows, GRID_W, H, dh)
    k = k.reshape(B, rows, GRID_W, H, dh)
    v = v.reshape(B, rows, GRID_W, H, dh)
    r = np.arange(rows)
    row_start = np.clip(r - kr // 2, 0, rows - kr)
    row_idx = row_start[:, None] + np.arange(kr)[None, :]
    dr = row_idx - r[:, None] + (WIN_ROWS_MAX - 1)
    k_rows = k[:, row_idx]
    v_rows = v[:, row_idx]
    scale = dh ** -0.5
    outs = []
    for j in range(GRID_W // Q_BLOCK_W):
        qc = np.arange(j * Q_BLOCK_W, (j + 1) * Q_BLOCK_W)
        kc0 = int(np.clip(j * Q_BLOCK_W - WIN_COLS // 2, 0, GRID_W - K_BLOCK_W))
        kc = np.arange(kc0, kc0 + K_BLOCK_W)
        col_start = np.clip(qc - WIN_COLS // 2, 0, GRID_W - WIN_COLS)
        in_win = (kc[None, :] >= col_start[:, None]) & (kc[None, :] < col_start[:, None] + WIN_COLS)
        dc = np.clip(kc[None, :] - qc[:, None], -(WIN_COLS - 1), WIN_COLS - 1) + (WIN_COLS - 1)
        q_blk = q[:, :, j * Q_BLOCK_W:(j + 1) * Q_BLOCK_W]
        k_blk = k_rows[:, :, :, kc0:kc0 + K_BLOCK_W]
        v_blk = v_rows[:, :, :, kc0:kc0 + K_BLOCK_W]
        s = jnp.einsum('brqhd,brakhd->bhrqak', q_blk, k_blk,
                       preferred_element_type=jnp.float32) * scale
        bias = rpb[:, dr[:, None, :, None], dc[None, :, None, :]]
        s = jnp.where(in_win[None, None, None, :, None, :], s + bias[None].astype(jnp.float32), -jnp.inf)
        p = jax.nn.softmax(s.reshape(B, H, rows, Q_BLOCK_W, kr * K_BLOCK_W), axis=-1)
        p = p.reshape(B, H, rows, Q_BLOCK_W, kr, K_BLOCK_W).astype(v.dtype)
        outs.append(jnp.einsum('bhrqak,brakhd->brqhd', p, v_blk))
    o = jnp.concatenate(outs, axis=2)
    return o.reshape(B, T, H * dh)


def hgrn2_chunk_scan(q, k, v, log_f):
    B, H, T, dk = q.shape
    dv = v.shape[-1]
    n = T // HG_CHUNK
    causal = np.tril(np.ones((HG_CHUNK, HG_CHUNK), dtype=bool))

    def to_chunks(a):
        return jnp.moveaxis(a.reshape(B, H, n, HG_CHUNK, a.shape[-1]), 2, 0)

    def step(S, inp):
        qc, kc, vc, gc = inp
        b = jnp.cumsum(gc, axis=2)
        b_last = b[:, :, -1:, :]
        o_inter = jnp.einsum('bhtd,bhde->bhte', qc * jnp.exp(b), S)
        diff = jnp.where(causal[:, :, None], b[:, :, :, None, :] - b[:, :, None, :, :], -jnp.inf)
        a = jnp.einsum('bhtd,bhtsd,bhsd->bhts', qc, jnp.exp(diff), kc)
        o_intra = jnp.einsum('bhts,bhse->bhte', a, vc)
        S = jnp.exp(b_last[:, :, 0, :, None]) * S + jnp.einsum('bhsd,bhse->bhde', kc * jnp.exp(b_last - b), vc)
        return S, o_inter + o_intra

    S0 = jnp.zeros((B, H, dk, dv), jnp.float32)
    _, o = lax.scan(step, S0, (to_chunks(q), to_chunks(k), to_chunks(v), to_chunks(log_f)))
    return jnp.moveaxis(o, 0, 2).reshape(B, H, T, dv)


def hgrn2_forget(z, lb_logits, layer):
    lb = jnp.cumsum(jax.nn.softmax(lb_logits.astype(jnp.float32), axis=0), axis=0)[layer]
    lb = lb.reshape(HG_HEADS, 1, HG_HEAD_DIM)
    f = lb + (1.0 - lb) * jax.nn.sigmoid(z)
    return jnp.log(f), 1.0 - f


def hybrid_layer(x, layer, w_in, b_in, rpb, lb_fwd_logits, lb_bwd_logits, hg_norm_gain,
                 w_out, b_out, ln_gain, ln_bias):
    B, T, _ = x.shape
    h = jnp.einsum('btd,de->bte', x, w_in) + b_in
    split_at = list(np.cumsum(IN_WIDTHS)[:-1])
    q_a, k_a, v_a, g_a, q_h, z_fwd, z_bwd, i_h, g_h = jnp.split(h, split_at, axis=-1)

    heads_a = lambda t: t.reshape(B, T, ATT_HEADS, ATT_HEAD_DIM)
    o_a = neighbourhood_attention(heads_a(q_a), heads_a(k_a), heads_a(v_a), rpb)
    o_a = o_a * jax.nn.silu(g_a)

    heads_h = lambda t: t.reshape(B, T, HG_HEADS, HG_HEAD_DIM).transpose(0, 2, 1, 3).astype(jnp.float32)
    qh, ih = heads_h(q_h), heads_h(i_h)
    logf_f, k_f = hgrn2_forget(heads_h(z_fwd), lb_fwd_logits, layer)
    logf_b, k_b = hgrn2_forget(heads_h(z_bwd), lb_bwd_logits, layer)
    o_fwd = hgrn2_chunk_scan(qh, k_f, ih, logf_f)
    flip = lambda t: jnp.flip(t, axis=2)
    o_bwd = flip(hgrn2_chunk_scan(flip(qh), flip(k_b), flip(ih), flip(logf_b)))
    o_h = (o_fwd + o_bwd).transpose(0, 2, 1, 3)
    o_h = o_h * lax.rsqrt(jnp.mean(jnp.square(o_h), axis=-1, keepdims=True) + RMS_EPS)
    o_h = o_h.reshape(B, T, D_HG) * hg_norm_gain.astype(jnp.float32)
    o_h = o_h.astype(x.dtype) * jax.nn.silu(g_h)

    y = jnp.einsum('bte,ed->btd', jnp.concatenate([o_a, o_h], axis=-1), w_out) + b_out
    r = (DEEPNORM_ALPHA * x + y).astype(jnp.float32)
    mu = jnp.mean(r, axis=-1, keepdims=True)
    var = jnp.mean(jnp.square(r - mu), axis=-1, keepdims=True)
    r = (r - mu) * lax.rsqrt(var + LN_EPS) * ln_gain.astype(jnp.float32) + ln_bias.astype(jnp.float32)
    return r.astype(x.dtype)


def setup_inputs(seed: int = 0) -> dict:
    key = jax.random.key(seed)
    ks = jax.random.split(key, 12)
    x = jax.random.normal(ks[0], (BATCH, SEQ, D_MODEL), jnp.float32)
    col_scale = np.ones((D_IN,), np.float32)
    off = np.cumsum((0,) + IN_WIDTHS)
    col_scale[off[2]:off[3]] = DEEPNORM_BETA
    col_scale[off[7]:off[8]] = DEEPNORM_BETA
    w_in = jax.random.normal(ks[1], (DEPTH, D_MODEL, D_IN), jnp.float32) * (D_MODEL ** -0.5) * jnp.asarray(col_scale)
    b_in = 0.02 * jax.random.normal(ks[2], (DEPTH, D_IN), jnp.float32)
    rpb = 0.1 * jax.random.normal(ks[3], (DEPTH, ATT_HEADS, 2 * WIN_ROWS_MAX - 1, 2 * WIN_COLS - 1), jnp.float32)
    lb_fwd_logits = 0.5 * jax.random.normal(ks[4], (DEPTH + 1, D_HG), jnp.float32)
    lb_bwd_logits = 0.5 * jax.random.normal(ks[5], (DEPTH + 1, D_HG), jnp.float32)
    hg_norm_gain = 1.0 + 0.02 * jax.random.normal(ks[6], (DEPTH, D_HG), jnp.float32)
    w_out = jax.random.normal(ks[7], (DEPTH, D_MIX, D_MODEL), jnp.float32) * (D_MIX ** -0.5) * DEEPNORM_BETA
    b_out = 0.02 * jax.random.normal(ks[8], (DEPTH, D_MODEL), jnp.float32)
    ln_gain = 1.0 + 0.02 * jax.random.normal(ks[9], (DEPTH, D_MODEL), jnp.float32)
    ln_bias = 0.02 * jax.random.normal(ks[10], (DEPTH, D_MODEL), jnp.float32)
    return {"x": x, "w_in": w_in, "b_in": b_in, "rpb": rpb,
            "lb_fwd_logits": lb_fwd_logits, "lb_bwd_logits": lb_bwd_logits,
            "hg_norm_gain": hg_norm_gain, "w_out": w_out, "b_out": b_out,
            "ln_gain": ln_gain, "ln_bias": ln_bias}


def reference(x, w_in, b_in, rpb, lb_fwd_logits, lb_bwd_logits, hg_norm_gain,
              w_out, b_out, ln_gain, ln_bias):
    for layer in range(DEPTH):
        x = hybrid_layer(x, layer, w_in[layer], b_in[layer], rpb[layer],
                         lb_fwd_logits, lb_bwd_logits, hg_norm_gain[layer],
                         w_out[layer], b_out[layer], ln_gain[layer], ln_bias[layer])
    return x
```

```python
import functools

import numpy as np
import jax
import jax.numpy as jnp
from jax import lax
from jax.experimental import pallas as pl
from jax.experimental.pallas import tpu as pltpu

GRID_W = 64
ATT_HEADS = 8
ATT_HEAD_DIM = 64
HG_HEADS = 4
HG_HEAD_DIM = 128
WIN_ROWS = 8
WIN_COLS = 16
LN_EPS = 1e-5
RMS_EPS = 1e-6
N_SLABS = 9
SLAB = 512

LANES = 128
VMEM_LIMIT_BYTES = 56 * 1024 * 1024
PROJ_TM = 512
ATT_ROWS_PER_STEP = 8
ATT_GROUP = 4
HG_CHUNK = 64
MASK_VALUE = -1e30

BF16 = jnp.bfloat16
F32 = jnp.float32


def _dot(a, b):
    return jnp.dot(a, b, preferred_element_type=F32)


def _dot_nt(a, b):
    return lax.dot_general(a, b, (((1,), (1,)), ((), ())), preferred_element_type=F32)


def _in_proj_kernel(x_ref, w_ref, b_ref, *out_refs):
    xb = x_ref[...].astype(BF16)
    for j, o_ref in enumerate(out_refs):
        cols = slice(j * SLAB, (j + 1) * SLAB)
        o_ref[...] = (_dot(xb, w_ref[:, cols]) + b_ref[:, cols]).astype(o_ref.dtype)


def _in_proj(x2d, w_bf16, b_row):
    n_tok, d_model = x2d.shape
    d_in = w_bf16.shape[1]
    assert d_in == N_SLABS * SLAB and n_tok % PROJ_TM == 0
    return pl.pallas_call(
        _in_proj_kernel,
        grid=(n_tok // PROJ_TM,),
        in_specs=[
            pl.BlockSpec((PROJ_TM, d_model), lambda i: (i, 0)),
            pl.BlockSpec((d_model, d_in), lambda i: (0, 0)),
            pl.BlockSpec((1, d_in), lambda i: (0, 0)),
        ],
        out_specs=[pl.BlockSpec((PROJ_TM, SLAB), lambda i: (i, 0))] * N_SLABS,
        out_shape=[jax.ShapeDtypeStruct((n_tok, SLAB), BF16)] * N_SLABS,
        compiler_params=pltpu.CompilerParams(
            dimension_semantics=("arbitrary",), vmem_limit_bytes=VMEM_LIMIT_BYTES),
        name="in_proj",
    )(x2d, w_bf16, b_row)


def _attention_bias_table(rpb, rows):
    kr = min(WIN_ROWS, rows)
    qc = np.arange(GRID_W)
    kc = np.arange(GRID_W)
    col_start = np.clip(qc - WIN_COLS // 2, 0, GRID_W - WIN_COLS)
    in_win = (kc[None, :] >= col_start[:, None]) & (kc[None, :] < col_start[:, None] + WIN_COLS)
    dc = np.clip(kc[None, :] - qc[:, None], -(WIN_COLS - 1), WIN_COLS - 1) + (WIN_COLS - 1)
    toeplitz = rpb[:, :, dc]
    toeplitz = jnp.where(in_win[None, None], toeplitz.astype(F32), MASK_VALUE)
    n_var = toeplitz.shape[1] - kr + 1
    variants = jnp.stack([toeplitz[:, v:v + kr] for v in range(n_var)])
    variants = variants.transpose(0, 1, 3, 2, 4)
    return variants.reshape(n_var, ATT_HEADS // ATT_GROUP, ATT_GROUP * GRID_W, kr * GRID_W)


def _attention_kernel(q_ref, k_ref, v_ref, g_ref, bias_ref, o_ref, *, rows, kr):
    step = pl.program_id(1)
    gw = ATT_GROUP * ATT_HEAD_DIM
    row_head = lax.broadcasted_iota(jnp.int32, (ATT_GROUP * GRID_W, gw), 0) // GRID_W
    lane_head = lax.broadcasted_iota(jnp.int32, (ATT_GROUP * GRID_W, gw), 1) // ATT_HEAD_DIM
    own_head = row_head == lane_head
    out_lane_head = lax.broadcasted_iota(jnp.int32, (GRID_W, gw), 1) // ATT_HEAD_DIM
    scale = ATT_HEAD_DIM ** -0.5

    def one_row(j, carry):
        r = step * ATT_ROWS_PER_STEP + j
        row_start = jnp.clip(r - kr // 2, 0, rows - kr)
        variant = row_start - r + (WIN_ROWS - 1)
        q_tok = pl.multiple_of(j * GRID_W, GRID_W)
        k_tok = pl.multiple_of(row_start * GRID_W, GRID_W)
        for grp in range(ATT_HEADS // ATT_GROUP):
            lanes = slice(grp * gw, (grp + 1) * gw)
            q = q_ref[0, pl.ds(q_tok, GRID_W), lanes]
            q = (q.astype(F32) * scale).astype(BF16)
            q_bd = jnp.where(own_head, jnp.concatenate([q] * ATT_GROUP, axis=0), 0)
            keys = k_ref[0, pl.ds(k_tok, kr * GRID_W), lanes]
            vals = v_ref[0, pl.ds(k_tok, kr * GRID_W), lanes]
            s = _dot_nt(q_bd, keys) + bias_ref[variant, grp]
            m = jnp.max(s, axis=-1, keepdims=True)
            p = jnp.exp(s - m)
            denom = jnp.sum(p, axis=-1, keepdims=True)
            pv = _dot(p.astype(BF16), vals) / denom
            o = jnp.zeros((GRID_W, gw), F32)
            for h in range(ATT_GROUP):
                o = o + jnp.where(out_lane_head == h, pv[h * GRID_W:(h + 1) * GRID_W], 0.0)
            gate = g_ref[0, pl.ds(q_tok, GRID_W), lanes].astype(F32)
            o = o * (gate * jax.nn.sigmoid(gate))
            o_ref[0, pl.ds(q_tok, GRID_W), lanes] = o.astype(o_ref.dtype)
        return carry

    lax.fori_loop(0, ATT_ROWS_PER_STEP, one_row, 0)


def _attention(q, k, v, g, bias):
    batch, seq, width = q.shape
    rows = seq // GRID_W
    kr = min(WIN_ROWS, rows)
    assert rows % ATT_ROWS_PER_STEP == 0 and kr == WIN_ROWS
    blk = ATT_ROWS_PER_STEP * GRID_W
    tile = pl.BlockSpec((1, blk, width), lambda b, i: (b, i, 0))
    whole = pl.BlockSpec((1, seq, width), lambda b, i: (b, 0, 0))
    return pl.pallas_call(
        functools.partial(_attention_kernel, rows=rows, kr=kr),
        grid=(batch, rows // ATT_ROWS_PER_STEP),
        in_specs=[tile, whole, whole, tile,
                  pl.BlockSpec(bias.shape, lambda b, i: (0, 0, 0, 0),
                               pipeline_mode=pl.Buffered(1))],
        out_specs=tile,
        out_shape=jax.ShapeDtypeStruct((batch, seq, width), BF16),
        compiler_params=pltpu.CompilerParams(
            dimension_semantics=("arbitrary", "arbitrary"), vmem_limit_bytes=VMEM_LIMIT_BYTES),
        name="nbr_attention",
    )(q, k, v, g, bias)


def _hgrn_level_sizes(chunk):
    sizes = []
    c = chunk
    while c >= 2:
        sizes.append(c)
        c //= 2
    return sizes


def _hgrn_constants(chunk):
    t = np.arange(chunk)[:, None]
    u = np.arange(chunk)[None, :]
    blocks = [(u <= t), (u > t)]
    masks = []
    for c in _hgrn_level_sizes(chunk):
        half = c // 2
        mid = (t // c) * c + half
        upper = (t % c) >= half
        blocks.append(np.where(upper, (u >= mid) & (u <= t), (u > t) & (u <= mid - 1)))
        s = u
        masks.append((t // c == s // c) & upper & ((s % c) < half))
    fwd_sums = np.concatenate(blocks, axis=0).astype(np.float32)
    fwd_masks = np.stack(masks).astype(np.float32)
    n_blocks = len(blocks)
    bwd_sums = fwd_sums.reshape(n_blocks, chunk, chunk)[:, ::-1, ::-1].reshape(n_blocks * chunk, chunk)
    bwd_masks = fwd_masks[:, ::-1, ::-1]
    return np.stack([fwd_sums, bwd_sums]), np.stack([fwd_masks, bwd_masks])


def _hgrn_kernel(q_ref, zf_ref, zb_ref, i_ref, g_ref, lbf_ref, lbb_ref, gain_ref,
                 sums_ref, masks_ref, o_ref, of_scr, ob_scr, sf_scr, sb_scr, *, layer, seq):
    chunk = HG_CHUNK
    n_chunks = seq // chunk
    n_levels = len(_hgrn_level_sizes(chunk))

    def lower_bound(logit_ref):
        logits = logit_ref[...].astype(F32)
        e = jnp.exp(logits - jnp.max(logits, axis=0, keepdims=True))
        return jnp.sum(e[:layer + 1], axis=0, keepdims=True) / jnp.sum(e, axis=0, keepdims=True)

    def chunk_step(direction, tok, z_ref, lb, state_ref, out_scr):
        rows = pl.ds(tok, chunk)
        q = q_ref[0, rows, :].astype(F32)
        v = i_ref[0, rows, :]
        f = lb + (1.0 - lb) * jax.nn.sigmoid(z_ref[0, rows, :].astype(F32))
        g = jnp.log(f)
        k = 1.0 - f
        g_hi = g.astype(BF16)
        g_lo = (g - g_hi.astype(F32)).astype(BF16)
        sums = _dot(sums_ref[direction], jnp.concatenate([g_hi, g_lo], axis=1))
        decay = jnp.exp(sums[:, :HG_HEAD_DIM] + sums[:, HG_HEAD_DIM:])
        d_query = decay[0:chunk]
        d_key = decay[chunk:2 * chunk]
        a = jnp.zeros((chunk, chunk), F32)
        for lvl in range(n_levels):
            d = decay[(2 + lvl) * chunk:(3 + lvl) * chunk]
            a = a + masks_ref[direction, lvl] * _dot_nt((q * d).astype(BF16), (k * d).astype(BF16))
        o = _dot(a.astype(BF16), v) + jnp.sum(q * k, axis=-1, keepdims=True) * v.astype(F32)
        state_t = state_ref[...]
        o = o + _dot_nt((q * d_query).astype(BF16), state_t.astype(BF16))
        out_scr[rows, :] = o
        d_total = d_query[chunk - 1:chunk] if direction == 0 else d_query[0:1]
        k_end = (k * d_key).astype(BF16)
        state_ref[...] = state_t * d_total + _dot(v.astype(F32).T.astype(BF16), k_end)

    lb_f = lower_bound(lbf_ref)
    lb_b = lower_bound(lbb_ref)
    sf_scr[...] = jnp.zeros_like(sf_scr)
    sb_scr[...] = jnp.zeros_like(sb_scr)

    def scan_body(n, carry):
        chunk_step(0, pl.multiple_of(n * chunk, chunk), zf_ref, lb_f, sf_scr, of_scr)
        chunk_step(1, pl.multiple_of((n_chunks - 1 - n) * chunk, chunk), zb_ref, lb_b, sb_scr, ob_scr)
        return carry

    lax.fori_loop(0, n_chunks, scan_body, 0)

    gain = gain_ref[...].astype(F32)
    norm_rows = 256

    def norm_body(n, carry):
        rows = pl.ds(pl.multiple_of(n * norm_rows, norm_rows), norm_rows)
        o = of_scr[rows, :] + ob_scr[rows, :]
        o = o * lax.rsqrt(jnp.mean(jnp.square(o), axis=-1, keepdims=True) + RMS_EPS) * gain
        gate = g_ref[0, rows, :].astype(F32)
        o_ref[0, rows, :] = (o * (gate * jax.nn.sigmoid(gate))).astype(o_ref.dtype)
        return carry

    lax.fori_loop(0, seq // norm_rows, norm_body, 0)


def _hgrn(q, zf, zb, i, g, lb_fwd_logits, lb_bwd_logits, gain_row, layer):
    batch, seq, width = q.shape
    assert width == HG_HEADS * HG_HEAD_DIM and seq % HG_CHUNK == 0 and seq % 256 == 0
    sums, masks = _hgrn_constants(HG_CHUNK)
    sums = jnp.asarray(sums, BF16)
    masks = jnp.asarray(masks, F32)
    n_layers = lb_fwd_logits.shape[0]
    head = pl.BlockSpec((1, seq, HG_HEAD_DIM), lambda b, h: (b, 0, h))
    per_head_row = lambda n: pl.BlockSpec((n, HG_HEAD_DIM), lambda b, h: (0, h))
    return pl.pallas_call(
        functools.partial(_hgrn_kernel, layer=layer, seq=seq),
        grid=(batch, HG_HEADS),
        in_specs=[head, head, head, head, head,
                  per_head_row(n_layers), per_head_row(n_layers), per_head_row(1),
                  pl.BlockSpec(sums.shape, lambda b, h: (0, 0, 0)),
                  pl.BlockSpec(masks.shape, lambda b, h: (0, 0, 0, 0))],
        out_specs=head,
        out_shape=jax.ShapeDtypeStruct((batch, seq, width), BF16),
        scratch_shapes=[pltpu.VMEM((seq, HG_HEAD_DIM), F32), pltpu.VMEM((seq, HG_HEAD_DIM), F32),
                        pltpu.VMEM((HG_HEAD_DIM, HG_HEAD_DIM), F32),
                        pltpu.VMEM((HG_HEAD_DIM, HG_HEAD_DIM), F32)],
        compiler_params=pltpu.CompilerParams(
            dimension_semantics=("arbitrary", "arbitrary"), vmem_limit_bytes=VMEM_LIMIT_BYTES),
        name="hgrn2_scan",
    )(q, zf, zb, i, g, lb_fwd_logits, lb_bwd_logits, gain_row, sums, masks)


def _out_proj_kernel(oa_ref, oh_ref, x_ref, w_ref, b_ref, gain_ref, bias_ref, o_ref, *, alpha):
    d_att = oa_ref.shape[1]
    y = _dot(oa_ref[...], w_ref[:d_att, :]) + _dot(oh_ref[...], w_ref[d_att:, :]) + b_ref[...]
    r = alpha * x_ref[...] + y
    mu = jnp.mean(r, axis=-1, keepdims=True)
    c = r - mu
    var = jnp.mean(jnp.square(c), axis=-1, keepdims=True)
    o_ref[...] = (c * lax.rsqrt(var + LN_EPS) * gain_ref[...] + bias_ref[...]).astype(o_ref.dtype)


def _out_proj(o_a, o_h, x2d, w_bf16, b_row, gain_row, bias_row, alpha):
    n_tok, d_model = x2d.shape
    d_att, d_hg = o_a.shape[1], o_h.shape[1]
    row = pl.BlockSpec((1, d_model), lambda i: (0, 0))
    return pl.pallas_call(
        functools.partial(_out_proj_kernel, alpha=alpha),
        grid=(n_tok // PROJ_TM,),
        in_specs=[pl.BlockSpec((PROJ_TM, d_att), lambda i: (i, 0)),
                  pl.BlockSpec((PROJ_TM, d_hg), lambda i: (i, 0)),
                  pl.BlockSpec((PROJ_TM, d_model), lambda i: (i, 0)),
                  pl.BlockSpec((d_att + d_hg, d_model), lambda i: (0, 0)),
                  row, row, row],
        out_specs=pl.BlockSpec((PROJ_TM, d_model), lambda i: (i, 0)),
        out_shape=jax.ShapeDtypeStruct((n_tok, d_model), x2d.dtype),
        compiler_params=pltpu.CompilerParams(
            dimension_semantics=("arbitrary",), vmem_limit_bytes=VMEM_LIMIT_BYTES),
        name="out_proj_layernorm",
    )(o_a, o_h, x2d, w_bf16, b_row, gain_row, bias_row)


def _layer(x, layer, depth, w_in, b_in, rpb, lb_fwd_logits, lb_bwd_logits, hg_norm_gain,
           w_out, b_out, ln_gain, ln_bias):
    batch, seq, d_model = x.shape
    x2d = x.reshape(batch * seq, d_model)
    slabs = _in_proj(x2d, w_in.astype(BF16), b_in.reshape(1, -1).astype(F32))
    q_a, k_a, v_a, g_a, q_h, z_f, z_b, i_h, g_h = [s.reshape(batch, seq, SLAB) for s in slabs]
    bias = _attention_bias_table(rpb, seq // GRID_W)
    o_a = _attention(q_a, k_a, v_a, g_a, bias)
    o_h = _hgrn(q_h, z_f, z_b, i_h, g_h, lb_fwd_logits, lb_bwd_logits,
                hg_norm_gain.reshape(1, -1), layer)
    alpha = (2.0 * depth) ** 0.25
    out = _out_proj(o_a.reshape(batch * seq, SLAB), o_h.reshape(batch * seq, SLAB), x2d,
                    w_out.astype(BF16), b_out.reshape(1, -1), ln_gain.reshape(1, -1),
                    ln_bias.reshape(1, -1), alpha)
    return out.reshape(batch, seq, d_model)


def kernel(x, w_in, b_in, rpb, lb_fwd_logits, lb_bwd_logits, hg_norm_gain, w_out, b_out, ln_gain, ln_bias):
    depth = w_in.shape[0]
    for layer in range(depth):
        x = _layer(x, layer, depth, w_in[layer], b_in[layer], rpb[layer], lb_fwd_logits,
                   lb_bwd_logits, hg_norm_gain[layer], w_out[layer], b_out[layer],
                   ln_gain[layer], ln_bias[layer])
    return x
```

```python
import functools

import numpy as np
import jax
import jax.numpy as jnp
from jax import lax
from jax.experimental import pallas as pl
from jax.experimental.pallas import tpu as pltpu

GRID_W = 64
ATT_HEADS = 8
ATT_HEAD_DIM = 64
HG_HEADS = 4
HG_HEAD_DIM = 128
WIN_ROWS = 8
WIN_COLS = 16
LN_EPS = 1e-5
RMS_EPS = 1e-6
N_SLABS = 9
SLAB = 512

LANES = 128
VMEM_LIMIT_BYTES = 56 * 1024 * 1024
PROJ_TM = 512
ATT_ROWS_PER_STEP = 8
ATT_GROUP = 4
HG_CHUNK = 64
MASK_VALUE = -1e30

BF16 = jnp.bfloat16
F32 = jnp.float32


def _dot(a, b):
    return jnp.dot(a, b, preferred_element_type=F32)


def _dot_nt(a, b):
    return lax.dot_general(a, b, (((1,), (1,)), ((), ())), preferred_element_type=F32)


def _in_proj_kernel(x_ref, w_ref, b_ref, *out_refs):
    xb = x_ref[...].astype(BF16)
    for j, o_ref in enumerate(out_refs):
        cols = slice(j * SLAB, (j + 1) * SLAB)
        o_ref[...] = (_dot(xb, w_ref[:, cols]) + b_ref[:, cols]).astype(o_ref.dtype)


def _in_proj(x2d, w_bf16, b_row):
    n_tok, d_model = x2d.shape
    d_in = w_bf16.shape[1]
    assert d_in == N_SLABS * SLAB and n_tok % PROJ_TM == 0
    return pl.pallas_call(
        _in_proj_kernel,
        grid=(n_tok // PROJ_TM,),
        in_specs=[
            pl.BlockSpec((PROJ_TM, d_model), lambda i: (i, 0)),
            pl.BlockSpec((d_model, d_in), lambda i: (0, 0)),
            pl.BlockSpec((1, d_in), lambda i: (0, 0)),
        ],
        out_specs=[pl.BlockSpec((PROJ_TM, SLAB), lambda i: (i, 0))] * N_SLABS,
        out_shape=[jax.ShapeDtypeStruct((n_tok, SLAB), BF16)] * N_SLABS,
        compiler_params=pltpu.CompilerParams(
            dimension_semantics=("arbitrary",), vmem_limit_bytes=VMEM_LIMIT_BYTES),
        name="in_proj",
    )(x2d, w_bf16, b_row)


def _attention_bias_table(rpb, rows):
    kr = min(WIN_ROWS, rows)
    qc = np.arange(GRID_W)
    kc = np.arange(GRID_W)
    col_start = np.clip(qc - WIN_COLS // 2, 0, GRID_W - WIN_COLS)
    in_win = (kc[None, :] >= col_start[:, None]) & (kc[None, :] < col_start[:, None] + WIN_COLS)
    dc = np.clip(kc[None, :] - qc[:, None], -(WIN_COLS - 1), WIN_COLS - 1) + (WIN_COLS - 1)
    toeplitz = rpb[:, :, dc]
    toeplitz = jnp.where(in_win[None, None], toeplitz.astype(F32), MASK_VALUE)
    n_var = toeplitz.shape[1] - kr + 1
    variants = jnp.stack([toeplitz[:, v:v + kr] for v in range(n_var)])
    variants = variants.transpose(0, 1, 3, 2, 4)
    return variants.reshape(n_var, ATT_HEADS // ATT_GROUP, ATT_GROUP * GRID_W, kr * GRID_W)


def _attention_kernel(q_ref, k_ref, v_ref, g_ref, bias_ref, o_ref, *, rows, kr):
    step = pl.program_id(1)
    gw = ATT_GROUP * ATT_HEAD_DIM
    row_head = lax.broadcasted_iota(jnp.int32, (ATT_GROUP * GRID_W, gw), 0) // GRID_W
    lane_head = lax.broadcasted_iota(jnp.int32, (ATT_GROUP * GRID_W, gw), 1) // ATT_HEAD_DIM
    own_head = row_head == lane_head
    out_lane_head = lax.broadcasted_iota(jnp.int32, (GRID_W, gw), 1) // ATT_HEAD_DIM
    scale = ATT_HEAD_DIM ** -0.5

    def one_row(j, carry):
        r = step * ATT_ROWS_PER_STEP + j
        row_start = jnp.clip(r - kr // 2, 0, rows - kr)
        variant = row_start - r + (WIN_ROWS - 1)
        q_tok = pl.multiple_of(j * GRID_W, GRID_W)
        k_tok = pl.multiple_of(row_start * GRID_W, GRID_W)
        for grp in range(ATT_HEADS // ATT_GROUP):
            lanes = slice(grp * gw, (grp + 1) * gw)
            q = q_ref[0, pl.ds(q_tok, GRID_W), lanes]
            q = (q.astype(F32) * scale).astype(BF16)
            q_bd = jnp.where(own_head, jnp.concatenate([q] * ATT_GROUP, axis=0), 0)
            keys = k_ref[0, pl.ds(k_tok, kr * GRID_W), lanes]
            vals = v_ref[0, pl.ds(k_tok, kr * GRID_W), lanes]
            s = _dot_nt(q_bd, keys) + bias_ref[variant, grp]
            m = jnp.max(s, axis=-1, keepdims=True)
            p = jnp.exp(s - m)
            denom = jnp.sum(p, axis=-1, keepdims=True)
            pv = _dot(p.astype(BF16), vals) / denom
            o = jnp.zeros((GRID_W, gw), F32)
            for h in range(ATT_GROUP):
                o = o + jnp.where(out_lane_head == h, pv[h * GRID_W:(h + 1) * GRID_W], 0.0)
            gate = g_ref[0, pl.ds(q_tok, GRID_W), lanes].astype(F32)
            o = o * (gate * jax.nn.sigmoid(gate))
            o_ref[0, pl.ds(q_tok, GRID_W), lanes] = o.astype(o_ref.dtype)
        return carry

    lax.fori_loop(0, ATT_ROWS_PER_STEP, one_row, 0)


def _attention(q, k, v, g, bias):
    batch, seq, width = q.shape
    rows = seq // GRID_W
    kr = min(WIN_ROWS, rows)
    assert rows % ATT_ROWS_PER_STEP == 0 and kr == WIN_ROWS
    blk = ATT_ROWS_PER_STEP * GRID_W
    tile = pl.BlockSpec((1, blk, width), lambda b, i: (b, i, 0))
    whole = pl.BlockSpec((1, seq, width), lambda b, i: (b, 0, 0))
    return pl.pallas_call(
        functools.partial(_attention_kernel, rows=rows, kr=kr),
        grid=(batch, rows // ATT_ROWS_PER_STEP),
        in_specs=[tile, whole, whole, tile,
                  pl.BlockSpec(bias.shape, lambda b, i: (0, 0, 0, 0),
                               pipeline_mode=pl.Buffered(1))],
        out_specs=tile,
        out_shape=jax.ShapeDtypeStruct((batch, seq, width), BF16),
        compiler_params=pltpu.CompilerParams(
            dimension_semantics=("arbitrary", "arbitrary"), vmem_limit_bytes=VMEM_LIMIT_BYTES),
        name="nbr_attention",
    )(q, k, v, g, bias)


def _hgrn_level_sizes(chunk):
    sizes = []
    c = chunk
    while c >= 2:
        sizes.append(c)
        c //= 2
    return sizes


def _hgrn_upper_rows(chunk):
    t = np.arange(chunk)
    fwd = np.stack([(t % c) >= c // 2 for c in _hgrn_level_sizes(chunk)])
    return np.stack([fwd, fwd[:, ::-1]])


def _hgrn_constants(chunk):
    t = np.arange(chunk)[:, None]
    u = np.arange(chunk)[None, :]
    blocks = [(u <= t), (u > t)]
    masks = []
    for c in _hgrn_level_sizes(chunk):
        half = c // 2
        mid = (t // c) * c + half
        upper = (t % c) >= half
        blocks.append(np.where(upper, (u >= mid) & (u <= t), (u > t) & (u <= mid - 1)))
        masks.append((t // c == u // c) & upper & ((u % c) < half))
    fwd_sums = np.stack(blocks).astype(np.float32)
    fwd_masks = np.stack(masks).astype(np.float32)
    bwd_sums = fwd_sums[:, ::-1, ::-1]
    bwd_masks = fwd_masks[:, ::-1, ::-1]
    flat = lambda m: np.concatenate([m.reshape(-1, chunk)] * 2, axis=1)
    return np.stack([flat(fwd_sums), flat(bwd_sums)]), np.concatenate([fwd_masks, bwd_masks], axis=2)


def _block_diag(a, b):
    zero = jnp.zeros_like(a)
    return jnp.concatenate([jnp.concatenate([a, zero], axis=1),
                            jnp.concatenate([zero, b], axis=1)], axis=0)


def _hgrn_kernel(q_ref, zf_ref, zb_ref, i_ref, g_ref, lbf_ref, lbb_ref, gain_ref,
                 sums_ref, masks_ref, o_ref, of_scr, ob_scr, sf_scr, sb_scr, *, layer, seq):
    chunk = HG_CHUNK
    pair = 2 * chunk
    n_pairs = seq // pair
    levels = _hgrn_level_sizes(chunk)
    upper = _hgrn_upper_rows(chunk)
    dh = HG_HEAD_DIM
    row_in_chunk = lax.broadcasted_iota(jnp.int32, (chunk, dh), 0)

    def lower_bound(logit_ref):
        logits = logit_ref[...].astype(F32)
        e = jnp.exp(logits - jnp.max(logits, axis=0, keepdims=True))
        return jnp.sum(e[:layer + 1], axis=0, keepdims=True) / jnp.sum(e, axis=0, keepdims=True)

    def gates(z, lb):
        f = lb + (1.0 - lb) * jax.nn.sigmoid(z.astype(F32))
        return jnp.log2(f), 1.0 - f

    def decays(direction, g2):
        hi = g2.astype(BF16)
        lo = (g2 - hi.astype(F32)).astype(BF16)
        stacked = [jnp.concatenate([hi[c * chunk:(c + 1) * chunk], lo[c * chunk:(c + 1) * chunk]], axis=0)
                   for c in range(2)]
        return jnp.exp2(_dot(sums_ref[direction], jnp.concatenate(stacked, axis=1)))

    def level_operand(direction, lvl, q, k, d):
        c = levels[lvl]
        half = c // 2
        if half % 8 == 0:
            qk = jnp.concatenate([(q if upper[direction, lvl, r] else k)[r:r + half]
                                  for r in range(0, chunk, half)], axis=0)
        else:
            pos = row_in_chunk % c
            qk = jnp.where((pos >= half) if direction == 0 else (pos < half), q, k)
        return (qk * d).astype(BF16)

    def chunk_part(c, q, k, v, dec, base):
        rows = slice(c * chunk, (c + 1) * chunk)
        return dict(q=q[rows], k=k[rows], v=v[rows], dec=dec[:, c * dh:(c + 1) * dh],
                    rows=pl.ds(pl.multiple_of(base + c * chunk, chunk), chunk))

    def pair_step(f, b):
        a = jnp.zeros((chunk, 2 * chunk), F32)
        for lvl in range(len(levels)):
            d_rows = slice((2 + lvl) * chunk, (3 + lvl) * chunk)
            xf = level_operand(0, lvl, f["q"], f["k"], f["dec"][d_rows])
            xb = level_operand(1, lvl, b["q"], b["k"], b["dec"][d_rows])
            a = a + masks_ref[lvl] * _dot_nt(jnp.concatenate([xf, xb], axis=1), _block_diag(xf, xb))
        o = _dot(a.astype(BF16), _block_diag(f["v"], b["v"]))
        sf = sf_scr[...]
        sb = sb_scr[...]
        q_dec = jnp.concatenate([(f["q"] * f["dec"][0:chunk]).astype(BF16),
                                 (b["q"] * b["dec"][0:chunk]).astype(BF16)], axis=1)
        o = o + _dot_nt(q_dec, _block_diag(sf.astype(BF16), sb.astype(BF16)))
        diag_f = jnp.sum(f["q"] * f["k"], axis=-1, keepdims=True)
        diag_b = jnp.sum(b["q"] * b["k"], axis=-1, keepdims=True)
        of_scr[f["rows"], :] = o[:, :dh] + diag_f * f["v"].astype(F32)
        ob_scr[b["rows"], :] = o[:, dh:] + diag_b * b["v"].astype(F32)
        k_dec = _block_diag((f["k"] * f["dec"][chunk:2 * chunk]).astype(BF16),
                            (b["k"] * b["dec"][chunk:2 * chunk]).astype(BF16))
        v_t = jnp.concatenate([f["v"], b["v"]], axis=0).astype(F32).T.astype(BF16)
        u = _dot(v_t, k_dec)
        sf_scr[...] = sf * f["dec"][chunk - 1:chunk] + u[:, :dh]
        sb_scr[...] = sb * b["dec"][0:1] + u[:, dh:]

    lb_f = lower_bound(lbf_ref)
    lb_b = lower_bound(lbb_ref)
    sf_scr[...] = jnp.zeros_like(sf_scr)
    sb_scr[...] = jnp.zeros_like(sb_scr)

    def scan_body(n, carry):
        base_f = pl.multiple_of(n * pair, pair)
        base_b = pl.multiple_of((n_pairs - 1 - n) * pair, pair)
        parts = []
        for direction, base, z_ref, lb in ((0, base_f, zf_ref, lb_f), (1, base_b, zb_ref, lb_b)):
            rows = pl.ds(base, pair)
            q = q_ref[0, rows, :].astype(F32)
            v = i_ref[0, rows, :]
            g2, k = gates(z_ref[0, rows, :], lb)
            dec = decays(direction, g2)
            parts.append([chunk_part(c, q, k, v, dec, base) for c in range(2)])
        pair_step(parts[0][0], parts[1][1])
        pair_step(parts[0][1], parts[1][0])
        return carry

    lax.fori_loop(0, n_pairs, scan_body, 0)

    gain = gain_ref[...].astype(F32)
    norm_rows = 256

    def norm_body(n, carry):
        rows = pl.ds(pl.multiple_of(n * norm_rows, norm_rows), norm_rows)
        o = of_scr[rows, :] + ob_scr[rows, :]
        o = o * lax.rsqrt(jnp.mean(jnp.square(o), axis=-1, keepdims=True) + RMS_EPS) * gain
        gate = g_ref[0, rows, :].astype(F32)
        o_ref[0, rows, :] = (o * (gate * jax.nn.sigmoid(gate))).astype(o_ref.dtype)
        return carry

    lax.fori_loop(0, seq // norm_rows, norm_body, 0)


def _hgrn(q, zf, zb, i, g, lb_fwd_logits, lb_bwd_logits, gain_row, layer):
    batch, seq, width = q.shape
    assert width == HG_HEADS * HG_HEAD_DIM and seq % (2 * HG_CHUNK) == 0 and seq % 256 == 0
    sums, masks = _hgrn_constants(HG_CHUNK)
    sums = jnp.asarray(sums, BF16)
    masks = jnp.asarray(masks, F32)
    n_layers = lb_fwd_logits.shape[0]
    head = pl.BlockSpec((1, seq, HG_HEAD_DIM), lambda b, h: (b, 0, h))
    per_head_row = lambda n: pl.BlockSpec((n, HG_HEAD_DIM), lambda b, h: (0, h))
    return pl.pallas_call(
        functools.partial(_hgrn_kernel, layer=layer, seq=seq),
        grid=(batch, HG_HEADS),
        in_specs=[head, head, head, head, head,
                  per_head_row(n_layers), per_head_row(n_layers), per_head_row(1),
                  pl.BlockSpec(sums.shape, lambda b, h: (0, 0, 0)),
                  pl.BlockSpec(masks.shape, lambda b, h: (0, 0, 0))],
        out_specs=head,
        out_shape=jax.ShapeDtypeStruct((batch, seq, width), BF16),
        scratch_shapes=[pltpu.VMEM((seq, HG_HEAD_DIM), F32), pltpu.VMEM((seq, HG_HEAD_DIM), F32),
                        pltpu.VMEM((HG_HEAD_DIM, HG_HEAD_DIM), F32),
                        pltpu.VMEM((HG_HEAD_DIM, HG_HEAD_DIM), F32)],
        compiler_params=pltpu.CompilerParams(
            dimension_semantics=("arbitrary", "arbitrary"), vmem_limit_bytes=VMEM_LIMIT_BYTES),
        name="hgrn2_scan",
    )(q, zf, zb, i, g, lb_fwd_logits, lb_bwd_logits, gain_row, sums, masks)


def _out_proj_kernel(oa_ref, oh_ref, x_ref, w_ref, b_ref, gain_ref, bias_ref, o_ref, *, alpha):
    d_att = oa_ref.shape[1]
    y = _dot(oa_ref[...], w_ref[:d_att, :]) + _dot(oh_ref[...], w_ref[d_att:, :]) + b_ref[...]
    r = alpha * x_ref[...] + y
    mu = jnp.mean(r, axis=-1, keepdims=True)
    c = r - mu
    var = jnp.mean(jnp.square(c), axis=-1, keepdims=True)
    o_ref[...] = (c * lax.rsqrt(var + LN_EPS) * gain_ref[...] + bias_ref[...]).astype(o_ref.dtype)


def _out_proj(o_a, o_h, x2d, w_bf16, b_row, gain_row, bias_row, alpha):
    n_tok, d_model = x2d.shape
    d_att, d_hg = o_a.shape[1], o_h.shape[1]
    row = pl.BlockSpec((1, d_model), lambda i: (0, 0))
    return pl.pallas_call(
        functools.partial(_out_proj_kernel, alpha=alpha),
        grid=(n_tok // PROJ_TM,),
        in_specs=[pl.BlockSpec((PROJ_TM, d_att), lambda i: (i, 0)),
                  pl.BlockSpec((PROJ_TM, d_hg), lambda i: (i, 0)),
                  pl.BlockSpec((PROJ_TM, d_model), lambda i: (i, 0)),
                  pl.BlockSpec((d_att + d_hg, d_model), lambda i: (0, 0)),
                  row, row, row],
        out_specs=pl.BlockSpec((PROJ_TM, d_model), lambda i: (i, 0)),
        out_shape=jax.ShapeDtypeStruct((n_tok, d_model), x2d.dtype),
        compiler_params=pltpu.CompilerParams(
            dimension_semantics=("arbitrary",), vmem_limit_bytes=VMEM_LIMIT_BYTES),
        name="out_proj_layernorm",
    )(o_a, o_h, x2d, w_bf16, b_row, gain_row, bias_row)


def _layer(x, layer, depth, w_in, b_in, rpb, lb_fwd_logits, lb_bwd_logits, hg_norm_gain,
           w_out, b_out, ln_gain, ln_bias):
    batch, seq, d_model = x.shape
    x2d = x.reshape(batch * seq, d_model)
    slabs = _in_proj(x2d, w_in.astype(BF16), b_in.reshape(1, -1).astype(F32))
    q_a, k_a, v_a, g_a, q_h, z_f, z_b, i_h, g_h = [s.reshape(batch, seq, SLAB) for s in slabs]
    bias = _attention_bias_table(rpb, seq // GRID_W)
    o_a = _attention(q_a, k_a, v_a, g_a, bias)
    o_h = _hgrn(q_h, z_f, z_b, i_h, g_h, lb_fwd_logits, lb_bwd_logits,
                hg_norm_gain.reshape(1, -1), layer)
    alpha = (2.0 * depth) ** 0.25
    out = _out_proj(o_a.reshape(batch * seq, SLAB), o_h.reshape(batch * seq, SLAB), x2d,
                    w_out.astype(BF16), b_out.reshape(1, -1), ln_gain.reshape(1, -1),
                    ln_bias.reshape(1, -1), alpha)
    return out.reshape(batch, seq, d_model)


def kernel(x, w_in, b_in, rpb, lb_fwd_logits, lb_bwd_logits, hg_norm_gain, w_out, b_out, ln_gain, ln_bias):
    depth = w_in.shape[0]
    for layer in range(depth):
        x = _layer(x, layer, depth, w_in[layer], b_in[layer], rpb[layer], lb_fwd_logits,
                   lb_bwd_logits, hg_norm_gain[layer], w_out[layer], b_out[layer],
                   ln_gain[layer], ln_bias[layer])
    return x
```

```python
import functools

import numpy as np
import jax
import jax.numpy as jnp
from jax import lax
from jax.experimental import pallas as pl
from jax.experimental.pallas import tpu as pltpu

GRID_W = 64
ATT_HEADS = 8
ATT_HEAD_DIM = 64
HG_HEADS = 4
HG_HEAD_DIM = 128
WIN_ROWS = 8
WIN_COLS = 16
LN_EPS = 1e-5
RMS_EPS = 1e-6
N_SLABS = 9
SLAB = 512

LANES = 128
VMEM_LIMIT_BYTES = 56 * 1024 * 1024
PROJ_TM = 512
ATT_ROWS_PER_STEP = 8
ATT_GROUP = 4
HG_CHUNK = 64
MASK_VALUE = -1e30

BF16 = jnp.bfloat16
F32 = jnp.float32


def _dot(a, b):
    return jnp.dot(a, b, preferred_element_type=F32)


def _dot_nt(a, b):
    return lax.dot_general(a, b, (((1,), (1,)), ((), ())), preferred_element_type=F32)


def _in_proj_kernel(x_ref, w_ref, b_ref, *out_refs):
    xb = x_ref[...].astype(BF16)
    for j, o_ref in enumerate(out_refs):
        cols = slice(j * SLAB, (j + 1) * SLAB)
        o_ref[...] = (_dot(xb, w_ref[:, cols]) + b_ref[:, cols]).astype(o_ref.dtype)


def _in_proj(x2d, w_bf16, b_row):
    n_tok, d_model = x2d.shape
    d_in = w_bf16.shape[1]
    assert d_in == N_SLABS * SLAB and n_tok % PROJ_TM == 0
    return pl.pallas_call(
        _in_proj_kernel,
        grid=(n_tok // PROJ_TM,),
        in_specs=[
            pl.BlockSpec((PROJ_TM, d_model), lambda i: (i, 0)),
            pl.BlockSpec((d_model, d_in), lambda i: (0, 0)),
            pl.BlockSpec((1, d_in), lambda i: (0, 0)),
        ],
        out_specs=[pl.BlockSpec((PROJ_TM, SLAB), lambda i: (i, 0))] * N_SLABS,
        out_shape=[jax.ShapeDtypeStruct((n_tok, SLAB), BF16)] * N_SLABS,
        compiler_params=pltpu.CompilerParams(
            dimension_semantics=("arbitrary",), vmem_limit_bytes=VMEM_LIMIT_BYTES),
        name="in_proj",
    )(x2d, w_bf16, b_row)


def _attention_bias_table(rpb, rows):
    kr = min(WIN_ROWS, rows)
    qc = np.arange(GRID_W)
    kc = np.arange(GRID_W)
    col_start = np.clip(qc - WIN_COLS // 2, 0, GRID_W - WIN_COLS)
    in_win = (kc[None, :] >= col_start[:, None]) & (kc[None, :] < col_start[:, None] + WIN_COLS)
    dc = np.clip(kc[None, :] - qc[:, None], -(WIN_COLS - 1), WIN_COLS - 1) + (WIN_COLS - 1)
    toeplitz = rpb[:, :, dc]
    toeplitz = jnp.where(in_win[None, None], toeplitz.astype(F32), MASK_VALUE)
    n_var = toeplitz.shape[1] - kr + 1
    variants = jnp.stack([toeplitz[:, v:v + kr] for v in range(n_var)])
    variants = variants.transpose(0, 1, 3, 2, 4)
    return variants.reshape(n_var, ATT_HEADS // ATT_GROUP, ATT_GROUP * GRID_W, kr * GRID_W)


def _attention_kernel(q_ref, k_ref, v_ref, g_ref, bias_ref, o_ref, *, rows, kr):
    step = pl.program_id(1)
    gw = ATT_GROUP * ATT_HEAD_DIM
    row_head = lax.broadcasted_iota(jnp.int32, (ATT_GROUP * GRID_W, gw), 0) // GRID_W
    lane_head = lax.broadcasted_iota(jnp.int32, (ATT_GROUP * GRID_W, gw), 1) // ATT_HEAD_DIM
    own_head = row_head == lane_head
    out_lane_head = lax.broadcasted_iota(jnp.int32, (GRID_W, gw), 1) // ATT_HEAD_DIM
    scale = ATT_HEAD_DIM ** -0.5

    def one_row(j, carry):
        r = step * ATT_ROWS_PER_STEP + j
        row_start = jnp.clip(r - kr // 2, 0, rows - kr)
        variant = row_start - r + (WIN_ROWS - 1)
        q_tok = pl.multiple_of(j * GRID_W, GRID_W)
        k_tok = pl.multiple_of(row_start * GRID_W, GRID_W)
        for grp in range(ATT_HEADS // ATT_GROUP):
            lanes = slice(grp * gw, (grp + 1) * gw)
            q = q_ref[0, pl.ds(q_tok, GRID_W), lanes]
            q = (q.astype(F32) * scale).astype(BF16)
            q_bd = jnp.where(own_head, jnp.concatenate([q] * ATT_GROUP, axis=0), 0)
            keys = k_ref[0, pl.ds(k_tok, kr * GRID_W), lanes]
            vals = v_ref[0, pl.ds(k_tok, kr * GRID_W), lanes]
            s = _dot_nt(q_bd, keys) + bias_ref[variant, grp]
            m = jnp.max(s, axis=-1, keepdims=True)
            p = jnp.exp(s - m)
            denom = jnp.sum(p, axis=-1, keepdims=True)
            pv = _dot(p.astype(BF16), vals) / denom
            o = jnp.zeros((GRID_W, gw), F32)
            for h in range(ATT_GROUP):
                o = o + jnp.where(out_lane_head == h, pv[h * GRID_W:(h + 1) * GRID_W], 0.0)
            gate = g_ref[0, pl.ds(q_tok, GRID_W), lanes].astype(F32)
            o = o * (gate * jax.nn.sigmoid(gate))
            o_ref[0, pl.ds(q_tok, GRID_W), lanes] = o.astype(o_ref.dtype)
        return carry

    lax.fori_loop(0, ATT_ROWS_PER_STEP, one_row, 0, unroll=2)


def _attention(q, k, v, g, bias):
    batch, seq, width = q.shape
    rows = seq // GRID_W
    kr = min(WIN_ROWS, rows)
    assert rows % ATT_ROWS_PER_STEP == 0 and kr == WIN_ROWS
    blk = ATT_ROWS_PER_STEP * GRID_W
    tile = pl.BlockSpec((1, blk, width), lambda b, i: (b, i, 0))
    whole = pl.BlockSpec((1, seq, width), lambda b, i: (b, 0, 0))
    return pl.pallas_call(
        functools.partial(_attention_kernel, rows=rows, kr=kr),
        grid=(batch, rows // ATT_ROWS_PER_STEP),
        in_specs=[tile, whole, whole, tile,
                  pl.BlockSpec(bias.shape, lambda b, i: (0, 0, 0, 0),
                               pipeline_mode=pl.Buffered(1))],
        out_specs=tile,
        out_shape=jax.ShapeDtypeStruct((batch, seq, width), BF16),
        compiler_params=pltpu.CompilerParams(
            dimension_semantics=("arbitrary", "arbitrary"), vmem_limit_bytes=VMEM_LIMIT_BYTES),
        name="nbr_attention",
    )(q, k, v, g, bias)


def _hgrn_level_sizes(chunk):
    sizes = []
    c = chunk
    while c >= 2:
        sizes.append(c)
        c //= 2
    return sizes


def _hgrn_upper_rows(chunk):
    t = np.arange(chunk)
    fwd = np.stack([(t % c) >= c // 2 for c in _hgrn_level_sizes(chunk)])
    return np.stack([fwd, fwd[:, ::-1]])


def _hgrn_constants(chunk):
    t = np.arange(chunk)[:, None]
    u = np.arange(chunk)[None, :]
    blocks = [(u <= t), (u > t)]
    masks = []
    for c in _hgrn_level_sizes(chunk):
        half = c // 2
        mid = (t // c) * c + half
        upper = (t % c) >= half
        blocks.append(np.where(upper, (u >= mid) & (u <= t), (u > t) & (u <= mid - 1)))
        masks.append((t // c == u // c) & upper & ((u % c) < half))
    fwd_sums = np.stack(blocks).astype(np.float32)
    fwd_masks = np.stack(masks).astype(np.float32)
    bwd_sums = fwd_sums[:, ::-1, ::-1]
    bwd_masks = fwd_masks[:, ::-1, ::-1]
    flat = lambda m: np.concatenate([m.reshape(-1, chunk)] * 2, axis=1)
    return np.stack([flat(fwd_sums), flat(bwd_sums)]), np.concatenate([fwd_masks, bwd_masks], axis=2)


def _block_diag(a, b):
    zero = jnp.zeros_like(a)
    return jnp.concatenate([jnp.concatenate([a, zero], axis=1),
                            jnp.concatenate([zero, b], axis=1)], axis=0)


def _hgrn_kernel(q_ref, zf_ref, zb_ref, i_ref, g_ref, lbf_ref, lbb_ref, gain_ref,
                 sums_ref, masks_ref, o_ref, of_scr, ob_scr, sf_scr, sb_scr, k_scr, dec_scr,
                 *, layer, seq):
    chunk = HG_CHUNK
    pair = 2 * chunk
    n_pairs = seq // pair
    levels = _hgrn_level_sizes(chunk)
    upper = _hgrn_upper_rows(chunk)
    dh = HG_HEAD_DIM
    row_in_chunk = lax.broadcasted_iota(jnp.int32, (chunk, dh), 0)

    def lower_bound(logit_ref):
        logits = logit_ref[...].astype(F32)
        e = jnp.exp(logits - jnp.max(logits, axis=0, keepdims=True))
        return jnp.sum(e[:layer + 1], axis=0, keepdims=True) / jnp.sum(e, axis=0, keepdims=True)

    def gates(z, lb):
        f = lb + (1.0 - lb) * jax.nn.sigmoid(z.astype(F32))
        return jnp.log2(f), 1.0 - f

    def decays(direction, g2):
        hi = g2.astype(BF16)
        lo = (g2 - hi.astype(F32)).astype(BF16)
        stacked = [jnp.concatenate([hi[c * chunk:(c + 1) * chunk], lo[c * chunk:(c + 1) * chunk]], axis=0)
                   for c in range(2)]
        return jnp.exp2(_dot(sums_ref[direction], jnp.concatenate(stacked, axis=1)))

    def level_operand(direction, lvl, q, k, d):
        c = levels[lvl]
        half = c // 2
        if half % 8 == 0:
            qk = jnp.concatenate([(q if upper[direction, lvl, r] else k)[r:r + half]
                                  for r in range(0, chunk, half)], axis=0)
        else:
            pos = row_in_chunk % c
            qk = jnp.where((pos >= half) if direction == 0 else (pos < half), q, k)
        return (qk * d).astype(BF16)

    def chunk_part(slot, direction, c, q, v, base):
        rows = slice(c * chunk, (c + 1) * chunk)
        lanes = slice(c * dh, (c + 1) * dh)
        return dict(q=q[rows], k=k_scr[slot, direction, rows, :], v=v[rows],
                    dec=lambda r0, r1: dec_scr[slot, direction, r0:r1, lanes],
                    rows=pl.ds(pl.multiple_of(base + c * chunk, chunk), chunk))

    def pair_step(f, b):
        a = jnp.zeros((chunk, 2 * chunk), F32)
        for lvl in range(len(levels)):
            d_rows = ((2 + lvl) * chunk, (3 + lvl) * chunk)
            xf = level_operand(0, lvl, f["q"], f["k"], f["dec"](*d_rows))
            xb = level_operand(1, lvl, b["q"], b["k"], b["dec"](*d_rows))
            a = a + masks_ref[lvl] * _dot_nt(jnp.concatenate([xf, xb], axis=1), _block_diag(xf, xb))
        o = _dot(a.astype(BF16), _block_diag(f["v"], b["v"]))
        sf = sf_scr[...]
        sb = sb_scr[...]
        q_dec = jnp.concatenate([(f["q"] * f["dec"](0, chunk)).astype(BF16),
                                 (b["q"] * b["dec"](0, chunk)).astype(BF16)], axis=1)
        o = o + _dot_nt(q_dec, _block_diag(sf.astype(BF16), sb.astype(BF16)))
        diag_f = jnp.sum(f["q"] * f["k"], axis=-1, keepdims=True)
        diag_b = jnp.sum(b["q"] * b["k"], axis=-1, keepdims=True)
        of_scr[f["rows"], :] = o[:, :dh] + diag_f * f["v"].astype(F32)
        ob_scr[b["rows"], :] = o[:, dh:] + diag_b * b["v"].astype(F32)
        k_dec = _block_diag((f["k"] * f["dec"](chunk, 2 * chunk)).astype(BF16),
                            (b["k"] * b["dec"](chunk, 2 * chunk)).astype(BF16))
        v_t = jnp.concatenate([f["v"], b["v"]], axis=0).astype(F32).T.astype(BF16)
        u = _dot(v_t, k_dec)
        sf_scr[...] = sf * f["dec"](chunk - 1, chunk) + u[:, :dh]
        sb_scr[...] = sb * b["dec"](0, 1) + u[:, dh:]

    lb_f = lower_bound(lbf_ref)
    lb_b = lower_bound(lbb_ref)
    sf_scr[...] = jnp.zeros_like(sf_scr)
    sb_scr[...] = jnp.zeros_like(sb_scr)

    def bases(n):
        return (pl.multiple_of(n * pair, pair), pl.multiple_of((n_pairs - 1 - n) * pair, pair))

    def prepare(n, slot):
        for direction, base, z_ref, lb in zip((0, 1), bases(n), (zf_ref, zb_ref), (lb_f, lb_b)):
            g2, k = gates(z_ref[0, pl.ds(base, pair), :], lb)
            k_scr[slot, direction] = k
            dec_scr[slot, direction] = decays(direction, g2)

    def consume(n, slot):
        parts = []
        for direction, base in zip((0, 1), bases(n)):
            rows = pl.ds(base, pair)
            q = q_ref[0, rows, :].astype(F32)
            v = i_ref[0, rows, :]
            parts.append([chunk_part(slot, direction, c, q, v, base) for c in range(2)])
        pair_step(parts[0][0], parts[1][1])
        pair_step(parts[0][1], parts[1][0])

    def scan_body(m, carry):
        prepare(2 * m + 1, 1)
        consume(2 * m, 0)
        prepare(jnp.minimum(2 * m + 2, n_pairs - 1), 0)
        consume(2 * m + 1, 1)
        return carry

    prepare(0, 0)
    lax.fori_loop(0, n_pairs // 2, scan_body, 0)

    gain = gain_ref[...].astype(F32)
    norm_rows = 256

    def norm_body(n, carry):
        rows = pl.ds(pl.multiple_of(n * norm_rows, norm_rows), norm_rows)
        o = of_scr[rows, :] + ob_scr[rows, :]
        o = o * lax.rsqrt(jnp.mean(jnp.square(o), axis=-1, keepdims=True) + RMS_EPS) * gain
        gate = g_ref[0, rows, :].astype(F32)
        o_ref[0, rows, :] = (o * (gate * jax.nn.sigmoid(gate))).astype(o_ref.dtype)
        return carry

    lax.fori_loop(0, seq // norm_rows, norm_body, 0)


def _hgrn(q, zf, zb, i, g, lb_fwd_logits, lb_bwd_logits, gain_row, layer):
    batch, seq, width = q.shape
    assert width == HG_HEADS * HG_HEAD_DIM and seq % (2 * HG_CHUNK) == 0 and seq % 256 == 0
    sums, masks = _hgrn_constants(HG_CHUNK)
    sums = jnp.asarray(sums, BF16)
    masks = jnp.asarray(masks, F32)
    n_layers = lb_fwd_logits.shape[0]
    head = pl.BlockSpec((1, seq, HG_HEAD_DIM), lambda b, h: (b, 0, h))
    per_head_row = lambda n: pl.BlockSpec((n, HG_HEAD_DIM), lambda b, h: (0, h))
    return pl.pallas_call(
        functools.partial(_hgrn_kernel, layer=layer, seq=seq),
        grid=(batch, HG_HEADS),
        in_specs=[head, head, head, head, head,
                  per_head_row(n_layers), per_head_row(n_layers), per_head_row(1),
                  pl.BlockSpec(sums.shape, lambda b, h: (0, 0, 0)),
                  pl.BlockSpec(masks.shape, lambda b, h: (0, 0, 0))],
        out_specs=head,
        out_shape=jax.ShapeDtypeStruct((batch, seq, width), BF16),
        scratch_shapes=[pltpu.VMEM((seq, HG_HEAD_DIM), F32), pltpu.VMEM((seq, HG_HEAD_DIM), F32),
                        pltpu.VMEM((HG_HEAD_DIM, HG_HEAD_DIM), F32),
                        pltpu.VMEM((HG_HEAD_DIM, HG_HEAD_DIM), F32),
                        pltpu.VMEM((2, 2, 2 * HG_CHUNK, HG_HEAD_DIM), F32),
                        pltpu.VMEM((2, 2, sums.shape[1], 2 * HG_HEAD_DIM), F32)],
        compiler_params=pltpu.CompilerParams(
            dimension_semantics=("arbitrary", "arbitrary"), vmem_limit_bytes=VMEM_LIMIT_BYTES),
        name="hgrn2_scan",
    )(q, zf, zb, i, g, lb_fwd_logits, lb_bwd_logits, gain_row, sums, masks)


def _out_proj_kernel(oa_ref, oh_ref, x_ref, w_ref, b_ref, gain_ref, bias_ref, o_ref, *, alpha):
    d_att = oa_ref.shape[1]
    y = _dot(oa_ref[...], w_ref[:d_att, :]) + _dot(oh_ref[...], w_ref[d_att:, :]) + b_ref[...]
    r = alpha * x_ref[...] + y
    mu = jnp.mean(r, axis=-1, keepdims=True)
    c = r - mu
    var = jnp.mean(jnp.square(c), axis=-1, keepdims=True)
    o_ref[...] = (c * lax.rsqrt(var + LN_EPS) * gain_ref[...] + bias_ref[...]).astype(o_ref.dtype)


def _out_proj(o_a, o_h, x2d, w_bf16, b_row, gain_row, bias_row, alpha):
    n_tok, d_model = x2d.shape
    d_att, d_hg = o_a.shape[1], o_h.shape[1]
    row = pl.BlockSpec((1, d_model), lambda i: (0, 0))
    return pl.pallas_call(
        functools.partial(_out_proj_kernel, alpha=alpha),
        grid=(n_tok // PROJ_TM,),
        in_specs=[pl.BlockSpec((PROJ_TM, d_att), lambda i: (i, 0)),
                  pl.BlockSpec((PROJ_TM, d_hg), lambda i: (i, 0)),
                  pl.BlockSpec((PROJ_TM, d_model), lambda i: (i, 0)),
                  pl.BlockSpec((d_att + d_hg, d_model), lambda i: (0, 0)),
                  row, row, row],
        out_specs=pl.BlockSpec((PROJ_TM, d_model), lambda i: (i, 0)),
        out_shape=jax.ShapeDtypeStruct((n_tok, d_model), x2d.dtype),
        compiler_params=pltpu.CompilerParams(
            dimension_semantics=("arbitrary",), vmem_limit_bytes=VMEM_LIMIT_BYTES),
        name="out_proj_layernorm",
    )(o_a, o_h, x2d, w_bf16, b_row, gain_row, bias_row)


def _layer(x, layer, depth, w_in, b_in, rpb, lb_fwd_logits, lb_bwd_logits, hg_norm_gain,
           w_out, b_out, ln_gain, ln_bias):
    batch, seq, d_model = x.shape
    x2d = x.reshape(batch * seq, d_model)
    slabs = _in_proj(x2d, w_in.astype(BF16), b_in.reshape(1, -1).astype(F32))
    q_a, k_a, v_a, g_a, q_h, z_f, z_b, i_h, g_h = [s.reshape(batch, seq, SLAB) for s in slabs]
    bias = _attention_bias_table(rpb, seq // GRID_W)
    o_a = _attention(q_a, k_a, v_a, g_a, bias)
    o_h = _hgrn(q_h, z_f, z_b, i_h, g_h, lb_fwd_logits, lb_bwd_logits,
                hg_norm_gain.reshape(1, -1), layer)
    alpha = (2.0 * depth) ** 0.25
    out = _out_proj(o_a.reshape(batch * seq, SLAB), o_h.reshape(batch * seq, SLAB), x2d,
                    w_out.astype(BF16), b_out.reshape(1, -1), ln_gain.reshape(1, -1),
                    ln_bias.reshape(1, -1), alpha)
    return out.reshape(batch, seq, d_model)


def kernel(x, w_in, b_in, rpb, lb_fwd_logits, lb_bwd_logits, hg_norm_gain, w_out, b_out, ln_gain, ln_bias):
    depth = w_in.shape[0]
    for layer in range(depth):
        x = _layer(x, layer, depth, w_in[layer], b_in[layer], rpb[layer], lb_fwd_logits,
                   lb_bwd_logits, hg_norm_gain[layer], w_out[layer], b_out[layer],
                   ln_gain[layer], ln_bias[layer])
    return x
```

```python
import functools

import numpy as np
import jax
import jax.numpy as jnp
from jax import lax
from jax.experimental import pallas as pl
from jax.experimental.pallas import tpu as pltpu

GRID_W = 64
ATT_HEADS = 8
ATT_HEAD_DIM = 64
HG_HEADS = 4
HG_HEAD_DIM = 128
WIN_ROWS = 8
WIN_COLS = 16
LN_EPS = 1e-5
RMS_EPS = 1e-6
N_SLABS = 9
SLAB = 512

LANES = 128
F32_SUBLANES = 8
VMEM_LIMIT_BYTES = 56 * 1024 * 1024
PROJ_TM = 512
ATT_ROWS_PER_STEP = 8
ATT_GROUP = 4
HG_CHUNK = 64
MASK_VALUE = -1e30
KEY_SLAB = 1
LOG2_E = 1.4426950408889634
KEY_SCALE = ATT_HEAD_DIM ** -0.5 * LOG2_E

BF16 = jnp.bfloat16
F32 = jnp.float32


def _dot(a, b):
    return jnp.dot(a, b, preferred_element_type=F32)


def _dot_nt(a, b):
    return lax.dot_general(a, b, (((1,), (1,)), ((), ())), preferred_element_type=F32)


def _in_proj_kernel(x_ref, w_ref, b_ref, *out_refs):
    xb = x_ref[...].astype(BF16)
    for j, o_ref in enumerate(out_refs):
        cols = slice(j * SLAB, (j + 1) * SLAB)
        h = _dot(xb, w_ref[:, cols]) + b_ref[:, cols]
        if j == KEY_SLAB:
            h = h * KEY_SCALE
        o_ref[...] = h.astype(o_ref.dtype)


def _in_proj(x2d, w_bf16, b_row):
    n_tok, d_model = x2d.shape
    d_in = w_bf16.shape[1]
    assert d_in == N_SLABS * SLAB and n_tok % PROJ_TM == 0
    return pl.pallas_call(
        _in_proj_kernel,
        grid=(n_tok // PROJ_TM,),
        in_specs=[
            pl.BlockSpec((PROJ_TM, d_model), lambda i: (i, 0)),
            pl.BlockSpec((d_model, d_in), lambda i: (0, 0)),
            pl.BlockSpec((1, d_in), lambda i: (0, 0)),
        ],
        out_specs=[pl.BlockSpec((PROJ_TM, SLAB), lambda i: (i, 0))] * N_SLABS,
        out_shape=[jax.ShapeDtypeStruct((n_tok, SLAB), BF16)] * N_SLABS,
        compiler_params=pltpu.CompilerParams(
            dimension_semantics=("arbitrary",), vmem_limit_bytes=VMEM_LIMIT_BYTES),
        name="in_proj",
    )(x2d, w_bf16, b_row)


def _attention_bias_table(rpb, rows):
    assert rows >= WIN_ROWS
    n_rel = 2 * WIN_ROWS - 1
    n_tiles = (n_rel - 1) // 2
    groups = ATT_HEADS // ATT_GROUP
    qc = np.arange(GRID_W)
    kc = np.arange(GRID_W)
    col_start = np.clip(qc - WIN_COLS // 2, 0, GRID_W - WIN_COLS)
    in_win = (kc[None, :] >= col_start[:, None]) & (kc[None, :] < col_start[:, None] + WIN_COLS)
    dc = np.clip(kc[None, :] - qc[:, None], -(WIN_COLS - 1), WIN_COLS - 1) + (WIN_COLS - 1)
    pick_col = (dc[None] == np.arange(2 * WIN_COLS - 1)[:, None, None]).astype(np.float32)
    rel = np.array([[[2 * t + p + a for a in range(2)] for t in range(n_tiles)] for p in range(2)])
    r = rpb.astype(F32)[:, rel].reshape(groups, ATT_GROUP, 2, n_tiles, 2, 2 * WIN_COLS - 1)
    table = jnp.einsum('ghptaj,jqk->pgthqak', r, pick_col, precision=lax.Precision.HIGHEST)
    table = jnp.where(in_win[:, None, :], table * LOG2_E, MASK_VALUE)
    return table.reshape(2, groups, n_tiles, ATT_GROUP * GRID_W, 2 * GRID_W)


def _attention_kernel(q_ref, k_ref, v_ref, g_ref, bias_ref, o_ref, *, rows, kr):
    step = pl.program_id(1)
    gw = ATT_GROUP * ATT_HEAD_DIM
    row_head = lax.broadcasted_iota(jnp.int32, (ATT_GROUP * GRID_W, gw), 0) // GRID_W
    lane_head = lax.broadcasted_iota(jnp.int32, (ATT_GROUP * GRID_W, gw), 1) // ATT_HEAD_DIM
    own_head = row_head == lane_head
    out_lane_head = lax.broadcasted_iota(jnp.int32, (GRID_W, gw), 1) // ATT_HEAD_DIM

    def one_row(j, carry):
        r = step * ATT_ROWS_PER_STEP + j
        row_start = jnp.clip(r - kr // 2, 0, rows - kr)
        variant = row_start - r + (WIN_ROWS - 1)
        parity = variant % 2
        tile0 = variant // 2
        q_tok = pl.multiple_of(j * GRID_W, GRID_W)
        k_tok = pl.multiple_of(row_start * GRID_W, GRID_W)
        for grp in range(ATT_HEADS // ATT_GROUP):
            lanes = slice(grp * gw, (grp + 1) * gw)
            q = q_ref[0, pl.ds(q_tok, GRID_W), lanes]
            q_bd = jnp.where(own_head, jnp.concatenate([q] * ATT_GROUP, axis=0), 0)
            keys = k_ref[0, pl.ds(k_tok, kr * GRID_W), lanes]
            vals = v_ref[0, pl.ds(k_tok, kr * GRID_W), lanes]
            bias = jnp.concatenate([bias_ref[parity, grp, tile0 + t] for t in range(kr // 2)], axis=1)
            s = _dot_nt(q_bd, keys) + bias
            m = jnp.max(s, axis=-1, keepdims=True)
            p = jnp.exp2(s - m)
            denom = jnp.sum(p, axis=-1, keepdims=True)
            pv = _dot(p.astype(BF16), vals) / denom
            o = jnp.zeros((GRID_W, gw), F32)
            for h in range(ATT_GROUP):
                o = o + jnp.where(out_lane_head == h, pv[h * GRID_W:(h + 1) * GRID_W], 0.0)
            gate = g_ref[0, pl.ds(q_tok, GRID_W), lanes].astype(F32)
            o = o * (gate * jax.nn.sigmoid(gate))
            o_ref[0, pl.ds(q_tok, GRID_W), lanes] = o.astype(o_ref.dtype)
        return carry

    lax.fori_loop(0, ATT_ROWS_PER_STEP, one_row, 0, unroll=2)


def _attention(q, k, v, g, bias):
    batch, seq, width = q.shape
    rows = seq // GRID_W
    kr = min(WIN_ROWS, rows)
    assert rows % ATT_ROWS_PER_STEP == 0 and kr == WIN_ROWS
    blk = ATT_ROWS_PER_STEP * GRID_W
    tile = pl.BlockSpec((1, blk, width), lambda b, i: (b, i, 0))
    whole = pl.BlockSpec((1, seq, width), lambda b, i: (b, 0, 0))
    return pl.pallas_call(
        functools.partial(_attention_kernel, rows=rows, kr=kr),
        grid=(batch, rows // ATT_ROWS_PER_STEP),
        in_specs=[tile, whole, whole, tile,
                  pl.BlockSpec(bias.shape, lambda b, i: (0,) * bias.ndim,
                               pipeline_mode=pl.Buffered(1))],
        out_specs=tile,
        out_shape=jax.ShapeDtypeStruct((batch, seq, width), BF16),
        compiler_params=pltpu.CompilerParams(
            dimension_semantics=("arbitrary", "arbitrary"), vmem_limit_bytes=VMEM_LIMIT_BYTES),
        name="nbr_attention",
    )(q, k, v, g, bias)


def _hgrn_level_sizes(chunk):
    sizes = []
    c = chunk
    while c >= 2:
        sizes.append(c)
        c //= 2
    return sizes


def _hgrn_upper_rows(chunk):
    t = np.arange(chunk)
    fwd = np.stack([(t % c) >= c // 2 for c in _hgrn_level_sizes(chunk)])
    return np.stack([fwd, fwd[:, ::-1]])


def _hgrn_is_coarse(c):
    return (c // 2) % F32_SUBLANES == 0


def _hgrn_constants(chunk):
    t = np.arange(chunk)[:, None]
    u = np.arange(chunk)[None, :]
    blocks = [(u <= t), (u > t)]
    masks = []
    for c in _hgrn_level_sizes(chunk):
        half = c // 2
        mid = (t // c) * c + half
        upper = (t % c) >= half
        if not _hgrn_is_coarse(c):
            blocks.append(np.where(upper, (u >= mid) & (u <= t), (u > t) & (u <= mid - 1)))
        masks.append((t // c == u // c) & upper & ((u % c) < half))
    fwd_sums = np.stack(blocks).astype(np.float32)
    fwd_masks = np.stack(masks).astype(np.float32)
    bwd_sums = fwd_sums[:, ::-1, ::-1]
    bwd_masks = fwd_masks[:, ::-1, ::-1]
    flat = lambda m: np.concatenate([m.reshape(-1, chunk)] * 2, axis=1)
    return np.stack([flat(fwd_sums), flat(bwd_sums)]), np.concatenate([fwd_masks, bwd_masks], axis=2)


def _block_diag(a, b):
    zero = jnp.zeros_like(a)
    return jnp.concatenate([jnp.concatenate([a, zero], axis=1),
                            jnp.concatenate([zero, b], axis=1)], axis=0)


def _hgrn_kernel(q_ref, zf_ref, zb_ref, i_ref, g_ref, lbf_ref, lbb_ref, gain_ref,
                 sums_ref, masks_ref, o_ref, of_scr, ob_scr, sf_scr, sb_scr, k_scr, dec_scr,
                 *, layer, seq):
    chunk = HG_CHUNK
    pair = 2 * chunk
    n_pairs = seq // pair
    levels = _hgrn_level_sizes(chunk)
    n_coarse = sum(_hgrn_is_coarse(c) for c in levels)
    upper = _hgrn_upper_rows(chunk)
    dh = HG_HEAD_DIM
    row_in_chunk = lax.broadcasted_iota(jnp.int32, (chunk, dh), 0)

    def lower_bound(logit_ref):
        logits = logit_ref[...].astype(F32)
        e = jnp.exp(logits - jnp.max(logits, axis=0, keepdims=True))
        return jnp.sum(e[:layer + 1], axis=0, keepdims=True) / jnp.sum(e, axis=0, keepdims=True)

    def gates(z, lb):
        f = lb + (1.0 - lb) * jax.nn.sigmoid(z.astype(F32))
        return jnp.log2(f), 1.0 - f

    def store_decays(slot, direction, g2):
        hi = g2.astype(BF16)
        lo = (g2 - hi.astype(F32)).astype(BF16)
        stacked = [jnp.concatenate([hi[c * chunk:(c + 1) * chunk], lo[c * chunk:(c + 1) * chunk]], axis=0)
                   for c in range(2)]
        raw = _dot(sums_ref[direction], jnp.concatenate(stacked, axis=1))
        dec = dec_scr.at[slot, direction]
        dec[0:2 * chunk, :] = jnp.exp2(raw[0:2 * chunk])
        dec[(2 + n_coarse) * chunk:, :] = jnp.exp2(raw[2 * chunk:])
        cum = raw[0:chunk]
        for lvl in range(n_coarse):
            c = levels[lvl]
            half = c // 2
            pieces = []
            for r0 in range(0, chunk, c):
                first, second = cum[r0:r0 + half], cum[r0 + half:r0 + c]
                if direction == 0:
                    ref = cum[r0 + half - 1:r0 + half]
                    pieces += [ref - first, second - ref]
                else:
                    ref = cum[r0 + half:r0 + half + 1]
                    pieces += [first - ref, ref - second]
            dec[(2 + lvl) * chunk:(3 + lvl) * chunk, :] = jnp.exp2(jnp.concatenate(pieces, axis=0))

    def level_operand(direction, lvl, q, k, d):
        c = levels[lvl]
        half = c // 2
        if _hgrn_is_coarse(c):
            qk = jnp.concatenate([(q if upper[direction, lvl, r] else k)[r:r + half]
                                  for r in range(0, chunk, half)], axis=0)
        else:
            pos = row_in_chunk % c
            qk = jnp.where((pos >= half) if direction == 0 else (pos < half), q, k)
        return (qk * d).astype(BF16)

    def chunk_part(slot, direction, c, q, v, base):
        rows = slice(c * chunk, (c + 1) * chunk)
        lanes = slice(c * dh, (c + 1) * dh)
        return dict(q=q[rows], k=k_scr[slot, direction, rows, :], v=v[rows],
                    dec=lambda r0, r1: dec_scr[slot, direction, r0:r1, lanes],
                    rows=pl.ds(pl.multiple_of(base + c * chunk, chunk), chunk))

    def pair_step(f, b):
        a = jnp.zeros((chunk, 2 * chunk), F32)
        for lvl in range(len(levels)):
            d_rows = ((2 + lvl) * chunk, (3 + lvl) * chunk)
            xf = level_operand(0, lvl, f["q"], f["k"], f["dec"](*d_rows))
            xb = level_operand(1, lvl, b["q"], b["k"], b["dec"](*d_rows))
            a = a + masks_ref[lvl] * _dot_nt(jnp.concatenate([xf, xb], axis=1), _block_diag(xf, xb))
        o = _dot(a.astype(BF16), _block_diag(f["v"], b["v"]))
        sf = sf_scr[...]
        sb = sb_scr[...]
        q_dec = jnp.concatenate([(f["q"] * f["dec"](0, chunk)).astype(BF16),
                                 (b["q"] * b["dec"](0, chunk)).astype(BF16)], axis=1)
        o = o + _dot_nt(q_dec, _block_diag(sf.astype(BF16), sb.astype(BF16)))
        diag_f = jnp.sum(f["q"] * f["k"], axis=-1, keepdims=True)
        diag_b = jnp.sum(b["q"] * b["k"], axis=-1, keepdims=True)
        of_scr[f["rows"], :] = o[:, :dh] + diag_f * f["v"].astype(F32)
        ob_scr[b["rows"], :] = o[:, dh:] + diag_b * b["v"].astype(F32)
        k_dec = _block_diag((f["k"] * f["dec"](chunk, 2 * chunk)).astype(BF16),
                            (b["k"] * b["dec"](chunk, 2 * chunk)).astype(BF16))
        v_t = jnp.concatenate([f["v"], b["v"]], axis=0).astype(F32).T.astype(BF16)
        u = _dot(v_t, k_dec)
        sf_scr[...] = sf * f["dec"](chunk - 1, chunk) + u[:, :dh]
        sb_scr[...] = sb * b["dec"](0, 1) + u[:, dh:]

    lb_f = lower_bound(lbf_ref)
    lb_b = lower_bound(lbb_ref)
    sf_scr[...] = jnp.zeros_like(sf_scr)
    sb_scr[...] = jnp.zeros_like(sb_scr)

    def bases(n):
        return (pl.multiple_of(n * pair, pair), pl.multiple_of((n_pairs - 1 - n) * pair, pair))

    def prepare(n, slot):
        for direction, base, z_ref, lb in zip((0, 1), bases(n), (zf_ref, zb_ref), (lb_f, lb_b)):
            g2, k = gates(z_ref[0, pl.ds(base, pair), :], lb)
            k_scr[slot, direction] = k
            store_decays(slot, direction, g2)

    def consume(n, slot):
        parts = []
        for direction, base in zip((0, 1), bases(n)):
            rows = pl.ds(base, pair)
            q = q_ref[0, rows, :].astype(F32)
            v = i_ref[0, rows, :]
            parts.append([chunk_part(slot, direction, c, q, v, base) for c in range(2)])
        pair_step(parts[0][0], parts[1][1])
        pair_step(parts[0][1], parts[1][0])

    def scan_body(m, carry):
        prepare(2 * m + 1, 1)
        consume(2 * m, 0)
        prepare(jnp.minimum(2 * m + 2, n_pairs - 1), 0)
        consume(2 * m + 1, 1)
        return carry

    prepare(0, 0)
    lax.fori_loop(0, n_pairs // 2, scan_body, 0, unroll=2)

    gain = gain_ref[...].astype(F32)
    norm_rows = 256

    def norm_body(n, carry):
        rows = pl.ds(pl.multiple_of(n * norm_rows, norm_rows), norm_rows)
        o = of_scr[rows, :] + ob_scr[rows, :]
        o = o * lax.rsqrt(jnp.mean(jnp.square(o), axis=-1, keepdims=True) + RMS_EPS) * gain
        gate = g_ref[0, rows, :].astype(F32)
        o_ref[0, rows, :] = (o * (gate * jax.nn.sigmoid(gate))).astype(o_ref.dtype)
        return carry

    lax.fori_loop(0, seq // norm_rows, norm_body, 0)


def _hgrn(q, zf, zb, i, g, lb_fwd_logits, lb_bwd_logits, gain_row, layer):
    batch, seq, width = q.shape
    assert width == HG_HEADS * HG_HEAD_DIM and seq % (2 * HG_CHUNK) == 0 and seq % 256 == 0
    sums, masks = _hgrn_constants(HG_CHUNK)
    sums = jnp.asarray(sums, BF16)
    masks = jnp.asarray(masks, F32)
    n_layers = lb_fwd_logits.shape[0]
    head = pl.BlockSpec((1, seq, HG_HEAD_DIM), lambda b, h: (b, 0, h))
    per_head_row = lambda n: pl.BlockSpec((n, HG_HEAD_DIM), lambda b, h: (0, h))
    return pl.pallas_call(
        functools.partial(_hgrn_kernel, layer=layer, seq=seq),
        grid=(batch, HG_HEADS),
        in_specs=[head, head, head, head, head,
                  per_head_row(n_layers), per_head_row(n_layers), per_head_row(1),
                  pl.BlockSpec(sums.shape, lambda b, h: (0, 0, 0)),
                  pl.BlockSpec(masks.shape, lambda b, h: (0, 0, 0))],
        out_specs=head,
        out_shape=jax.ShapeDtypeStruct((batch, seq, width), BF16),
        scratch_shapes=[pltpu.VMEM((seq, HG_HEAD_DIM), F32), pltpu.VMEM((seq, HG_HEAD_DIM), F32),
                        pltpu.VMEM((HG_HEAD_DIM, HG_HEAD_DIM), F32),
                        pltpu.VMEM((HG_HEAD_DIM, HG_HEAD_DIM), F32),
                        pltpu.VMEM((2, 2, 2 * HG_CHUNK, HG_HEAD_DIM), F32),
                        pltpu.VMEM((2, 2, (2 + len(_hgrn_level_sizes(HG_CHUNK))) * HG_CHUNK, 2 * HG_HEAD_DIM), F32)],
        compiler_params=pltpu.CompilerParams(
            dimension_semantics=("arbitrary", "arbitrary"), vmem_limit_bytes=VMEM_LIMIT_BYTES),
        name="hgrn2_scan",
    )(q, zf, zb, i, g, lb_fwd_logits, lb_bwd_logits, gain_row, sums, masks)


def _out_proj_kernel(oa_ref, oh_ref, x_ref, w_ref, b_ref, gain_ref, bias_ref, o_ref, *, alpha):
    d_att = oa_ref.shape[1]
    y = _dot(oa_ref[...], w_ref[:d_att, :]) + _dot(oh_ref[...], w_ref[d_att:, :]) + b_ref[...]
    r = alpha * x_ref[...] + y
    mu = jnp.mean(r, axis=-1, keepdims=True)
    c = r - mu
    var = jnp.mean(jnp.square(c), axis=-1, keepdims=True)
    o_ref[...] = (c * lax.rsqrt(var + LN_EPS) * gain_ref[...] + bias_ref[...]).astype(o_ref.dtype)


def _out_proj(o_a, o_h, x2d, w_bf16, b_row, gain_row, bias_row, alpha):
    n_tok, d_model = x2d.shape
    d_att, d_hg = o_a.shape[1], o_h.shape[1]
    row = pl.BlockSpec((1, d_model), lambda i: (0, 0))
    return pl.pallas_call(
        functools.partial(_out_proj_kernel, alpha=alpha),
        grid=(n_tok // PROJ_TM,),
        in_specs=[pl.BlockSpec((PROJ_TM, d_att), lambda i: (i, 0)),
                  pl.BlockSpec((PROJ_TM, d_hg), lambda i: (i, 0)),
                  pl.BlockSpec((PROJ_TM, d_model), lambda i: (i, 0)),
                  pl.BlockSpec((d_att + d_hg, d_model), lambda i: (0, 0)),
                  row, row, row],
        out_specs=pl.BlockSpec((PROJ_TM, d_model), lambda i: (i, 0)),
        out_shape=jax.ShapeDtypeStruct((n_tok, d_model), x2d.dtype),
        compiler_params=pltpu.CompilerParams(
            dimension_semantics=("arbitrary",), vmem_limit_bytes=VMEM_LIMIT_BYTES),
        name="out_proj_layernorm",
    )(o_a, o_h, x2d, w_bf16, b_row, gain_row, bias_row)


def _layer(x, layer, depth, w_in, b_in, rpb, lb_fwd_logits, lb_bwd_logits, hg_norm_gain,
           w_out, b_out, ln_gain, ln_bias):
    batch, seq, d_model = x.shape
    x2d = x.reshape(batch * seq, d_model)
    slabs = _in_proj(x2d, w_in.astype(BF16), b_in.reshape(1, -1).astype(F32))
    q_a, k_a, v_a, g_a, q_h, z_f, z_b, i_h, g_h = [s.reshape(batch, seq, SLAB) for s in slabs]
    bias = _attention_bias_table(rpb, seq // GRID_W)
    o_a = _attention(q_a, k_a, v_a, g_a, bias)
    o_h = _hgrn(q_h, z_f, z_b, i_h, g_h, lb_fwd_logits, lb_bwd_logits,
                hg_norm_gain.reshape(1, -1), layer)
    alpha = (2.0 * depth) ** 0.25
    out = _out_proj(o_a.reshape(batch * seq, SLAB), o_h.reshape(batch * seq, SLAB), x2d,
                    w_out.astype(BF16), b_out.reshape(1, -1), ln_gain.reshape(1, -1),
                    ln_bias.reshape(1, -1), alpha)
    return out.reshape(batch, seq, d_model)


def kernel(x, w_in, b_in, rpb, lb_fwd_logits, lb_bwd_logits, hg_norm_gain, w_out, b_out, ln_gain, ln_bias):
    depth = w_in.shape[0]
    for layer in range(depth):
        x = _layer(x, layer, depth, w_in[layer], b_in[layer], rpb[layer], lb_fwd_logits,
                   lb_bwd_logits, hg_norm_gain[layer], w_out[layer], b_out[layer],
                   ln_gain[layer], ln_bias[layer])
    return x
```

```python
import functools

import numpy as np
import jax
import jax.numpy as jnp
from jax import lax
from jax.experimental import pallas as pl
from jax.experimental.pallas import tpu as pltpu

GRID_W = 64
ATT_HEADS = 8
ATT_HEAD_DIM = 64
HG_HEADS = 4
HG_HEAD_DIM = 128
WIN_ROWS = 8
WIN_COLS = 16
LN_EPS = 1e-5
RMS_EPS = 1e-6
N_SLABS = 9
SLAB = 512

LANES = 128
F32_SUBLANES = 8
VMEM_LIMIT_BYTES = 56 * 1024 * 1024
PROJ_TM = 512
OUT_PROJ_TM = 1024
OUT_PROJ_ROWS = 256
ATT_ROWS_PER_STEP = 8
ATT_GROUP = 4
HG_CHUNK = 64
MASK_VALUE = -1e30
KEY_SLAB = 1
LOG2_E = 1.4426950408889634
KEY_SCALE = ATT_HEAD_DIM ** -0.5 * LOG2_E

BF16 = jnp.bfloat16
F32 = jnp.float32


def _dot(a, b):
    return jnp.dot(a, b, preferred_element_type=F32)


def _dot_nt(a, b):
    return lax.dot_general(a, b, (((1,), (1,)), ((), ())), preferred_element_type=F32)


def _cast_weight_once(w_ref, w_scr):
    @pl.when(pl.program_id(0) == 0)
    def _():
        for c0 in range(0, w_ref.shape[1], SLAB):
            w_scr[:, c0:c0 + SLAB] = w_ref[:, c0:c0 + SLAB].astype(w_scr.dtype)


def _in_proj_kernel(x_ref, w_ref, b_ref, *refs):
    out_refs, w_scr = refs[:-1], refs[-1]
    _cast_weight_once(w_ref, w_scr)
    xb = x_ref[...].astype(BF16)
    for j, o_ref in enumerate(out_refs):
        cols = slice(j * SLAB, (j + 1) * SLAB)
        h = _dot(xb, w_scr[:, cols]) + b_ref[:, cols]
        if j == KEY_SLAB:
            h = h * KEY_SCALE
        o_ref[...] = h.astype(o_ref.dtype)


def _in_proj(x2d, w, b_row):
    n_tok, d_model = x2d.shape
    d_in = w.shape[1]
    assert d_in == N_SLABS * SLAB and n_tok % PROJ_TM == 0
    return pl.pallas_call(
        _in_proj_kernel,
        grid=(n_tok // PROJ_TM,),
        in_specs=[
            pl.BlockSpec((PROJ_TM, d_model), lambda i: (i, 0)),
            pl.BlockSpec((d_model, d_in), lambda i: (0, 0), pipeline_mode=pl.Buffered(1)),
            pl.BlockSpec((1, d_in), lambda i: (0, 0)),
        ],
        out_specs=[pl.BlockSpec((PROJ_TM, SLAB), lambda i: (i, 0))] * N_SLABS,
        out_shape=[jax.ShapeDtypeStruct((n_tok, SLAB), BF16)] * N_SLABS,
        scratch_shapes=[pltpu.VMEM((d_model, d_in), BF16)],
        compiler_params=pltpu.CompilerParams(
            dimension_semantics=("arbitrary",), vmem_limit_bytes=VMEM_LIMIT_BYTES),
        name="in_proj",
    )(x2d, w, b_row)


ATT_REL_ROWS = 2 * WIN_ROWS - 1
ATT_BIAS_TILES = (ATT_REL_ROWS - 1) // 2
ATT_GROUPS = ATT_HEADS // ATT_GROUP


def _attention_bias_rows(rpb):
    n_rel_cols = 2 * WIN_COLS - 1
    m = np.arange(LANES)
    first = np.where(m < WIN_COLS, m + WIN_COLS - 1, m - (LANES - WIN_COLS + 1))
    first_ok = (m < WIN_COLS) | (m > LANES - WIN_COLS)
    second = m - (GRID_W - WIN_COLS + 1)
    second_ok = (second >= 0) & (second < n_rel_cols)
    col = np.where(first_ok, first, np.where(second_ok, second, 0))
    p = np.arange(2)[:, None, None]
    t = np.arange(ATT_BIAS_TILES)[None, :, None]
    rel = 2 * t + p + np.where(first_ok, 0, 1)[None, None, :]
    gen = jnp.where(first_ok | second_ok, rpb.astype(F32)[:, rel, col[None, None, :]], 0.0) * LOG2_E
    gen = gen.reshape(ATT_GROUPS, ATT_GROUP, 2, ATT_BIAS_TILES, LANES).transpose(2, 0, 3, 1, 4)
    return gen.reshape(2 * ATT_GROUPS * ATT_BIAS_TILES * ATT_GROUP, 1, LANES)


def _attention_window_mask():
    qc = np.arange(GRID_W)[:, None]
    kc = np.arange(LANES)[None, :] % GRID_W
    col_start = np.clip(qc - WIN_COLS // 2, 0, GRID_W - WIN_COLS)
    return ((kc >= col_start) & (kc < col_start + WIN_COLS)).astype(np.float32)


def _attention_kernel(q_ref, k_ref, v_ref, g_ref, gen_ref, win_ref, o_ref, bias_scr, *, rows, kr):
    step = pl.program_id(1)
    gw = ATT_GROUP * ATT_HEAD_DIM
    row_head = lax.broadcasted_iota(jnp.int32, (ATT_GROUP * GRID_W, gw), 0) // GRID_W
    lane_head = lax.broadcasted_iota(jnp.int32, (ATT_GROUP * GRID_W, gw), 1) // ATT_HEAD_DIM
    own_head = row_head == lane_head
    out_lane_head = lax.broadcasted_iota(jnp.int32, (GRID_W, gw), 1) // ATT_HEAD_DIM

    @pl.when((pl.program_id(0) == 0) & (step == 0))
    def _build_bias():
        in_window = win_ref[...] > 0.0

        def one_block(n, carry):
            gen = jnp.broadcast_to(gen_ref[n], (GRID_W, LANES))
            toeplitz = pltpu.roll(gen, 0, 1, stride=1, stride_axis=0)
            head_rows = pl.ds(pl.multiple_of((n % ATT_GROUP) * GRID_W, GRID_W), GRID_W)
            bias_scr[n // ATT_GROUP, head_rows, :] = jnp.where(in_window, toeplitz, MASK_VALUE)
            return carry

        lax.fori_loop(0, gen_ref.shape[0], one_block, 0, unroll=8)

    def one_row(j, carry):
        r = step * ATT_ROWS_PER_STEP + j
        row_start = jnp.clip(r - kr // 2, 0, rows - kr)
        variant = row_start - r + (WIN_ROWS - 1)
        q_tok = pl.multiple_of(j * GRID_W, GRID_W)
        k_tok = pl.multiple_of(row_start * GRID_W, GRID_W)
        for grp in range(ATT_GROUPS):
            lanes = slice(grp * gw, (grp + 1) * gw)
            q = q_ref[0, pl.ds(q_tok, GRID_W), lanes]
            q_bd = jnp.where(own_head, jnp.concatenate([q] * ATT_GROUP, axis=0), 0)
            keys = k_ref[0, pl.ds(k_tok, kr * GRID_W), lanes]
            vals = v_ref[0, pl.ds(k_tok, kr * GRID_W), lanes]
            tile0 = ((variant % 2) * ATT_GROUPS + grp) * ATT_BIAS_TILES + variant // 2
            bias = jnp.concatenate([bias_scr[tile0 + t] for t in range(kr // 2)], axis=1)
            s = _dot_nt(q_bd, keys) + bias
            m = jnp.max(s, axis=-1, keepdims=True)
            p = jnp.exp2(s - m)
            denom = jnp.sum(p, axis=-1, keepdims=True)
            pv = _dot(p.astype(BF16), vals) / denom
            o = jnp.zeros((GRID_W, gw), F32)
            for h in range(ATT_GROUP):
                o = o + jnp.where(out_lane_head == h, pv[h * GRID_W:(h + 1) * GRID_W], 0.0)
            gate = g_ref[0, pl.ds(q_tok, GRID_W), lanes].astype(F32)
            o = o * (gate * jax.nn.sigmoid(gate))
            o_ref[0, pl.ds(q_tok, GRID_W), lanes] = o.astype(o_ref.dtype)
        return carry

    lax.fori_loop(0, ATT_ROWS_PER_STEP, one_row, 0, unroll=2)


def _attention(q, k, v, g, rpb):
    batch, seq, width = q.shape
    rows = seq // GRID_W
    kr = min(WIN_ROWS, rows)
    assert rows % ATT_ROWS_PER_STEP == 0 and kr == WIN_ROWS and 2 * GRID_W == LANES
    gen = _attention_bias_rows(rpb)
    win = jnp.asarray(_attention_window_mask())
    blk = ATT_ROWS_PER_STEP * GRID_W
    tile = pl.BlockSpec((1, blk, width), lambda b, i: (b, i, 0))
    whole = pl.BlockSpec((1, seq, width), lambda b, i: (b, 0, 0))
    return pl.pallas_call(
        functools.partial(_attention_kernel, rows=rows, kr=kr),
        grid=(batch, rows // ATT_ROWS_PER_STEP),
        in_specs=[tile, whole, whole, tile,
                  pl.BlockSpec(gen.shape, lambda b, i: (0, 0, 0)),
                  pl.BlockSpec(win.shape, lambda b, i: (0, 0))],
        out_specs=tile,
        out_shape=jax.ShapeDtypeStruct((batch, seq, width), BF16),
        scratch_shapes=[pltpu.VMEM((2 * ATT_GROUPS * ATT_BIAS_TILES, ATT_GROUP * GRID_W, 2 * GRID_W), F32)],
        compiler_params=pltpu.CompilerParams(
            dimension_semantics=("arbitrary", "arbitrary"), vmem_limit_bytes=VMEM_LIMIT_BYTES),
        name="nbr_attention",
    )(q, k, v, g, gen, win)


def _hgrn_level_sizes(chunk):
    sizes = []
    c = chunk
    while c >= 2:
        sizes.append(c)
        c //= 2
    return sizes


def _hgrn_upper_rows(chunk):
    t = np.arange(chunk)
    fwd = np.stack([(t % c) >= c // 2 for c in _hgrn_level_sizes(chunk)])
    return np.stack([fwd, fwd[:, ::-1]])


def _hgrn_is_coarse(c):
    return (c // 2) % F32_SUBLANES == 0


def _hgrn_constants(chunk):
    t = np.arange(chunk)[:, None]
    u = np.arange(chunk)[None, :]
    blocks = [(u <= t), (u > t)]
    masks = []
    for c in _hgrn_level_sizes(chunk):
        half = c // 2
        mid = (t // c) * c + half
        upper = (t % c) >= half
        if not _hgrn_is_coarse(c):
            blocks.append(np.where(upper, (u >= mid) & (u <= t), (u > t) & (u <= mid - 1)))
        masks.append((t // c == u // c) & upper & ((u % c) < half))
    fwd_sums = np.stack(blocks).astype(np.float32)
    fwd_masks = np.stack(masks).astype(np.float32)
    bwd_sums = fwd_sums[:, ::-1, ::-1]
    bwd_masks = fwd_masks[:, ::-1, ::-1]
    flat = lambda m: np.concatenate([m.reshape(-1, chunk)] * 2, axis=1)
    return np.stack([flat(fwd_sums), flat(bwd_sums)]), np.concatenate([fwd_masks, bwd_masks], axis=2)


def _block_diag(a, b):
    zero = jnp.zeros_like(a)
    return jnp.concatenate([jnp.concatenate([a, zero], axis=1),
                            jnp.concatenate([zero, b], axis=1)], axis=0)


def _hgrn_kernel(q_ref, zf_ref, zb_ref, i_ref, g_ref, lbf_ref, lbb_ref, gain_ref,
                 sums_ref, masks_ref, o_ref, of_scr, ob_scr, sf_scr, sb_scr, k_scr, dec_scr,
                 *, layer, seq):
    chunk = HG_CHUNK
    pair = 2 * chunk
    n_pairs = seq // pair
    levels = _hgrn_level_sizes(chunk)
    n_coarse = sum(_hgrn_is_coarse(c) for c in levels)
    upper = _hgrn_upper_rows(chunk)
    dh = HG_HEAD_DIM
    row_in_chunk = lax.broadcasted_iota(jnp.int32, (chunk, dh), 0)

    def lower_bound(logit_ref):
        logits = logit_ref[...].astype(F32)
        e = jnp.exp(logits - jnp.max(logits, axis=0, keepdims=True))
        return jnp.sum(e[:layer + 1], axis=0, keepdims=True) / jnp.sum(e, axis=0, keepdims=True)

    def gates(z, lb):
        f = lb + (1.0 - lb) * jax.nn.sigmoid(z.astype(F32))
        return jnp.log2(f), 1.0 - f

    def store_decays(slot, direction, g2):
        hi = g2.astype(BF16)
        lo = (g2 - hi.astype(F32)).astype(BF16)
        stacked = [jnp.concatenate([hi[c * chunk:(c + 1) * chunk], lo[c * chunk:(c + 1) * chunk]], axis=0)
                   for c in range(2)]
        raw = _dot(sums_ref[direction], jnp.concatenate(stacked, axis=1))
        dec = dec_scr.at[slot, direction]
        dec[0:2 * chunk, :] = jnp.exp2(raw[0:2 * chunk])
        dec[(2 + n_coarse) * chunk:, :] = jnp.exp2(raw[2 * chunk:])
        cum = raw[0:chunk]
        for lvl in range(n_coarse):
            c = levels[lvl]
            half = c // 2
            pieces = []
            for r0 in range(0, chunk, c):
                first, second = cum[r0:r0 + half], cum[r0 + half:r0 + c]
                if direction == 0:
                    ref = cum[r0 + half - 1:r0 + half]
                    pieces += [ref - first, second - ref]
                else:
                    ref = cum[r0 + half:r0 + half + 1]
                    pieces += [first - ref, ref - second]
            dec[(2 + lvl) * chunk:(3 + lvl) * chunk, :] = jnp.exp2(jnp.concatenate(pieces, axis=0))

    def level_operand(direction, lvl, q, k, d):
        c = levels[lvl]
        half = c // 2
        if _hgrn_is_coarse(c):
            qk = jnp.concatenate([(q if upper[direction, lvl, r] else k)[r:r + half]
                                  for r in range(0, chunk, half)], axis=0)
        else:
            pos = row_in_chunk % c
            qk = jnp.where((pos >= half) if direction == 0 else (pos < half), q, k)
        return (qk * d).astype(BF16)

    def chunk_part(slot, direction, c, q, v, base):
        rows = slice(c * chunk, (c + 1) * chunk)
        lanes = slice(c * dh, (c + 1) * dh)
        return dict(q=q[rows], k=k_scr[slot, direction, rows, :], v=v[rows],
                    dec=lambda r0, r1: dec_scr[slot, direction, r0:r1, lanes],
                    rows=pl.ds(pl.multiple_of(base + c * chunk, chunk), chunk))

    def pair_step(f, b):
        a = jnp.zeros((chunk, 2 * chunk), F32)
        for lvl in range(len(levels)):
            d_rows = ((2 + lvl) * chunk, (3 + lvl) * chunk)
            xf = level_operand(0, lvl, f["q"], f["k"], f["dec"](*d_rows))
            xb = level_operand(1, lvl, b["q"], b["k"], b["dec"](*d_rows))
            a = a + masks_ref[lvl] * _dot_nt(jnp.concatenate([xf, xb], axis=1), _block_diag(xf, xb))
        o = _dot(a.astype(BF16), _block_diag(f["v"], b["v"]))
        sf = sf_scr[...]
        sb = sb_scr[...]
        q_dec = jnp.concatenate([(f["q"] * f["dec"](0, chunk)).astype(BF16),
                                 (b["q"] * b["dec"](0, chunk)).astype(BF16)], axis=1)
        o = o + _dot_nt(q_dec, _block_diag(sf.astype(BF16), sb.astype(BF16)))
        diag_f = jnp.sum(f["q"] * f["k"], axis=-1, keepdims=True)
        diag_b = jnp.sum(b["q"] * b["k"], axis=-1, keepdims=True)
        of_scr[f["rows"], :] = o[:, :dh] + diag_f * f["v"].astype(F32)
        ob_scr[b["rows"], :] = o[:, dh:] + diag_b * b["v"].astype(F32)
        k_dec = _block_diag((f["k"] * f["dec"](chunk, 2 * chunk)).astype(BF16),
                            (b["k"] * b["dec"](chunk, 2 * chunk)).astype(BF16))
        v_t = jnp.concatenate([f["v"], b["v"]], axis=0).astype(F32).T.astype(BF16)
        u = _dot(v_t, k_dec)
        sf_scr[...] = sf * f["dec"](chunk - 1, chunk) + u[:, :dh]
        sb_scr[...] = sb * b["dec"](0, 1) + u[:, dh:]

    lb_f = lower_bound(lbf_ref)
    lb_b = lower_bound(lbb_ref)
    sf_scr[...] = jnp.zeros_like(sf_scr)
    sb_scr[...] = jnp.zeros_like(sb_scr)

    def bases(n):
        return (pl.multiple_of(n * pair, pair), pl.multiple_of((n_pairs - 1 - n) * pair, pair))

    def prepare(n, slot):
        for direction, base, z_ref, lb in zip((0, 1), bases(n), (zf_ref, zb_ref), (lb_f, lb_b)):
            g2, k = gates(z_ref[0, pl.ds(base, pair), :], lb)
            k_scr[slot, direction] = k
            store_decays(slot, direction, g2)

    def consume(n, slot):
        parts = []
        for direction, base in zip((0, 1), bases(n)):
            rows = pl.ds(base, pair)
            q = q_ref[0, rows, :].astype(F32)
            v = i_ref[0, rows, :]
            parts.append([chunk_part(slot, direction, c, q, v, base) for c in range(2)])
        pair_step(parts[0][0], parts[1][1])
        pair_step(parts[0][1], parts[1][0])

    def scan_body(m, carry):
        prepare(2 * m + 1, 1)
        consume(2 * m, 0)
        prepare(jnp.minimum(2 * m + 2, n_pairs - 1), 0)
        consume(2 * m + 1, 1)
        return carry

    prepare(0, 0)
    lax.fori_loop(0, n_pairs // 2, scan_body, 0, unroll=2)

    gain = gain_ref[...].astype(F32)
    norm_rows = 256

    def norm_body(n, carry):
        rows = pl.ds(pl.multiple_of(n * norm_rows, norm_rows), norm_rows)
        o = of_scr[rows, :] + ob_scr[rows, :]
        o = o * lax.rsqrt(jnp.mean(jnp.square(o), axis=-1, keepdims=True) + RMS_EPS) * gain
        gate = g_ref[0, rows, :].astype(F32)
        o_ref[0, rows, :] = (o * (gate * jax.nn.sigmoid(gate))).astype(o_ref.dtype)
        return carry

    lax.fori_loop(0, seq // norm_rows, norm_body, 0)


def _hgrn(q, zf, zb, i, g, lb_fwd_logits, lb_bwd_logits, gain_row, layer):
    batch, seq, width = q.shape
    assert width == HG_HEADS * HG_HEAD_DIM and seq % (2 * HG_CHUNK) == 0 and seq % 256 == 0
    sums, masks = _hgrn_constants(HG_CHUNK)
    sums = jnp.asarray(sums, BF16)
    masks = jnp.asarray(masks, F32)
    n_layers = lb_fwd_logits.shape[0]
    head = pl.BlockSpec((1, seq, HG_HEAD_DIM), lambda b, h: (b, 0, h))
    per_head_row = lambda n: pl.BlockSpec((n, HG_HEAD_DIM), lambda b, h: (0, h))
    return pl.pallas_call(
        functools.partial(_hgrn_kernel, layer=layer, seq=seq),
        grid=(batch, HG_HEADS),
        in_specs=[head, head, head, head, head,
                  per_head_row(n_layers), per_head_row(n_layers), per_head_row(1),
                  pl.BlockSpec(sums.shape, lambda b, h: (0, 0, 0)),
                  pl.BlockSpec(masks.shape, lambda b, h: (0, 0, 0))],
        out_specs=head,
        out_shape=jax.ShapeDtypeStruct((batch, seq, width), BF16),
        scratch_shapes=[pltpu.VMEM((seq, HG_HEAD_DIM), F32), pltpu.VMEM((seq, HG_HEAD_DIM), F32),
                        pltpu.VMEM((HG_HEAD_DIM, HG_HEAD_DIM), F32),
                        pltpu.VMEM((HG_HEAD_DIM, HG_HEAD_DIM), F32),
                        pltpu.VMEM((2, 2, 2 * HG_CHUNK, HG_HEAD_DIM), F32),
                        pltpu.VMEM((2, 2, (2 + len(_hgrn_level_sizes(HG_CHUNK))) * HG_CHUNK, 2 * HG_HEAD_DIM), F32)],
        compiler_params=pltpu.CompilerParams(
            dimension_semantics=("arbitrary", "arbitrary"), vmem_limit_bytes=VMEM_LIMIT_BYTES),
        name="hgrn2_scan",
    )(q, zf, zb, i, g, lb_fwd_logits, lb_bwd_logits, gain_row, sums, masks)


def _out_proj_kernel(oa_ref, oh_ref, x_ref, w_ref, b_ref, gain_ref, bias_ref, o_ref, w_scr, *, alpha):
    d_att = oa_ref.shape[1]
    _cast_weight_once(w_ref, w_scr)
    for r0 in range(0, o_ref.shape[0], OUT_PROJ_ROWS):
        rows = slice(r0, r0 + OUT_PROJ_ROWS)
        y = _dot(oa_ref[rows, :], w_scr[:d_att, :]) + _dot(oh_ref[rows, :], w_scr[d_att:, :]) + b_ref[...]
        r = alpha * x_ref[rows, :] + y
        mu = jnp.mean(r, axis=-1, keepdims=True)
        c = r - mu
        var = jnp.mean(jnp.square(c), axis=-1, keepdims=True)
        o_ref[rows, :] = (c * lax.rsqrt(var + LN_EPS) * gain_ref[...] + bias_ref[...]).astype(o_ref.dtype)


def _out_proj(o_a, o_h, x2d, w, b_row, gain_row, bias_row, alpha):
    n_tok, d_model = x2d.shape
    d_att, d_hg = o_a.shape[1], o_h.shape[1]
    row = pl.BlockSpec((1, d_model), lambda i: (0, 0))
    return pl.pallas_call(
        functools.partial(_out_proj_kernel, alpha=alpha),
        grid=(n_tok // OUT_PROJ_TM,),
        in_specs=[pl.BlockSpec((OUT_PROJ_TM, d_att), lambda i: (i, 0)),
                  pl.BlockSpec((OUT_PROJ_TM, d_hg), lambda i: (i, 0)),
                  pl.BlockSpec((OUT_PROJ_TM, d_model), lambda i: (i, 0)),
                  pl.BlockSpec((d_att + d_hg, d_model), lambda i: (0, 0), pipeline_mode=pl.Buffered(1)),
                  row, row, row],
        out_specs=pl.BlockSpec((OUT_PROJ_TM, d_model), lambda i: (i, 0)),
        out_shape=jax.ShapeDtypeStruct((n_tok, d_model), x2d.dtype),
        scratch_shapes=[pltpu.VMEM((d_att + d_hg, d_model), BF16)],
        compiler_params=pltpu.CompilerParams(
            dimension_semantics=("arbitrary",), vmem_limit_bytes=VMEM_LIMIT_BYTES),
        name="out_proj_layernorm",
    )(o_a, o_h, x2d, w, b_row, gain_row, bias_row)


def _layer(x, layer, depth, w_in, b_in, rpb, lb_fwd_logits, lb_bwd_logits, hg_norm_gain,
           w_out, b_out, ln_gain, ln_bias):
    batch, seq, d_model = x.shape
    x2d = x.reshape(batch * seq, d_model)
    slabs = _in_proj(x2d, w_in, b_in.reshape(1, -1))
    q_a, k_a, v_a, g_a, q_h, z_f, z_b, i_h, g_h = [s.reshape(batch, seq, SLAB) for s in slabs]
    o_a = _attention(q_a, k_a, v_a, g_a, rpb)
    o_h = _hgrn(q_h, z_f, z_b, i_h, g_h, lb_fwd_logits, lb_bwd_logits,
                hg_norm_gain.reshape(1, -1), layer)
    alpha = (2.0 * depth) ** 0.25
    out = _out_proj(o_a.reshape(batch * seq, SLAB), o_h.reshape(batch * seq, SLAB), x2d,
                    w_out, b_out.reshape(1, -1), ln_gain.reshape(1, -1),
                    ln_bias.reshape(1, -1), alpha)
    return out.reshape(batch, seq, d_model)


def kernel(x, w_in, b_in, rpb, lb_fwd_logits, lb_bwd_logits, hg_norm_gain, w_out, b_out, ln_gain, ln_bias):
    depth = w_in.shape[0]
    for layer in range(depth):
        x = _layer(x, layer, depth, w_in[layer], b_in[layer], rpb[layer], lb_fwd_logits,
                   lb_bwd_logits, hg_norm_gain[layer], w_out[layer], b_out[layer],
                   ln_gain[layer], ln_bias[layer])
    return x
```

```python
import functools

import numpy as np
import jax
import jax.numpy as jnp
from jax import lax
from jax.experimental import pallas as pl
from jax.experimental.pallas import tpu as pltpu

GRID_W = 64
ATT_HEADS = 8
ATT_HEAD_DIM = 64
HG_HEADS = 4
HG_HEAD_DIM = 128
WIN_ROWS = 8
WIN_COLS = 16
LN_EPS = 1e-5
RMS_EPS = 1e-6
N_SLABS = 9
SLAB = 512

LANES = 128
F32_SUBLANES = 8
VMEM_LIMIT_BYTES = 56 * 1024 * 1024
PROJ_TM = 512
OUT_PROJ_TM = 1024
OUT_PROJ_ROWS = 256
ATT_ROWS_PER_STEP = 16
ATT_UNROLL = 4
ATT_GROUP = 4
HG_CHUNK = 64
MASK_VALUE = -1e30
LOG2_E = 1.4426950408889634
SLAB_SCALES = {1: ATT_HEAD_DIM ** -0.5 * LOG2_E, 3: 0.5, 5: 0.5, 6: 0.5, 8: 0.5}


def _silu_from_half(half_g):
    return half_g * (1.0 + jnp.tanh(half_g))

BF16 = jnp.bfloat16
F32 = jnp.float32


def _dot(a, b):
    return jnp.dot(a, b, preferred_element_type=F32)


def _dot_nt(a, b):
    return lax.dot_general(a, b, (((1,), (1,)), ((), ())), preferred_element_type=F32)


def _cast_weight_once(w_ref, w_scr):
    @pl.when(pl.program_id(0) == 0)
    def _():
        for c0 in range(0, w_ref.shape[1], SLAB):
            w_scr[:, c0:c0 + SLAB] = w_ref[:, c0:c0 + SLAB].astype(w_scr.dtype)


def _in_proj_kernel(x_ref, w_ref, b_ref, *refs):
    out_refs, w_scr = refs[:-1], refs[-1]
    _cast_weight_once(w_ref, w_scr)
    xb = x_ref[...].astype(BF16)
    for j, o_ref in enumerate(out_refs):
        cols = slice(j * SLAB, (j + 1) * SLAB)
        h = _dot(xb, w_scr[:, cols]) + b_ref[:, cols]
        if j in SLAB_SCALES:
            h = h * SLAB_SCALES[j]
        o_ref[...] = h.astype(o_ref.dtype)


def _in_proj(x2d, w, b_row):
    n_tok, d_model = x2d.shape
    d_in = w.shape[1]
    assert d_in == N_SLABS * SLAB and n_tok % PROJ_TM == 0
    return pl.pallas_call(
        _in_proj_kernel,
        grid=(n_tok // PROJ_TM,),
        in_specs=[
            pl.BlockSpec((PROJ_TM, d_model), lambda i: (i, 0)),
            pl.BlockSpec((d_model, d_in), lambda i: (0, 0), pipeline_mode=pl.Buffered(1)),
            pl.BlockSpec((1, d_in), lambda i: (0, 0)),
        ],
        out_specs=[pl.BlockSpec((PROJ_TM, SLAB), lambda i: (i, 0))] * N_SLABS,
        out_shape=[jax.ShapeDtypeStruct((n_tok, SLAB), BF16)] * N_SLABS,
        scratch_shapes=[pltpu.VMEM((d_model, d_in), BF16)],
        compiler_params=pltpu.CompilerParams(
            dimension_semantics=("arbitrary",), vmem_limit_bytes=VMEM_LIMIT_BYTES),
        name="in_proj",
    )(x2d, w, b_row)


ATT_REL_ROWS = 2 * WIN_ROWS - 1
ATT_BIAS_TILES = (ATT_REL_ROWS - 1) // 2
ATT_GROUPS = ATT_HEADS // ATT_GROUP


def _attention_bias_rows(rpb):
    n_rel_cols = 2 * WIN_COLS - 1
    m = np.arange(LANES)
    first = np.where(m < WIN_COLS, m + WIN_COLS - 1, m - (LANES - WIN_COLS + 1))
    first_ok = (m < WIN_COLS) | (m > LANES - WIN_COLS)
    second = m - (GRID_W - WIN_COLS + 1)
    second_ok = (second >= 0) & (second < n_rel_cols)
    col = np.where(first_ok, first, np.where(second_ok, second, 0))
    p = np.arange(2)[:, None, None]
    t = np.arange(ATT_BIAS_TILES)[None, :, None]
    rel = 2 * t + p + np.where(first_ok, 0, 1)[None, None, :]
    gen = jnp.where(first_ok | second_ok, rpb.astype(F32)[:, rel, col[None, None, :]], 0.0) * LOG2_E
    gen = gen.reshape(ATT_GROUPS, ATT_GROUP, 2, ATT_BIAS_TILES, LANES).transpose(2, 0, 3, 1, 4)
    return gen.reshape(2 * ATT_GROUPS * ATT_BIAS_TILES * ATT_GROUP, 1, LANES)


def _attention_window_mask():
    qc = np.arange(GRID_W)[:, None]
    kc = np.arange(LANES)[None, :] % GRID_W
    col_start = np.clip(qc - WIN_COLS // 2, 0, GRID_W - WIN_COLS)
    return ((kc >= col_start) & (kc < col_start + WIN_COLS)).astype(np.float32)


def _attention_kernel(q_ref, k_ref, v_ref, g_ref, gen_ref, win_ref, o_ref, bias_scr, *, rows, kr):
    step = pl.program_id(1)
    gw = ATT_GROUP * ATT_HEAD_DIM
    row_head = lax.broadcasted_iota(jnp.int32, (ATT_GROUP * GRID_W, gw), 0) // GRID_W
    lane_head = lax.broadcasted_iota(jnp.int32, (ATT_GROUP * GRID_W, gw), 1) // ATT_HEAD_DIM
    own_head = row_head == lane_head
    out_lane_head = lax.broadcasted_iota(jnp.int32, (GRID_W, gw), 1) // ATT_HEAD_DIM

    @pl.when((pl.program_id(0) == 0) & (step == 0))
    def _build_bias():
        in_window = win_ref[...] > 0.0

        def one_block(n, carry):
            gen = jnp.broadcast_to(gen_ref[n], (GRID_W, LANES))
            toeplitz = pltpu.roll(gen, 0, 1, stride=1, stride_axis=0)
            head_rows = pl.ds(pl.multiple_of((n % ATT_GROUP) * GRID_W, GRID_W), GRID_W)
            bias_scr[n // ATT_GROUP, head_rows, :] = jnp.where(in_window, toeplitz, MASK_VALUE)
            return carry

        lax.fori_loop(0, gen_ref.shape[0], one_block, 0, unroll=8)

    def one_row(j, carry):
        r = step * ATT_ROWS_PER_STEP + j
        row_start = jnp.clip(r - kr // 2, 0, rows - kr)
        variant = row_start - r + (WIN_ROWS - 1)
        q_tok = pl.multiple_of(j * GRID_W, GRID_W)
        k_tok = pl.multiple_of(row_start * GRID_W, GRID_W)
        for grp in range(ATT_GROUPS):
            lanes = slice(grp * gw, (grp + 1) * gw)
            q = q_ref[0, pl.ds(q_tok, GRID_W), lanes]
            q_bd = jnp.where(own_head, jnp.concatenate([q] * ATT_GROUP, axis=0), 0)
            keys = k_ref[0, pl.ds(k_tok, kr * GRID_W), lanes]
            vals = v_ref[0, pl.ds(k_tok, kr * GRID_W), lanes]
            tile0 = ((variant % 2) * ATT_GROUPS + grp) * ATT_BIAS_TILES + variant // 2
            bias = jnp.concatenate([bias_scr[tile0 + t] for t in range(kr // 2)], axis=1)
            s = _dot_nt(q_bd, keys) + bias
            m = jnp.max(s, axis=-1, keepdims=True)
            p = jnp.exp2(s - m)
            denom = jnp.sum(p, axis=-1, keepdims=True)
            pv = _dot(p.astype(BF16), vals) / denom
            o = jnp.zeros((GRID_W, gw), F32)
            for h in range(ATT_GROUP):
                o = o + jnp.where(out_lane_head == h, pv[h * GRID_W:(h + 1) * GRID_W], 0.0)
            gate = g_ref[0, pl.ds(q_tok, GRID_W), lanes].astype(F32)
            o = o * _silu_from_half(gate)
            o_ref[0, pl.ds(q_tok, GRID_W), lanes] = o.astype(o_ref.dtype)
        return carry

    lax.fori_loop(0, ATT_ROWS_PER_STEP, one_row, 0, unroll=ATT_UNROLL)


def _attention(q, k, v, g, rpb):
    batch, seq, width = q.shape
    rows = seq // GRID_W
    kr = min(WIN_ROWS, rows)
    assert rows % ATT_ROWS_PER_STEP == 0 and ATT_ROWS_PER_STEP % ATT_UNROLL == 0
    assert kr == WIN_ROWS and 2 * GRID_W == LANES
    gen = _attention_bias_rows(rpb)
    win = jnp.asarray(_attention_window_mask())
    blk = ATT_ROWS_PER_STEP * GRID_W
    tile = pl.BlockSpec((1, blk, width), lambda b, i: (b, i, 0))
    whole = pl.BlockSpec((1, seq, width), lambda b, i: (b, 0, 0))
    return pl.pallas_call(
        functools.partial(_attention_kernel, rows=rows, kr=kr),
        grid=(batch, rows // ATT_ROWS_PER_STEP),
        in_specs=[tile, whole, whole, tile,
                  pl.BlockSpec(gen.shape, lambda b, i: (0, 0, 0)),
                  pl.BlockSpec(win.shape, lambda b, i: (0, 0))],
        out_specs=tile,
        out_shape=jax.ShapeDtypeStruct((batch, seq, width), BF16),
        scratch_shapes=[pltpu.VMEM((2 * ATT_GROUPS * ATT_BIAS_TILES, ATT_GROUP * GRID_W, 2 * GRID_W), F32)],
        compiler_params=pltpu.CompilerParams(
            dimension_semantics=("arbitrary", "arbitrary"), vmem_limit_bytes=VMEM_LIMIT_BYTES),
        name="nbr_attention",
    )(q, k, v, g, gen, win)


def _hgrn_level_sizes(chunk):
    sizes = []
    c = chunk
    while c >= 2:
        sizes.append(c)
        c //= 2
    return sizes


def _hgrn_upper_rows(chunk):
    t = np.arange(chunk)
    fwd = np.stack([(t % c) >= c // 2 for c in _hgrn_level_sizes(chunk)])
    return np.stack([fwd, fwd[:, ::-1]])


def _hgrn_level_kind(c):
    if (c // 2) % F32_SUBLANES == 0:
        return "prefix"
    return "gate" if c == 2 else "matmul"


def _hgrn_constants(chunk):
    t = np.arange(chunk)[:, None]
    u = np.arange(chunk)[None, :]
    blocks = [(u <= t), (u > t)]
    masks = []
    for c in _hgrn_level_sizes(chunk):
        half = c // 2
        mid = (t // c) * c + half
        upper = (t % c) >= half
        if _hgrn_level_kind(c) == "matmul":
            blocks.append(np.where(upper, (u >= mid) & (u <= t), (u > t) & (u <= mid - 1)))
        masks.append((t // c == u // c) & upper & ((u % c) < half))
    fwd_sums = np.stack(blocks).astype(np.float32)
    fwd_masks = np.stack(masks).astype(np.float32)
    bwd_sums = fwd_sums[:, ::-1, ::-1]
    bwd_masks = fwd_masks[:, ::-1, ::-1]
    flat = lambda m: np.concatenate([m.reshape(-1, chunk)] * 2, axis=1)
    return np.stack([flat(fwd_sums), flat(bwd_sums)]), np.concatenate([fwd_masks, bwd_masks], axis=2)


def _block_diag(a, b):
    zero = jnp.zeros_like(a)
    return jnp.concatenate([jnp.concatenate([a, zero], axis=1),
                            jnp.concatenate([zero, b], axis=1)], axis=0)


def _hgrn_kernel(q_ref, zf_ref, zb_ref, i_ref, g_ref, lbf_ref, lbb_ref, gain_ref,
                 sums_ref, masks_ref, o_ref, of_scr, ob_scr, sf_scr, sb_scr,
                 k_scr0, k_scr1, dec_scr0, dec_scr1, *, layer, seq):
    chunk = HG_CHUNK
    pair = 2 * chunk
    n_pairs = seq // pair
    levels = _hgrn_level_sizes(chunk)
    kinds = [_hgrn_level_kind(c) for c in levels]
    n_coarse = kinds.count("prefix")
    upper = _hgrn_upper_rows(chunk)
    dh = HG_HEAD_DIM
    k_slots = (k_scr0, k_scr1)
    dec_slots = (dec_scr0, dec_scr1)
    row_in_chunk = lax.broadcasted_iota(jnp.int32, (chunk, dh), 0)

    def lower_bound(logit_ref):
        logits = logit_ref[...].astype(F32)
        e = jnp.exp(logits - jnp.max(logits, axis=0, keepdims=True))
        return jnp.sum(e[:layer + 1], axis=0, keepdims=True) / jnp.sum(e, axis=0, keepdims=True)

    def gates(half_z, lb):
        span = 0.5 * (1.0 - lb)
        moved = span * jnp.tanh(half_z.astype(F32))
        return jnp.log2((1.0 - span) + moved), span - moved

    def store_decays(slot, direction, g2):
        hi = g2.astype(BF16)
        lo = (g2 - hi.astype(F32)).astype(BF16)
        stacked = [jnp.concatenate([hi[c * chunk:(c + 1) * chunk], lo[c * chunk:(c + 1) * chunk]], axis=0)
                   for c in range(2)]
        raw = _dot(sums_ref[direction], jnp.concatenate(stacked, axis=1))
        dec = dec_slots[slot].at[direction]
        dec[0:2 * chunk, :] = jnp.exp2(raw[0:2 * chunk])
        dec[(2 + n_coarse) * chunk:, :] = jnp.exp2(raw[2 * chunk:])
        cum = raw[0:chunk]
        for lvl in range(n_coarse):
            c = levels[lvl]
            half = c // 2
            pieces = []
            for r0 in range(0, chunk, c):
                first, second = cum[r0:r0 + half], cum[r0 + half:r0 + c]
                if direction == 0:
                    ref = cum[r0 + half - 1:r0 + half]
                    pieces += [ref - first, second - ref]
                else:
                    ref = cum[r0 + half:r0 + half + 1]
                    pieces += [first - ref, ref - second]
            dec[(2 + lvl) * chunk:(3 + lvl) * chunk, :] = jnp.exp2(jnp.concatenate(pieces, axis=0))

    def level_operand(direction, lvl, q, k, d):
        c = levels[lvl]
        half = c // 2
        if kinds[lvl] == "prefix":
            qk = jnp.concatenate([(q if upper[direction, lvl, r] else k)[r:r + half]
                                  for r in range(0, chunk, half)], axis=0)
            return (qk * d).astype(BF16)
        pos = row_in_chunk % c
        is_upper = (pos >= half) if direction == 0 else (pos < half)
        if kinds[lvl] == "gate":
            return jnp.where(is_upper, q * (1.0 - k), k).astype(BF16)
        return (jnp.where(is_upper, q, k) * d).astype(BF16)

    def chunk_part(slot, direction, c, q, v, base):
        rows = slice(c * chunk, (c + 1) * chunk)
        lanes = slice(c * dh, (c + 1) * dh)
        return dict(q=q[rows], k=k_slots[slot][direction, rows, :], v=v[rows],
                    dec=lambda r0, r1: dec_slots[slot][direction, r0:r1, lanes],
                    rows=pl.ds(pl.multiple_of(base + c * chunk, chunk), chunk))

    def pair_step(f, b):
        a = jnp.zeros((chunk, 2 * chunk), F32)
        for lvl in range(len(levels)):
            d_rows = ((2 + lvl) * chunk, (3 + lvl) * chunk)
            decay = (lambda part: None) if kinds[lvl] == "gate" else (lambda part: part["dec"](*d_rows))
            xf = level_operand(0, lvl, f["q"], f["k"], decay(f))
            xb = level_operand(1, lvl, b["q"], b["k"], decay(b))
            a = a + masks_ref[lvl] * _dot_nt(jnp.concatenate([xf, xb], axis=1), _block_diag(xf, xb))
        o = _dot(a.astype(BF16), _block_diag(f["v"], b["v"]))
        sf = sf_scr[...]
        sb = sb_scr[...]
        q_dec = jnp.concatenate([(f["q"] * f["dec"](0, chunk)).astype(BF16),
                                 (b["q"] * b["dec"](0, chunk)).astype(BF16)], axis=1)
        o = o + _dot_nt(q_dec, _block_diag(sf.astype(BF16), sb.astype(BF16)))
        diag_f = jnp.sum(f["q"] * f["k"], axis=-1, keepdims=True)
        diag_b = jnp.sum(b["q"] * b["k"], axis=-1, keepdims=True)
        of_scr[f["rows"], :] = o[:, :dh] + diag_f * f["v"].astype(F32)
        ob_scr[b["rows"], :] = o[:, dh:] + diag_b * b["v"].astype(F32)
        k_dec = _block_diag((f["k"] * f["dec"](chunk, 2 * chunk)).astype(BF16),
                            (b["k"] * b["dec"](chunk, 2 * chunk)).astype(BF16))
        v_t = jnp.concatenate([f["v"], b["v"]], axis=0).astype(F32).T.astype(BF16)
        u = _dot(v_t, k_dec)
        sf_scr[...] = sf * f["dec"](chunk - 1, chunk) + u[:, :dh]
        sb_scr[...] = sb * b["dec"](0, 1) + u[:, dh:]

    lb_f = lower_bound(lbf_ref)
    lb_b = lower_bound(lbb_ref)
    sf_scr[...] = jnp.zeros_like(sf_scr)
    sb_scr[...] = jnp.zeros_like(sb_scr)

    def bases(n):
        return (pl.multiple_of(n * pair, pair), pl.multiple_of((n_pairs - 1 - n) * pair, pair))

    def prepare(n, slot):
        for direction, base, z_ref, lb in zip((0, 1), bases(n), (zf_ref, zb_ref), (lb_f, lb_b)):
            g2, k = gates(z_ref[0, pl.ds(base, pair), :], lb)
            k_slots[slot][direction] = k
            store_decays(slot, direction, g2)

    def consume(n, slot):
        parts = []
        for direction, base in zip((0, 1), bases(n)):
            rows = pl.ds(base, pair)
            q = q_ref[0, rows, :].astype(F32)
            v = i_ref[0, rows, :]
            parts.append([chunk_part(slot, direction, c, q, v, base) for c in range(2)])
        pair_step(parts[0][0], parts[1][1])
        pair_step(parts[0][1], parts[1][0])

    def scan_body(m, carry):
        prepare(2 * m + 1, 1)
        consume(2 * m, 0)
        prepare(jnp.minimum(2 * m + 2, n_pairs - 1), 0)
        consume(2 * m + 1, 1)
        return carry

    prepare(0, 0)
    lax.fori_loop(0, n_pairs // 2, scan_body, 0, unroll=2)

    gain = gain_ref[...].astype(F32)
    norm_rows = 256

    def norm_body(n, carry):
        rows = pl.ds(pl.multiple_of(n * norm_rows, norm_rows), norm_rows)
        o = of_scr[rows, :] + ob_scr[rows, :]
        o = o * lax.rsqrt(jnp.mean(jnp.square(o), axis=-1, keepdims=True) + RMS_EPS) * gain
        gate = g_ref[0, rows, :].astype(F32)
        o_ref[0, rows, :] = (o * _silu_from_half(gate)).astype(o_ref.dtype)
        return carry

    lax.fori_loop(0, seq // norm_rows, norm_body, 0, unroll=4)


def _hgrn(q, zf, zb, i, g, lb_fwd_logits, lb_bwd_logits, gain_row, layer):
    batch, seq, width = q.shape
    assert width == HG_HEADS * HG_HEAD_DIM and seq % (2 * HG_CHUNK) == 0 and seq % 256 == 0
    sums, masks = _hgrn_constants(HG_CHUNK)
    sums = jnp.asarray(sums, BF16)
    masks = jnp.asarray(masks, F32)
    n_layers = lb_fwd_logits.shape[0]
    n_dec_rows = (2 + sum(_hgrn_level_kind(c) != "gate" for c in _hgrn_level_sizes(HG_CHUNK))) * HG_CHUNK
    head = pl.BlockSpec((1, seq, HG_HEAD_DIM), lambda b, h: (b, 0, h))
    per_head_row = lambda n: pl.BlockSpec((n, HG_HEAD_DIM), lambda b, h: (0, h))
    return pl.pallas_call(
        functools.partial(_hgrn_kernel, layer=layer, seq=seq),
        grid=(batch, HG_HEADS),
        in_specs=[head, head, head, head, head,
                  per_head_row(n_layers), per_head_row(n_layers), per_head_row(1),
                  pl.BlockSpec(sums.shape, lambda b, h: (0, 0, 0)),
                  pl.BlockSpec(masks.shape, lambda b, h: (0, 0, 0))],
        out_specs=head,
        out_shape=jax.ShapeDtypeStruct((batch, seq, width), BF16),
        scratch_shapes=[pltpu.VMEM((seq, HG_HEAD_DIM), F32), pltpu.VMEM((seq, HG_HEAD_DIM), F32),
                        pltpu.VMEM((HG_HEAD_DIM, HG_HEAD_DIM), F32),
                        pltpu.VMEM((HG_HEAD_DIM, HG_HEAD_DIM), F32),
                        pltpu.VMEM((2, 2 * HG_CHUNK, HG_HEAD_DIM), F32),
                        pltpu.VMEM((2, 2 * HG_CHUNK, HG_HEAD_DIM), F32),
                        pltpu.VMEM((2, n_dec_rows, 2 * HG_HEAD_DIM), F32),
                        pltpu.VMEM((2, n_dec_rows, 2 * HG_HEAD_DIM), F32)],
        compiler_params=pltpu.CompilerParams(
            dimension_semantics=("arbitrary", "arbitrary"), vmem_limit_bytes=VMEM_LIMIT_BYTES),
        name="hgrn2_scan",
    )(q, zf, zb, i, g, lb_fwd_logits, lb_bwd_logits, gain_row, sums, masks)


def _out_proj_kernel(oa_ref, oh_ref, x_ref, w_ref, b_ref, gain_ref, bias_ref, o_ref, w_scr, *, alpha):
    d_att = oa_ref.shape[1]
    _cast_weight_once(w_ref, w_scr)
    for r0 in range(0, o_ref.shape[0], OUT_PROJ_ROWS):
        rows = slice(r0, r0 + OUT_PROJ_ROWS)
        y = _dot(oa_ref[rows, :], w_scr[:d_att, :]) + _dot(oh_ref[rows, :], w_scr[d_att:, :]) + b_ref[...]
        r = alpha * x_ref[rows, :] + y
        mu = jnp.mean(r, axis=-1, keepdims=True)
        c = r - mu
        var = jnp.mean(jnp.square(c), axis=-1, keepdims=True)
        o_ref[rows, :] = (c * lax.rsqrt(var + LN_EPS) * gain_ref[...] + bias_ref[...]).astype(o_ref.dtype)


def _out_proj(o_a, o_h, x2d, w, b_row, gain_row, bias_row, alpha):
    n_tok, d_model = x2d.shape
    d_att, d_hg = o_a.shape[1], o_h.shape[1]
    row = pl.BlockSpec((1, d_model), lambda i: (0, 0))
    return pl.pallas_call(
        functools.partial(_out_proj_kernel, alpha=alpha),
        grid=(n_tok // OUT_PROJ_TM,),
        in_specs=[pl.BlockSpec((OUT_PROJ_TM, d_att), lambda i: (i, 0)),
                  pl.BlockSpec((OUT_PROJ_TM, d_hg), lambda i: (i, 0)),
                  pl.BlockSpec((OUT_PROJ_TM, d_model), lambda i: (i, 0)),
                  pl.BlockSpec((d_att + d_hg, d_model), lambda i: (0, 0), pipeline_mode=pl.Buffered(1)),
                  row, row, row],
        out_specs=pl.BlockSpec((OUT_PROJ_TM, d_model), lambda i: (i, 0)),
        out_shape=jax.ShapeDtypeStruct((n_tok, d_model), x2d.dtype),
        scratch_shapes=[pltpu.VMEM((d_att + d_hg, d_model), BF16)],
        compiler_params=pltpu.CompilerParams(
            dimension_semantics=("arbitrary",), vmem_limit_bytes=VMEM_LIMIT_BYTES),
        name="out_proj_layernorm",
    )(o_a, o_h, x2d, w, b_row, gain_row, bias_row)


def _layer(x, layer, depth, w_in, b_in, rpb, lb_fwd_logits, lb_bwd_logits, hg_norm_gain,
           w_out, b_out, ln_gain, ln_bias):
    batch, seq, d_model = x.shape
    x2d = x.reshape(batch * seq, d_model)
    slabs = _in_proj(x2d, w_in, b_in.reshape(1, -1))
    q_a, k_a, v_a, g_a, q_h, z_f, z_b, i_h, g_h = [s.reshape(batch, seq, SLAB) for s in slabs]
    o_a = _attention(q_a, k_a, v_a, g_a, rpb)
    o_h = _hgrn(q_h, z_f, z_b, i_h, g_h, lb_fwd_logits, lb_bwd_logits,
                hg_norm_gain.reshape(1, -1), layer)
    alpha = (2.0 * depth) ** 0.25
    out = _out_proj(o_a.reshape(batch * seq, SLAB), o_h.reshape(batch * seq, SLAB), x2d,
                    w_out, b_out.reshape(1, -1), ln_gain.reshape(1, -1),
                    ln_bias.reshape(1, -1), alpha)
    return out.reshape(batch, seq, d_model)


def kernel(x, w_in, b_in, rpb, lb_fwd_logits, lb_bwd_logits, hg_norm_gain, w_out, b_out, ln_gain, ln_bias):
    depth = w_in.shape[0]
    for layer in range(depth):
        x = _layer(x, layer, depth, w_in[layer], b_in[layer], rpb[layer], lb_fwd_logits,
                   lb_bwd_logits, hg_norm_gain[layer], w_out[layer], b_out[layer],
                   ln_gain[layer], ln_bias[layer])
    return x
```

```python
import functools

import numpy as np
import jax
import jax.numpy as jnp
from jax import lax
from jax.experimental import pallas as pl
from jax.experimental.pallas import tpu as pltpu

GRID_W = 64
ATT_HEADS = 8
ATT_HEAD_DIM = 64
HG_HEADS = 4
HG_HEAD_DIM = 128
WIN_ROWS = 8
WIN_COLS = 16
LN_EPS = 1e-5
RMS_EPS = 1e-6
N_SLABS = 9
SLAB = 512

LANES = 128
F32_SUBLANES = 8
VMEM_LIMIT_BYTES = 56 * 1024 * 1024
PROJ_TM = 512
OUT_PROJ_TM = 1024
OUT_PROJ_ROWS = 256
ATT_ROWS_PER_STEP = 16
ATT_UNROLL = 4
ATT_GROUP = 4
HG_CHUNK = 64
MASK_VALUE = -1e30
LOG2_E = 1.4426950408889634
SLAB_SCALES = {1: ATT_HEAD_DIM ** -0.5 * LOG2_E, 3: 0.5, 5: 0.5, 6: 0.5, 8: 0.5}


def _silu_from_half(half_g):
    return half_g * (1.0 + jnp.tanh(half_g))

BF16 = jnp.bfloat16
F32 = jnp.float32


def _dot(a, b):
    return jnp.dot(a, b, preferred_element_type=F32)


def _dot_nt(a, b):
    return lax.dot_general(a, b, (((1,), (1,)), ((), ())), preferred_element_type=F32)


def _cast_weight_once(w_ref, w_scr):
    @pl.when(pl.program_id(0) == 0)
    def _():
        for c0 in range(0, w_ref.shape[1], SLAB):
            w_scr[:, c0:c0 + SLAB] = w_ref[:, c0:c0 + SLAB].astype(w_scr.dtype)


def _in_proj_kernel(x_ref, w_ref, b_ref, *refs):
    out_refs, w_scr = refs[:-1], refs[-1]
    _cast_weight_once(w_ref, w_scr)
    xb = x_ref[...].astype(BF16)
    for j, o_ref in enumerate(out_refs):
        cols = slice(j * SLAB, (j + 1) * SLAB)
        h = _dot(xb, w_scr[:, cols]) + b_ref[:, cols]
        if j in SLAB_SCALES:
            h = h * SLAB_SCALES[j]
        o_ref[...] = h.astype(o_ref.dtype)


def _in_proj(x2d, w, b_row):
    n_tok, d_model = x2d.shape
    d_in = w.shape[1]
    assert d_in == N_SLABS * SLAB and n_tok % PROJ_TM == 0
    return pl.pallas_call(
        _in_proj_kernel,
        grid=(n_tok // PROJ_TM,),
        in_specs=[
            pl.BlockSpec((PROJ_TM, d_model), lambda i: (i, 0)),
            pl.BlockSpec((d_model, d_in), lambda i: (0, 0), pipeline_mode=pl.Buffered(1)),
            pl.BlockSpec((1, d_in), lambda i: (0, 0)),
        ],
        out_specs=[pl.BlockSpec((PROJ_TM, SLAB), lambda i: (i, 0))] * N_SLABS,
        out_shape=[jax.ShapeDtypeStruct((n_tok, SLAB), BF16)] * N_SLABS,
        scratch_shapes=[pltpu.VMEM((d_model, d_in), BF16)],
        compiler_params=pltpu.CompilerParams(
            dimension_semantics=("arbitrary",), vmem_limit_bytes=VMEM_LIMIT_BYTES),
        name="in_proj",
    )(x2d, w, b_row)


ATT_REL_ROWS = 2 * WIN_ROWS - 1
ATT_BIAS_TILES = (ATT_REL_ROWS - 1) // 2
ATT_GROUPS = ATT_HEADS // ATT_GROUP


def _attention_bias_rows(rpb):
    n_rel_cols = 2 * WIN_COLS - 1
    m = np.arange(LANES)
    first = np.where(m < WIN_COLS, m + WIN_COLS - 1, m - (LANES - WIN_COLS + 1))
    first_ok = (m < WIN_COLS) | (m > LANES - WIN_COLS)
    second = m - (GRID_W - WIN_COLS + 1)
    second_ok = (second >= 0) & (second < n_rel_cols)
    col = np.where(first_ok, first, np.where(second_ok, second, 0))
    p = np.arange(2)[:, None, None]
    t = np.arange(ATT_BIAS_TILES)[None, :, None]
    rel = 2 * t + p + np.where(first_ok, 0, 1)[None, None, :]
    gen = jnp.where(first_ok | second_ok, rpb.astype(F32)[:, rel, col[None, None, :]], 0.0) * LOG2_E
    gen = gen.reshape(ATT_GROUPS, ATT_GROUP, 2, ATT_BIAS_TILES, LANES).transpose(2, 0, 3, 1, 4)
    return gen.reshape(2 * ATT_GROUPS * ATT_BIAS_TILES * ATT_GROUP, 1, LANES)


def _attention_window_mask():
    qc = np.arange(GRID_W)[:, None]
    kc = np.arange(LANES)[None, :] % GRID_W
    col_start = np.clip(qc - WIN_COLS // 2, 0, GRID_W - WIN_COLS)
    return ((kc >= col_start) & (kc < col_start + WIN_COLS)).astype(np.float32)


def _attention_kernel(q_ref, k_ref, v_ref, g_ref, gen_ref, win_ref, o_ref, bias_scr, *, rows, kr):
    step = pl.program_id(1)
    gw = ATT_GROUP * ATT_HEAD_DIM
    row_head = lax.broadcasted_iota(jnp.int32, (ATT_GROUP * GRID_W, gw), 0) // GRID_W
    lane_head = lax.broadcasted_iota(jnp.int32, (ATT_GROUP * GRID_W, gw), 1) // ATT_HEAD_DIM
    own_head = row_head == lane_head
    out_lane_head = lax.broadcasted_iota(jnp.int32, (GRID_W, gw), 1) // ATT_HEAD_DIM

    @pl.when((pl.program_id(0) == 0) & (step == 0))
    def _build_bias():
        in_window = win_ref[...] > 0.0

        def one_block(n, carry):
            gen = jnp.broadcast_to(gen_ref[n], (GRID_W, LANES))
            toeplitz = pltpu.roll(gen, 0, 1, stride=1, stride_axis=0)
            head_rows = pl.ds(pl.multiple_of((n % ATT_GROUP) * GRID_W, GRID_W), GRID_W)
            bias_scr[n // ATT_GROUP, head_rows, :] = jnp.where(in_window, toeplitz, MASK_VALUE)
            return carry

        lax.fori_loop(0, gen_ref.shape[0], one_block, 0, unroll=8)

    def one_row(j, carry):
        r = step * ATT_ROWS_PER_STEP + j
        row_start = jnp.clip(r - kr // 2, 0, rows - kr)
        variant = row_start - r + (WIN_ROWS - 1)
        q_tok = pl.multiple_of(j * GRID_W, GRID_W)
        k_tok = pl.multiple_of(row_start * GRID_W, GRID_W)
        for grp in range(ATT_GROUPS):
            lanes = slice(grp * gw, (grp + 1) * gw)
            q = q_ref[0, pl.ds(q_tok, GRID_W), lanes]
            q_bd = jnp.where(own_head, jnp.concatenate([q] * ATT_GROUP, axis=0), 0)
            keys = k_ref[0, pl.ds(k_tok, kr * GRID_W), lanes]
            vals = v_ref[0, pl.ds(k_tok, kr * GRID_W), lanes]
            tile0 = ((variant % 2) * ATT_GROUPS + grp) * ATT_BIAS_TILES + variant // 2
            bias = jnp.concatenate([bias_scr[tile0 + t] for t in range(kr // 2)], axis=1)
            s = _dot_nt(q_bd, keys) + bias
            m = jnp.max(s, axis=-1, keepdims=True)
            p = jnp.exp2(s - m)
            denom = jnp.sum(p, axis=-1, keepdims=True)
            pv = _dot(p.astype(BF16), vals) / denom
            o = jnp.zeros((GRID_W, gw), F32)
            for h in range(ATT_GROUP):
                o = o + jnp.where(out_lane_head == h, pv[h * GRID_W:(h + 1) * GRID_W], 0.0)
            gate = g_ref[0, pl.ds(q_tok, GRID_W), lanes].astype(F32)
            o = o * _silu_from_half(gate)
            o_ref[0, pl.ds(q_tok, GRID_W), lanes] = o.astype(o_ref.dtype)
        return carry

    lax.fori_loop(0, ATT_ROWS_PER_STEP, one_row, 0, unroll=ATT_UNROLL)


def _attention(q, k, v, g, rpb):
    batch, seq, width = q.shape
    rows = seq // GRID_W
    kr = min(WIN_ROWS, rows)
    assert rows % ATT_ROWS_PER_STEP == 0 and ATT_ROWS_PER_STEP % ATT_UNROLL == 0
    assert kr == WIN_ROWS and 2 * GRID_W == LANES
    gen = _attention_bias_rows(rpb)
    win = jnp.asarray(_attention_window_mask())
    blk = ATT_ROWS_PER_STEP * GRID_W
    tile = pl.BlockSpec((1, blk, width), lambda b, i: (b, i, 0))
    whole = pl.BlockSpec((1, seq, width), lambda b, i: (b, 0, 0))
    return pl.pallas_call(
        functools.partial(_attention_kernel, rows=rows, kr=kr),
        grid=(batch, rows // ATT_ROWS_PER_STEP),
        in_specs=[tile, whole, whole, tile,
                  pl.BlockSpec(gen.shape, lambda b, i: (0, 0, 0)),
                  pl.BlockSpec(win.shape, lambda b, i: (0, 0))],
        out_specs=tile,
        out_shape=jax.ShapeDtypeStruct((batch, seq, width), BF16),
        scratch_shapes=[pltpu.VMEM((2 * ATT_GROUPS * ATT_BIAS_TILES, ATT_GROUP * GRID_W, 2 * GRID_W), F32)],
        compiler_params=pltpu.CompilerParams(
            dimension_semantics=("arbitrary", "arbitrary"), vmem_limit_bytes=VMEM_LIMIT_BYTES),
        name="nbr_attention",
    )(q, k, v, g, gen, win)


def _hgrn_level_sizes(chunk):
    sizes = []
    c = chunk
    while c >= 2:
        sizes.append(c)
        c //= 2
    return sizes


def _hgrn_upper_rows(chunk):
    t = np.arange(chunk)
    fwd = np.stack([(t % c) >= c // 2 for c in _hgrn_level_sizes(chunk)])
    return np.stack([fwd, fwd[:, ::-1]])


def _hgrn_level_kind(c):
    if (c // 2) % F32_SUBLANES == 0:
        return "prefix"
    return "gate" if c == 2 else "matmul"


def _hgrn_constants(chunk):
    t = np.arange(chunk)[:, None]
    u = np.arange(chunk)[None, :]
    blocks = [(u <= t), (u > t)]
    masks = []
    for c in _hgrn_level_sizes(chunk):
        half = c // 2
        mid = (t // c) * c + half
        upper = (t % c) >= half
        if _hgrn_level_kind(c) == "matmul":
            blocks.append(np.where(upper, (u >= mid) & (u <= t), (u > t) & (u <= mid - 1)))
        masks.append((t // c == u // c) & upper & ((u % c) < half))
    fwd_sums = np.stack(blocks).astype(np.float32)
    fwd_masks = np.stack(masks).astype(np.float32)
    bwd_sums = fwd_sums[:, ::-1, ::-1]
    bwd_masks = fwd_masks[:, ::-1, ::-1]
    flat = lambda m: np.concatenate([m.reshape(-1, chunk)] * 2, axis=1)
    return np.stack([flat(fwd_sums), flat(bwd_sums)]), np.concatenate([fwd_masks, bwd_masks], axis=2)


def _block_diag(a, b):
    zero = jnp.zeros_like(a)
    return jnp.concatenate([jnp.concatenate([a, zero], axis=1),
                            jnp.concatenate([zero, b], axis=1)], axis=0)


def _hgrn_kernel(q_ref, zf_ref, zb_ref, i_ref, g_ref, lbf_ref, lbb_ref, gain_ref,
                 sums_ref, masks_ref, o_ref, of_scr, ob_scr, sf_scr, sb_scr,
                 k_scr0, k_scr1, dec_scr0, dec_scr1, *, layer, seq):
    chunk = HG_CHUNK
    pair = 2 * chunk
    n_pairs = seq // pair
    levels = _hgrn_level_sizes(chunk)
    kinds = [_hgrn_level_kind(c) for c in levels]
    n_coarse = kinds.count("prefix")
    upper = _hgrn_upper_rows(chunk)
    dh = HG_HEAD_DIM
    k_slots = (k_scr0, k_scr1)
    dec_slots = (dec_scr0, dec_scr1)
    row_in_chunk = lax.broadcasted_iota(jnp.int32, (chunk, dh), 0)

    def lower_bound(logit_ref):
        logits = logit_ref[...].astype(F32)
        e = jnp.exp(logits - jnp.max(logits, axis=0, keepdims=True))
        return jnp.sum(e[:layer + 1], axis=0, keepdims=True) / jnp.sum(e, axis=0, keepdims=True)

    def gates(half_z, lb):
        span = 0.5 * (1.0 - lb)
        moved = span * jnp.tanh(half_z.astype(F32))
        return jnp.log2((1.0 - span) + moved), span - moved

    def store_decays(slot, direction, g2):
        hi = g2.astype(BF16)
        lo = (g2 - hi.astype(F32)).astype(BF16)
        stacked = [jnp.concatenate([hi[c * chunk:(c + 1) * chunk], lo[c * chunk:(c + 1) * chunk]], axis=0)
                   for c in range(2)]
        raw = _dot(sums_ref[direction], jnp.concatenate(stacked, axis=1))
        dec = dec_slots[slot].at[direction]
        dec[0:2 * chunk, :] = jnp.exp2(raw[0:2 * chunk])
        dec[(2 + n_coarse) * chunk:, :] = jnp.exp2(raw[2 * chunk:])
        cum = raw[0:chunk]
        for lvl in range(n_coarse):
            c = levels[lvl]
            half = c // 2
            pieces = []
            for r0 in range(0, chunk, c):
                first, second = cum[r0:r0 + half], cum[r0 + half:r0 + c]
                if direction == 0:
                    ref = cum[r0 + half - 1:r0 + half]
                    pieces += [ref - first, second - ref]
                else:
                    ref = cum[r0 + half:r0 + half + 1]
                    pieces += [first - ref, ref - second]
            dec[(2 + lvl) * chunk:(3 + lvl) * chunk, :] = jnp.exp2(jnp.concatenate(pieces, axis=0))

    def level_operand(direction, lvl, q, k, d):
        c = levels[lvl]
        half = c // 2
        if kinds[lvl] == "prefix":
            qk = jnp.concatenate([(q if upper[direction, lvl, r] else k)[r:r + half]
                                  for r in range(0, chunk, half)], axis=0)
            return (qk * d).astype(BF16)
        pos = row_in_chunk % c
        is_upper = (pos >= half) if direction == 0 else (pos < half)
        if kinds[lvl] == "gate":
            return jnp.where(is_upper, q * (1.0 - k), k).astype(BF16)
        return (jnp.where(is_upper, q, k) * d).astype(BF16)

    def chunk_part(slot, direction, c, q, v, base):
        rows = slice(c * chunk, (c + 1) * chunk)
        lanes = slice(c * dh, (c + 1) * dh)
        return dict(q=q[rows], k=k_slots[slot][direction, rows, :], v=v[rows],
                    dec=lambda r0, r1: dec_slots[slot][direction, r0:r1, lanes],
                    rows=pl.ds(pl.multiple_of(base + c * chunk, chunk), chunk))

    def pair_step(f, b):
        a = jnp.zeros((chunk, 2 * chunk), F32)
        for lvl in range(len(levels)):
            d_rows = ((2 + lvl) * chunk, (3 + lvl) * chunk)
            decay = (lambda part: None) if kinds[lvl] == "gate" else (lambda part: part["dec"](*d_rows))
            xf = level_operand(0, lvl, f["q"], f["k"], decay(f))
            xb = level_operand(1, lvl, b["q"], b["k"], decay(b))
            a = a + masks_ref[lvl] * _dot_nt(jnp.concatenate([xf, xb], axis=1), _block_diag(xf, xb))
        o = _dot(a.astype(BF16), _block_diag(f["v"], b["v"]))
        sf = sf_scr[...]
        sb = sb_scr[...]
        q_dec = jnp.concatenate([(f["q"] * f["dec"](0, chunk)).astype(BF16),
                                 (b["q"] * b["dec"](0, chunk)).astype(BF16)], axis=1)
        o = o + _dot_nt(q_dec, _block_diag(sf.astype(BF16), sb.astype(BF16)))
        diag_f = jnp.sum(f["q"] * f["k"], axis=-1, keepdims=True)
        diag_b = jnp.sum(b["q"] * b["k"], axis=-1, keepdims=True)
        of_scr[f["rows"], :] = o[:, :dh] + diag_f * f["v"].astype(F32)
        ob_scr[b["rows"], :] = o[:, dh:] + diag_b * b["v"].astype(F32)
        k_dec = _block_diag((f["k"] * f["dec"](chunk, 2 * chunk)).astype(BF16),
                            (b["k"] * b["dec"](chunk, 2 * chunk)).astype(BF16))
        v_t = jnp.concatenate([f["v"], b["v"]], axis=0).astype(F32).T.astype(BF16)
        u = _dot(v_t, k_dec)
        sf_scr[...] = sf * f["dec"](chunk - 1, chunk) + u[:, :dh]
        sb_scr[...] = sb * b["dec"](0, 1) + u[:, dh:]

    lb_f = lower_bound(lbf_ref)
    lb_b = lower_bound(lbb_ref)
    sf_scr[...] = jnp.zeros_like(sf_scr)
    sb_scr[...] = jnp.zeros_like(sb_scr)

    def bases(n):
        return (pl.multiple_of(n * pair, pair), pl.multiple_of((n_pairs - 1 - n) * pair, pair))

    def prepare(n, slot):
        for direction, base, z_ref, lb in zip((0, 1), bases(n), (zf_ref, zb_ref), (lb_f, lb_b)):
            g2, k = gates(z_ref[0, pl.ds(base, pair), :], lb)
            k_slots[slot][direction] = k
            store_decays(slot, direction, g2)

    def consume(n, slot):
        parts = []
        for direction, base in zip((0, 1), bases(n)):
            rows = pl.ds(base, pair)
            q = q_ref[0, rows, :].astype(F32)
            v = i_ref[0, rows, :]
            parts.append([chunk_part(slot, direction, c, q, v, base) for c in range(2)])
        pair_step(parts[0][0], parts[1][1])
        pair_step(parts[0][1], parts[1][0])

    def scan_body(m, carry):
        prepare(2 * m + 1, 1)
        consume(2 * m, 0)
        prepare(jnp.minimum(2 * m + 2, n_pairs - 1), 0)
        consume(2 * m + 1, 1)
        return carry

    prepare(0, 0)
    lax.fori_loop(0, n_pairs // 2, scan_body, 0, unroll=4)

    gain = gain_ref[...].astype(F32)
    norm_rows = 256

    def norm_body(n, carry):
        rows = pl.ds(pl.multiple_of(n * norm_rows, norm_rows), norm_rows)
        o = of_scr[rows, :] + ob_scr[rows, :]
        o = o * lax.rsqrt(jnp.mean(jnp.square(o), axis=-1, keepdims=True) + RMS_EPS) * gain
        gate = g_ref[0, rows, :].astype(F32)
        o_ref[0, rows, :] = (o * _silu_from_half(gate)).astype(o_ref.dtype)
        return carry

    lax.fori_loop(0, seq // norm_rows, norm_body, 0, unroll=4)


def _hgrn(q, zf, zb, i, g, lb_fwd_logits, lb_bwd_logits, gain_row, layer):
    batch, seq, width = q.shape
    assert width == HG_HEADS * HG_HEAD_DIM and seq % (2 * HG_CHUNK) == 0 and seq % 256 == 0
    sums, masks = _hgrn_constants(HG_CHUNK)
    sums = jnp.asarray(sums, BF16)
    masks = jnp.asarray(masks, F32)
    n_layers = lb_fwd_logits.shape[0]
    n_dec_rows = (2 + sum(_hgrn_level_kind(c) != "gate" for c in _hgrn_level_sizes(HG_CHUNK))) * HG_CHUNK
    head = pl.BlockSpec((1, seq, HG_HEAD_DIM), lambda b, h: (b, 0, h))
    per_head_row = lambda n: pl.BlockSpec((n, HG_HEAD_DIM), lambda b, h: (0, h))
    return pl.pallas_call(
        functools.partial(_hgrn_kernel, layer=layer, seq=seq),
        grid=(batch, HG_HEADS),
        in_specs=[head, head, head, head, head,
                  per_head_row(n_layers), per_head_row(n_layers), per_head_row(1),
                  pl.BlockSpec(sums.shape, lambda b, h: (0, 0, 0)),
                  pl.BlockSpec(masks.shape, lambda b, h: (0, 0, 0))],
        out_specs=head,
        out_shape=jax.ShapeDtypeStruct((batch, seq, width), BF16),
        scratch_shapes=[pltpu.VMEM((seq, HG_HEAD_DIM), F32), pltpu.VMEM((seq, HG_HEAD_DIM), F32),
                        pltpu.VMEM((HG_HEAD_DIM, HG_HEAD_DIM), F32),
                        pltpu.VMEM((HG_HEAD_DIM, HG_HEAD_DIM), F32),
                        pltpu.VMEM((2, 2 * HG_CHUNK, HG_HEAD_DIM), F32),
                        pltpu.VMEM((2, 2 * HG_CHUNK, HG_HEAD_DIM), F32),
                        pltpu.VMEM((2, n_dec_rows, 2 * HG_HEAD_DIM), F32),
                        pltpu.VMEM((2, n_dec_rows, 2 * HG_HEAD_DIM), F32)],
        compiler_params=pltpu.CompilerParams(
            dimension_semantics=("arbitrary", "arbitrary"), vmem_limit_bytes=VMEM_LIMIT_BYTES),
        name="hgrn2_scan",
    )(q, zf, zb, i, g, lb_fwd_logits, lb_bwd_logits, gain_row, sums, masks)


def _out_proj_kernel(oa_ref, oh_ref, x_ref, w_ref, b_ref, gain_ref, bias_ref, o_ref, w_scr, *, alpha):
    d_att = oa_ref.shape[1]
    _cast_weight_once(w_ref, w_scr)
    for r0 in range(0, o_ref.shape[0], OUT_PROJ_ROWS):
        rows = slice(r0, r0 + OUT_PROJ_ROWS)
        y = _dot(oa_ref[rows, :], w_scr[:d_att, :]) + _dot(oh_ref[rows, :], w_scr[d_att:, :]) + b_ref[...]
        r = alpha * x_ref[rows, :] + y
        mu = jnp.mean(r, axis=-1, keepdims=True)
        c = r - mu
        var = jnp.mean(jnp.square(c), axis=-1, keepdims=True)
        o_ref[rows, :] = (c * lax.rsqrt(var + LN_EPS) * gain_ref[...] + bias_ref[...]).astype(o_ref.dtype)


def _out_proj(o_a, o_h, x2d, w, b_row, gain_row, bias_row, alpha):
    n_tok, d_model = x2d.shape
    d_att, d_hg = o_a.shape[1], o_h.shape[1]
    row = pl.BlockSpec((1, d_model), lambda i: (0, 0))
    return pl.pallas_call(
        functools.partial(_out_proj_kernel, alpha=alpha),
        grid=(n_tok // OUT_PROJ_TM,),
        in_specs=[pl.BlockSpec((OUT_PROJ_TM, d_att), lambda i: (i, 0)),
                  pl.BlockSpec((OUT_PROJ_TM, d_hg), lambda i: (i, 0)),
                  pl.BlockSpec((OUT_PROJ_TM, d_model), lambda i: (i, 0)),
                  pl.BlockSpec((d_att + d_hg, d_model), lambda i: (0, 0), pipeline_mode=pl.Buffered(1)),
                  row, row, row],
        out_specs=pl.BlockSpec((OUT_PROJ_TM, d_model), lambda i: (i, 0)),
        out_shape=jax.ShapeDtypeStruct((n_tok, d_model), x2d.dtype),
        scratch_shapes=[pltpu.VMEM((d_att + d_hg, d_model), BF16)],
        compiler_params=pltpu.CompilerParams(
            dimension_semantics=("arbitrary",), vmem_limit_bytes=VMEM_LIMIT_BYTES),
        name="out_proj_layernorm",
    )(o_a, o_h, x2d, w, b_row, gain_row, bias_row)


def _layer(x, layer, depth, w_in, b_in, rpb, lb_fwd_logits, lb_bwd_logits, hg_norm_gain,
           w_out, b_out, ln_gain, ln_bias):
    batch, seq, d_model = x.shape
    x2d = x.reshape(batch * seq, d_model)
    slabs = _in_proj(x2d, w_in, b_in.reshape(1, -1))
    q_a, k_a, v_a, g_a, q_h, z_f, z_b, i_h, g_h = [s.reshape(batch, seq, SLAB) for s in slabs]
    o_a = _attention(q_a, k_a, v_a, g_a, rpb)
    o_h = _hgrn(q_h, z_f, z_b, i_h, g_h, lb_fwd_logits, lb_bwd_logits,
                hg_norm_gain.reshape(1, -1), layer)
    alpha = (2.0 * depth) ** 0.25
    out = _out_proj(o_a.reshape(batch * seq, SLAB), o_h.reshape(batch * seq, SLAB), x2d,
                    w_out, b_out.reshape(1, -1), ln_gain.reshape(1, -1),
                    ln_bias.reshape(1, -1), alpha)
    return out.reshape(batch, seq, d_model)


def kernel(x, w_in, b_in, rpb, lb_fwd_logits, lb_bwd_logits, hg_norm_gain, w_out, b_out, ln_gain, ln_bias):
    depth = w_in.shape[0]
    for layer in range(depth):
        x = _layer(x, layer, depth, w_in[layer], b_in[layer], rpb[layer], lb_fwd_logits,
                   lb_bwd_logits, hg_norm_gain[layer], w_out[layer], b_out[layer],
                   ln_gain[layer], ln_bias[layer])
    return x
```

```python
import functools

import numpy as np
import jax
import jax.numpy as jnp
from jax import lax
from jax.experimental import pallas as pl
from jax.experimental.pallas import tpu as pltpu

GRID_W = 64
ATT_HEADS = 8
ATT_HEAD_DIM = 64
HG_HEADS = 4
HG_HEAD_DIM = 128
WIN_ROWS = 8
WIN_COLS = 16
LN_EPS = 1e-5
RMS_EPS = 1e-6
N_SLABS = 9
SLAB = 512

LANES = 128
F32_SUBLANES = 8
VMEM_LIMIT_BYTES = 56 * 1024 * 1024
PROJ_TM = 512
OUT_PROJ_TM = 1024
OUT_PROJ_ROWS = 256
ATT_ROWS_PER_STEP = 16
ATT_UNROLL = 4
ATT_GROUP = 4
HG_CHUNK = 64
HG_GROUP = F32_SUBLANES
HG_NORM_ROWS = 256
MASK_VALUE = -1e30
LOG2_E = 1.4426950408889634
SLAB_SCALES = {1: ATT_HEAD_DIM ** -0.5 * LOG2_E, 3: 0.5, 5: 0.5, 6: 0.5, 8: 0.5}

BF16 = jnp.bfloat16
F32 = jnp.float32


def _silu_from_half(half_g):
    return half_g * (1.0 + jnp.tanh(half_g))


def _dot(a, b):
    return jnp.dot(a, b, preferred_element_type=F32)


def _dot_nt(a, b):
    return lax.dot_general(a, b, (((1,), (1,)), ((), ())), preferred_element_type=F32)


def _cast_weight_once(w_ref, w_scr):
    @pl.when(pl.program_id(0) == 0)
    def _():
        for c0 in range(0, w_ref.shape[1], SLAB):
            w_scr[:, c0:c0 + SLAB] = w_ref[:, c0:c0 + SLAB].astype(w_scr.dtype)


def _in_proj_kernel(x_ref, w_ref, b_ref, *refs):
    out_refs, w_scr = refs[:-1], refs[-1]
    _cast_weight_once(w_ref, w_scr)
    xb = x_ref[...].astype(BF16)
    for j, o_ref in enumerate(out_refs):
        cols = slice(j * SLAB, (j + 1) * SLAB)
        h = _dot(xb, w_scr[:, cols]) + b_ref[:, cols]
        if j in SLAB_SCALES:
            h = h * SLAB_SCALES[j]
        o_ref[...] = h.astype(o_ref.dtype)


def _in_proj(x2d, w, b_row):
    n_tok, d_model = x2d.shape
    d_in = w.shape[1]
    assert d_in == N_SLABS * SLAB and n_tok % PROJ_TM == 0
    return pl.pallas_call(
        _in_proj_kernel,
        grid=(n_tok // PROJ_TM,),
        in_specs=[
            pl.BlockSpec((PROJ_TM, d_model), lambda i: (i, 0)),
            pl.BlockSpec((d_model, d_in), lambda i: (0, 0), pipeline_mode=pl.Buffered(1)),
            pl.BlockSpec((1, d_in), lambda i: (0, 0)),
        ],
        out_specs=[pl.BlockSpec((PROJ_TM, SLAB), lambda i: (i, 0))] * N_SLABS,
        out_shape=[jax.ShapeDtypeStruct((n_tok, SLAB), BF16)] * N_SLABS,
        scratch_shapes=[pltpu.VMEM((d_model, d_in), BF16)],
        compiler_params=pltpu.CompilerParams(
            dimension_semantics=("arbitrary",), vmem_limit_bytes=VMEM_LIMIT_BYTES),
        name="in_proj",
    )(x2d, w, b_row)


ATT_REL_ROWS = 2 * WIN_ROWS - 1
ATT_BIAS_TILES = (ATT_REL_ROWS - 1) // 2
ATT_GROUPS = ATT_HEADS // ATT_GROUP


def _attention_bias_rows(rpb):
    n_rel_cols = 2 * WIN_COLS - 1
    m = np.arange(LANES)
    first = np.where(m < WIN_COLS, m + WIN_COLS - 1, m - (LANES - WIN_COLS + 1))
    first_ok = (m < WIN_COLS) | (m > LANES - WIN_COLS)
    second = m - (GRID_W - WIN_COLS + 1)
    second_ok = (second >= 0) & (second < n_rel_cols)
    col = np.where(first_ok, first, np.where(second_ok, second, 0))
    p = np.arange(2)[:, None, None]
    t = np.arange(ATT_BIAS_TILES)[None, :, None]
    rel = 2 * t + p + np.where(first_ok, 0, 1)[None, None, :]
    gen = jnp.where(first_ok | second_ok, rpb.astype(F32)[:, rel, col[None, None, :]], 0.0) * LOG2_E
    gen = gen.reshape(ATT_GROUPS, ATT_GROUP, 2, ATT_BIAS_TILES, LANES).transpose(2, 0, 3, 1, 4)
    return gen.reshape(2 * ATT_GROUPS * ATT_BIAS_TILES * ATT_GROUP, 1, LANES)


def _attention_window_mask():
    qc = np.arange(GRID_W)[:, None]
    kc = np.arange(LANES)[None, :] % GRID_W
    col_start = np.clip(qc - WIN_COLS // 2, 0, GRID_W - WIN_COLS)
    return ((kc >= col_start) & (kc < col_start + WIN_COLS)).astype(np.float32)


def _attention_kernel(q_ref, k_ref, v_ref, g_ref, gen_ref, win_ref, o_ref, bias_scr, *, rows, kr):
    step = pl.program_id(1)
    gw = ATT_GROUP * ATT_HEAD_DIM
    row_head = lax.broadcasted_iota(jnp.int32, (ATT_GROUP * GRID_W, gw), 0) // GRID_W
    lane_head = lax.broadcasted_iota(jnp.int32, (ATT_GROUP * GRID_W, gw), 1) // ATT_HEAD_DIM
    own_head = row_head == lane_head
    out_lane_head = lax.broadcasted_iota(jnp.int32, (GRID_W, gw), 1) // ATT_HEAD_DIM

    @pl.when((pl.program_id(0) == 0) & (step == 0))
    def _build_bias():
        in_window = win_ref[...] > 0.0

        def one_block(n, carry):
            gen = jnp.broadcast_to(gen_ref[n], (GRID_W, LANES))
            toeplitz = pltpu.roll(gen, 0, 1, stride=1, stride_axis=0)
            head_rows = pl.ds(pl.multiple_of((n % ATT_GROUP) * GRID_W, GRID_W), GRID_W)
            bias_scr[n // ATT_GROUP, head_rows, :] = jnp.where(in_window, toeplitz, MASK_VALUE)
            return carry

        lax.fori_loop(0, gen_ref.shape[0], one_block, 0, unroll=8)

    def one_row(j, carry):
        r = step * ATT_ROWS_PER_STEP + j
        row_start = jnp.clip(r - kr // 2, 0, rows - kr)
        variant = row_start - r + (WIN_ROWS - 1)
        q_tok = pl.multiple_of(j * GRID_W, GRID_W)
        k_tok = pl.multiple_of(row_start * GRID_W, GRID_W)
        for grp in range(ATT_GROUPS):
            lanes = slice(grp * gw, (grp + 1) * gw)
            q = q_ref[0, pl.ds(q_tok, GRID_W), lanes]
            q_bd = jnp.where(own_head, jnp.concatenate([q] * ATT_GROUP, axis=0), 0)
            keys = k_ref[0, pl.ds(k_tok, kr * GRID_W), lanes]
            vals = v_ref[0, pl.ds(k_tok, kr * GRID_W), lanes]
            tile0 = ((variant % 2) * ATT_GROUPS + grp) * ATT_BIAS_TILES + variant // 2
            bias = jnp.concatenate([bias_scr[tile0 + t] for t in range(kr // 2)], axis=1)
            s = _dot_nt(q_bd, keys) + bias
            m = jnp.max(s, axis=-1, keepdims=True)
            p = jnp.exp2(s - m)
            denom = jnp.sum(p, axis=-1, keepdims=True)
            pv = _dot(p.astype(BF16), vals) / denom
            o = jnp.zeros((GRID_W, gw), F32)
            for h in range(ATT_GROUP):
                o = o + jnp.where(out_lane_head == h, pv[h * GRID_W:(h + 1) * GRID_W], 0.0)
            gate = g_ref[0, pl.ds(q_tok, GRID_W), lanes].astype(F32)
            o = o * _silu_from_half(gate)
            o_ref[0, pl.ds(q_tok, GRID_W), lanes] = o.astype(o_ref.dtype)
        return carry

    lax.fori_loop(0, ATT_ROWS_PER_STEP, one_row, 0, unroll=ATT_UNROLL)


def _attention(q, k, v, g, rpb):
    batch, seq, width = q.shape
    rows = seq // GRID_W
    kr = min(WIN_ROWS, rows)
    assert rows % ATT_ROWS_PER_STEP == 0 and ATT_ROWS_PER_STEP % ATT_UNROLL == 0
    assert kr == WIN_ROWS and 2 * GRID_W == LANES
    gen = _attention_bias_rows(rpb)
    win = jnp.asarray(_attention_window_mask())
    blk = ATT_ROWS_PER_STEP * GRID_W
    tile = pl.BlockSpec((1, blk, width), lambda b, i: (b, i, 0))
    whole = pl.BlockSpec((1, seq, width), lambda b, i: (b, 0, 0))
    return pl.pallas_call(
        functools.partial(_attention_kernel, rows=rows, kr=kr),
        grid=(batch, rows // ATT_ROWS_PER_STEP),
        in_specs=[tile, whole, whole, tile,
                  pl.BlockSpec(gen.shape, lambda b, i: (0, 0, 0)),
                  pl.BlockSpec(win.shape, lambda b, i: (0, 0))],
        out_specs=tile,
        out_shape=jax.ShapeDtypeStruct((batch, seq, width), BF16),
        scratch_shapes=[pltpu.VMEM((2 * ATT_GROUPS * ATT_BIAS_TILES, ATT_GROUP * GRID_W, 2 * GRID_W), F32)],
        compiler_params=pltpu.CompilerParams(
            dimension_semantics=("arbitrary", "arbitrary"), vmem_limit_bytes=VMEM_LIMIT_BYTES),
        name="nbr_attention",
    )(q, k, v, g, gen, win)


def _hgrn_fine_levels():
    sizes = []
    c = HG_GROUP
    while c >= 2:
        sizes.append(c)
        c //= 2
    return sizes


def _hgrn_group_pairs(direction, chunk):
    n = chunk // HG_GROUP
    if direction == 0:
        return [(t, g) for t in range(1, n) for g in range(t)]
    return [(t, g) for t in range(n - 2, -1, -1) for g in range(t + 1, n)]


def _hgrn_constants(chunk):
    t = np.arange(chunk)[:, None]
    u = np.arange(chunk)[None, :]
    blocks = [(u <= t), (u > t)]
    masks = []
    for c in _hgrn_fine_levels():
        half = c // 2
        mid = (t // c) * c + half
        upper = (t % c) >= half
        if c > 2:
            blocks.append(np.where(upper, (u >= mid) & (u <= t), (u > t) & (u <= mid - 1)))
        masks.append((t // c == u // c) & upper & ((u % c) < half))
    fwd_sums = np.stack(blocks).astype(np.float32)
    fwd_masks = np.stack(masks).astype(np.float32)
    twice = lambda m: np.concatenate([m.reshape(-1, chunk)] * 2, axis=1)
    sums = np.stack([twice(fwd_sums), twice(fwd_sums[:, ::-1, ::-1])])
    masks = np.concatenate([fwd_masks, fwd_masks[:, ::-1, ::-1]], axis=2)
    n_groups = chunk // HG_GROUP
    lane = np.arange(2 * chunk)
    group_lanes = np.stack([(lane[None, :] // chunk == d) & ((lane[None, :] % chunk) // HG_GROUP == np.arange(n_groups)[:, None])
                            for d in range(2)])
    group_lanes = np.broadcast_to(group_lanes[:, :, None, :], (2, n_groups, HG_GROUP, 2 * chunk)).astype(np.float32)
    return sums, masks, group_lanes


def _block_diag(a, b):
    zero = jnp.zeros_like(a)
    return jnp.concatenate([jnp.concatenate([a, zero], axis=1),
                            jnp.concatenate([zero, b], axis=1)], axis=0)


def _hgrn_dec_rows(chunk):
    n_var = len(_hgrn_group_pairs(0, chunk)) * HG_GROUP
    n_mm = sum(c > 2 for c in _hgrn_fine_levels())
    edge, var, fine = 2 * chunk, 3 * chunk, 3 * chunk + n_var
    return edge, var, fine, fine + n_mm * chunk


def _hgrn_kernel(q_ref, zf_ref, zb_ref, i_ref, g_ref, lbf_ref, lbb_ref, gain_ref,
                 sums_ref, masks_ref, group_lanes_ref, o_ref, of_scr, ob_scr, sf_scr, sb_scr,
                 k_scr0, k_scr1, dec_scr0, dec_scr1, *, layer, seq):
    chunk = HG_CHUNK
    pair = 2 * chunk
    n_pairs = seq // pair
    dh = HG_HEAD_DIM
    grp = HG_GROUP
    n_groups = chunk // grp
    fine = _hgrn_fine_levels()
    group_pairs = [_hgrn_group_pairs(direction, chunk) for direction in (0, 1)]
    r_edge, r_var, r_fine, _ = _hgrn_dec_rows(chunk)
    k_slots = (k_scr0, k_scr1)
    dec_slots = (dec_scr0, dec_scr1)
    row_in_chunk = lax.broadcasted_iota(jnp.int32, (chunk, dh), 0)

    def lower_bound(logit_ref):
        logits = logit_ref[...].astype(F32)
        e = jnp.exp(logits - jnp.max(logits, axis=0, keepdims=True))
        return jnp.sum(e[:layer + 1], axis=0, keepdims=True) / jnp.sum(e, axis=0, keepdims=True)

    def gates(half_z, lb):
        span = 0.5 * (1.0 - lb)
        moved = span * jnp.tanh(half_z.astype(F32))
        return jnp.log2((1.0 - span) + moved), span - moved

    def store_decays(slot, direction, g2):
        hi = g2.astype(BF16)
        lo = (g2 - hi.astype(F32)).astype(BF16)
        stacked = [jnp.concatenate([hi[c * chunk:(c + 1) * chunk], lo[c * chunk:(c + 1) * chunk]], axis=0)
                   for c in range(2)]
        raw = _dot(sums_ref[direction], jnp.concatenate(stacked, axis=1))
        dec = dec_slots[slot].at[direction]
        dec[0:2 * chunk, :] = jnp.exp2(raw[0:2 * chunk])
        dec[r_fine:, :] = jnp.exp2(raw[2 * chunk:])
        cum = raw[0:chunk]
        edge = grp - 1 if direction == 0 else 0
        edges = [cum[g * grp + edge:g * grp + edge + 1] for g in range(n_groups)]
        group = lambda g: cum[g * grp:(g + 1) * grp]
        dec[r_edge:r_var, :] = jnp.exp2(jnp.concatenate([edges[g] - group(g) for g in range(n_groups)], axis=0))
        dec[r_var:r_fine, :] = jnp.exp2(jnp.concatenate([group(t) - edges[g]
                                                         for t, g in group_pairs[direction]], axis=0))

    def fine_operand(direction, lvl, q, k, dec):
        c = fine[lvl]
        pos = row_in_chunk % c
        is_upper = (pos >= c // 2) if direction == 0 else (pos < c // 2)
        if c == 2:
            return jnp.where(is_upper, q * (1.0 - k), k).astype(BF16)
        return (jnp.where(is_upper, q, k) * dec(r_fine + lvl * chunk, r_fine + (lvl + 1) * chunk)).astype(BF16)

    def chunk_part(slot, direction, c, q, v, base):
        rows = slice(c * chunk, (c + 1) * chunk)
        lanes = slice(c * dh, (c + 1) * dh)
        return dict(q=q[rows], k=k_slots[slot][direction, rows, :], v=v[rows],
                    dec=lambda r0, r1: dec_slots[slot][direction, r0:r1, lanes],
                    rows=pl.ds(pl.multiple_of(base + c * chunk, chunk), chunk))

    def pair_step(f, b):
        lhs = jnp.concatenate(
            [(jnp.concatenate([p["q"][t * grp:(t + 1) * grp] for t, _ in group_pairs[d]], axis=0)
              * p["dec"](r_var, r_fine)).astype(BF16) for d, p in enumerate((f, b))], axis=1)
        keys = [(p["k"] * p["dec"](r_edge, r_var)).astype(BF16) for p in (f, b)]
        r = _dot_nt(lhs, _block_diag(*keys))
        rows_of = [jnp.zeros((grp, 2 * chunk), F32)] * n_groups
        for i, ((tf, gf), (tb, gb)) in enumerate(zip(*group_pairs)):
            block = r[i * grp:(i + 1) * grp]
            rows_of[tf] = rows_of[tf] + group_lanes_ref[0, gf] * block
            rows_of[tb] = rows_of[tb] + group_lanes_ref[1, gb] * block
        a = jnp.concatenate(rows_of, axis=0)
        for lvl in range(len(fine)):
            xf = fine_operand(0, lvl, f["q"], f["k"], f["dec"])
            xb = fine_operand(1, lvl, b["q"], b["k"], b["dec"])
            a = a + masks_ref[lvl] * _dot_nt(jnp.concatenate([xf, xb], axis=1), _block_diag(xf, xb))
        o = _dot(a.astype(BF16), _block_diag(f["v"], b["v"]))
        sf = sf_scr[...]
        sb = sb_scr[...]
        q_dec = jnp.concatenate([(f["q"] * f["dec"](0, chunk)).astype(BF16),
                                 (b["q"] * b["dec"](0, chunk)).astype(BF16)], axis=1)
        o = o + _dot_nt(q_dec, _block_diag(sf.astype(BF16), sb.astype(BF16)))
        diag_f = jnp.sum(f["q"] * f["k"], axis=-1, keepdims=True)
        diag_b = jnp.sum(b["q"] * b["k"], axis=-1, keepdims=True)
        of_scr[f["rows"], :] = o[:, :dh] + diag_f * f["v"].astype(F32)
        ob_scr[b["rows"], :] = o[:, dh:] + diag_b * b["v"].astype(F32)
        k_dec = _block_diag((f["k"] * f["dec"](chunk, 2 * chunk)).astype(BF16),
                            (b["k"] * b["dec"](chunk, 2 * chunk)).astype(BF16))
        v_t = jnp.concatenate([f["v"], b["v"]], axis=0).astype(F32).T.astype(BF16)
        u = _dot(v_t, k_dec)
        sf_scr[...] = sf * f["dec"](chunk - 1, chunk) + u[:, :dh]
        sb_scr[...] = sb * b["dec"](0, 1) + u[:, dh:]

    lb_f = lower_bound(lbf_ref)
    lb_b = lower_bound(lbb_ref)
    sf_scr[...] = jnp.zeros_like(sf_scr)
    sb_scr[...] = jnp.zeros_like(sb_scr)

    def bases(n):
        return (pl.multiple_of(n * pair, pair), pl.multiple_of((n_pairs - 1 - n) * pair, pair))

    def prepare(n, slot):
        for direction, base, z_ref, lb in zip((0, 1), bases(n), (zf_ref, zb_ref), (lb_f, lb_b)):
            g2, k = gates(z_ref[0, pl.ds(base, pair), :], lb)
            k_slots[slot][direction] = k
            store_decays(slot, direction, g2)

    def consume(n, slot):
        parts = []
        for direction, base in zip((0, 1), bases(n)):
            rows = pl.ds(base, pair)
            q = q_ref[0, rows, :].astype(F32)
            v = i_ref[0, rows, :]
            parts.append([chunk_part(slot, direction, c, q, v, base) for c in range(2)])
        pair_step(parts[0][0], parts[1][1])
        pair_step(parts[0][1], parts[1][0])

    def scan_body(m, carry):
        prepare(2 * m + 1, 1)
        consume(2 * m, 0)
        prepare(jnp.minimum(2 * m + 2, n_pairs - 1), 0)
        consume(2 * m + 1, 1)
        return carry

    prepare(0, 0)
    lax.fori_loop(0, n_pairs // 2, scan_body, 0, unroll=4)

    gain = gain_ref[...].astype(F32)

    def norm_body(n, carry):
        rows = pl.ds(pl.multiple_of(n * HG_NORM_ROWS, HG_NORM_ROWS), HG_NORM_ROWS)
        o = of_scr[rows, :] + ob_scr[rows, :]
        o = o * lax.rsqrt(jnp.mean(jnp.square(o), axis=-1, keepdims=True) + RMS_EPS) * gain
        gate = g_ref[0, rows, :].astype(F32)
        o_ref[0, rows, :] = (o * _silu_from_half(gate)).astype(o_ref.dtype)
        return carry

    lax.fori_loop(0, seq // HG_NORM_ROWS, norm_body, 0, unroll=4)


def _hgrn(q, zf, zb, i, g, lb_fwd_logits, lb_bwd_logits, gain_row, layer):
    batch, seq, width = q.shape
    assert width == HG_HEADS * HG_HEAD_DIM and seq % (16 * HG_CHUNK) == 0 and seq % (4 * HG_NORM_ROWS) == 0
    sums, masks, group_lanes = _hgrn_constants(HG_CHUNK)
    sums = jnp.asarray(sums, BF16)
    masks = jnp.asarray(masks, F32)
    group_lanes = jnp.asarray(group_lanes, F32)
    n_layers = lb_fwd_logits.shape[0]
    n_dec_rows = _hgrn_dec_rows(HG_CHUNK)[-1]
    head = pl.BlockSpec((1, seq, HG_HEAD_DIM), lambda b, h: (b, 0, h))
    per_head_row = lambda n: pl.BlockSpec((n, HG_HEAD_DIM), lambda b, h: (0, h))
    whole = lambda a: pl.BlockSpec(a.shape, lambda b, h: (0,) * a.ndim)
    return pl.pallas_call(
        functools.partial(_hgrn_kernel, layer=layer, seq=seq),
        grid=(batch, HG_HEADS),
        in_specs=[head, head, head, head, head,
                  per_head_row(n_layers), per_head_row(n_layers), per_head_row(1),
                  whole(sums), whole(masks), whole(group_lanes)],
        out_specs=head,
        out_shape=jax.ShapeDtypeStruct((batch, seq, width), BF16),
        scratch_shapes=[pltpu.VMEM((seq, HG_HEAD_DIM), F32), pltpu.VMEM((seq, HG_HEAD_DIM), F32),
                        pltpu.VMEM((HG_HEAD_DIM, HG_HEAD_DIM), F32),
                        pltpu.VMEM((HG_HEAD_DIM, HG_HEAD_DIM), F32),
                        pltpu.VMEM((2, 2 * HG_CHUNK, HG_HEAD_DIM), F32),
                        pltpu.VMEM((2, 2 * HG_CHUNK, HG_HEAD_DIM), F32),
                        pltpu.VMEM((2, n_dec_rows, 2 * HG_HEAD_DIM), F32),
                        pltpu.VMEM((2, n_dec_rows, 2 * HG_HEAD_DIM), F32)],
        compiler_params=pltpu.CompilerParams(
            dimension_semantics=("arbitrary", "arbitrary"), vmem_limit_bytes=VMEM_LIMIT_BYTES),
        name="hgrn2_scan",
    )(q, zf, zb, i, g, lb_fwd_logits, lb_bwd_logits, gain_row, sums, masks, group_lanes)


def _out_proj_kernel(oa_ref, oh_ref, x_ref, w_ref, b_ref, gain_ref, bias_ref, o_ref, w_scr, *, alpha):
    d_att = oa_ref.shape[1]
    _cast_weight_once(w_ref, w_scr)
    for r0 in range(0, o_ref.shape[0], OUT_PROJ_ROWS):
        rows = slice(r0, r0 + OUT_PROJ_ROWS)
        y = _dot(oa_ref[rows, :], w_scr[:d_att, :]) + _dot(oh_ref[rows, :], w_scr[d_att:, :]) + b_ref[...]
        r = alpha * x_ref[rows, :] + y
        mu = jnp.mean(r, axis=-1, keepdims=True)
        c = r - mu
        var = jnp.mean(jnp.square(c), axis=-1, keepdims=True)
        o_ref[rows, :] = (c * lax.rsqrt(var + LN_EPS) * gain_ref[...] + bias_ref[...]).astype(o_ref.dtype)


def _out_proj(o_a, o_h, x2d, w, b_row, gain_row, bias_row, alpha):
    n_tok, d_model = x2d.shape
    d_att, d_hg = o_a.shape[1], o_h.shape[1]
    row = pl.BlockSpec((1, d_model), lambda i: (0, 0))
    return pl.pallas_call(
        functools.partial(_out_proj_kernel, alpha=alpha),
        grid=(n_tok // OUT_PROJ_TM,),
        in_specs=[pl.BlockSpec((OUT_PROJ_TM, d_att), lambda i: (i, 0)),
                  pl.BlockSpec((OUT_PROJ_TM, d_hg), lambda i: (i, 0)),
                  pl.BlockSpec((OUT_PROJ_TM, d_model), lambda i: (i, 0)),
                  pl.BlockSpec((d_att + d_hg, d_model), lambda i: (0, 0), pipeline_mode=pl.Buffered(1)),
                  row, row, row],
        out_specs=pl.BlockSpec((OUT_PROJ_TM, d_model), lambda i: (i, 0)),
        out_shape=jax.ShapeDtypeStruct((n_tok, d_model), x2d.dtype),
        scratch_shapes=[pltpu.VMEM((d_att + d_hg, d_model), BF16)],
        compiler_params=pltpu.CompilerParams(
            dimension_semantics=("arbitrary",), vmem_limit_bytes=VMEM_LIMIT_BYTES),
        name="out_proj_layernorm",
    )(o_a, o_h, x2d, w, b_row, gain_row, bias_row)


def _layer(x, layer, depth, w_in, b_in, rpb, lb_fwd_logits, lb_bwd_logits, hg_norm_gain,
           w_out, b_out, ln_gain, ln_bias):
    batch, seq, d_model = x.shape
    x2d = x.reshape(batch * seq, d_model)
    slabs = _in_proj(x2d, w_in, b_in.reshape(1, -1))
    q_a, k_a, v_a, g_a, q_h, z_f, z_b, i_h, g_h = [s.reshape(batch, seq, SLAB) for s in slabs]
    o_a = _attention(q_a, k_a, v_a, g_a, rpb)
    o_h = _hgrn(q_h, z_f, z_b, i_h, g_h, lb_fwd_logits, lb_bwd_logits,
                hg_norm_gain.reshape(1, -1), layer)
    alpha = (2.0 * depth) ** 0.25
    out = _out_proj(o_a.reshape(batch * seq, SLAB), o_h.reshape(batch * seq, SLAB), x2d,
                    w_out, b_out.reshape(1, -1), ln_gain.reshape(1, -1),
                    ln_bias.reshape(1, -1), alpha)
    return out.reshape(batch, seq, d_model)


def kernel(x, w_in, b_in, rpb, lb_fwd_logits, lb_bwd_logits, hg_norm_gain, w_out, b_out, ln_gain, ln_bias):
    depth = w_in.shape[0]
    for layer in range(depth):
        x = _layer(x, layer, depth, w_in[layer], b_in[layer], rpb[layer], lb_fwd_logits,
                   lb_bwd_logits, hg_norm_gain[layer], w_out[layer], b_out[layer],
                   ln_gain[layer], ln_bias[layer])
    return x
```

```python
import functools

import numpy as np
import jax
import jax.numpy as jnp
from jax import lax
from jax.experimental import pallas as pl
from jax.experimental.pallas import tpu as pltpu

GRID_W = 64
ATT_HEADS = 8
ATT_HEAD_DIM = 64
HG_HEADS = 4
HG_HEAD_DIM = 128
WIN_ROWS = 8
WIN_COLS = 16
LN_EPS = 1e-5
RMS_EPS = 1e-6
N_SLABS = 9
SLAB = 512

LANES = 128
F32_SUBLANES = 8
VMEM_LIMIT_BYTES = 56 * 1024 * 1024
PROJ_TM = 512
OUT_PROJ_TM = 1024
OUT_PROJ_ROWS = 256
ATT_ROWS_PER_STEP = 16
ATT_UNROLL = 4
ATT_GROUP = 4
HG_CHUNK = 64
HG_GROUP = F32_SUBLANES
HG_NORM_ROWS = 256
MASK_VALUE = -1e30
LOG2_E = 1.4426950408889634
SLAB_SCALES = {1: ATT_HEAD_DIM ** -0.5 * LOG2_E, 3: 0.5, 5: 0.5, 6: 0.5, 8: 0.5}

BF16 = jnp.bfloat16
F32 = jnp.float32


def _silu_from_half(half_g):
    return half_g * (1.0 + jnp.tanh(half_g))


def _dot(a, b):
    return jnp.dot(a, b, preferred_element_type=F32)


def _dot_nt(a, b):
    return lax.dot_general(a, b, (((1,), (1,)), ((), ())), preferred_element_type=F32)


def _cast_weight_once(w_ref, w_scr):
    @pl.when(pl.program_id(0) == 0)
    def _():
        for c0 in range(0, w_ref.shape[1], SLAB):
            w_scr[:, c0:c0 + SLAB] = w_ref[:, c0:c0 + SLAB].astype(w_scr.dtype)


def _in_proj_kernel(x_ref, w_ref, b_ref, *refs):
    out_refs, w_scr = refs[:-1], refs[-1]
    _cast_weight_once(w_ref, w_scr)
    xb = x_ref[...].astype(BF16)
    for j, o_ref in enumerate(out_refs):
        cols = slice(j * SLAB, (j + 1) * SLAB)
        h = _dot(xb, w_scr[:, cols]) + b_ref[:, cols]
        if j in SLAB_SCALES:
            h = h * SLAB_SCALES[j]
        o_ref[...] = h.astype(o_ref.dtype)


def _in_proj(x2d, w, b_row):
    n_tok, d_model = x2d.shape
    d_in = w.shape[1]
    assert d_in == N_SLABS * SLAB and n_tok % PROJ_TM == 0
    return pl.pallas_call(
        _in_proj_kernel,
        grid=(n_tok // PROJ_TM,),
        in_specs=[
            pl.BlockSpec((PROJ_TM, d_model), lambda i: (i, 0)),
            pl.BlockSpec((d_model, d_in), lambda i: (0, 0), pipeline_mode=pl.Buffered(1)),
            pl.BlockSpec((1, d_in), lambda i: (0, 0)),
        ],
        out_specs=[pl.BlockSpec((PROJ_TM, SLAB), lambda i: (i, 0))] * N_SLABS,
        out_shape=[jax.ShapeDtypeStruct((n_tok, SLAB), BF16)] * N_SLABS,
        scratch_shapes=[pltpu.VMEM((d_model, d_in), BF16)],
        compiler_params=pltpu.CompilerParams(
            dimension_semantics=("arbitrary",), vmem_limit_bytes=VMEM_LIMIT_BYTES),
        name="in_proj",
    )(x2d, w, b_row)


ATT_REL_ROWS = 2 * WIN_ROWS - 1
ATT_BIAS_TILES = (ATT_REL_ROWS - 1) // 2
ATT_GROUPS = ATT_HEADS // ATT_GROUP


def _attention_bias_rows(rpb):
    n_rel_cols = 2 * WIN_COLS - 1
    m = np.arange(LANES)
    first = np.where(m < WIN_COLS, m + WIN_COLS - 1, m - (LANES - WIN_COLS + 1))
    first_ok = (m < WIN_COLS) | (m > LANES - WIN_COLS)
    second = m - (GRID_W - WIN_COLS + 1)
    second_ok = (second >= 0) & (second < n_rel_cols)
    col = np.where(first_ok, first, np.where(second_ok, second, 0))
    p = np.arange(2)[:, None, None]
    t = np.arange(ATT_BIAS_TILES)[None, :, None]
    rel = 2 * t + p + np.where(first_ok, 0, 1)[None, None, :]
    gen = jnp.where(first_ok | second_ok, rpb.astype(F32)[:, rel, col[None, None, :]], 0.0) * LOG2_E
    gen = gen.reshape(ATT_GROUPS, ATT_GROUP, 2, ATT_BIAS_TILES, LANES).transpose(2, 0, 3, 1, 4)
    return gen.reshape(2 * ATT_GROUPS * ATT_BIAS_TILES * ATT_GROUP, 1, LANES)


def _attention_window_mask():
    qc = np.arange(GRID_W)[:, None]
    kc = np.arange(LANES)[None, :] % GRID_W
    col_start = np.clip(qc - WIN_COLS // 2, 0, GRID_W - WIN_COLS)
    return ((kc >= col_start) & (kc < col_start + WIN_COLS)).astype(np.float32)


def _attention_kernel(q_ref, k_ref, v_ref, g_ref, gen_ref, win_ref, o_ref, bias_scr, *, rows, kr):
    step = pl.program_id(1)
    gw = ATT_GROUP * ATT_HEAD_DIM
    row_head = lax.broadcasted_iota(jnp.int32, (ATT_GROUP * GRID_W, gw), 0) // GRID_W
    lane_head = lax.broadcasted_iota(jnp.int32, (ATT_GROUP * GRID_W, gw), 1) // ATT_HEAD_DIM
    own_head = row_head == lane_head
    out_lane_head = lax.broadcasted_iota(jnp.int32, (GRID_W, gw), 1) // ATT_HEAD_DIM

    @pl.when((pl.program_id(0) == 0) & (step == 0))
    def _build_bias():
        in_window = win_ref[...] > 0.0

        def one_block(n, carry):
            gen = jnp.broadcast_to(gen_ref[n], (GRID_W, LANES))
            toeplitz = pltpu.roll(gen, 0, 1, stride=1, stride_axis=0)
            head_rows = pl.ds(pl.multiple_of((n % ATT_GROUP) * GRID_W, GRID_W), GRID_W)
            bias_scr[n // ATT_GROUP, head_rows, :] = jnp.where(in_window, toeplitz, MASK_VALUE)
            return carry

        lax.fori_loop(0, gen_ref.shape[0], one_block, 0, unroll=8)

    def one_row(j, carry):
        r = step * ATT_ROWS_PER_STEP + j
        row_start = jnp.clip(r - kr // 2, 0, rows - kr)
        variant = row_start - r + (WIN_ROWS - 1)
        q_tok = pl.multiple_of(j * GRID_W, GRID_W)
        k_tok = pl.multiple_of(row_start * GRID_W, GRID_W)
        for grp in range(ATT_GROUPS):
            lanes = slice(grp * gw, (grp + 1) * gw)
            q = q_ref[0, pl.ds(q_tok, GRID_W), lanes]
            q_bd = jnp.where(own_head, jnp.concatenate([q] * ATT_GROUP, axis=0), 0)
            keys = k_ref[0, pl.ds(k_tok, kr * GRID_W), lanes]
            vals = v_ref[0, pl.ds(k_tok, kr * GRID_W), lanes]
            tile0 = ((variant % 2) * ATT_GROUPS + grp) * ATT_BIAS_TILES + variant // 2
            bias = jnp.concatenate([bias_scr[tile0 + t] for t in range(kr // 2)], axis=1)
            s = _dot_nt(q_bd, keys) + bias
            m = jnp.max(s, axis=-1, keepdims=True)
            p = jnp.exp2(s - m)
            denom = jnp.sum(p, axis=-1, keepdims=True)
            pv = _dot(p.astype(BF16), vals) / denom
            o = jnp.zeros((GRID_W, gw), F32)
            for h in range(ATT_GROUP):
                o = o + jnp.where(out_lane_head == h, pv[h * GRID_W:(h + 1) * GRID_W], 0.0)
            gate = g_ref[0, pl.ds(q_tok, GRID_W), lanes].astype(F32)
            o = o * _silu_from_half(gate)
            o_ref[0, pl.ds(q_tok, GRID_W), lanes] = o.astype(o_ref.dtype)
        return carry

    lax.fori_loop(0, ATT_ROWS_PER_STEP, one_row, 0, unroll=ATT_UNROLL)


def _attention(q, k, v, g, rpb):
    batch, seq, width = q.shape
    rows = seq // GRID_W
    kr = min(WIN_ROWS, rows)
    assert rows % ATT_ROWS_PER_STEP == 0 and ATT_ROWS_PER_STEP % ATT_UNROLL == 0
    assert kr == WIN_ROWS and 2 * GRID_W == LANES
    gen = _attention_bias_rows(rpb)
    win = jnp.asarray(_attention_window_mask())
    blk = ATT_ROWS_PER_STEP * GRID_W
    tile = pl.BlockSpec((1, blk, width), lambda b, i: (b, i, 0))
    whole = pl.BlockSpec((1, seq, width), lambda b, i: (b, 0, 0))
    return pl.pallas_call(
        functools.partial(_attention_kernel, rows=rows, kr=kr),
        grid=(batch, rows // ATT_ROWS_PER_STEP),
        in_specs=[tile, whole, whole, tile,
                  pl.BlockSpec(gen.shape, lambda b, i: (0, 0, 0)),
                  pl.BlockSpec(win.shape, lambda b, i: (0, 0))],
        out_specs=tile,
        out_shape=jax.ShapeDtypeStruct((batch, seq, width), BF16),
        scratch_shapes=[pltpu.VMEM((2 * ATT_GROUPS * ATT_BIAS_TILES, ATT_GROUP * GRID_W, 2 * GRID_W), F32)],
        compiler_params=pltpu.CompilerParams(
            dimension_semantics=("arbitrary", "arbitrary"), vmem_limit_bytes=VMEM_LIMIT_BYTES),
        name="nbr_attention",
    )(q, k, v, g, gen, win)


def _hgrn_fine_levels():
    sizes = []
    c = HG_GROUP
    while c >= 2:
        sizes.append(c)
        c //= 2
    return sizes


def _hgrn_group_pairs(direction, chunk):
    n = chunk // HG_GROUP
    if direction == 0:
        return [(t, g) for t in range(1, n) for g in range(t)]
    return [(t, g) for t in range(n - 2, -1, -1) for g in range(t + 1, n)]


def _hgrn_constants(chunk):
    t = np.arange(chunk)[:, None]
    u = np.arange(chunk)[None, :]
    blocks = [(u <= t), (u > t)]
    masks = []
    for c in _hgrn_fine_levels():
        half = c // 2
        mid = (t // c) * c + half
        upper = (t % c) >= half
        if c > 2:
            blocks.append(np.where(upper, (u >= mid) & (u <= t), (u > t) & (u <= mid - 1)))
        masks.append((t // c == u // c) & upper & ((u % c) < half))
    fwd_sums = np.stack(blocks).astype(np.float32)
    fwd_masks = np.stack(masks).astype(np.float32)
    twice = lambda m: np.concatenate([m.reshape(-1, chunk)] * 2, axis=1)
    sums = np.stack([twice(fwd_sums), twice(fwd_sums[:, ::-1, ::-1])])
    masks = np.concatenate([fwd_masks, fwd_masks[:, ::-1, ::-1]], axis=2)
    return sums, masks


def _block_diag(a, b):
    zero = jnp.zeros_like(a)
    return jnp.concatenate([jnp.concatenate([a, zero], axis=1),
                            jnp.concatenate([zero, b], axis=1)], axis=0)


def _hgrn_dec_rows(chunk):
    n_var = len(_hgrn_group_pairs(0, chunk)) * HG_GROUP
    n_mm = sum(c > 2 for c in _hgrn_fine_levels())
    edge, var, fine = 2 * chunk, 3 * chunk, 3 * chunk + n_var
    return edge, var, fine, fine + n_mm * chunk


def _hgrn_kernel(q_ref, zf_ref, zb_ref, i_ref, g_ref, lbf_ref, lbb_ref, gain_ref,
                 sums_ref, masks_ref, o_ref, of_scr, ob_scr, sf_scr, sb_scr,
                 k_scr0, k_scr1, dec_scr0, dec_scr1, *, layer, seq):
    chunk = HG_CHUNK
    pair = 2 * chunk
    n_pairs = seq // pair
    dh = HG_HEAD_DIM
    grp = HG_GROUP
    n_groups = chunk // grp
    fine = _hgrn_fine_levels()
    group_pairs = [_hgrn_group_pairs(direction, chunk) for direction in (0, 1)]
    r_edge, r_var, r_fine, _ = _hgrn_dec_rows(chunk)
    k_slots = (k_scr0, k_scr1)
    dec_slots = (dec_scr0, dec_scr1)
    row_in_chunk = lax.broadcasted_iota(jnp.int32, (chunk, dh), 0)
    lane_group = lax.broadcasted_iota(jnp.int32, (grp, 2 * chunk), 1) // grp

    def lower_bound(logit_ref):
        logits = logit_ref[...].astype(F32)
        e = jnp.exp(logits - jnp.max(logits, axis=0, keepdims=True))
        return jnp.sum(e[:layer + 1], axis=0, keepdims=True) / jnp.sum(e, axis=0, keepdims=True)

    def gates(half_z, lb):
        span = 0.5 * (1.0 - lb)
        moved = span * jnp.tanh(half_z.astype(F32))
        return jnp.log2((1.0 - span) + moved), span - moved

    def store_decays(slot, direction, g2):
        hi = g2.astype(BF16)
        lo = (g2 - hi.astype(F32)).astype(BF16)
        stacked = [jnp.concatenate([hi[c * chunk:(c + 1) * chunk], lo[c * chunk:(c + 1) * chunk]], axis=0)
                   for c in range(2)]
        raw = _dot(sums_ref[direction], jnp.concatenate(stacked, axis=1))
        dec = dec_slots[slot].at[direction]
        dec[0:2 * chunk, :] = jnp.exp2(raw[0:2 * chunk])
        dec[r_fine:, :] = jnp.exp2(raw[2 * chunk:])
        cum = raw[0:chunk]
        edge = grp - 1 if direction == 0 else 0
        edges = [cum[g * grp + edge:g * grp + edge + 1] for g in range(n_groups)]
        group = lambda g: cum[g * grp:(g + 1) * grp]
        dec[r_edge:r_var, :] = jnp.exp2(jnp.concatenate([edges[g] - group(g) for g in range(n_groups)], axis=0))
        dec[r_var:r_fine, :] = jnp.exp2(jnp.concatenate([group(t) - edges[g]
                                                         for t, g in group_pairs[direction]], axis=0))

    def fine_operand(direction, lvl, q, k, dec):
        c = fine[lvl]
        pos = row_in_chunk % c
        is_upper = (pos >= c // 2) if direction == 0 else (pos < c // 2)
        if c == 2:
            return jnp.where(is_upper, q * (1.0 - k), k).astype(BF16)
        return (jnp.where(is_upper, q, k) * dec(r_fine + lvl * chunk, r_fine + (lvl + 1) * chunk)).astype(BF16)

    def chunk_part(slot, direction, c, q, v, base):
        rows = slice(c * chunk, (c + 1) * chunk)
        lanes = slice(c * dh, (c + 1) * dh)
        return dict(q=q[rows], k=k_slots[slot][direction, rows, :], v=v[rows],
                    dec=lambda r0, r1: dec_slots[slot][direction, r0:r1, lanes],
                    rows=pl.ds(pl.multiple_of(base + c * chunk, chunk), chunk))

    def pair_step(f, b):
        lhs = jnp.concatenate(
            [(jnp.concatenate([p["q"][t * grp:(t + 1) * grp] for t, _ in group_pairs[d]], axis=0)
              * p["dec"](r_var, r_fine)).astype(BF16) for d, p in enumerate((f, b))], axis=1)
        keys = [(p["k"] * p["dec"](r_edge, r_var)).astype(BF16) for p in (f, b)]
        r = _dot_nt(lhs, _block_diag(*keys))
        rows_of = [jnp.zeros((grp, 2 * chunk), F32)] * n_groups
        for i, ((tf, gf), (tb, gb)) in enumerate(zip(*group_pairs)):
            block = r[i * grp:(i + 1) * grp]
            rows_of[tf] = jnp.where(lane_group == gf, block, rows_of[tf])
            rows_of[tb] = jnp.where(lane_group == n_groups + gb, block, rows_of[tb])
        a = jnp.concatenate(rows_of, axis=0)
        for lvl in range(len(fine)):
            xf = fine_operand(0, lvl, f["q"], f["k"], f["dec"])
            xb = fine_operand(1, lvl, b["q"], b["k"], b["dec"])
            a = a + masks_ref[lvl] * _dot_nt(jnp.concatenate([xf, xb], axis=1), _block_diag(xf, xb))
        o = _dot(a.astype(BF16), _block_diag(f["v"], b["v"]))
        sf = sf_scr[...]
        sb = sb_scr[...]
        q_dec = jnp.concatenate([(f["q"] * f["dec"](0, chunk)).astype(BF16),
                                 (b["q"] * b["dec"](0, chunk)).astype(BF16)], axis=1)
        o = o + _dot_nt(q_dec, _block_diag(sf.astype(BF16), sb.astype(BF16)))
        diag_f = jnp.sum(f["q"] * f["k"], axis=-1, keepdims=True)
        diag_b = jnp.sum(b["q"] * b["k"], axis=-1, keepdims=True)
        of_scr[f["rows"], :] = o[:, :dh] + diag_f * f["v"].astype(F32)
        ob_scr[b["rows"], :] = o[:, dh:] + diag_b * b["v"].astype(F32)
        k_dec = _block_diag((f["k"] * f["dec"](chunk, 2 * chunk)).astype(BF16),
                            (b["k"] * b["dec"](chunk, 2 * chunk)).astype(BF16))
        v_t = jnp.concatenate([f["v"], b["v"]], axis=0).astype(F32).T.astype(BF16)
        u = _dot(v_t, k_dec)
        sf_scr[...] = sf * f["dec"](chunk - 1, chunk) + u[:, :dh]
        sb_scr[...] = sb * b["dec"](0, 1) + u[:, dh:]

    lb_f = lower_bound(lbf_ref)
    lb_b = lower_bound(lbb_ref)
    sf_scr[...] = jnp.zeros_like(sf_scr)
    sb_scr[...] = jnp.zeros_like(sb_scr)

    def bases(n):
        return (pl.multiple_of(n * pair, pair), pl.multiple_of((n_pairs - 1 - n) * pair, pair))

    def prepare(n, slot):
        for direction, base, z_ref, lb in zip((0, 1), bases(n), (zf_ref, zb_ref), (lb_f, lb_b)):
            g2, k = gates(z_ref[0, pl.ds(base, pair), :], lb)
            k_slots[slot][direction] = k
            store_decays(slot, direction, g2)

    def consume(n, slot):
        parts = []
        for direction, base in zip((0, 1), bases(n)):
            rows = pl.ds(base, pair)
            q = q_ref[0, rows, :].astype(F32)
            v = i_ref[0, rows, :]
            parts.append([chunk_part(slot, direction, c, q, v, base) for c in range(2)])
        pair_step(parts[0][0], parts[1][1])
        pair_step(parts[0][1], parts[1][0])

    def scan_body(m, carry):
        prepare(2 * m + 1, 1)
        consume(2 * m, 0)
        prepare(jnp.minimum(2 * m + 2, n_pairs - 1), 0)
        consume(2 * m + 1, 1)
        return carry

    prepare(0, 0)
    lax.fori_loop(0, n_pairs // 2, scan_body, 0, unroll=4)

    gain = gain_ref[...].astype(F32)

    def norm_body(n, carry):
        rows = pl.ds(pl.multiple_of(n * HG_NORM_ROWS, HG_NORM_ROWS), HG_NORM_ROWS)
        o = of_scr[rows, :] + ob_scr[rows, :]
        o = o * lax.rsqrt(jnp.mean(jnp.square(o), axis=-1, keepdims=True) + RMS_EPS) * gain
        gate = g_ref[0, rows, :].astype(F32)
        o_ref[0, rows, :] = (o * _silu_from_half(gate)).astype(o_ref.dtype)
        return carry

    lax.fori_loop(0, seq // HG_NORM_ROWS, norm_body, 0, unroll=4)


def _hgrn(q, zf, zb, i, g, lb_fwd_logits, lb_bwd_logits, gain_row, layer):
    batch, seq, width = q.shape
    assert width == HG_HEADS * HG_HEAD_DIM and seq % (16 * HG_CHUNK) == 0 and seq % (4 * HG_NORM_ROWS) == 0
    sums, masks = _hgrn_constants(HG_CHUNK)
    sums = jnp.asarray(sums, BF16)
    masks = jnp.asarray(masks, F32)
    n_layers = lb_fwd_logits.shape[0]
    n_dec_rows = _hgrn_dec_rows(HG_CHUNK)[-1]
    head = pl.BlockSpec((1, seq, HG_HEAD_DIM), lambda b, h: (b, 0, h))
    per_head_row = lambda n: pl.BlockSpec((n, HG_HEAD_DIM), lambda b, h: (0, h))
    whole = lambda a: pl.BlockSpec(a.shape, lambda b, h: (0,) * a.ndim)
    return pl.pallas_call(
        functools.partial(_hgrn_kernel, layer=layer, seq=seq),
        grid=(batch, HG_HEADS),
        in_specs=[head, head, head, head, head,
                  per_head_row(n_layers), per_head_row(n_layers), per_head_row(1),
                  whole(sums), whole(masks)],
        out_specs=head,
        out_shape=jax.ShapeDtypeStruct((batch, seq, width), BF16),
        scratch_shapes=[pltpu.VMEM((seq, HG_HEAD_DIM), F32), pltpu.VMEM((seq, HG_HEAD_DIM), F32),
                        pltpu.VMEM((HG_HEAD_DIM, HG_HEAD_DIM), F32),
                        pltpu.VMEM((HG_HEAD_DIM, HG_HEAD_DIM), F32),
                        pltpu.VMEM((2, 2 * HG_CHUNK, HG_HEAD_DIM), F32),
                        pltpu.VMEM((2, 2 * HG_CHUNK, HG_HEAD_DIM), F32),
                        pltpu.VMEM((2, n_dec_rows, 2 * HG_HEAD_DIM), F32),
                        pltpu.VMEM((2, n_dec_rows, 2 * HG_HEAD_DIM), F32)],
        compiler_params=pltpu.CompilerParams(
            dimension_semantics=("arbitrary", "arbitrary"), vmem_limit_bytes=VMEM_LIMIT_BYTES),
        name="hgrn2_scan",
    )(q, zf, zb, i, g, lb_fwd_logits, lb_bwd_logits, gain_row, sums, masks)


def _out_proj_kernel(oa_ref, oh_ref, x_ref, w_ref, b_ref, gain_ref, bias_ref, o_ref, w_scr, *, alpha):
    d_att = oa_ref.shape[1]
    _cast_weight_once(w_ref, w_scr)
    for r0 in range(0, o_ref.shape[0], OUT_PROJ_ROWS):
        rows = slice(r0, r0 + OUT_PROJ_ROWS)
        y = _dot(oa_ref[rows, :], w_scr[:d_att, :]) + _dot(oh_ref[rows, :], w_scr[d_att:, :]) + b_ref[...]
        r = alpha * x_ref[rows, :] + y
        mu = jnp.mean(r, axis=-1, keepdims=True)
        c = r - mu
        var = jnp.mean(jnp.square(c), axis=-1, keepdims=True)
        o_ref[rows, :] = (c * lax.rsqrt(var + LN_EPS) * gain_ref[...] + bias_ref[...]).astype(o_ref.dtype)


def _out_proj(o_a, o_h, x2d, w, b_row, gain_row, bias_row, alpha):
    n_tok, d_model = x2d.shape
    d_att, d_hg = o_a.shape[1], o_h.shape[1]
    row = pl.BlockSpec((1, d_model), lambda i: (0, 0))
    return pl.pallas_call(
        functools.partial(_out_proj_kernel, alpha=alpha),
        grid=(n_tok // OUT_PROJ_TM,),
        in_specs=[pl.BlockSpec((OUT_PROJ_TM, d_att), lambda i: (i, 0)),
                  pl.BlockSpec((OUT_PROJ_TM, d_hg), lambda i: (i, 0)),
                  pl.BlockSpec((OUT_PROJ_TM, d_model), lambda i: (i, 0)),
                  pl.BlockSpec((d_att + d_hg, d_model), lambda i: (0, 0), pipeline_mode=pl.Buffered(1)),
                  row, row, row],
        out_specs=pl.BlockSpec((OUT_PROJ_TM, d_model), lambda i: (i, 0)),
        out_shape=jax.ShapeDtypeStruct((n_tok, d_model), x2d.dtype),
        scratch_shapes=[pltpu.VMEM((d_att + d_hg, d_model), BF16)],
        compiler_params=pltpu.CompilerParams(
            dimension_semantics=("arbitrary",), vmem_limit_bytes=VMEM_LIMIT_BYTES),
        name="out_proj_layernorm",
    )(o_a, o_h, x2d, w, b_row, gain_row, bias_row)


def _layer(x, layer, depth, w_in, b_in, rpb, lb_fwd_logits, lb_bwd_logits, hg_norm_gain,
           w_out, b_out, ln_gain, ln_bias):
    batch, seq, d_model = x.shape
    x2d = x.reshape(batch * seq, d_model)
    slabs = _in_proj(x2d, w_in, b_in.reshape(1, -1))
    q_a, k_a, v_a, g_a, q_h, z_f, z_b, i_h, g_h = [s.reshape(batch, seq, SLAB) for s in slabs]
    o_a = _attention(q_a, k_a, v_a, g_a, rpb)
    o_h = _hgrn(q_h, z_f, z_b, i_h, g_h, lb_fwd_logits, lb_bwd_logits,
                hg_norm_gain.reshape(1, -1), layer)
    alpha = (2.0 * depth) ** 0.25
    out = _out_proj(o_a.reshape(batch * seq, SLAB), o_h.reshape(batch * seq, SLAB), x2d,
                    w_out, b_out.reshape(1, -1), ln_gain.reshape(1, -1),
                    ln_bias.reshape(1, -1), alpha)
    return out.reshape(batch, seq, d_model)


def kernel(x, w_in, b_in, rpb, lb_fwd_logits, lb_bwd_logits, hg_norm_gain, w_out, b_out, ln_gain, ln_bias):
    depth = w_in.shape[0]
    for layer in range(depth):
        x = _layer(x, layer, depth, w_in[layer], b_in[layer], rpb[layer], lb_fwd_logits,
                   lb_bwd_logits, hg_norm_gain[layer], w_out[layer], b_out[layer],
                   ln_gain[layer], ln_bias[layer])
    return x
```

```python
import functools

import numpy as np
import jax
import jax.numpy as jnp
from jax import lax
from jax.experimental import pallas as pl
from jax.experimental.pallas import tpu as pltpu

GRID_W = 64
ATT_HEADS = 8
ATT_HEAD_DIM = 64
HG_HEADS = 4
HG_HEAD_DIM = 128
WIN_ROWS = 8
WIN_COLS = 16
LN_EPS = 1e-5
RMS_EPS = 1e-6
N_SLABS = 9
SLAB = 512

LANES = 128
F32_SUBLANES = 8
VMEM_LIMIT_BYTES = 56 * 1024 * 1024
PROJ_TM = 512
OUT_PROJ_TM = 1024
OUT_PROJ_ROWS = 256
ATT_ROWS_PER_STEP = 16
ATT_UNROLL = 4
ATT_GROUP = 4
HG_CHUNK = 64
HG_GROUP = F32_SUBLANES
HG_NORM_ROWS = 256
MASK_VALUE = -1e30
LOG2_E = 1.4426950408889634
SLAB_SCALES = {1: ATT_HEAD_DIM ** -0.5 * LOG2_E, 3: 0.5, 8: 0.5}

BF16 = jnp.bfloat16
F32 = jnp.float32


def _silu_from_half(half_g):
    return half_g * (1.0 + jnp.tanh(half_g))


def _dot(a, b):
    return jnp.dot(a, b, preferred_element_type=F32)


def _dot_nt(a, b):
    return lax.dot_general(a, b, (((1,), (1,)), ((), ())), preferred_element_type=F32)


def _cast_weight_once(w_ref, w_scr):
    @pl.when(pl.program_id(0) == 0)
    def _():
        for c0 in range(0, w_ref.shape[1], SLAB):
            w_scr[:, c0:c0 + SLAB] = w_ref[:, c0:c0 + SLAB].astype(w_scr.dtype)


def _in_proj_kernel(x_ref, w_ref, b_ref, *refs):
    out_refs, w_scr = refs[:-1], refs[-1]
    _cast_weight_once(w_ref, w_scr)
    xb = x_ref[...].astype(BF16)
    for j, o_ref in enumerate(out_refs):
        cols = slice(j * SLAB, (j + 1) * SLAB)
        h = _dot(xb, w_scr[:, cols]) + b_ref[:, cols]
        if j in SLAB_SCALES:
            h = h * SLAB_SCALES[j]
        o_ref[...] = h.astype(o_ref.dtype)


def _in_proj(x2d, w, b_row):
    n_tok, d_model = x2d.shape
    d_in = w.shape[1]
    assert d_in == N_SLABS * SLAB and n_tok % PROJ_TM == 0
    return pl.pallas_call(
        _in_proj_kernel,
        grid=(n_tok // PROJ_TM,),
        in_specs=[
            pl.BlockSpec((PROJ_TM, d_model), lambda i: (i, 0)),
            pl.BlockSpec((d_model, d_in), lambda i: (0, 0), pipeline_mode=pl.Buffered(1)),
            pl.BlockSpec((1, d_in), lambda i: (0, 0)),
        ],
        out_specs=[pl.BlockSpec((PROJ_TM, SLAB), lambda i: (i, 0))] * N_SLABS,
        out_shape=[jax.ShapeDtypeStruct((n_tok, SLAB), BF16)] * N_SLABS,
        scratch_shapes=[pltpu.VMEM((d_model, d_in), BF16)],
        compiler_params=pltpu.CompilerParams(
            dimension_semantics=("arbitrary",), vmem_limit_bytes=VMEM_LIMIT_BYTES),
        name="in_proj",
    )(x2d, w, b_row)


ATT_REL_ROWS = 2 * WIN_ROWS - 1
ATT_BIAS_TILES = (ATT_REL_ROWS - 1) // 2
ATT_GROUPS = ATT_HEADS // ATT_GROUP


def _attention_bias_rows(rpb):
    n_rel_cols = 2 * WIN_COLS - 1
    m = np.arange(LANES)
    first = np.where(m < WIN_COLS, m + WIN_COLS - 1, m - (LANES - WIN_COLS + 1))
    first_ok = (m < WIN_COLS) | (m > LANES - WIN_COLS)
    second = m - (GRID_W - WIN_COLS + 1)
    second_ok = (second >= 0) & (second < n_rel_cols)
    col = np.where(first_ok, first, np.where(second_ok, second, 0))
    p = np.arange(2)[:, None, None]
    t = np.arange(ATT_BIAS_TILES)[None, :, None]
    rel = 2 * t + p + np.where(first_ok, 0, 1)[None, None, :]
    gen = jnp.where(first_ok | second_ok, rpb.astype(F32)[:, rel, col[None, None, :]], 0.0) * LOG2_E
    gen = gen.reshape(ATT_GROUPS, ATT_GROUP, 2, ATT_BIAS_TILES, LANES).transpose(2, 0, 3, 1, 4)
    return gen.reshape(2 * ATT_GROUPS * ATT_BIAS_TILES * ATT_GROUP, 1, LANES)


def _attention_window_mask():
    qc = np.arange(GRID_W)[:, None]
    kc = np.arange(LANES)[None, :] % GRID_W
    col_start = np.clip(qc - WIN_COLS // 2, 0, GRID_W - WIN_COLS)
    return ((kc >= col_start) & (kc < col_start + WIN_COLS)).astype(np.float32)


def _attention_kernel(q_ref, k_ref, v_ref, g_ref, gen_ref, win_ref, o_ref, bias_scr, *, rows, kr):
    step = pl.program_id(1)
    gw = ATT_GROUP * ATT_HEAD_DIM
    row_head = lax.broadcasted_iota(jnp.int32, (ATT_GROUP * GRID_W, gw), 0) // GRID_W
    lane_head = lax.broadcasted_iota(jnp.int32, (ATT_GROUP * GRID_W, gw), 1) // ATT_HEAD_DIM
    own_head = row_head == lane_head
    out_lane_head = lax.broadcasted_iota(jnp.int32, (GRID_W, gw), 1) // ATT_HEAD_DIM

    @pl.when((pl.program_id(0) == 0) & (step == 0))
    def _build_bias():
        in_window = win_ref[...] > 0.0

        def one_block(n, carry):
            gen = jnp.broadcast_to(gen_ref[n], (GRID_W, LANES))
            toeplitz = pltpu.roll(gen, 0, 1, stride=1, stride_axis=0)
            head_rows = pl.ds(pl.multiple_of((n % ATT_GROUP) * GRID_W, GRID_W), GRID_W)
            bias_scr[n // ATT_GROUP, head_rows, :] = jnp.where(in_window, toeplitz, MASK_VALUE)
            return carry

        lax.fori_loop(0, gen_ref.shape[0], one_block, 0, unroll=8)

    def one_row(j, carry):
        r = step * ATT_ROWS_PER_STEP + j
        row_start = jnp.clip(r - kr // 2, 0, rows - kr)
        variant = row_start - r + (WIN_ROWS - 1)
        q_tok = pl.multiple_of(j * GRID_W, GRID_W)
        k_tok = pl.multiple_of(row_start * GRID_W, GRID_W)
        for grp in range(ATT_GROUPS):
            lanes = slice(grp * gw, (grp + 1) * gw)
            q = q_ref[0, pl.ds(q_tok, GRID_W), lanes]
            q_bd = jnp.where(own_head, jnp.concatenate([q] * ATT_GROUP, axis=0), 0)
            keys = k_ref[0, pl.ds(k_tok, kr * GRID_W), lanes]
            vals = v_ref[0, pl.ds(k_tok, kr * GRID_W), lanes]
            tile0 = ((variant % 2) * ATT_GROUPS + grp) * ATT_BIAS_TILES + variant // 2
            bias = jnp.concatenate([bias_scr[tile0 + t] for t in range(kr // 2)], axis=1)
            s = _dot_nt(q_bd, keys) + bias
            m = jnp.max(s, axis=-1, keepdims=True)
            p = jnp.exp2(s - m)
            denom = jnp.sum(p, axis=-1, keepdims=True)
            pv = _dot(p.astype(BF16), vals) / denom
            o = jnp.zeros((GRID_W, gw), F32)
            for h in range(ATT_GROUP):
                o = o + jnp.where(out_lane_head == h, pv[h * GRID_W:(h + 1) * GRID_W], 0.0)
            gate = g_ref[0, pl.ds(q_tok, GRID_W), lanes].astype(F32)
            o = o * _silu_from_half(gate)
            o_ref[0, pl.ds(q_tok, GRID_W), lanes] = o.astype(o_ref.dtype)
        return carry

    lax.fori_loop(0, ATT_ROWS_PER_STEP, one_row, 0, unroll=ATT_UNROLL)


def _attention(q, k, v, g, rpb):
    batch, seq, width = q.shape
    rows = seq // GRID_W
    kr = min(WIN_ROWS, rows)
    assert rows % ATT_ROWS_PER_STEP == 0 and ATT_ROWS_PER_STEP % ATT_UNROLL == 0
    assert kr == WIN_ROWS and 2 * GRID_W == LANES
    gen = _attention_bias_rows(rpb)
    win = jnp.asarray(_attention_window_mask())
    blk = ATT_ROWS_PER_STEP * GRID_W
    tile = pl.BlockSpec((1, blk, width), lambda b, i: (b, i, 0))
    whole = pl.BlockSpec((1, seq, width), lambda b, i: (b, 0, 0))
    return pl.pallas_call(
        functools.partial(_attention_kernel, rows=rows, kr=kr),
        grid=(batch, rows // ATT_ROWS_PER_STEP),
        in_specs=[tile, whole, whole, tile,
                  pl.BlockSpec(gen.shape, lambda b, i: (0, 0, 0)),
                  pl.BlockSpec(win.shape, lambda b, i: (0, 0))],
        out_specs=tile,
        out_shape=jax.ShapeDtypeStruct((batch, seq, width), BF16),
        scratch_shapes=[pltpu.VMEM((2 * ATT_GROUPS * ATT_BIAS_TILES, ATT_GROUP * GRID_W, 2 * GRID_W), F32)],
        compiler_params=pltpu.CompilerParams(
            dimension_semantics=("arbitrary", "arbitrary"), vmem_limit_bytes=VMEM_LIMIT_BYTES),
        name="nbr_attention",
    )(q, k, v, g, gen, win)


def _hgrn_fine_levels():
    sizes = []
    c = HG_GROUP
    while c >= 2:
        sizes.append(c)
        c //= 2
    return sizes


def _hgrn_group_pairs(direction, chunk):
    n = chunk // HG_GROUP
    if direction == 0:
        return [(t, g) for t in range(1, n) for g in range(t)]
    return [(t, g) for t in range(n - 2, -1, -1) for g in range(t + 1, n)]


def _hgrn_constants(chunk):
    t = np.arange(chunk)[:, None]
    u = np.arange(chunk)[None, :]
    blocks = [(u <= t), (u > t)]
    masks = []
    for c in _hgrn_fine_levels():
        half = c // 2
        mid = (t // c) * c + half
        upper = (t % c) >= half
        if c > 2:
            blocks.append(np.where(upper, (u >= mid) & (u <= t), (u > t) & (u <= mid - 1)))
        masks.append((t // c == u // c) & upper & ((u % c) < half))
    fwd_sums = np.stack(blocks).astype(np.float32)
    fwd_masks = np.stack(masks).astype(np.float32)
    twice = lambda m: np.concatenate([m.reshape(-1, chunk)] * 2, axis=1)
    sums = np.stack([twice(fwd_sums), twice(fwd_sums[:, ::-1, ::-1])])
    masks = np.concatenate([fwd_masks, fwd_masks[:, ::-1, ::-1]], axis=2)
    return sums, masks


def _block_diag(a, b):
    zero = jnp.zeros_like(a)
    return jnp.concatenate([jnp.concatenate([a, zero], axis=1),
                            jnp.concatenate([zero, b], axis=1)], axis=0)


def _hgrn_dec_rows(chunk):
    n_var = len(_hgrn_group_pairs(0, chunk)) * HG_GROUP
    n_mm = sum(c > 2 for c in _hgrn_fine_levels())
    edge, var, fine = 2 * chunk, 3 * chunk, 3 * chunk + n_var
    return edge, var, fine, fine + n_mm * chunk


def _hgrn_kernel(q_ref, zf_ref, zb_ref, i_ref, g_ref, lbf_ref, lbb_ref, gain_ref,
                 sums_ref, masks_ref, o_ref, of_scr, ob_scr, sf_scr, sb_scr,
                 k_scr0, k_scr1, dec_scr0, dec_scr1, *, layer, seq):
    chunk = HG_CHUNK
    pair = 2 * chunk
    n_pairs = seq // pair
    dh = HG_HEAD_DIM
    grp = HG_GROUP
    n_groups = chunk // grp
    fine = _hgrn_fine_levels()
    group_pairs = [_hgrn_group_pairs(direction, chunk) for direction in (0, 1)]
    r_edge, r_var, r_fine, _ = _hgrn_dec_rows(chunk)
    k_slots = (k_scr0, k_scr1)
    dec_slots = (dec_scr0, dec_scr1)
    row_in_chunk = lax.broadcasted_iota(jnp.int32, (chunk, dh), 0)
    lane_group = lax.broadcasted_iota(jnp.int32, (grp, 2 * chunk), 1) // grp

    def lower_bound(logit_ref):
        logits = logit_ref[...].astype(F32)
        e = jnp.exp(logits - jnp.max(logits, axis=0, keepdims=True))
        return jnp.sum(e[:layer + 1], axis=0, keepdims=True) / jnp.sum(e, axis=0, keepdims=True)

    def gates(z, lb):
        f = lb + (1.0 - lb) * jax.nn.sigmoid(z.astype(F32))
        return jnp.log2(f), 1.0 - f

    def store_decays(slot, direction, g2):
        hi = g2.astype(BF16)
        lo = (g2 - hi.astype(F32)).astype(BF16)
        stacked = [jnp.concatenate([hi[c * chunk:(c + 1) * chunk], lo[c * chunk:(c + 1) * chunk]], axis=0)
                   for c in range(2)]
        raw = _dot(sums_ref[direction], jnp.concatenate(stacked, axis=1))
        dec = dec_slots[slot].at[direction]
        dec[0:2 * chunk, :] = jnp.exp2(raw[0:2 * chunk])
        dec[r_fine:, :] = jnp.exp2(raw[2 * chunk:])
        cum = raw[0:chunk]
        edge = grp - 1 if direction == 0 else 0
        edges = [cum[g * grp + edge:g * grp + edge + 1] for g in range(n_groups)]
        group = lambda g: cum[g * grp:(g + 1) * grp]
        dec[r_edge:r_var, :] = jnp.exp2(jnp.concatenate([edges[g] - group(g) for g in range(n_groups)], axis=0))
        dec[r_var:r_fine, :] = jnp.exp2(jnp.concatenate([group(t) - edges[g]
                                                         for t, g in group_pairs[direction]], axis=0))

    def fine_operand(direction, lvl, q, k, dec):
        c = fine[lvl]
        pos = row_in_chunk % c
        is_upper = (pos >= c // 2) if direction == 0 else (pos < c // 2)
        if c == 2:
            return jnp.where(is_upper, q * (1.0 - k), k).astype(BF16)
        return (jnp.where(is_upper, q, k) * dec(r_fine + lvl * chunk, r_fine + (lvl + 1) * chunk)).astype(BF16)

    def chunk_part(slot, direction, c, q, v, base):
        rows = slice(c * chunk, (c + 1) * chunk)
        lanes = slice(c * dh, (c + 1) * dh)
        return dict(q=q[rows], k=k_slots[slot][direction, rows, :], v=v[rows],
                    dec=lambda r0, r1: dec_slots[slot][direction, r0:r1, lanes],
                    rows=pl.ds(pl.multiple_of(base + c * chunk, chunk), chunk))

    def pair_step(f, b):
        lhs = jnp.concatenate(
            [(jnp.concatenate([p["q"][t * grp:(t + 1) * grp] for t, _ in group_pairs[d]], axis=0)
              * p["dec"](r_var, r_fine)).astype(BF16) for d, p in enumerate((f, b))], axis=1)
        keys = [(p["k"] * p["dec"](r_edge, r_var)).astype(BF16) for p in (f, b)]
        r = _dot_nt(lhs, _block_diag(*keys))
        rows_of = [jnp.zeros((grp, 2 * chunk), F32)] * n_groups
        for i, ((tf, gf), (tb, gb)) in enumerate(zip(*group_pairs)):
            block = r[i * grp:(i + 1) * grp]
            rows_of[tf] = jnp.where(lane_group == gf, block, rows_of[tf])
            rows_of[tb] = jnp.where(lane_group == n_groups + gb, block, rows_of[tb])
        a = jnp.concatenate(rows_of, axis=0)
        for lvl in range(len(fine)):
            xf = fine_operand(0, lvl, f["q"], f["k"], f["dec"])
            xb = fine_operand(1, lvl, b["q"], b["k"], b["dec"])
            a = a + masks_ref[lvl] * _dot_nt(jnp.concatenate([xf, xb], axis=1), _block_diag(xf, xb))
        o = _dot(a.astype(BF16), _block_diag(f["v"], b["v"]))
        sf = sf_scr[...]
        sb = sb_scr[...]
        q_dec = jnp.concatenate([(f["q"] * f["dec"](0, chunk)).astype(BF16),
                                 (b["q"] * b["dec"](0, chunk)).astype(BF16)], axis=1)
        o = o + _dot_nt(q_dec, _block_diag(sf.astype(BF16), sb.astype(BF16)))
        diag_f = jnp.sum(f["q"] * f["k"], axis=-1, keepdims=True)
        diag_b = jnp.sum(b["q"] * b["k"], axis=-1, keepdims=True)
        of_scr[f["rows"], :] = o[:, :dh] + diag_f * f["v"].astype(F32)
        ob_scr[b["rows"], :] = o[:, dh:] + diag_b * b["v"].astype(F32)
        k_dec = _block_diag((f["k"] * f["dec"](chunk, 2 * chunk)).astype(BF16),
                            (b["k"] * b["dec"](chunk, 2 * chunk)).astype(BF16))
        v_t = jnp.concatenate([f["v"], b["v"]], axis=0).astype(F32).T.astype(BF16)
        u = _dot(v_t, k_dec)
        sf_scr[...] = sf * f["dec"](chunk - 1, chunk) + u[:, :dh]
        sb_scr[...] = sb * b["dec"](0, 1) + u[:, dh:]

    lb_f = lower_bound(lbf_ref)
    lb_b = lower_bound(lbb_ref)
    sf_scr[...] = jnp.zeros_like(sf_scr)
    sb_scr[...] = jnp.zeros_like(sb_scr)

    def bases(n):
        return (pl.multiple_of(n * pair, pair), pl.multiple_of((n_pairs - 1 - n) * pair, pair))

    def prepare(n, slot):
        for direction, base, z_ref, lb in zip((0, 1), bases(n), (zf_ref, zb_ref), (lb_f, lb_b)):
            g2, k = gates(z_ref[0, pl.ds(base, pair), :], lb)
            k_slots[slot][direction] = k
            store_decays(slot, direction, g2)

    def consume(n, slot):
        parts = []
        for direction, base in zip((0, 1), bases(n)):
            rows = pl.ds(base, pair)
            q = q_ref[0, rows, :].astype(F32)
            v = i_ref[0, rows, :]
            parts.append([chunk_part(slot, direction, c, q, v, base) for c in range(2)])
        pair_step(parts[0][0], parts[1][1])
        pair_step(parts[0][1], parts[1][0])

    def scan_body(m, carry):
        prepare(2 * m + 1, 1)
        consume(2 * m, 0)
        prepare(jnp.minimum(2 * m + 2, n_pairs - 1), 0)
        consume(2 * m + 1, 1)
        return carry

    prepare(0, 0)
    lax.fori_loop(0, n_pairs // 2, scan_body, 0, unroll=4)

    gain = gain_ref[...].astype(F32)

    def norm_body(n, carry):
        rows = pl.ds(pl.multiple_of(n * HG_NORM_ROWS, HG_NORM_ROWS), HG_NORM_ROWS)
        o = of_scr[rows, :] + ob_scr[rows, :]
        o = o * lax.rsqrt(jnp.mean(jnp.square(o), axis=-1, keepdims=True) + RMS_EPS) * gain
        gate = g_ref[0, rows, :].astype(F32)
        o_ref[0, rows, :] = (o * _silu_from_half(gate)).astype(o_ref.dtype)
        return carry

    lax.fori_loop(0, seq // HG_NORM_ROWS, norm_body, 0, unroll=4)


def _hgrn(q, zf, zb, i, g, lb_fwd_logits, lb_bwd_logits, gain_row, layer):
    batch, seq, width = q.shape
    assert width == HG_HEADS * HG_HEAD_DIM and seq % (16 * HG_CHUNK) == 0 and seq % (4 * HG_NORM_ROWS) == 0
    sums, masks = _hgrn_constants(HG_CHUNK)
    sums = jnp.asarray(sums, BF16)
    masks = jnp.asarray(masks, F32)
    n_layers = lb_fwd_logits.shape[0]
    n_dec_rows = _hgrn_dec_rows(HG_CHUNK)[-1]
    head = pl.BlockSpec((1, seq, HG_HEAD_DIM), lambda b, h: (b, 0, h))
    per_head_row = lambda n: pl.BlockSpec((n, HG_HEAD_DIM), lambda b, h: (0, h))
    whole = lambda a: pl.BlockSpec(a.shape, lambda b, h: (0,) * a.ndim)
    return pl.pallas_call(
        functools.partial(_hgrn_kernel, layer=layer, seq=seq),
        grid=(batch, HG_HEADS),
        in_specs=[head, head, head, head, head,
                  per_head_row(n_layers), per_head_row(n_layers), per_head_row(1),
                  whole(sums), whole(masks)],
        out_specs=head,
        out_shape=jax.ShapeDtypeStruct((batch, seq, width), BF16),
        scratch_shapes=[pltpu.VMEM((seq, HG_HEAD_DIM), F32), pltpu.VMEM((seq, HG_HEAD_DIM), F32),
                        pltpu.VMEM((HG_HEAD_DIM, HG_HEAD_DIM), F32),
                        pltpu.VMEM((HG_HEAD_DIM, HG_HEAD_DIM), F32),
                        pltpu.VMEM((2, 2 * HG_CHUNK, HG_HEAD_DIM), F32),
                        pltpu.VMEM((2, 2 * HG_CHUNK, HG_HEAD_DIM), F32),
                        pltpu.VMEM((2, n_dec_rows, 2 * HG_HEAD_DIM), F32),
                        pltpu.VMEM((2, n_dec_rows, 2 * HG_HEAD_DIM), F32)],
        compiler_params=pltpu.CompilerParams(
            dimension_semantics=("arbitrary", "arbitrary"), vmem_limit_bytes=VMEM_LIMIT_BYTES),
        name="hgrn2_scan",
    )(q, zf, zb, i, g, lb_fwd_logits, lb_bwd_logits, gain_row, sums, masks)


def _out_proj_kernel(oa_ref, oh_ref, x_ref, w_ref, b_ref, gain_ref, bias_ref, o_ref, w_scr, *, alpha):
    d_att = oa_ref.shape[1]
    _cast_weight_once(w_ref, w_scr)
    for r0 in range(0, o_ref.shape[0], OUT_PROJ_ROWS):
        rows = slice(r0, r0 + OUT_PROJ_ROWS)
        y = _dot(oa_ref[rows, :], w_scr[:d_att, :]) + _dot(oh_ref[rows, :], w_scr[d_att:, :]) + b_ref[...]
        r = alpha * x_ref[rows, :] + y
        mu = jnp.mean(r, axis=-1, keepdims=True)
        c = r - mu
        var = jnp.mean(jnp.square(c), axis=-1, keepdims=True)
        o_ref[rows, :] = (c * lax.rsqrt(var + LN_EPS) * gain_ref[...] + bias_ref[...]).astype(o_ref.dtype)


def _out_proj(o_a, o_h, x2d, w, b_row, gain_row, bias_row, alpha):
    n_tok, d_model = x2d.shape
    d_att, d_hg = o_a.shape[1], o_h.shape[1]
    row = pl.BlockSpec((1, d_model), lambda i: (0, 0))
    return pl.pallas_call(
        functools.partial(_out_proj_kernel, alpha=alpha),
        grid=(n_tok // OUT_PROJ_TM,),
        in_specs=[pl.BlockSpec((OUT_PROJ_TM, d_att), lambda i: (i, 0)),
                  pl.BlockSpec((OUT_PROJ_TM, d_hg), lambda i: (i, 0)),
                  pl.BlockSpec((OUT_PROJ_TM, d_model), lambda i: (i, 0)),
                  pl.BlockSpec((d_att + d_hg, d_model), lambda i: (0, 0), pipeline_mode=pl.Buffered(1)),
                  row, row, row],
        out_specs=pl.BlockSpec((OUT_PROJ_TM, d_model), lambda i: (i, 0)),
        out_shape=jax.ShapeDtypeStruct((n_tok, d_model), x2d.dtype),
        scratch_shapes=[pltpu.VMEM((d_att + d_hg, d_model), BF16)],
        compiler_params=pltpu.CompilerParams(
            dimension_semantics=("arbitrary",), vmem_limit_bytes=VMEM_LIMIT_BYTES),
        name="out_proj_layernorm",
    )(o_a, o_h, x2d, w, b_row, gain_row, bias_row)


def _layer(x, layer, depth, w_in, b_in, rpb, lb_fwd_logits, lb_bwd_logits, hg_norm_gain,
           w_out, b_out, ln_gain, ln_bias):
    batch, seq, d_model = x.shape
    x2d = x.reshape(batch * seq, d_model)
    slabs = _in_proj(x2d, w_in, b_in.reshape(1, -1))
    q_a, k_a, v_a, g_a, q_h, z_f, z_b, i_h, g_h = [s.reshape(batch, seq, SLAB) for s in slabs]
    o_a = _attention(q_a, k_a, v_a, g_a, rpb)
    o_h = _hgrn(q_h, z_f, z_b, i_h, g_h, lb_fwd_logits, lb_bwd_logits,
                hg_norm_gain.reshape(1, -1), layer)
    alpha = (2.0 * depth) ** 0.25
    out = _out_proj(o_a.reshape(batch * seq, SLAB), o_h.reshape(batch * seq, SLAB), x2d,
                    w_out, b_out.reshape(1, -1), ln_gain.reshape(1, -1),
                    ln_bias.reshape(1, -1), alpha)
    return out.reshape(batch, seq, d_model)


def kernel(x, w_in, b_in, rpb, lb_fwd_logits, lb_bwd_logits, hg_norm_gain, w_out, b_out, ln_gain, ln_bias):
    depth = w_in.shape[0]
    for layer in range(depth):
        x = _layer(x, layer, depth, w_in[layer], b_in[layer], rpb[layer], lb_fwd_logits,
                   lb_bwd_logits, hg_norm_gain[layer], w_out[layer], b_out[layer],
                   ln_gain[layer], ln_bias[layer])
    return x
```

```python
import functools

import numpy as np
import jax
import jax.numpy as jnp
from jax import lax
from jax.experimental import pallas as pl
from jax.experimental.pallas import tpu as pltpu

GRID_W = 64
ATT_HEADS = 8
ATT_HEAD_DIM = 64
HG_HEADS = 4
HG_HEAD_DIM = 128
WIN_ROWS = 8
WIN_COLS = 16
LN_EPS = 1e-5
RMS_EPS = 1e-6
N_SLABS = 9
SLAB = 512

LANES = 128
F32_SUBLANES = 8
VMEM_LIMIT_BYTES = 56 * 1024 * 1024
PROJ_TM = 512
OUT_PROJ_TM = 1024
OUT_PROJ_ROWS = 256
ATT_ROWS_PER_STEP = 16
ATT_UNROLL = 4
ATT_GROUP = 4
HG_CHUNK = 64
HG_GROUP = F32_SUBLANES
HG_NORM_ROWS = 256
MASK_VALUE = -1e30
LOG2_E = 1.4426950408889634
SLAB_SCALES = {1: ATT_HEAD_DIM ** -0.5 * LOG2_E, 3: 0.5, 8: 0.5}

BF16 = jnp.bfloat16
F32 = jnp.float32


def _silu_from_half(half_g):
    return half_g * (1.0 + jnp.tanh(half_g))


def _dot(a, b):
    return jnp.dot(a, b, preferred_element_type=F32)


def _dot_nt(a, b):
    return lax.dot_general(a, b, (((1,), (1,)), ((), ())), preferred_element_type=F32)


def _cast_weight_once(w_ref, w_scr):
    @pl.when(pl.program_id(0) == 0)
    def _():
        for c0 in range(0, w_ref.shape[1], SLAB):
            w_scr[:, c0:c0 + SLAB] = w_ref[:, c0:c0 + SLAB].astype(w_scr.dtype)


def _in_proj_kernel(x_ref, w_ref, b_ref, *refs):
    out_refs, w_scr = refs[:-1], refs[-1]
    _cast_weight_once(w_ref, w_scr)
    xb = x_ref[...].astype(BF16)
    for j, o_ref in enumerate(out_refs):
        cols = slice(j * SLAB, (j + 1) * SLAB)
        h = _dot(xb, w_scr[:, cols]) + b_ref[:, cols]
        if j in SLAB_SCALES:
            h = h * SLAB_SCALES[j]
        o_ref[...] = h.astype(o_ref.dtype)


def _in_proj(x2d, w, b_row):
    n_tok, d_model = x2d.shape
    d_in = w.shape[1]
    assert d_in == N_SLABS * SLAB and n_tok % PROJ_TM == 0
    return pl.pallas_call(
        _in_proj_kernel,
        grid=(n_tok // PROJ_TM,),
        in_specs=[
            pl.BlockSpec((PROJ_TM, d_model), lambda i: (i, 0)),
            pl.BlockSpec((d_model, d_in), lambda i: (0, 0), pipeline_mode=pl.Buffered(1)),
            pl.BlockSpec((1, d_in), lambda i: (0, 0)),
        ],
        out_specs=[pl.BlockSpec((PROJ_TM, SLAB), lambda i: (i, 0))] * N_SLABS,
        out_shape=[jax.ShapeDtypeStruct((n_tok, SLAB), BF16)] * N_SLABS,
        scratch_shapes=[pltpu.VMEM((d_model, d_in), BF16)],
        compiler_params=pltpu.CompilerParams(
            dimension_semantics=("arbitrary",), vmem_limit_bytes=VMEM_LIMIT_BYTES),
        name="in_proj",
    )(x2d, w, b_row)


ATT_REL_ROWS = 2 * WIN_ROWS - 1
ATT_BIAS_TILES = (ATT_REL_ROWS - 1) // 2
ATT_GROUPS = ATT_HEADS // ATT_GROUP


def _attention_bias_rows(rpb):
    n_rel_cols = 2 * WIN_COLS - 1
    m = np.arange(LANES)
    first = np.where(m < WIN_COLS, m + WIN_COLS - 1, m - (LANES - WIN_COLS + 1))
    first_ok = (m < WIN_COLS) | (m > LANES - WIN_COLS)
    second = m - (GRID_W - WIN_COLS + 1)
    second_ok = (second >= 0) & (second < n_rel_cols)
    col = np.where(first_ok, first, np.where(second_ok, second, 0))
    p = np.arange(2)[:, None, None]
    t = np.arange(ATT_BIAS_TILES)[None, :, None]
    rel = 2 * t + p + np.where(first_ok, 0, 1)[None, None, :]
    gen = jnp.where(first_ok | second_ok, rpb.astype(F32)[:, rel, col[None, None, :]], 0.0) * LOG2_E
    gen = gen.reshape(ATT_GROUPS, ATT_GROUP, 2, ATT_BIAS_TILES, LANES).transpose(2, 0, 3, 1, 4)
    return gen.reshape(2 * ATT_GROUPS * ATT_BIAS_TILES * ATT_GROUP, 1, LANES)


def _attention_window_mask():
    qc = np.arange(GRID_W)[:, None]
    kc = np.arange(LANES)[None, :] % GRID_W
    col_start = np.clip(qc - WIN_COLS // 2, 0, GRID_W - WIN_COLS)
    return ((kc >= col_start) & (kc < col_start + WIN_COLS)).astype(np.float32)


def _attention_kernel(q_ref, k_ref, v_ref, g_ref, gen_ref, win_ref, o_ref, bias_scr, *, rows, kr):
    step = pl.program_id(1)
    gw = ATT_GROUP * ATT_HEAD_DIM
    row_head = lax.broadcasted_iota(jnp.int32, (ATT_GROUP * GRID_W, gw), 0) // GRID_W
    lane_head = lax.broadcasted_iota(jnp.int32, (ATT_GROUP * GRID_W, gw), 1) // ATT_HEAD_DIM
    own_head = row_head == lane_head
    out_lane_head = lax.broadcasted_iota(jnp.int32, (GRID_W, gw), 1) // ATT_HEAD_DIM

    @pl.when((pl.program_id(0) == 0) & (step == 0))
    def _build_bias():
        in_window = win_ref[...] > 0.0

        def one_block(n, carry):
            gen = jnp.broadcast_to(gen_ref[n], (GRID_W, LANES))
            toeplitz = pltpu.roll(gen, 0, 1, stride=1, stride_axis=0)
            head_rows = pl.ds(pl.multiple_of((n % ATT_GROUP) * GRID_W, GRID_W), GRID_W)
            bias_scr[n // ATT_GROUP, head_rows, :] = jnp.where(in_window, toeplitz, MASK_VALUE)
            return carry

        lax.fori_loop(0, gen_ref.shape[0], one_block, 0, unroll=8)

    def one_row(j, carry):
        r = step * ATT_ROWS_PER_STEP + j
        row_start = jnp.clip(r - kr // 2, 0, rows - kr)
        variant = row_start - r + (WIN_ROWS - 1)
        q_tok = pl.multiple_of(j * GRID_W, GRID_W)
        k_tok = pl.multiple_of(row_start * GRID_W, GRID_W)
        for grp in range(ATT_GROUPS):
            lanes = slice(grp * gw, (grp + 1) * gw)
            q = q_ref[0, pl.ds(q_tok, GRID_W), lanes]
            q_bd = jnp.where(own_head, jnp.concatenate([q] * ATT_GROUP, axis=0), 0)
            keys = k_ref[0, pl.ds(k_tok, kr * GRID_W), lanes]
            vals = v_ref[0, pl.ds(k_tok, kr * GRID_W), lanes]
            tile0 = ((variant % 2) * ATT_GROUPS + grp) * ATT_BIAS_TILES + variant // 2
            bias = jnp.concatenate([bias_scr[tile0 + t] for t in range(kr // 2)], axis=1)
            s = _dot_nt(q_bd, keys) + bias
            m = jnp.max(s, axis=-1, keepdims=True)
            p = jnp.exp2(s - m)
            denom = jnp.sum(p, axis=-1, keepdims=True)
            pv = _dot(p.astype(BF16), vals) / denom
            o = jnp.zeros((GRID_W, gw), F32)
            for h in range(ATT_GROUP):
                o = o + jnp.where(out_lane_head == h, pv[h * GRID_W:(h + 1) * GRID_W], 0.0)
            gate = g_ref[0, pl.ds(q_tok, GRID_W), lanes].astype(F32)
            o = o * _silu_from_half(gate)
            o_ref[0, pl.ds(q_tok, GRID_W), lanes] = o.astype(o_ref.dtype)
        return carry

    lax.fori_loop(0, ATT_ROWS_PER_STEP, one_row, 0, unroll=ATT_UNROLL)


def _attention(q, k, v, g, rpb):
    batch, seq, width = q.shape
    rows = seq // GRID_W
    kr = min(WIN_ROWS, rows)
    assert rows % ATT_ROWS_PER_STEP == 0 and ATT_ROWS_PER_STEP % ATT_UNROLL == 0
    assert kr == WIN_ROWS and 2 * GRID_W == LANES
    gen = _attention_bias_rows(rpb)
    win = jnp.asarray(_attention_window_mask())
    blk = ATT_ROWS_PER_STEP * GRID_W
    tile = pl.BlockSpec((1, blk, width), lambda b, i: (b, i, 0))
    whole = pl.BlockSpec((1, seq, width), lambda b, i: (b, 0, 0))
    return pl.pallas_call(
        functools.partial(_attention_kernel, rows=rows, kr=kr),
        grid=(batch, rows // ATT_ROWS_PER_STEP),
        in_specs=[tile, whole, whole, tile,
                  pl.BlockSpec(gen.shape, lambda b, i: (0, 0, 0)),
                  pl.BlockSpec(win.shape, lambda b, i: (0, 0))],
        out_specs=tile,
        out_shape=jax.ShapeDtypeStruct((batch, seq, width), BF16),
        scratch_shapes=[pltpu.VMEM((2 * ATT_GROUPS * ATT_BIAS_TILES, ATT_GROUP * GRID_W, 2 * GRID_W), F32)],
        compiler_params=pltpu.CompilerParams(
            dimension_semantics=("arbitrary", "arbitrary"), vmem_limit_bytes=VMEM_LIMIT_BYTES),
        name="nbr_attention",
    )(q, k, v, g, gen, win)


def _hgrn_fine_levels():
    sizes = []
    c = HG_GROUP
    while c >= 2:
        sizes.append(c)
        c //= 2
    return sizes


def _hgrn_group_pairs(direction, chunk):
    n = chunk // HG_GROUP
    if direction == 0:
        return [(t, g) for t in range(1, n) for g in range(t)]
    return [(t, g) for t in range(n - 2, -1, -1) for g in range(t + 1, n)]


def _hgrn_constants(chunk):
    t = np.arange(chunk)[:, None]
    u = np.arange(chunk)[None, :]
    blocks = [(u <= t), (u > t)]
    masks = []
    for c in _hgrn_fine_levels():
        half = c // 2
        mid = (t // c) * c + half
        upper = (t % c) >= half
        if c > 2:
            blocks.append(np.where(upper, (u >= mid) & (u <= t), (u > t) & (u <= mid - 1)))
        masks.append((t // c == u // c) & upper & ((u % c) < half))
    fwd_sums = np.stack(blocks).astype(np.float32)
    fwd_masks = np.stack(masks).astype(np.float32)
    twice = lambda m: np.concatenate([m.reshape(-1, chunk)] * 2, axis=1)
    sums = np.stack([twice(fwd_sums), twice(fwd_sums[:, ::-1, ::-1])])
    masks = np.concatenate([fwd_masks, fwd_masks[:, ::-1, ::-1]], axis=2)
    return sums, masks


def _block_diag(a, b):
    zero = jnp.zeros_like(a)
    return jnp.concatenate([jnp.concatenate([a, zero], axis=1),
                            jnp.concatenate([zero, b], axis=1)], axis=0)


def _hgrn_dec_rows(chunk):
    n_var = len(_hgrn_group_pairs(0, chunk)) * HG_GROUP
    n_mm = sum(c > 2 for c in _hgrn_fine_levels())
    edge, var, fine = 2 * chunk, 3 * chunk, 3 * chunk + n_var
    return edge, var, fine, fine + n_mm * chunk


def _hgrn_kernel(q_ref, zf_ref, zb_ref, i_ref, g_ref, lbf_ref, lbb_ref, gain_ref,
                 sums_ref, masks_ref, o_ref, of_scr, ob_scr, sf_scr, sb_scr,
                 k_scr0, k_scr1, dec_scr0, dec_scr1, *, layer, seq):
    chunk = HG_CHUNK
    pair = 2 * chunk
    n_pairs = seq // pair
    dh = HG_HEAD_DIM
    grp = HG_GROUP
    n_groups = chunk // grp
    fine = _hgrn_fine_levels()
    group_pairs = [_hgrn_group_pairs(direction, chunk) for direction in (0, 1)]
    r_edge, r_var, r_fine, _ = _hgrn_dec_rows(chunk)
    k_slots = (k_scr0, k_scr1)
    dec_slots = (dec_scr0, dec_scr1)
    row_in_group = lax.broadcasted_iota(jnp.int32, (grp, dh), 0)
    lane_group = lax.broadcasted_iota(jnp.int32, (grp, 2 * chunk), 1) // grp

    def lower_bound(logit_ref):
        logits = logit_ref[...].astype(F32)
        e = jnp.exp(logits - jnp.max(logits, axis=0, keepdims=True))
        return jnp.sum(e[:layer + 1], axis=0, keepdims=True) / jnp.sum(e, axis=0, keepdims=True)

    def gates(z, lb):
        f = lb + (1.0 - lb) * jax.nn.sigmoid(z.astype(F32))
        return jnp.log2(f), 1.0 - f

    def store_decays(slot, direction, g2):
        hi = g2.astype(BF16)
        lo = (g2 - hi.astype(F32)).astype(BF16)
        stacked = [jnp.concatenate([hi[c * chunk:(c + 1) * chunk], lo[c * chunk:(c + 1) * chunk]], axis=0)
                   for c in range(2)]
        raw = _dot(sums_ref[direction], jnp.concatenate(stacked, axis=1))
        dec = dec_slots[slot].at[direction]
        dec[0:2 * chunk, :] = jnp.exp2(raw[0:2 * chunk])
        dec[r_fine:, :] = jnp.exp2(raw[2 * chunk:])
        cum = raw[0:chunk]
        edge = grp - 1 if direction == 0 else 0
        edges = [cum[g * grp + edge:g * grp + edge + 1] for g in range(n_groups)]
        group = lambda g: cum[g * grp:(g + 1) * grp]
        dec[r_edge:r_var, :] = jnp.exp2(jnp.concatenate([edges[g] - group(g) for g in range(n_groups)], axis=0))
        dec[r_var:r_fine, :] = jnp.exp2(jnp.concatenate([group(t) - edges[g]
                                                         for t, g in group_pairs[direction]], axis=0))

    def fine_operand(direction, lvl, q, k, dec):
        c = fine[lvl]
        pos = row_in_group % c
        is_upper = (pos >= c // 2) if direction == 0 else (pos < c // 2)
        pick = lambda x, y: jnp.where(is_upper[None], x.reshape(n_groups, grp, dh),
                                      y.reshape(n_groups, grp, dh)).reshape(chunk, dh)
        if c == 2:
            return pick(q * (1.0 - k), k).astype(BF16)
        return (pick(q, k) * dec(r_fine + lvl * chunk, r_fine + (lvl + 1) * chunk)).astype(BF16)

    def chunk_part(slot, direction, c, q, v, base):
        rows = slice(c * chunk, (c + 1) * chunk)
        lanes = slice(c * dh, (c + 1) * dh)
        return dict(q=q[rows], k=k_slots[slot][direction, rows, :], v=v[rows],
                    dec=lambda r0, r1: dec_slots[slot][direction, r0:r1, lanes],
                    rows=pl.ds(pl.multiple_of(base + c * chunk, chunk), chunk))

    def pair_step(f, b):
        lhs = jnp.concatenate(
            [(jnp.concatenate([p["q"][t * grp:(t + 1) * grp] for t, _ in group_pairs[d]], axis=0)
              * p["dec"](r_var, r_fine)).astype(BF16) for d, p in enumerate((f, b))], axis=1)
        keys = [(p["k"] * p["dec"](r_edge, r_var)).astype(BF16) for p in (f, b)]
        r = _dot_nt(lhs, _block_diag(*keys))
        rows_of = [jnp.zeros((grp, 2 * chunk), F32)] * n_groups
        for i, ((tf, gf), (tb, gb)) in enumerate(zip(*group_pairs)):
            block = r[i * grp:(i + 1) * grp]
            rows_of[tf] = jnp.where(lane_group == gf, block, rows_of[tf])
            rows_of[tb] = jnp.where(lane_group == n_groups + gb, block, rows_of[tb])
        a = jnp.concatenate(rows_of, axis=0)
        for lvl in range(len(fine)):
            xf = fine_operand(0, lvl, f["q"], f["k"], f["dec"])
            xb = fine_operand(1, lvl, b["q"], b["k"], b["dec"])
            a = a + masks_ref[lvl] * _dot_nt(jnp.concatenate([xf, xb], axis=1), _block_diag(xf, xb))
        o = _dot(a.astype(BF16), _block_diag(f["v"], b["v"]))
        sf = sf_scr[...]
        sb = sb_scr[...]
        q_dec = jnp.concatenate([(f["q"] * f["dec"](0, chunk)).astype(BF16),
                                 (b["q"] * b["dec"](0, chunk)).astype(BF16)], axis=1)
        o = o + _dot_nt(q_dec, _block_diag(sf.astype(BF16), sb.astype(BF16)))
        diag_f = jnp.sum(f["q"] * f["k"], axis=-1, keepdims=True)
        diag_b = jnp.sum(b["q"] * b["k"], axis=-1, keepdims=True)
        of_scr[f["rows"], :] = o[:, :dh] + diag_f * f["v"].astype(F32)
        ob_scr[b["rows"], :] = o[:, dh:] + diag_b * b["v"].astype(F32)
        k_dec = _block_diag((f["k"] * f["dec"](chunk, 2 * chunk)).astype(BF16),
                            (b["k"] * b["dec"](chunk, 2 * chunk)).astype(BF16))
        v_t = jnp.concatenate([f["v"], b["v"]], axis=0).astype(F32).T.astype(BF16)
        u = _dot(v_t, k_dec)
        sf_scr[...] = sf * f["dec"](chunk - 1, chunk) + u[:, :dh]
        sb_scr[...] = sb * b["dec"](0, 1) + u[:, dh:]

    lb_f = lower_bound(lbf_ref)
    lb_b = lower_bound(lbb_ref)
    sf_scr[...] = jnp.zeros_like(sf_scr)
    sb_scr[...] = jnp.zeros_like(sb_scr)

    def bases(n):
        return (pl.multiple_of(n * pair, pair), pl.multiple_of((n_pairs - 1 - n) * pair, pair))

    def prepare(n, slot):
        for direction, base, z_ref, lb in zip((0, 1), bases(n), (zf_ref, zb_ref), (lb_f, lb_b)):
            g2, k = gates(z_ref[0, pl.ds(base, pair), :], lb)
            k_slots[slot][direction] = k
            store_decays(slot, direction, g2)

    def consume(n, slot):
        parts = []
        for direction, base in zip((0, 1), bases(n)):
            rows = pl.ds(base, pair)
            q = q_ref[0, rows, :].astype(F32)
            v = i_ref[0, rows, :]
            parts.append([chunk_part(slot, direction, c, q, v, base) for c in range(2)])
        pair_step(parts[0][0], parts[1][1])
        pair_step(parts[0][1], parts[1][0])

    def scan_body(m, carry):
        prepare(2 * m + 1, 1)
        consume(2 * m, 0)
        prepare(jnp.minimum(2 * m + 2, n_pairs - 1), 0)
        consume(2 * m + 1, 1)
        return carry

    prepare(0, 0)
    lax.fori_loop(0, n_pairs // 2, scan_body, 0, unroll=4)

    gain = gain_ref[...].astype(F32)

    def norm_body(n, carry):
        rows = pl.ds(pl.multiple_of(n * HG_NORM_ROWS, HG_NORM_ROWS), HG_NORM_ROWS)
        o = of_scr[rows, :] + ob_scr[rows, :]
        o = o * lax.rsqrt(jnp.mean(jnp.square(o), axis=-1, keepdims=True) + RMS_EPS) * gain
        gate = g_ref[0, rows, :].astype(F32)
        o_ref[0, rows, :] = (o * _silu_from_half(gate)).astype(o_ref.dtype)
        return carry

    lax.fori_loop(0, seq // HG_NORM_ROWS, norm_body, 0, unroll=4)


def _hgrn(q, zf, zb, i, g, lb_fwd_logits, lb_bwd_logits, gain_row, layer):
    batch, seq, width = q.shape
    assert width == HG_HEADS * HG_HEAD_DIM and seq % (16 * HG_CHUNK) == 0 and seq % (4 * HG_NORM_ROWS) == 0
    sums, masks = _hgrn_constants(HG_CHUNK)
    sums = jnp.asarray(sums, BF16)
    masks = jnp.asarray(masks, F32)
    n_layers = lb_fwd_logits.shape[0]
    n_dec_rows = _hgrn_dec_rows(HG_CHUNK)[-1]
    head = pl.BlockSpec((1, seq, HG_HEAD_DIM), lambda b, h: (b, 0, h))
    per_head_row = lambda n: pl.BlockSpec((n, HG_HEAD_DIM), lambda b, h: (0, h))
    whole = lambda a: pl.BlockSpec(a.shape, lambda b, h: (0,) * a.ndim)
    return pl.pallas_call(
        functools.partial(_hgrn_kernel, layer=layer, seq=seq),
        grid=(batch, HG_HEADS),
        in_specs=[head, head, head, head, head,
                  per_head_row(n_layers), per_head_row(n_layers), per_head_row(1),
                  whole(sums), whole(masks)],
        out_specs=head,
        out_shape=jax.ShapeDtypeStruct((batch, seq, width), BF16),
        scratch_shapes=[pltpu.VMEM((seq, HG_HEAD_DIM), F32), pltpu.VMEM((seq, HG_HEAD_DIM), F32),
                        pltpu.VMEM((HG_HEAD_DIM, HG_HEAD_DIM), F32),
                        pltpu.VMEM((HG_HEAD_DIM, HG_HEAD_DIM), F32),
                        pltpu.VMEM((2, 2 * HG_CHUNK, HG_HEAD_DIM), F32),
                        pltpu.VMEM((2, 2 * HG_CHUNK, HG_HEAD_DIM), F32),
                        pltpu.VMEM((2, n_dec_rows, 2 * HG_HEAD_DIM), F32),
                        pltpu.VMEM((2, n_dec_rows, 2 * HG_HEAD_DIM), F32)],
        compiler_params=pltpu.CompilerParams(
            dimension_semantics=("arbitrary", "arbitrary"), vmem_limit_bytes=VMEM_LIMIT_BYTES),
        name="hgrn2_scan",
    )(q, zf, zb, i, g, lb_fwd_logits, lb_bwd_logits, gain_row, sums, masks)


def _out_proj_kernel(oa_ref, oh_ref, x_ref, w_ref, b_ref, gain_ref, bias_ref, o_ref, w_scr, *, alpha):
    d_att = oa_ref.shape[1]
    _cast_weight_once(w_ref, w_scr)
    for r0 in range(0, o_ref.shape[0], OUT_PROJ_ROWS):
        rows = slice(r0, r0 + OUT_PROJ_ROWS)
        y = _dot(oa_ref[rows, :], w_scr[:d_att, :]) + _dot(oh_ref[rows, :], w_scr[d_att:, :]) + b_ref[...]
        r = alpha * x_ref[rows, :] + y
        mu = jnp.mean(r, axis=-1, keepdims=True)
        c = r - mu
        var = jnp.mean(jnp.square(c), axis=-1, keepdims=True)
        o_ref[rows, :] = (c * lax.rsqrt(var + LN_EPS) * gain_ref[...] + bias_ref[...]).astype(o_ref.dtype)


def _out_proj(o_a, o_h, x2d, w, b_row, gain_row, bias_row, alpha):
    n_tok, d_model = x2d.shape
    d_att, d_hg = o_a.shape[1], o_h.shape[1]
    row = pl.BlockSpec((1, d_model), lambda i: (0, 0))
    return pl.pallas_call(
        functools.partial(_out_proj_kernel, alpha=alpha),
        grid=(n_tok // OUT_PROJ_TM,),
        in_specs=[pl.BlockSpec((OUT_PROJ_TM, d_att), lambda i: (i, 0)),
                  pl.BlockSpec((OUT_PROJ_TM, d_hg), lambda i: (i, 0)),
                  pl.BlockSpec((OUT_PROJ_TM, d_model), lambda i: (i, 0)),
                  pl.BlockSpec((d_att + d_hg, d_model), lambda i: (0, 0), pipeline_mode=pl.Buffered(1)),
                  row, row, row],
        out_specs=pl.BlockSpec((OUT_PROJ_TM, d_model), lambda i: (i, 0)),
        out_shape=jax.ShapeDtypeStruct((n_tok, d_model), x2d.dtype),
        scratch_shapes=[pltpu.VMEM((d_att + d_hg, d_model), BF16)],
        compiler_params=pltpu.CompilerParams(
            dimension_semantics=("arbitrary",), vmem_limit_bytes=VMEM_LIMIT_BYTES),
        name="out_proj_layernorm",
    )(o_a, o_h, x2d, w, b_row, gain_row, bias_row)


def _layer(x, layer, depth, w_in, b_in, rpb, lb_fwd_logits, lb_bwd_logits, hg_norm_gain,
           w_out, b_out, ln_gain, ln_bias):
    batch, seq, d_model = x.shape
    x2d = x.reshape(batch * seq, d_model)
    slabs = _in_proj(x2d, w_in, b_in.reshape(1, -1))
    q_a, k_a, v_a, g_a, q_h, z_f, z_b, i_h, g_h = [s.reshape(batch, seq, SLAB) for s in slabs]
    o_a = _attention(q_a, k_a, v_a, g_a, rpb)
    o_h = _hgrn(q_h, z_f, z_b, i_h, g_h, lb_fwd_logits, lb_bwd_logits,
                hg_norm_gain.reshape(1, -1), layer)
    alpha = (2.0 * depth) ** 0.25
    out = _out_proj(o_a.reshape(batch * seq, SLAB), o_h.reshape(batch * seq, SLAB), x2d,
                    w_out, b_out.reshape(1, -1), ln_gain.reshape(1, -1),
                    ln_bias.reshape(1, -1), alpha)
    return out.reshape(batch, seq, d_model)


def kernel(x, w_in, b_in, rpb, lb_fwd_logits, lb_bwd_logits, hg_norm_gain, w_out, b_out, ln_gain, ln_bias):
    depth = w_in.shape[0]
    for layer in range(depth):
        x = _layer(x, layer, depth, w_in[layer], b_in[layer], rpb[layer], lb_fwd_logits,
                   lb_bwd_logits, hg_norm_gain[layer], w_out[layer], b_out[layer],
                   ln_gain[layer], ln_bias[layer])
    return x
```

```python
import functools

import numpy as np
import jax
import jax.numpy as jnp
from jax import lax
from jax.experimental import pallas as pl
from jax.experimental.pallas import tpu as pltpu

GRID_W = 64
ATT_HEADS = 8
ATT_HEAD_DIM = 64
HG_HEADS = 4
HG_HEAD_DIM = 128
WIN_ROWS = 8
WIN_COLS = 16
LN_EPS = 1e-5
RMS_EPS = 1e-6
N_SLABS = 9
SLAB = 512

LANES = 128
F32_SUBLANES = 8
VMEM_LIMIT_BYTES = 56 * 1024 * 1024
PROJ_TM = 512
OUT_PROJ_TM = 1024
OUT_PROJ_ROWS = 256
ATT_ROWS_PER_STEP = 16
ATT_UNROLL = 4
ATT_GROUP = 4
ATT_BIAS_UNROLL = 8
HG_CHUNK = 64
HG_GROUP = F32_SUBLANES
HG_NORM_ROWS = 256
HG_SCAN_UNROLL = 8
MASK_VALUE = -1e30
LOG2_E = 1.4426950408889634
SLAB_SCALES = {1: ATT_HEAD_DIM ** -0.5 * LOG2_E, 3: 0.5, 8: 0.5}

BF16 = jnp.bfloat16
F32 = jnp.float32


def _silu_from_half(half_g):
    return half_g * (1.0 + jnp.tanh(half_g))


def _dot(a, b):
    return jnp.dot(a, b, preferred_element_type=F32)


def _dot_nt(a, b):
    return lax.dot_general(a, b, (((1,), (1,)), ((), ())), preferred_element_type=F32)


def _cast_weight_once(w_ref, w_scr):
    @pl.when(pl.program_id(0) == 0)
    def _():
        for c0 in range(0, w_ref.shape[1], SLAB):
            w_scr[:, c0:c0 + SLAB] = w_ref[:, c0:c0 + SLAB].astype(w_scr.dtype)


def _in_proj_kernel(x_ref, w_ref, b_ref, *refs):
    out_refs, w_scr = refs[:-1], refs[-1]
    _cast_weight_once(w_ref, w_scr)
    xb = x_ref[...].astype(BF16)
    for j, o_ref in enumerate(out_refs):
        cols = slice(j * SLAB, (j + 1) * SLAB)
        h = _dot(xb, w_scr[:, cols]) + b_ref[:, cols]
        if j in SLAB_SCALES:
            h = h * SLAB_SCALES[j]
        o_ref[...] = h.astype(o_ref.dtype)


def _in_proj(x2d, w, b_row):
    n_tok, d_model = x2d.shape
    d_in = w.shape[1]
    assert d_in == N_SLABS * SLAB and n_tok % PROJ_TM == 0
    return pl.pallas_call(
        _in_proj_kernel,
        grid=(n_tok // PROJ_TM,),
        in_specs=[
            pl.BlockSpec((PROJ_TM, d_model), lambda i: (i, 0)),
            pl.BlockSpec((d_model, d_in), lambda i: (0, 0), pipeline_mode=pl.Buffered(1)),
            pl.BlockSpec((1, d_in), lambda i: (0, 0)),
        ],
        out_specs=[pl.BlockSpec((PROJ_TM, SLAB), lambda i: (i, 0))] * N_SLABS,
        out_shape=[jax.ShapeDtypeStruct((n_tok, SLAB), BF16)] * N_SLABS,
        scratch_shapes=[pltpu.VMEM((d_model, d_in), BF16)],
        compiler_params=pltpu.CompilerParams(
            dimension_semantics=("arbitrary",), vmem_limit_bytes=VMEM_LIMIT_BYTES),
        name="in_proj",
    )(x2d, w, b_row)


ATT_REL_ROWS = 2 * WIN_ROWS - 1
ATT_BIAS_TILES = (ATT_REL_ROWS - 1) // 2
ATT_GROUPS = ATT_HEADS // ATT_GROUP


def _attention_bias_rows(rpb):
    n_rel_cols = 2 * WIN_COLS - 1
    scaled = rpb.astype(F32) * LOG2_E
    gap = jnp.zeros((ATT_HEADS, ATT_BIAS_TILES, (LANES - 2 * n_rel_cols) // 2), F32)
    per_parity = []
    for p in range(2):
        first = scaled[:, p:p + 2 * ATT_BIAS_TILES - 1:2]
        second = scaled[:, p + 1:p + 2 * ATT_BIAS_TILES:2]
        per_parity.append(jnp.concatenate(
            [first[..., WIN_COLS - 1:], gap, second, gap, first[..., :WIN_COLS - 1]], axis=-1))
    gen = jnp.stack(per_parity)
    gen = gen.reshape(2, ATT_GROUPS, ATT_GROUP, ATT_BIAS_TILES, LANES).transpose(0, 1, 3, 2, 4)
    return gen.reshape(2 * ATT_GROUPS * ATT_BIAS_TILES * ATT_GROUP, 1, LANES)


def _attention_window_mask():
    qc = np.arange(GRID_W)[:, None]
    kc = np.arange(LANES)[None, :] % GRID_W
    col_start = np.clip(qc - WIN_COLS // 2, 0, GRID_W - WIN_COLS)
    return ((kc >= col_start) & (kc < col_start + WIN_COLS)).astype(np.float32)


def _attention_kernel(q_ref, k_ref, v_ref, g_ref, gen_ref, win_ref, o_ref, bias_scr, *, rows, kr):
    step = pl.program_id(1)
    gw = ATT_GROUP * ATT_HEAD_DIM
    row_head = lax.broadcasted_iota(jnp.int32, (ATT_GROUP * GRID_W, gw), 0) // GRID_W
    lane_head = lax.broadcasted_iota(jnp.int32, (ATT_GROUP * GRID_W, gw), 1) // ATT_HEAD_DIM
    own_head = row_head == lane_head
    out_lane_head = lax.broadcasted_iota(jnp.int32, (GRID_W, gw), 1) // ATT_HEAD_DIM

    @pl.when((pl.program_id(0) == 0) & (step == 0))
    def _build_bias():
        in_window = win_ref[...] > 0.0

        def one_block(n, carry):
            gen = jnp.broadcast_to(gen_ref[n], (GRID_W, LANES))
            toeplitz = pltpu.roll(gen, 0, 1, stride=1, stride_axis=0)
            head_rows = pl.ds(pl.multiple_of((n % ATT_GROUP) * GRID_W, GRID_W), GRID_W)
            bias_scr[n // ATT_GROUP, head_rows, :] = jnp.where(in_window, toeplitz, MASK_VALUE)
            return carry

        lax.fori_loop(0, gen_ref.shape[0], one_block, 0, unroll=ATT_BIAS_UNROLL)

    def one_row(j, carry):
        r = step * ATT_ROWS_PER_STEP + j
        row_start = jnp.clip(r - kr // 2, 0, rows - kr)
        variant = row_start - r + (WIN_ROWS - 1)
        q_tok = pl.multiple_of(j * GRID_W, GRID_W)
        k_tok = pl.multiple_of(row_start * GRID_W, GRID_W)
        for grp in range(ATT_GROUPS):
            lanes = slice(grp * gw, (grp + 1) * gw)
            q = q_ref[0, pl.ds(q_tok, GRID_W), lanes]
            q_bd = jnp.where(own_head, jnp.concatenate([q] * ATT_GROUP, axis=0), 0)
            keys = k_ref[0, pl.ds(k_tok, kr * GRID_W), lanes]
            vals = v_ref[0, pl.ds(k_tok, kr * GRID_W), lanes]
            tile0 = ((variant % 2) * ATT_GROUPS + grp) * ATT_BIAS_TILES + variant // 2
            bias = jnp.concatenate([bias_scr[tile0 + t] for t in range(kr // 2)], axis=1)
            s = _dot_nt(q_bd, keys) + bias
            m = jnp.max(s, axis=-1, keepdims=True)
            p = jnp.exp2(s - m)
            denom = jnp.sum(p, axis=-1, keepdims=True)
            pv = _dot(p.astype(BF16), vals) / denom
            o = jnp.zeros((GRID_W, gw), F32)
            for h in range(ATT_GROUP):
                o = o + jnp.where(out_lane_head == h, pv[h * GRID_W:(h + 1) * GRID_W], 0.0)
            gate = g_ref[0, pl.ds(q_tok, GRID_W), lanes].astype(F32)
            o = o * _silu_from_half(gate)
            o_ref[0, pl.ds(q_tok, GRID_W), lanes] = o.astype(o_ref.dtype)
        return carry

    lax.fori_loop(0, ATT_ROWS_PER_STEP, one_row, 0, unroll=ATT_UNROLL)


def _attention(q, k, v, g, rpb):
    batch, seq, width = q.shape
    rows = seq // GRID_W
    kr = min(WIN_ROWS, rows)
    assert rows % ATT_ROWS_PER_STEP == 0 and ATT_ROWS_PER_STEP % ATT_UNROLL == 0
    assert kr == WIN_ROWS and 2 * GRID_W == LANES
    gen = _attention_bias_rows(rpb)
    win = jnp.asarray(_attention_window_mask())
    blk = ATT_ROWS_PER_STEP * GRID_W
    tile = pl.BlockSpec((1, blk, width), lambda b, i: (b, i, 0))
    whole = pl.BlockSpec((1, seq, width), lambda b, i: (b, 0, 0))
    return pl.pallas_call(
        functools.partial(_attention_kernel, rows=rows, kr=kr),
        grid=(batch, rows // ATT_ROWS_PER_STEP),
        in_specs=[tile, whole, whole, tile,
                  pl.BlockSpec(gen.shape, lambda b, i: (0, 0, 0)),
                  pl.BlockSpec(win.shape, lambda b, i: (0, 0))],
        out_specs=tile,
        out_shape=jax.ShapeDtypeStruct((batch, seq, width), BF16),
        scratch_shapes=[pltpu.VMEM((2 * ATT_GROUPS * ATT_BIAS_TILES, ATT_GROUP * GRID_W, 2 * GRID_W), F32)],
        compiler_params=pltpu.CompilerParams(
            dimension_semantics=("arbitrary", "arbitrary"), vmem_limit_bytes=VMEM_LIMIT_BYTES),
        name="nbr_attention",
    )(q, k, v, g, gen, win)


def _hgrn_fine_levels():
    sizes = []
    c = HG_GROUP
    while c >= 2:
        sizes.append(c)
        c //= 2
    return sizes


def _hgrn_group_pairs(direction, chunk):
    n = chunk // HG_GROUP
    if direction == 0:
        return [(t, g) for t in range(1, n) for g in range(t)]
    return [(t, g) for t in range(n - 2, -1, -1) for g in range(t + 1, n)]


def _hgrn_constants(chunk):
    t = np.arange(chunk)[:, None]
    u = np.arange(chunk)[None, :]
    blocks = [(u <= t)]
    masks = []
    for c in _hgrn_fine_levels():
        half = c // 2
        mid = (t // c) * c + half
        upper = (t % c) >= half
        if c > 2:
            blocks.append(np.where(upper, (u >= mid) & (u <= t), (u > t) & (u <= mid - 1)))
        masks.append((t // c == u // c) & upper & ((u % c) < half))
    fwd_sums = np.stack(blocks).astype(np.float32)
    fwd_masks = np.stack(masks).astype(np.float32)
    twice = lambda m: np.concatenate([m.reshape(-1, chunk)] * 2, axis=1)
    sums = np.stack([twice(fwd_sums), twice(fwd_sums[:, ::-1, ::-1])])
    masks = np.concatenate([fwd_masks, fwd_masks[:, ::-1, ::-1]], axis=2)
    return sums, masks


def _block_diag(a, b):
    zero = jnp.zeros_like(a)
    return jnp.concatenate([jnp.concatenate([a, zero], axis=1),
                            jnp.concatenate([zero, b], axis=1)], axis=0)


def _hgrn_dec_rows(chunk):
    n_var = len(_hgrn_group_pairs(0, chunk)) * HG_GROUP
    n_mm = sum(c > 2 for c in _hgrn_fine_levels())
    edge, var, fine = 2 * chunk, 3 * chunk, 3 * chunk + n_var
    return edge, var, fine, fine + n_mm * chunk


def _hgrn_kernel(q_ref, zf_ref, zb_ref, i_ref, g_ref, lbf_ref, lbb_ref, gain_ref,
                 sums_ref, masks_ref, o_ref, of_scr, ob_scr, sf_scr, sb_scr,
                 k_scr0, k_scr1, dec_scr0, dec_scr1, *, layer, seq):
    chunk = HG_CHUNK
    pair = 2 * chunk
    n_pairs = seq // pair
    dh = HG_HEAD_DIM
    grp = HG_GROUP
    n_groups = chunk // grp
    fine = _hgrn_fine_levels()
    group_pairs = [_hgrn_group_pairs(direction, chunk) for direction in (0, 1)]
    r_edge, r_var, r_fine, _ = _hgrn_dec_rows(chunk)
    k_slots = (k_scr0, k_scr1)
    dec_slots = (dec_scr0, dec_scr1)
    row_in_group = lax.broadcasted_iota(jnp.int32, (grp, dh), 0)
    lane_group = lax.broadcasted_iota(jnp.int32, (grp, 2 * chunk), 1) // grp

    def lower_bound(logit_ref):
        logits = logit_ref[...].astype(F32)
        e = jnp.exp(logits - jnp.max(logits, axis=0, keepdims=True))
        return jnp.sum(e[:layer + 1], axis=0, keepdims=True) / jnp.sum(e, axis=0, keepdims=True)

    def gates(z, lb):
        f = lb + (1.0 - lb) * jax.nn.sigmoid(z.astype(F32))
        return jnp.log2(f), 1.0 - f

    def store_decays(slot, direction, g2):
        hi = g2.astype(BF16)
        lo = (g2 - hi.astype(F32)).astype(BF16)
        stacked = [jnp.concatenate([hi[c * chunk:(c + 1) * chunk], lo[c * chunk:(c + 1) * chunk]], axis=0)
                   for c in range(2)]
        raw = _dot(sums_ref[direction], jnp.concatenate(stacked, axis=1))
        dec = dec_slots[slot].at[direction]
        cum = raw[0:chunk]
        whole = cum[chunk - 1:chunk] if direction == 0 else cum[0:1]
        dec[0:chunk, :] = jnp.exp2(cum)
        dec[chunk:2 * chunk, :] = jnp.exp2(whole - cum)
        dec[r_fine:, :] = jnp.exp2(raw[chunk:])
        edge = grp - 1 if direction == 0 else 0
        edges = [cum[g * grp + edge:g * grp + edge + 1] for g in range(n_groups)]
        group = lambda g: cum[g * grp:(g + 1) * grp]
        dec[r_edge:r_var, :] = jnp.exp2(jnp.concatenate([edges[g] - group(g) for g in range(n_groups)], axis=0))
        dec[r_var:r_fine, :] = jnp.exp2(jnp.concatenate([group(t) - edges[g]
                                                         for t, g in group_pairs[direction]], axis=0))

    def fine_operand(direction, lvl, q, k, dec):
        c = fine[lvl]
        pos = row_in_group % c
        is_upper = (pos >= c // 2) if direction == 0 else (pos < c // 2)
        pick = lambda x, y: jnp.where(is_upper[None], x.reshape(n_groups, grp, dh),
                                      y.reshape(n_groups, grp, dh)).reshape(chunk, dh)
        if c == 2:
            return pick(q * (1.0 - k), k).astype(BF16)
        return (pick(q, k) * dec(r_fine + lvl * chunk, r_fine + (lvl + 1) * chunk)).astype(BF16)

    def chunk_part(slot, direction, c, q, v, base):
        rows = slice(c * chunk, (c + 1) * chunk)
        lanes = slice(c * dh, (c + 1) * dh)
        return dict(q=q[rows], k=k_slots[slot][direction, rows, :], v=v[rows],
                    dec=lambda r0, r1: dec_slots[slot][direction, r0:r1, lanes],
                    rows=pl.ds(pl.multiple_of(base + c * chunk, chunk), chunk))

    def pair_step(f, b):
        lhs = jnp.concatenate(
            [(jnp.concatenate([p["q"][t * grp:(t + 1) * grp] for t, _ in group_pairs[d]], axis=0)
              * p["dec"](r_var, r_fine)).astype(BF16) for d, p in enumerate((f, b))], axis=1)
        keys = [(p["k"] * p["dec"](r_edge, r_var)).astype(BF16) for p in (f, b)]
        r = _dot_nt(lhs, _block_diag(*keys))
        rows_of = [jnp.zeros((grp, 2 * chunk), F32)] * n_groups
        for i, ((tf, gf), (tb, gb)) in enumerate(zip(*group_pairs)):
            block = r[i * grp:(i + 1) * grp]
            rows_of[tf] = jnp.where(lane_group == gf, block, rows_of[tf])
            rows_of[tb] = jnp.where(lane_group == n_groups + gb, block, rows_of[tb])
        a = jnp.concatenate(rows_of, axis=0)
        for lvl in range(len(fine)):
            xf = fine_operand(0, lvl, f["q"], f["k"], f["dec"])
            xb = fine_operand(1, lvl, b["q"], b["k"], b["dec"])
            a = a + masks_ref[lvl] * _dot_nt(jnp.concatenate([xf, xb], axis=1), _block_diag(xf, xb))
        o = _dot(a.astype(BF16), _block_diag(f["v"], b["v"]))
        sf = sf_scr[...]
        sb = sb_scr[...]
        q_dec = jnp.concatenate([(f["q"] * f["dec"](0, chunk)).astype(BF16),
                                 (b["q"] * b["dec"](0, chunk)).astype(BF16)], axis=1)
        o = o + _dot_nt(q_dec, _block_diag(sf.astype(BF16), sb.astype(BF16)))
        diag_f = jnp.sum(f["q"] * f["k"], axis=-1, keepdims=True)
        diag_b = jnp.sum(b["q"] * b["k"], axis=-1, keepdims=True)
        of_scr[f["rows"], :] = o[:, :dh] + diag_f * f["v"].astype(F32)
        ob_scr[b["rows"], :] = o[:, dh:] + diag_b * b["v"].astype(F32)
        k_dec = _block_diag((f["k"] * f["dec"](chunk, 2 * chunk)).astype(BF16),
                            (b["k"] * b["dec"](chunk, 2 * chunk)).astype(BF16))
        v_t = jnp.concatenate([f["v"], b["v"]], axis=0).astype(F32).T.astype(BF16)
        u = _dot(v_t, k_dec)
        sf_scr[...] = sf * f["dec"](chunk - 1, chunk) + u[:, :dh]
        sb_scr[...] = sb * b["dec"](0, 1) + u[:, dh:]

    lb_f = lower_bound(lbf_ref)
    lb_b = lower_bound(lbb_ref)
    sf_scr[...] = jnp.zeros_like(sf_scr)
    sb_scr[...] = jnp.zeros_like(sb_scr)

    def bases(n):
        return (pl.multiple_of(n * pair, pair), pl.multiple_of((n_pairs - 1 - n) * pair, pair))

    def prepare(n, slot):
        for direction, base, z_ref, lb in zip((0, 1), bases(n), (zf_ref, zb_ref), (lb_f, lb_b)):
            g2, k = gates(z_ref[0, pl.ds(base, pair), :], lb)
            k_slots[slot][direction] = k
            store_decays(slot, direction, g2)

    def consume(n, slot):
        parts = []
        for direction, base in zip((0, 1), bases(n)):
            rows = pl.ds(base, pair)
            q = q_ref[0, rows, :].astype(F32)
            v = i_ref[0, rows, :]
            parts.append([chunk_part(slot, direction, c, q, v, base) for c in range(2)])
        pair_step(parts[0][0], parts[1][1])
        pair_step(parts[0][1], parts[1][0])

    def scan_body(m, carry):
        prepare(2 * m + 1, 1)
        consume(2 * m, 0)
        prepare(jnp.minimum(2 * m + 2, n_pairs - 1), 0)
        consume(2 * m + 1, 1)
        return carry

    prepare(0, 0)
    lax.fori_loop(0, n_pairs // 2, scan_body, 0, unroll=HG_SCAN_UNROLL)

    gain = gain_ref[...].astype(F32)

    def norm_body(n, carry):
        rows = pl.ds(pl.multiple_of(n * HG_NORM_ROWS, HG_NORM_ROWS), HG_NORM_ROWS)
        o = of_scr[rows, :] + ob_scr[rows, :]
        o = o * lax.rsqrt(jnp.mean(jnp.square(o), axis=-1, keepdims=True) + RMS_EPS) * gain
        gate = g_ref[0, rows, :].astype(F32)
        o_ref[0, rows, :] = (o * _silu_from_half(gate)).astype(o_ref.dtype)
        return carry

    lax.fori_loop(0, seq // HG_NORM_ROWS, norm_body, 0, unroll=4)


def _hgrn(q, zf, zb, i, g, lb_fwd_logits, lb_bwd_logits, gain_row, layer):
    batch, seq, width = q.shape
    assert width == HG_HEADS * HG_HEAD_DIM and seq % (4 * HG_NORM_ROWS) == 0
    assert seq % (4 * HG_CHUNK * HG_SCAN_UNROLL) == 0
    sums, masks = _hgrn_constants(HG_CHUNK)
    sums = jnp.asarray(sums, BF16)
    masks = jnp.asarray(masks, F32)
    n_layers = lb_fwd_logits.shape[0]
    n_dec_rows = _hgrn_dec_rows(HG_CHUNK)[-1]
    head = pl.BlockSpec((1, seq, HG_HEAD_DIM), lambda b, h: (b, 0, h))
    per_head_row = lambda n: pl.BlockSpec((n, HG_HEAD_DIM), lambda b, h: (0, h))
    whole = lambda a: pl.BlockSpec(a.shape, lambda b, h: (0,) * a.ndim)
    return pl.pallas_call(
        functools.partial(_hgrn_kernel, layer=layer, seq=seq),
        grid=(batch, HG_HEADS),
        in_specs=[head, head, head, head, head,
                  per_head_row(n_layers), per_head_row(n_layers), per_head_row(1),
                  whole(sums), whole(masks)],
        out_specs=head,
        out_shape=jax.ShapeDtypeStruct((batch, seq, width), BF16),
        scratch_shapes=[pltpu.VMEM((seq, HG_HEAD_DIM), F32), pltpu.VMEM((seq, HG_HEAD_DIM), F32),
                        pltpu.VMEM((HG_HEAD_DIM, HG_HEAD_DIM), F32),
                        pltpu.VMEM((HG_HEAD_DIM, HG_HEAD_DIM), F32),
                        pltpu.VMEM((2, 2 * HG_CHUNK, HG_HEAD_DIM), F32),
                        pltpu.VMEM((2, 2 * HG_CHUNK, HG_HEAD_DIM), F32),
                        pltpu.VMEM((2, n_dec_rows, 2 * HG_HEAD_DIM), F32),
                        pltpu.VMEM((2, n_dec_rows, 2 * HG_HEAD_DIM), F32)],
        compiler_params=pltpu.CompilerParams(
            dimension_semantics=("arbitrary", "arbitrary"), vmem_limit_bytes=VMEM_LIMIT_BYTES),
        name="hgrn2_scan",
    )(q, zf, zb, i, g, lb_fwd_logits, lb_bwd_logits, gain_row, sums, masks)


def _out_proj_kernel(oa_ref, oh_ref, x_ref, w_ref, b_ref, gain_ref, bias_ref, o_ref, w_scr, *, alpha):
    d_att = oa_ref.shape[1]
    _cast_weight_once(w_ref, w_scr)
    for r0 in range(0, o_ref.shape[0], OUT_PROJ_ROWS):
        rows = slice(r0, r0 + OUT_PROJ_ROWS)
        y = _dot(oa_ref[rows, :], w_scr[:d_att, :]) + _dot(oh_ref[rows, :], w_scr[d_att:, :]) + b_ref[...]
        r = alpha * x_ref[rows, :] + y
        mu = jnp.mean(r, axis=-1, keepdims=True)
        c = r - mu
        var = jnp.mean(jnp.square(c), axis=-1, keepdims=True)
        o_ref[rows, :] = (c * lax.rsqrt(var + LN_EPS) * gain_ref[...] + bias_ref[...]).astype(o_ref.dtype)


def _out_proj(o_a, o_h, x2d, w, b_row, gain_row, bias_row, alpha):
    n_tok, d_model = x2d.shape
    d_att, d_hg = o_a.shape[1], o_h.shape[1]
    row = pl.BlockSpec((1, d_model), lambda i: (0, 0))
    return pl.pallas_call(
        functools.partial(_out_proj_kernel, alpha=alpha),
        grid=(n_tok // OUT_PROJ_TM,),
        in_specs=[pl.BlockSpec((OUT_PROJ_TM, d_att), lambda i: (i, 0)),
                  pl.BlockSpec((OUT_PROJ_TM, d_hg), lambda i: (i, 0)),
                  pl.BlockSpec((OUT_PROJ_TM, d_model), lambda i: (i, 0)),
                  pl.BlockSpec((d_att + d_hg, d_model), lambda i: (0, 0), pipeline_mode=pl.Buffered(1)),
                  row, row, row],
        out_specs=pl.BlockSpec((OUT_PROJ_TM, d_model), lambda i: (i, 0)),
        out_shape=jax.ShapeDtypeStruct((n_tok, d_model), x2d.dtype),
        scratch_shapes=[pltpu.VMEM((d_att + d_hg, d_model), BF16)],
        compiler_params=pltpu.CompilerParams(
            dimension_semantics=("arbitrary",), vmem_limit_bytes=VMEM_LIMIT_BYTES),
        name="out_proj_layernorm",
    )(o_a, o_h, x2d, w, b_row, gain_row, bias_row)


def _layer(x, layer, depth, w_in, b_in, rpb, lb_fwd_logits, lb_bwd_logits, hg_norm_gain,
           w_out, b_out, ln_gain, ln_bias):
    batch, seq, d_model = x.shape
    x2d = x.reshape(batch * seq, d_model)
    slabs = _in_proj(x2d, w_in, b_in.reshape(1, -1))
    q_a, k_a, v_a, g_a, q_h, z_f, z_b, i_h, g_h = [s.reshape(batch, seq, SLAB) for s in slabs]
    o_a = _attention(q_a, k_a, v_a, g_a, rpb)
    o_h = _hgrn(q_h, z_f, z_b, i_h, g_h, lb_fwd_logits, lb_bwd_logits,
                hg_norm_gain.reshape(1, -1), layer)
    alpha = (2.0 * depth) ** 0.25
    out = _out_proj(o_a.reshape(batch * seq, SLAB), o_h.reshape(batch * seq, SLAB), x2d,
                    w_out, b_out.reshape(1, -1), ln_gain.reshape(1, -1),
                    ln_bias.reshape(1, -1), alpha)
    return out.reshape(batch, seq, d_model)


def kernel(x, w_in, b_in, rpb, lb_fwd_logits, lb_bwd_logits, hg_norm_gain, w_out, b_out, ln_gain, ln_bias):
    depth = w_in.shape[0]
    for layer in range(depth):
        x = _layer(x, layer, depth, w_in[layer], b_in[layer], rpb[layer], lb_fwd_logits,
                   lb_bwd_logits, hg_norm_gain[layer], w_out[layer], b_out[layer],
                   ln_gain[layer], ln_bias[layer])
    return x
```

```python
import functools

import numpy as np
import jax
import jax.numpy as jnp
from jax import lax
from jax.experimental import pallas as pl
from jax.experimental.pallas import tpu as pltpu

GRID_W = 64
ATT_HEADS = 8
ATT_HEAD_DIM = 64
HG_HEADS = 4
HG_HEAD_DIM = 128
WIN_ROWS = 8
WIN_COLS = 16
LN_EPS = 1e-5
RMS_EPS = 1e-6
N_SLABS = 9
SLAB = 512

LANES = 128
F32_SUBLANES = 8
VMEM_LIMIT_BYTES = 56 * 1024 * 1024
PROJ_TM = 512
OUT_PROJ_TM = 1024
OUT_PROJ_ROWS = 256
ATT_ROWS_PER_STEP = 16
ATT_UNROLL = 4
ATT_GROUP = 4
ATT_BIAS_UNROLL = 8
HG_CHUNK = 64
HG_GROUP = F32_SUBLANES
HG_NORM_ROWS = 256
HG_SCAN_UNROLL = 8
MASK_VALUE = -1e30
LOG2_E = 1.4426950408889634
SLAB_SCALES = {1: ATT_HEAD_DIM ** -0.5 * LOG2_E, 3: 0.5, 8: 0.5}

BF16 = jnp.bfloat16
F32 = jnp.float32


def _silu_from_half(half_g):
    return half_g * (1.0 + jnp.tanh(half_g))


def _dot(a, b):
    return jnp.dot(a, b, preferred_element_type=F32)


def _dot_nt(a, b):
    return lax.dot_general(a, b, (((1,), (1,)), ((), ())), preferred_element_type=F32)


def _cast_weight_once(w_ref, w_scr):
    @pl.when(pl.program_id(0) == 0)
    def _():
        for c0 in range(0, w_ref.shape[1], SLAB):
            w_scr[:, c0:c0 + SLAB] = w_ref[:, c0:c0 + SLAB].astype(w_scr.dtype)


def _in_proj_kernel(x_ref, w_ref, b_ref, *refs):
    out_refs, w_scr = refs[:-1], refs[-1]
    _cast_weight_once(w_ref, w_scr)
    xb = x_ref[...].astype(BF16)
    for j, o_ref in enumerate(out_refs):
        cols = slice(j * SLAB, (j + 1) * SLAB)
        h = _dot(xb, w_scr[:, cols]) + b_ref[:, cols]
        if j in SLAB_SCALES:
            h = h * SLAB_SCALES[j]
        o_ref[...] = h.astype(o_ref.dtype)


def _in_proj(x2d, w, b_row):
    n_tok, d_model = x2d.shape
    d_in = w.shape[1]
    assert d_in == N_SLABS * SLAB and n_tok % PROJ_TM == 0
    return pl.pallas_call(
        _in_proj_kernel,
        grid=(n_tok // PROJ_TM,),
        in_specs=[
            pl.BlockSpec((PROJ_TM, d_model), lambda i: (i, 0)),
            pl.BlockSpec((d_model, d_in), lambda i: (0, 0), pipeline_mode=pl.Buffered(1)),
            pl.BlockSpec((1, d_in), lambda i: (0, 0)),
        ],
        out_specs=[pl.BlockSpec((PROJ_TM, SLAB), lambda i: (i, 0))] * N_SLABS,
        out_shape=[jax.ShapeDtypeStruct((n_tok, SLAB), BF16)] * N_SLABS,
        scratch_shapes=[pltpu.VMEM((d_model, d_in), BF16)],
        compiler_params=pltpu.CompilerParams(
            dimension_semantics=("arbitrary",), vmem_limit_bytes=VMEM_LIMIT_BYTES),
        name="in_proj",
    )(x2d, w, b_row)


ATT_REL_ROWS = 2 * WIN_ROWS - 1
ATT_BIAS_TILES = (ATT_REL_ROWS - 1) // 2
ATT_GROUPS = ATT_HEADS // ATT_GROUP


def _attention_bias_rows(rpb):
    n_rel_cols = 2 * WIN_COLS - 1
    scaled = rpb.astype(F32) * LOG2_E
    gap = jnp.zeros((ATT_HEADS, ATT_BIAS_TILES, (LANES - 2 * n_rel_cols) // 2), F32)
    per_parity = []
    for p in range(2):
        first = scaled[:, p:p + 2 * ATT_BIAS_TILES - 1:2]
        second = scaled[:, p + 1:p + 2 * ATT_BIAS_TILES:2]
        per_parity.append(jnp.concatenate(
            [first[..., WIN_COLS - 1:], gap, second, gap, first[..., :WIN_COLS - 1]], axis=-1))
    gen = jnp.stack(per_parity)
    gen = gen.reshape(2, ATT_GROUPS, ATT_GROUP, ATT_BIAS_TILES, LANES).transpose(0, 1, 3, 2, 4)
    return gen.reshape(2 * ATT_GROUPS * ATT_BIAS_TILES * ATT_GROUP, 1, LANES)


def _attention_window_mask():
    qc = np.arange(GRID_W)[:, None]
    kc = np.arange(LANES)[None, :] % GRID_W
    col_start = np.clip(qc - WIN_COLS // 2, 0, GRID_W - WIN_COLS)
    return ((kc >= col_start) & (kc < col_start + WIN_COLS)).astype(np.float32)


def _attention_kernel(q_ref, k_ref, v_ref, g_ref, gen_ref, win_ref, o_ref, bias_scr, *, rows, kr):
    step = pl.program_id(1)
    gw = ATT_GROUP * ATT_HEAD_DIM
    row_head = lax.broadcasted_iota(jnp.int32, (ATT_GROUP * GRID_W, gw), 0) // GRID_W
    lane_head = lax.broadcasted_iota(jnp.int32, (ATT_GROUP * GRID_W, gw), 1) // ATT_HEAD_DIM
    own_head = row_head == lane_head
    out_lane_head = lax.broadcasted_iota(jnp.int32, (GRID_W, gw), 1) // ATT_HEAD_DIM

    @pl.when((pl.program_id(0) == 0) & (step == 0))
    def _build_bias():
        in_window = win_ref[...] > 0.0

        def one_block(n, carry):
            gen = jnp.broadcast_to(gen_ref[n], (GRID_W, LANES))
            toeplitz = pltpu.roll(gen, 0, 1, stride=1, stride_axis=0)
            head_rows = pl.ds(pl.multiple_of((n % ATT_GROUP) * GRID_W, GRID_W), GRID_W)
            bias_scr[n // ATT_GROUP, head_rows, :] = jnp.where(in_window, toeplitz, MASK_VALUE)
            return carry

        lax.fori_loop(0, gen_ref.shape[0], one_block, 0, unroll=ATT_BIAS_UNROLL)

    def one_row(j, carry):
        r = step * ATT_ROWS_PER_STEP + j
        row_start = jnp.clip(r - kr // 2, 0, rows - kr)
        variant = row_start - r + (WIN_ROWS - 1)
        q_tok = pl.multiple_of(j * GRID_W, GRID_W)
        k_tok = pl.multiple_of(row_start * GRID_W, GRID_W)
        for grp in range(ATT_GROUPS):
            lanes = slice(grp * gw, (grp + 1) * gw)
            q = q_ref[0, pl.ds(q_tok, GRID_W), lanes]
            q_bd = jnp.where(own_head, jnp.concatenate([q] * ATT_GROUP, axis=0), 0)
            keys = k_ref[0, pl.ds(k_tok, kr * GRID_W), lanes]
            vals = v_ref[0, pl.ds(k_tok, kr * GRID_W), lanes]
            tile0 = ((variant % 2) * ATT_GROUPS + grp) * ATT_BIAS_TILES + variant // 2
            bias = jnp.concatenate([bias_scr[tile0 + t] for t in range(kr // 2)], axis=1)
            s = _dot_nt(q_bd, keys) + bias
            m = jnp.max(s, axis=-1, keepdims=True)
            p = jnp.exp2(s - m)
            denom = jnp.sum(p, axis=-1, keepdims=True)
            pv = _dot(p.astype(BF16), vals) / denom
            o = jnp.zeros((GRID_W, gw), F32)
            for h in range(ATT_GROUP):
                o = o + jnp.where(out_lane_head == h, pv[h * GRID_W:(h + 1) * GRID_W], 0.0)
            gate = g_ref[0, pl.ds(q_tok, GRID_W), lanes].astype(F32)
            o = o * _silu_from_half(gate)
            o_ref[0, pl.ds(q_tok, GRID_W), lanes] = o.astype(o_ref.dtype)
        return carry

    lax.fori_loop(0, ATT_ROWS_PER_STEP, one_row, 0, unroll=ATT_UNROLL)


def _attention(q, k, v, g, rpb):
    batch, seq, width = q.shape
    rows = seq // GRID_W
    kr = min(WIN_ROWS, rows)
    assert rows % ATT_ROWS_PER_STEP == 0 and ATT_ROWS_PER_STEP % ATT_UNROLL == 0
    assert kr == WIN_ROWS and 2 * GRID_W == LANES
    gen = _attention_bias_rows(rpb)
    win = jnp.asarray(_attention_window_mask())
    blk = ATT_ROWS_PER_STEP * GRID_W
    tile = pl.BlockSpec((1, blk, width), lambda b, i: (b, i, 0))
    whole = pl.BlockSpec((1, seq, width), lambda b, i: (b, 0, 0))
    return pl.pallas_call(
        functools.partial(_attention_kernel, rows=rows, kr=kr),
        grid=(batch, rows // ATT_ROWS_PER_STEP),
        in_specs=[tile, whole, whole, tile,
                  pl.BlockSpec(gen.shape, lambda b, i: (0, 0, 0)),
                  pl.BlockSpec(win.shape, lambda b, i: (0, 0))],
        out_specs=tile,
        out_shape=jax.ShapeDtypeStruct((batch, seq, width), BF16),
        scratch_shapes=[pltpu.VMEM((2 * ATT_GROUPS * ATT_BIAS_TILES, ATT_GROUP * GRID_W, 2 * GRID_W), F32)],
        compiler_params=pltpu.CompilerParams(
            dimension_semantics=("arbitrary", "arbitrary"), vmem_limit_bytes=VMEM_LIMIT_BYTES),
        name="nbr_attention",
    )(q, k, v, g, gen, win)


def _hgrn_fine_levels():
    sizes = []
    c = HG_GROUP
    while c >= 2:
        sizes.append(c)
        c //= 2
    return sizes


def _hgrn_group_pairs(direction, chunk):
    n = chunk // HG_GROUP
    if direction == 0:
        return [(t, g) for t in range(1, n) for g in range(t)]
    return [(t, g) for t in range(n - 2, -1, -1) for g in range(t + 1, n)]


def _hgrn_constants(chunk):
    t = np.arange(chunk)[:, None]
    u = np.arange(chunk)[None, :]
    blocks = [(u <= t)]
    masks = []
    for c in _hgrn_fine_levels():
        half = c // 2
        mid = (t // c) * c + half
        upper = (t % c) >= half
        if 2 < c < HG_GROUP:
            blocks.append(np.where(upper, (u >= mid) & (u <= t), (u > t) & (u <= mid - 1)))
        masks.append((t // c == u // c) & upper & ((u % c) < half))
    fwd_sums = np.stack(blocks).astype(np.float32)
    fwd_masks = np.stack(masks).astype(np.float32)
    twice = lambda m: np.concatenate([m.reshape(-1, chunk)] * 2, axis=1)
    sums = np.stack([twice(fwd_sums), twice(fwd_sums[:, ::-1, ::-1])])
    masks = np.concatenate([fwd_masks, fwd_masks[:, ::-1, ::-1]], axis=2)
    return sums, masks


def _block_diag(a, b):
    zero = jnp.zeros_like(a)
    return jnp.concatenate([jnp.concatenate([a, zero], axis=1),
                            jnp.concatenate([zero, b], axis=1)], axis=0)


def _hgrn_dec_rows(chunk):
    n_var = len(_hgrn_group_pairs(0, chunk)) * HG_GROUP
    n_mm = sum(c > 2 for c in _hgrn_fine_levels())
    edge, var, fine = 2 * chunk, 3 * chunk, 3 * chunk + n_var
    return edge, var, fine, fine + n_mm * chunk


def _hgrn_kernel(q_ref, zf_ref, zb_ref, i_ref, g_ref, lbf_ref, lbb_ref, gain_ref,
                 sums_ref, masks_ref, o_ref, of_scr, ob_scr, sf_scr, sb_scr,
                 k_scr0, k_scr1, dec_scr0, dec_scr1, *, layer, seq):
    chunk = HG_CHUNK
    pair = 2 * chunk
    n_pairs = seq // pair
    dh = HG_HEAD_DIM
    grp = HG_GROUP
    n_groups = chunk // grp
    fine = _hgrn_fine_levels()
    group_pairs = [_hgrn_group_pairs(direction, chunk) for direction in (0, 1)]
    r_edge, r_var, r_fine, _ = _hgrn_dec_rows(chunk)
    k_slots = (k_scr0, k_scr1)
    dec_slots = (dec_scr0, dec_scr1)
    row_in_group = lax.broadcasted_iota(jnp.int32, (grp, dh), 0)
    lane_group = lax.broadcasted_iota(jnp.int32, (grp, 2 * chunk), 1) // grp

    def lower_bound(logit_ref):
        logits = logit_ref[...].astype(F32)
        e = jnp.exp(logits - jnp.max(logits, axis=0, keepdims=True))
        return jnp.sum(e[:layer + 1], axis=0, keepdims=True) / jnp.sum(e, axis=0, keepdims=True)

    def gates(z, lb):
        f = lb + (1.0 - lb) * jax.nn.sigmoid(z.astype(F32))
        return jnp.log2(f), 1.0 - f

    def store_decays(slot, direction, g2):
        hi = g2.astype(BF16)
        lo = (g2 - hi.astype(F32)).astype(BF16)
        stacked = [jnp.concatenate([hi[c * chunk:(c + 1) * chunk], lo[c * chunk:(c + 1) * chunk]], axis=0)
                   for c in range(2)]
        raw = _dot(sums_ref[direction], jnp.concatenate(stacked, axis=1))
        dec = dec_slots[slot].at[direction]
        cum = raw[0:chunk]
        whole = cum[chunk - 1:chunk] if direction == 0 else cum[0:1]
        dec[0:chunk, :] = jnp.exp2(cum)
        dec[chunk:2 * chunk, :] = jnp.exp2(whole - cum)
        by_group = cum.reshape(n_groups, grp, 2 * dh)
        mid_row = grp // 2 - 1 if direction == 0 else grp // 2
        to_mid = -jnp.abs(by_group - by_group[:, mid_row:mid_row + 1, :])
        dec[r_fine:r_fine + chunk, :] = jnp.exp2(to_mid.reshape(chunk, 2 * dh))
        dec[r_fine + chunk:, :] = jnp.exp2(raw[chunk:])
        edge = grp - 1 if direction == 0 else 0
        edges = [cum[g * grp + edge:g * grp + edge + 1] for g in range(n_groups)]
        group = lambda g: cum[g * grp:(g + 1) * grp]
        dec[r_edge:r_var, :] = jnp.exp2(jnp.concatenate([edges[g] - group(g) for g in range(n_groups)], axis=0))
        dec[r_var:r_fine, :] = jnp.exp2(jnp.concatenate([group(t) - edges[g]
                                                         for t, g in group_pairs[direction]], axis=0))

    def fine_operand(direction, lvl, q, k, dec):
        c = fine[lvl]
        pos = row_in_group % c
        is_upper = (pos >= c // 2) if direction == 0 else (pos < c // 2)
        pick = lambda x, y: jnp.where(is_upper[None], x.reshape(n_groups, grp, dh),
                                      y.reshape(n_groups, grp, dh)).reshape(chunk, dh)
        if c == 2:
            return pick(q * (1.0 - k), k).astype(BF16)
        return (pick(q, k) * dec(r_fine + lvl * chunk, r_fine + (lvl + 1) * chunk)).astype(BF16)

    def chunk_part(slot, direction, c, q, v, base):
        rows = slice(c * chunk, (c + 1) * chunk)
        lanes = slice(c * dh, (c + 1) * dh)
        return dict(q=q[rows], k=k_slots[slot][direction, rows, :], v=v[rows],
                    dec=lambda r0, r1: dec_slots[slot][direction, r0:r1, lanes],
                    rows=pl.ds(pl.multiple_of(base + c * chunk, chunk), chunk))

    def pair_step(f, b):
        lhs = jnp.concatenate(
            [(jnp.concatenate([p["q"][t * grp:(t + 1) * grp] for t, _ in group_pairs[d]], axis=0)
              * p["dec"](r_var, r_fine)).astype(BF16) for d, p in enumerate((f, b))], axis=1)
        keys = [(p["k"] * p["dec"](r_edge, r_var)).astype(BF16) for p in (f, b)]
        r = _dot_nt(lhs, _block_diag(*keys))
        rows_of = [jnp.zeros((grp, 2 * chunk), F32)] * n_groups
        for i, ((tf, gf), (tb, gb)) in enumerate(zip(*group_pairs)):
            block = r[i * grp:(i + 1) * grp]
            rows_of[tf] = jnp.where(lane_group == gf, block, rows_of[tf])
            rows_of[tb] = jnp.where(lane_group == n_groups + gb, block, rows_of[tb])
        a = jnp.concatenate(rows_of, axis=0)
        for lvl in range(len(fine)):
            xf = fine_operand(0, lvl, f["q"], f["k"], f["dec"])
            xb = fine_operand(1, lvl, b["q"], b["k"], b["dec"])
            a = a + masks_ref[lvl] * _dot_nt(jnp.concatenate([xf, xb], axis=1), _block_diag(xf, xb))
        o = _dot(a.astype(BF16), _block_diag(f["v"], b["v"]))
        sf = sf_scr[...]
        sb = sb_scr[...]
        q_dec = jnp.concatenate([(f["q"] * f["dec"](0, chunk)).astype(BF16),
                                 (b["q"] * b["dec"](0, chunk)).astype(BF16)], axis=1)
        o = o + _dot_nt(q_dec, _block_diag(sf.astype(BF16), sb.astype(BF16)))
        diag_f = jnp.sum(f["q"] * f["k"], axis=-1, keepdims=True)
        diag_b = jnp.sum(b["q"] * b["k"], axis=-1, keepdims=True)
        of_scr[f["rows"], :] = o[:, :dh] + diag_f * f["v"].astype(F32)
        ob_scr[b["rows"], :] = o[:, dh:] + diag_b * b["v"].astype(F32)
        k_dec = _block_diag((f["k"] * f["dec"](chunk, 2 * chunk)).astype(BF16),
                            (b["k"] * b["dec"](chunk, 2 * chunk)).astype(BF16))
        v_t = jnp.concatenate([f["v"], b["v"]], axis=0).astype(F32).T.astype(BF16)
        u = _dot(v_t, k_dec)
        sf_scr[...] = sf * f["dec"](chunk - 1, chunk) + u[:, :dh]
        sb_scr[...] = sb * b["dec"](0, 1) + u[:, dh:]

    lb_f = lower_bound(lbf_ref)
    lb_b = lower_bound(lbb_ref)
    sf_scr[...] = jnp.zeros_like(sf_scr)
    sb_scr[...] = jnp.zeros_like(sb_scr)

    def bases(n):
        return (pl.multiple_of(n * pair, pair), pl.multiple_of((n_pairs - 1 - n) * pair, pair))

    def prepare(n, slot):
        for direction, base, z_ref, lb in zip((0, 1), bases(n), (zf_ref, zb_ref), (lb_f, lb_b)):
            g2, k = gates(z_ref[0, pl.ds(base, pair), :], lb)
            k_slots[slot][direction] = k
            store_decays(slot, direction, g2)

    def consume(n, slot):
        parts = []
        for direction, base in zip((0, 1), bases(n)):
            rows = pl.ds(base, pair)
            q = q_ref[0, rows, :].astype(F32)
            v = i_ref[0, rows, :]
            parts.append([chunk_part(slot, direction, c, q, v, base) for c in range(2)])
        pair_step(parts[0][0], parts[1][1])
        pair_step(parts[0][1], parts[1][0])

    def scan_body(m, carry):
        prepare(2 * m + 1, 1)
        consume(2 * m, 0)
        prepare(jnp.minimum(2 * m + 2, n_pairs - 1), 0)
        consume(2 * m + 1, 1)
        return carry

    prepare(0, 0)
    lax.fori_loop(0, n_pairs // 2, scan_body, 0, unroll=HG_SCAN_UNROLL)

    gain = gain_ref[...].astype(F32)

    def norm_body(n, carry):
        rows = pl.ds(pl.multiple_of(n * HG_NORM_ROWS, HG_NORM_ROWS), HG_NORM_ROWS)
        o = of_scr[rows, :] + ob_scr[rows, :]
        o = o * lax.rsqrt(jnp.mean(jnp.square(o), axis=-1, keepdims=True) + RMS_EPS) * gain
        gate = g_ref[0, rows, :].astype(F32)
        o_ref[0, rows, :] = (o * _silu_from_half(gate)).astype(o_ref.dtype)
        return carry

    lax.fori_loop(0, seq // HG_NORM_ROWS, norm_body, 0, unroll=4)


def _hgrn(q, zf, zb, i, g, lb_fwd_logits, lb_bwd_logits, gain_row, layer):
    batch, seq, width = q.shape
    assert width == HG_HEADS * HG_HEAD_DIM and seq % (4 * HG_NORM_ROWS) == 0
    assert seq % (4 * HG_CHUNK * HG_SCAN_UNROLL) == 0
    sums, masks = _hgrn_constants(HG_CHUNK)
    sums = jnp.asarray(sums, BF16)
    masks = jnp.asarray(masks, F32)
    n_layers = lb_fwd_logits.shape[0]
    n_dec_rows = _hgrn_dec_rows(HG_CHUNK)[-1]
    head = pl.BlockSpec((1, seq, HG_HEAD_DIM), lambda b, h: (b, 0, h))
    per_head_row = lambda n: pl.BlockSpec((n, HG_HEAD_DIM), lambda b, h: (0, h))
    whole = lambda a: pl.BlockSpec(a.shape, lambda b, h: (0,) * a.ndim)
    return pl.pallas_call(
        functools.partial(_hgrn_kernel, layer=layer, seq=seq),
        grid=(batch, HG_HEADS),
        in_specs=[head, head, head, head, head,
                  per_head_row(n_layers), per_head_row(n_layers), per_head_row(1),
                  whole(sums), whole(masks)],
        out_specs=head,
        out_shape=jax.ShapeDtypeStruct((batch, seq, width), BF16),
        scratch_shapes=[pltpu.VMEM((seq, HG_HEAD_DIM), F32), pltpu.VMEM((seq, HG_HEAD_DIM), F32),
                        pltpu.VMEM((HG_HEAD_DIM, HG_HEAD_DIM), F32),
                        pltpu.VMEM((HG_HEAD_DIM, HG_HEAD_DIM), F32),
                        pltpu.VMEM((2, 2 * HG_CHUNK, HG_HEAD_DIM), F32),
                        pltpu.VMEM((2, 2 * HG_CHUNK, HG_HEAD_DIM), F32),
                        pltpu.VMEM((2, n_dec_rows, 2 * HG_HEAD_DIM), F32),
                        pltpu.VMEM((2, n_dec_rows, 2 * HG_HEAD_DIM), F32)],
        compiler_params=pltpu.CompilerParams(
            dimension_semantics=("arbitrary", "arbitrary"), vmem_limit_bytes=VMEM_LIMIT_BYTES),
        name="hgrn2_scan",
    )(q, zf, zb, i, g, lb_fwd_logits, lb_bwd_logits, gain_row, sums, masks)


def _out_proj_kernel(oa_ref, oh_ref, x_ref, w_ref, b_ref, gain_ref, bias_ref, o_ref, w_scr, *, alpha):
    d_att = oa_ref.shape[1]
    _cast_weight_once(w_ref, w_scr)
    for r0 in range(0, o_ref.shape[0], OUT_PROJ_ROWS):
        rows = slice(r0, r0 + OUT_PROJ_ROWS)
        y = _dot(oa_ref[rows, :], w_scr[:d_att, :]) + _dot(oh_ref[rows, :], w_scr[d_att:, :]) + b_ref[...]
        r = alpha * x_ref[rows, :] + y
        mu = jnp.mean(r, axis=-1, keepdims=True)
        c = r - mu
        var = jnp.mean(jnp.square(c), axis=-1, keepdims=True)
        o_ref[rows, :] = (c * lax.rsqrt(var + LN_EPS) * gain_ref[...] + bias_ref[...]).astype(o_ref.dtype)


def _out_proj(o_a, o_h, x2d, w, b_row, gain_row, bias_row, alpha):
    n_tok, d_model = x2d.shape
    d_att, d_hg = o_a.shape[1], o_h.shape[1]
    row = pl.BlockSpec((1, d_model), lambda i: (0, 0))
    return pl.pallas_call(
        functools.partial(_out_proj_kernel, alpha=alpha),
        grid=(n_tok // OUT_PROJ_TM,),
        in_specs=[pl.BlockSpec((OUT_PROJ_TM, d_att), lambda i: (i, 0)),
                  pl.BlockSpec((OUT_PROJ_TM, d_hg), lambda i: (i, 0)),
                  pl.BlockSpec((OUT_PROJ_TM, d_model), lambda i: (i, 0)),
                  pl.BlockSpec((d_att + d_hg, d_model), lambda i: (0, 0), pipeline_mode=pl.Buffered(1)),
                  row, row, row],
        out_specs=pl.BlockSpec((OUT_PROJ_TM, d_model), lambda i: (i, 0)),
        out_shape=jax.ShapeDtypeStruct((n_tok, d_model), x2d.dtype),
        scratch_shapes=[pltpu.VMEM((d_att + d_hg, d_model), BF16)],
        compiler_params=pltpu.CompilerParams(
            dimension_semantics=("arbitrary",), vmem_limit_bytes=VMEM_LIMIT_BYTES),
        name="out_proj_layernorm",
    )(o_a, o_h, x2d, w, b_row, gain_row, bias_row)


def _layer(x, layer, depth, w_in, b_in, rpb, lb_fwd_logits, lb_bwd_logits, hg_norm_gain,
           w_out, b_out, ln_gain, ln_bias):
    batch, seq, d_model = x.shape
    x2d = x.reshape(batch * seq, d_model)
    slabs = _in_proj(x2d, w_in, b_in.reshape(1, -1))
    q_a, k_a, v_a, g_a, q_h, z_f, z_b, i_h, g_h = [s.reshape(batch, seq, SLAB) for s in slabs]
    o_a = _attention(q_a, k_a, v_a, g_a, rpb)
    o_h = _hgrn(q_h, z_f, z_b, i_h, g_h, lb_fwd_logits, lb_bwd_logits,
                hg_norm_gain.reshape(1, -1), layer)
    alpha = (2.0 * depth) ** 0.25
    out = _out_proj(o_a.reshape(batch * seq, SLAB), o_h.reshape(batch * seq, SLAB), x2d,
                    w_out, b_out.reshape(1, -1), ln_gain.reshape(1, -1),
                    ln_bias.reshape(1, -1), alpha)
    return out.reshape(batch, seq, d_model)


def kernel(x, w_in, b_in, rpb, lb_fwd_logits, lb_bwd_logits, hg_norm_gain, w_out, b_out, ln_gain, ln_bias):
    depth = w_in.shape[0]
    for layer in range(depth):
        x = _layer(x, layer, depth, w_in[layer], b_in[layer], rpb[layer], lb_fwd_logits,
                   lb_bwd_logits, hg_norm_gain[layer], w_out[layer], b_out[layer],
                   ln_gain[layer], ln_bias[layer])
    return x
```

```python
import functools

import numpy as np
import jax
import jax.numpy as jnp
from jax import lax
from jax.experimental import pallas as pl
from jax.experimental.pallas import tpu as pltpu

GRID_W = 64
ATT_HEADS = 8
ATT_HEAD_DIM = 64
HG_HEADS = 4
HG_HEAD_DIM = 128
WIN_ROWS = 8
WIN_COLS = 16
LN_EPS = 1e-5
RMS_EPS = 1e-6
N_SLABS = 9
SLAB = 512

LANES = 128
F32_SUBLANES = 8
VMEM_LIMIT_BYTES = 56 * 1024 * 1024
PROJ_TM = 512
OUT_PROJ_TM = 1024
OUT_PROJ_ROWS = 256
ATT_ROWS_PER_STEP = 16
ATT_UNROLL = 4
ATT_GROUP = 4
ATT_BIAS_UNROLL = 8
HG_CHUNK = 64
HG_GROUP = F32_SUBLANES
HG_NORM_ROWS = 256
HG_SCAN_UNROLL = 8
MASK_VALUE = -1e30
LOG2_E = 1.4426950408889634
SLAB_SCALES = {1: ATT_HEAD_DIM ** -0.5 * LOG2_E, 3: 0.5, 8: 0.5}

BF16 = jnp.bfloat16
F32 = jnp.float32


def _silu_from_half(half_g):
    return half_g * (1.0 + jnp.tanh(half_g))


def _dot(a, b):
    return jnp.dot(a, b, preferred_element_type=F32)


def _dot_nt(a, b):
    return lax.dot_general(a, b, (((1,), (1,)), ((), ())), preferred_element_type=F32)


def _cast_weight_once(w_ref, w_scr):
    @pl.when(pl.program_id(0) == 0)
    def _():
        for c0 in range(0, w_ref.shape[1], SLAB):
            w_scr[:, c0:c0 + SLAB] = w_ref[:, c0:c0 + SLAB].astype(w_scr.dtype)


def _in_proj_kernel(x_ref, w_ref, b_ref, *refs):
    out_refs, w_scr = refs[:-1], refs[-1]
    _cast_weight_once(w_ref, w_scr)
    xb = x_ref[...].astype(BF16)
    for j, o_ref in enumerate(out_refs):
        cols = slice(j * SLAB, (j + 1) * SLAB)
        h = _dot(xb, w_scr[:, cols]) + b_ref[:, cols]
        if j in SLAB_SCALES:
            h = h * SLAB_SCALES[j]
        o_ref[...] = h.astype(o_ref.dtype)


def _in_proj(x2d, w, b_row):
    n_tok, d_model = x2d.shape
    d_in = w.shape[1]
    assert d_in == N_SLABS * SLAB and n_tok % PROJ_TM == 0
    return pl.pallas_call(
        _in_proj_kernel,
        grid=(n_tok // PROJ_TM,),
        in_specs=[
            pl.BlockSpec((PROJ_TM, d_model), lambda i: (i, 0)),
            pl.BlockSpec((d_model, d_in), lambda i: (0, 0), pipeline_mode=pl.Buffered(1)),
            pl.BlockSpec((1, d_in), lambda i: (0, 0)),
        ],
        out_specs=[pl.BlockSpec((PROJ_TM, SLAB), lambda i: (i, 0))] * N_SLABS,
        out_shape=[jax.ShapeDtypeStruct((n_tok, SLAB), BF16)] * N_SLABS,
        scratch_shapes=[pltpu.VMEM((d_model, d_in), BF16)],
        compiler_params=pltpu.CompilerParams(
            dimension_semantics=("arbitrary",), vmem_limit_bytes=VMEM_LIMIT_BYTES),
        name="in_proj",
    )(x2d, w, b_row)


ATT_REL_ROWS = 2 * WIN_ROWS - 1
ATT_BIAS_TILES = (ATT_REL_ROWS - 1) // 2
ATT_GROUPS = ATT_HEADS // ATT_GROUP


def _attention_bias_rows(rpb):
    n_rel_cols = 2 * WIN_COLS - 1
    scaled = rpb.astype(F32) * LOG2_E
    gap = jnp.zeros((ATT_HEADS, ATT_BIAS_TILES, (LANES - 2 * n_rel_cols) // 2), F32)
    per_parity = []
    for p in range(2):
        first = scaled[:, p:p + 2 * ATT_BIAS_TILES - 1:2]
        second = scaled[:, p + 1:p + 2 * ATT_BIAS_TILES:2]
        per_parity.append(jnp.concatenate(
            [first[..., WIN_COLS - 1:], gap, second, gap, first[..., :WIN_COLS - 1]], axis=-1))
    gen = jnp.stack(per_parity)
    gen = gen.reshape(2, ATT_GROUPS, ATT_GROUP, ATT_BIAS_TILES, LANES).transpose(0, 1, 3, 2, 4)
    return gen.reshape(2 * ATT_GROUPS * ATT_BIAS_TILES * ATT_GROUP, 1, LANES)


def _attention_window_mask():
    qc = np.arange(GRID_W)[:, None]
    kc = np.arange(LANES)[None, :] % GRID_W
    col_start = np.clip(qc - WIN_COLS // 2, 0, GRID_W - WIN_COLS)
    return ((kc >= col_start) & (kc < col_start + WIN_COLS)).astype(np.float32)


def _attention_kernel(q_ref, k_ref, v_ref, g_ref, gen_ref, win_ref, o_ref, bias_scr, *, rows, kr):
    step = pl.program_id(1)
    gw = ATT_GROUP * ATT_HEAD_DIM
    row_head = lax.broadcasted_iota(jnp.int32, (ATT_GROUP * GRID_W, gw), 0) // GRID_W
    lane_head = lax.broadcasted_iota(jnp.int32, (ATT_GROUP * GRID_W, gw), 1) // ATT_HEAD_DIM
    own_head = row_head == lane_head
    out_lane_head = lax.broadcasted_iota(jnp.int32, (GRID_W, gw), 1) // ATT_HEAD_DIM

    @pl.when((pl.program_id(0) == 0) & (step == 0))
    def _build_bias():
        in_window = win_ref[...] > 0.0

        def one_block(n, carry):
            gen = jnp.broadcast_to(gen_ref[n], (GRID_W, LANES))
            toeplitz = pltpu.roll(gen, 0, 1, stride=1, stride_axis=0)
            head_rows = pl.ds(pl.multiple_of((n % ATT_GROUP) * GRID_W, GRID_W), GRID_W)
            bias_scr[n // ATT_GROUP, head_rows, :] = jnp.where(in_window, toeplitz, MASK_VALUE)
            return carry

        lax.fori_loop(0, gen_ref.shape[0], one_block, 0, unroll=ATT_BIAS_UNROLL)

    def one_row(j, carry):
        r = step * ATT_ROWS_PER_STEP + j
        row_start = jnp.clip(r - kr // 2, 0, rows - kr)
        variant = row_start - r + (WIN_ROWS - 1)
        q_tok = pl.multiple_of(j * GRID_W, GRID_W)
        k_tok = pl.multiple_of(row_start * GRID_W, GRID_W)
        for grp in range(ATT_GROUPS):
            lanes = slice(grp * gw, (grp + 1) * gw)
            q = q_ref[0, pl.ds(q_tok, GRID_W), lanes]
            q_bd = jnp.where(own_head, jnp.concatenate([q] * ATT_GROUP, axis=0), 0)
            keys = k_ref[0, pl.ds(k_tok, kr * GRID_W), lanes]
            vals = v_ref[0, pl.ds(k_tok, kr * GRID_W), lanes]
            tile0 = ((variant % 2) * ATT_GROUPS + grp) * ATT_BIAS_TILES + variant // 2
            bias = jnp.concatenate([bias_scr[tile0 + t] for t in range(kr // 2)], axis=1)
            s = _dot_nt(q_bd, keys) + bias
            m = jnp.max(s, axis=-1, keepdims=True)
            p = jnp.exp2(s - m)
            denom = jnp.sum(p, axis=-1, keepdims=True)
            pv = _dot(p.astype(BF16), vals) / denom
            o = jnp.zeros((GRID_W, gw), F32)
            for h in range(ATT_GROUP):
                o = o + jnp.where(out_lane_head == h, pv[h * GRID_W:(h + 1) * GRID_W], 0.0)
            gate = g_ref[0, pl.ds(q_tok, GRID_W), lanes].astype(F32)
            o = o * _silu_from_half(gate)
            o_ref[0, pl.ds(q_tok, GRID_W), lanes] = o.astype(o_ref.dtype)
        return carry

    lax.fori_loop(0, ATT_ROWS_PER_STEP, one_row, 0, unroll=ATT_UNROLL)


def _attention(q, k, v, g, rpb):
    batch, seq, width = q.shape
    rows = seq // GRID_W
    kr = min(WIN_ROWS, rows)
    assert rows % ATT_ROWS_PER_STEP == 0 and ATT_ROWS_PER_STEP % ATT_UNROLL == 0
    assert kr == WIN_ROWS and 2 * GRID_W == LANES
    gen = _attention_bias_rows(rpb)
    win = jnp.asarray(_attention_window_mask())
    blk = ATT_ROWS_PER_STEP * GRID_W
    tile = pl.BlockSpec((1, blk, width), lambda b, i: (b, i, 0))
    whole = pl.BlockSpec((1, seq, width), lambda b, i: (b, 0, 0))
    return pl.pallas_call(
        functools.partial(_attention_kernel, rows=rows, kr=kr),
        grid=(batch, rows // ATT_ROWS_PER_STEP),
        in_specs=[tile, whole, whole, tile,
                  pl.BlockSpec(gen.shape, lambda b, i: (0, 0, 0)),
                  pl.BlockSpec(win.shape, lambda b, i: (0, 0))],
        out_specs=tile,
        out_shape=jax.ShapeDtypeStruct((batch, seq, width), BF16),
        scratch_shapes=[pltpu.VMEM((2 * ATT_GROUPS * ATT_BIAS_TILES, ATT_GROUP * GRID_W, 2 * GRID_W), F32)],
        compiler_params=pltpu.CompilerParams(
            dimension_semantics=("arbitrary", "arbitrary"), vmem_limit_bytes=VMEM_LIMIT_BYTES),
        name="nbr_attention",
    )(q, k, v, g, gen, win)


def _hgrn_fine_levels():
    sizes = []
    c = HG_GROUP
    while c >= 2:
        sizes.append(c)
        c //= 2
    return sizes


def _hgrn_group_pairs(direction, chunk):
    n = chunk // HG_GROUP
    if direction == 0:
        return [(t, g) for t in range(1, n) for g in range(t)]
    return [(t, g) for t in range(n - 2, -1, -1) for g in range(t + 1, n)]


def _hgrn_constants(chunk):
    t = np.arange(chunk)[:, None]
    u = np.arange(chunk)[None, :]
    blocks = [(u <= t)]
    masks = []
    for c in _hgrn_fine_levels():
        half = c // 2
        mid = (t // c) * c + half
        upper = (t % c) >= half
        if 2 < c < HG_GROUP:
            blocks.append(np.where(upper, (u >= mid) & (u <= t), (u > t) & (u <= mid - 1)))
        masks.append((t // c == u // c) & upper & ((u % c) < half))
    fwd_sums = np.stack(blocks).astype(np.float32)
    fwd_masks = np.stack(masks).astype(np.float32)
    twice = lambda m: np.concatenate([m.reshape(-1, chunk)] * 2, axis=1)
    sums = np.stack([twice(fwd_sums), twice(fwd_sums[:, ::-1, ::-1])])
    masks = np.concatenate([fwd_masks, fwd_masks[:, ::-1, ::-1]], axis=2)
    return sums, masks


def _block_diag(a, b):
    zero = jnp.zeros_like(a)
    return jnp.concatenate([jnp.concatenate([a, zero], axis=1),
                            jnp.concatenate([zero, b], axis=1)], axis=0)


def _hgrn_dec_rows(chunk):
    n_var = len(_hgrn_group_pairs(0, chunk)) * HG_GROUP
    n_mm = sum(c > 2 for c in _hgrn_fine_levels())
    edge, var, fine = 2 * chunk, 3 * chunk, 3 * chunk + n_var
    return edge, var, fine, fine + n_mm * chunk


def _hgrn_kernel(q_ref, zf_ref, zb_ref, i_ref, g_ref, lbf_ref, lbb_ref, gain_ref,
                 sums_ref, masks_ref, o_ref, of_scr, ob_scr, sf_scr, sb_scr,
                 k_scr0, k_scr1, dec_scr0, dec_scr1, *, layer, seq):
    chunk = HG_CHUNK
    pair = 2 * chunk
    n_pairs = seq // pair
    dh = HG_HEAD_DIM
    grp = HG_GROUP
    n_groups = chunk // grp
    fine = _hgrn_fine_levels()
    group_pairs = [_hgrn_group_pairs(direction, chunk) for direction in (0, 1)]
    r_edge, r_var, r_fine, _ = _hgrn_dec_rows(chunk)
    k_slots = (k_scr0, k_scr1)
    dec_slots = (dec_scr0, dec_scr1)
    row_in_group = lax.broadcasted_iota(jnp.int32, (grp, dh), 0)
    lane_group = lax.broadcasted_iota(jnp.int32, (grp, 2 * chunk), 1) // grp

    def lower_bound(logit_ref):
        logits = logit_ref[...].astype(F32)
        e = jnp.exp(logits - jnp.max(logits, axis=0, keepdims=True))
        return jnp.sum(e[:layer + 1], axis=0, keepdims=True) / jnp.sum(e, axis=0, keepdims=True)

    def gates(z, lb):
        f = lb + (1.0 - lb) * jax.nn.sigmoid(z.astype(F32))
        return jnp.log2(f), 1.0 - f

    def store_decays(slot, direction, g2):
        hi = g2.astype(BF16)
        lo = (g2 - hi.astype(F32)).astype(BF16)
        stacked = [jnp.concatenate([hi[c * chunk:(c + 1) * chunk], lo[c * chunk:(c + 1) * chunk]], axis=0)
                   for c in range(2)]
        raw = _dot(sums_ref[direction], jnp.concatenate(stacked, axis=1))
        dec = dec_slots[slot].at[direction]
        cum = raw[0:chunk]
        whole = cum[chunk - 1:chunk] if direction == 0 else cum[0:1]
        dec[0:chunk, :] = jnp.exp2(cum)
        dec[chunk:2 * chunk, :] = jnp.exp2(whole - cum)
        by_group = cum.reshape(n_groups, grp, 2 * dh)
        mid_row = grp // 2 - 1 if direction == 0 else grp // 2
        to_mid = -jnp.abs(by_group - by_group[:, mid_row:mid_row + 1, :])
        dec[r_fine:r_fine + chunk, :] = jnp.exp2(to_mid.reshape(chunk, 2 * dh))
        dec[r_fine + chunk:, :] = jnp.exp2(raw[chunk:])
        edge = grp - 1 if direction == 0 else 0
        edges = [cum[g * grp + edge:g * grp + edge + 1] for g in range(n_groups)]
        group = lambda g: cum[g * grp:(g + 1) * grp]
        dec[r_edge:r_var, :] = jnp.exp2(jnp.concatenate([edges[g] - group(g) for g in range(n_groups)], axis=0))
        dec[r_var:r_fine, :] = jnp.exp2(jnp.concatenate([group(t) - edges[g]
                                                         for t, g in group_pairs[direction]], axis=0))

    def fine_operand(direction, lvl, q, k, dec):
        c = fine[lvl]
        pos = row_in_group % c
        is_upper = (pos >= c // 2) if direction == 0 else (pos < c // 2)
        pick = lambda x, y: jnp.where(is_upper[None], x.reshape(n_groups, grp, dh),
                                      y.reshape(n_groups, grp, dh)).reshape(chunk, dh)
        return (pick(q, k) * dec(r_fine + lvl * chunk, r_fine + (lvl + 1) * chunk)).astype(BF16)

    def neighbour_weight(direction, q, k):
        shift = 1 if direction == 0 else grp - 1
        k_other = pltpu.roll(k.reshape(n_groups, grp, dh), shift, 1).reshape(chunk, dh)
        return jnp.sum(q * (1.0 - k) * k_other, axis=-1, keepdims=True)

    def chunk_part(slot, direction, c, q, v, base):
        rows = slice(c * chunk, (c + 1) * chunk)
        lanes = slice(c * dh, (c + 1) * dh)
        return dict(q=q[rows], k=k_slots[slot][direction, rows, :], v=v[rows],
                    dec=lambda r0, r1: dec_slots[slot][direction, r0:r1, lanes],
                    rows=pl.ds(pl.multiple_of(base + c * chunk, chunk), chunk))

    def pair_step(f, b):
        lhs = jnp.concatenate(
            [(jnp.concatenate([p["q"][t * grp:(t + 1) * grp] for t, _ in group_pairs[d]], axis=0)
              * p["dec"](r_var, r_fine)).astype(BF16) for d, p in enumerate((f, b))], axis=1)
        keys = [(p["k"] * p["dec"](r_edge, r_var)).astype(BF16) for p in (f, b)]
        r = _dot_nt(lhs, _block_diag(*keys))
        rows_of = [jnp.zeros((grp, 2 * chunk), F32)] * n_groups
        for i, ((tf, gf), (tb, gb)) in enumerate(zip(*group_pairs)):
            block = r[i * grp:(i + 1) * grp]
            rows_of[tf] = jnp.where(lane_group == gf, block, rows_of[tf])
            rows_of[tb] = jnp.where(lane_group == n_groups + gb, block, rows_of[tb])
        a = jnp.concatenate(rows_of, axis=0)
        for lvl, c in enumerate(fine):
            if c == 2:
                pair_w = jnp.where(lane_group[0:1] < n_groups, neighbour_weight(0, f["q"], f["k"]),
                                   neighbour_weight(1, b["q"], b["k"]))
                a = a + masks_ref[lvl] * pair_w
                continue
            xf = fine_operand(0, lvl, f["q"], f["k"], f["dec"])
            xb = fine_operand(1, lvl, b["q"], b["k"], b["dec"])
            a = a + masks_ref[lvl] * _dot_nt(jnp.concatenate([xf, xb], axis=1), _block_diag(xf, xb))
        o = _dot(a.astype(BF16), _block_diag(f["v"], b["v"]))
        sf = sf_scr[...]
        sb = sb_scr[...]
        q_dec = jnp.concatenate([(f["q"] * f["dec"](0, chunk)).astype(BF16),
                                 (b["q"] * b["dec"](0, chunk)).astype(BF16)], axis=1)
        o = o + _dot_nt(q_dec, _block_diag(sf.astype(BF16), sb.astype(BF16)))
        diag_f = jnp.sum(f["q"] * f["k"], axis=-1, keepdims=True)
        diag_b = jnp.sum(b["q"] * b["k"], axis=-1, keepdims=True)
        of_scr[f["rows"], :] = o[:, :dh] + diag_f * f["v"].astype(F32)
        ob_scr[b["rows"], :] = o[:, dh:] + diag_b * b["v"].astype(F32)
        k_dec = _block_diag((f["k"] * f["dec"](chunk, 2 * chunk)).astype(BF16),
                            (b["k"] * b["dec"](chunk, 2 * chunk)).astype(BF16))
        v_t = jnp.concatenate([f["v"], b["v"]], axis=0).astype(F32).T.astype(BF16)
        u = _dot(v_t, k_dec)
        sf_scr[...] = sf * f["dec"](chunk - 1, chunk) + u[:, :dh]
        sb_scr[...] = sb * b["dec"](0, 1) + u[:, dh:]

    lb_f = lower_bound(lbf_ref)
    lb_b = lower_bound(lbb_ref)
    sf_scr[...] = jnp.zeros_like(sf_scr)
    sb_scr[...] = jnp.zeros_like(sb_scr)

    def bases(n):
        return (pl.multiple_of(n * pair, pair), pl.multiple_of((n_pairs - 1 - n) * pair, pair))

    def prepare(n, slot):
        for direction, base, z_ref, lb in zip((0, 1), bases(n), (zf_ref, zb_ref), (lb_f, lb_b)):
            g2, k = gates(z_ref[0, pl.ds(base, pair), :], lb)
            k_slots[slot][direction] = k
            store_decays(slot, direction, g2)

    def consume(n, slot):
        parts = []
        for direction, base in zip((0, 1), bases(n)):
            rows = pl.ds(base, pair)
            q = q_ref[0, rows, :].astype(F32)
            v = i_ref[0, rows, :]
            parts.append([chunk_part(slot, direction, c, q, v, base) for c in range(2)])
        pair_step(parts[0][0], parts[1][1])
        pair_step(parts[0][1], parts[1][0])

    def scan_body(m, carry):
        prepare(2 * m + 1, 1)
        consume(2 * m, 0)
        prepare(jnp.minimum(2 * m + 2, n_pairs - 1), 0)
        consume(2 * m + 1, 1)
        return carry

    prepare(0, 0)
    lax.fori_loop(0, n_pairs // 2, scan_body, 0, unroll=HG_SCAN_UNROLL)

    gain = gain_ref[...].astype(F32)

    def norm_body(n, carry):
        rows = pl.ds(pl.multiple_of(n * HG_NORM_ROWS, HG_NORM_ROWS), HG_NORM_ROWS)
        o = of_scr[rows, :] + ob_scr[rows, :]
        o = o * lax.rsqrt(jnp.mean(jnp.square(o), axis=-1, keepdims=True) + RMS_EPS) * gain
        gate = g_ref[0, rows, :].astype(F32)
        o_ref[0, rows, :] = (o * _silu_from_half(gate)).astype(o_ref.dtype)
        return carry

    lax.fori_loop(0, seq // HG_NORM_ROWS, norm_body, 0, unroll=4)


def _hgrn(q, zf, zb, i, g, lb_fwd_logits, lb_bwd_logits, gain_row, layer):
    batch, seq, width = q.shape
    assert width == HG_HEADS * HG_HEAD_DIM and seq % (4 * HG_NORM_ROWS) == 0
    assert seq % (4 * HG_CHUNK * HG_SCAN_UNROLL) == 0
    sums, masks = _hgrn_constants(HG_CHUNK)
    sums = jnp.asarray(sums, BF16)
    masks = jnp.asarray(masks, F32)
    n_layers = lb_fwd_logits.shape[0]
    n_dec_rows = _hgrn_dec_rows(HG_CHUNK)[-1]
    head = pl.BlockSpec((1, seq, HG_HEAD_DIM), lambda b, h: (b, 0, h))
    per_head_row = lambda n: pl.BlockSpec((n, HG_HEAD_DIM), lambda b, h: (0, h))
    whole = lambda a: pl.BlockSpec(a.shape, lambda b, h: (0,) * a.ndim)
    return pl.pallas_call(
        functools.partial(_hgrn_kernel, layer=layer, seq=seq),
        grid=(batch, HG_HEADS),
        in_specs=[head, head, head, head, head,
                  per_head_row(n_layers), per_head_row(n_layers), per_head_row(1),
                  whole(sums), whole(masks)],
        out_specs=head,
        out_shape=jax.ShapeDtypeStruct((batch, seq, width), BF16),
        scratch_shapes=[pltpu.VMEM((seq, HG_HEAD_DIM), F32), pltpu.VMEM((seq, HG_HEAD_DIM), F32),
                        pltpu.VMEM((HG_HEAD_DIM, HG_HEAD_DIM), F32),
                        pltpu.VMEM((HG_HEAD_DIM, HG_HEAD_DIM), F32),
                        pltpu.VMEM((2, 2 * HG_CHUNK, HG_HEAD_DIM), F32),
                        pltpu.VMEM((2, 2 * HG_CHUNK, HG_HEAD_DIM), F32),
                        pltpu.VMEM((2, n_dec_rows, 2 * HG_HEAD_DIM), F32),
                        pltpu.VMEM((2, n_dec_rows, 2 * HG_HEAD_DIM), F32)],
        compiler_params=pltpu.CompilerParams(
            dimension_semantics=("arbitrary", "arbitrary"), vmem_limit_bytes=VMEM_LIMIT_BYTES),
        name="hgrn2_scan",
    )(q, zf, zb, i, g, lb_fwd_logits, lb_bwd_logits, gain_row, sums, masks)


def _out_proj_kernel(oa_ref, oh_ref, x_ref, w_ref, b_ref, gain_ref, bias_ref, o_ref, w_scr, *, alpha):
    d_att = oa_ref.shape[1]
    _cast_weight_once(w_ref, w_scr)
    for r0 in range(0, o_ref.shape[0], OUT_PROJ_ROWS):
        rows = slice(r0, r0 + OUT_PROJ_ROWS)
        y = _dot(oa_ref[rows, :], w_scr[:d_att, :]) + _dot(oh_ref[rows, :], w_scr[d_att:, :]) + b_ref[...]
        r = alpha * x_ref[rows, :] + y
        mu = jnp.mean(r, axis=-1, keepdims=True)
        c = r - mu
        var = jnp.mean(jnp.square(c), axis=-1, keepdims=True)
        o_ref[rows, :] = (c * lax.rsqrt(var + LN_EPS) * gain_ref[...] + bias_ref[...]).astype(o_ref.dtype)


def _out_proj(o_a, o_h, x2d, w, b_row, gain_row, bias_row, alpha):
    n_tok, d_model = x2d.shape
    d_att, d_hg = o_a.shape[1], o_h.shape[1]
    row = pl.BlockSpec((1, d_model), lambda i: (0, 0))
    return pl.pallas_call(
        functools.partial(_out_proj_kernel, alpha=alpha),
        grid=(n_tok // OUT_PROJ_TM,),
        in_specs=[pl.BlockSpec((OUT_PROJ_TM, d_att), lambda i: (i, 0)),
                  pl.BlockSpec((OUT_PROJ_TM, d_hg), lambda i: (i, 0)),
                  pl.BlockSpec((OUT_PROJ_TM, d_model), lambda i: (i, 0)),
                  pl.BlockSpec((d_att + d_hg, d_model), lambda i: (0, 0), pipeline_mode=pl.Buffered(1)),
                  row, row, row],
        out_specs=pl.BlockSpec((OUT_PROJ_TM, d_model), lambda i: (i, 0)),
        out_shape=jax.ShapeDtypeStruct((n_tok, d_model), x2d.dtype),
        scratch_shapes=[pltpu.VMEM((d_att + d_hg, d_model), BF16)],
        compiler_params=pltpu.CompilerParams(
            dimension_semantics=("arbitrary",), vmem_limit_bytes=VMEM_LIMIT_BYTES),
        name="out_proj_layernorm",
    )(o_a, o_h, x2d, w, b_row, gain_row, bias_row)


def _layer(x, layer, depth, w_in, b_in, rpb, lb_fwd_logits, lb_bwd_logits, hg_norm_gain,
           w_out, b_out, ln_gain, ln_bias):
    batch, seq, d_model = x.shape
    x2d = x.reshape(batch * seq, d_model)
    slabs = _in_proj(x2d, w_in, b_in.reshape(1, -1))
    q_a, k_a, v_a, g_a, q_h, z_f, z_b, i_h, g_h = [s.reshape(batch, seq, SLAB) for s in slabs]
    o_a = _attention(q_a, k_a, v_a, g_a, rpb)
    o_h = _hgrn(q_h, z_f, z_b, i_h, g_h, lb_fwd_logits, lb_bwd_logits,
                hg_norm_gain.reshape(1, -1), layer)
    alpha = (2.0 * depth) ** 0.25
    out = _out_proj(o_a.reshape(batch * seq, SLAB), o_h.reshape(batch * seq, SLAB), x2d,
                    w_out, b_out.reshape(1, -1), ln_gain.reshape(1, -1),
                    ln_bias.reshape(1, -1), alpha)
    return out.reshape(batch, seq, d_model)


def kernel(x, w_in, b_in, rpb, lb_fwd_logits, lb_bwd_logits, hg_norm_gain, w_out, b_out, ln_gain, ln_bias):
    depth = w_in.shape[0]
    for layer in range(depth):
        x = _layer(x, layer, depth, w_in[layer], b_in[layer], rpb[layer], lb_fwd_logits,
                   lb_bwd_logits, hg_norm_gain[layer], w_out[layer], b_out[layer],
                   ln_gain[layer], ln_bias[layer])
    return x
```

```python
import functools

import numpy as np
import jax
import jax.numpy as jnp
from jax import lax
from jax.experimental import pallas as pl
from jax.experimental.pallas import tpu as pltpu

GRID_W = 64
ATT_HEADS = 8
ATT_HEAD_DIM = 64
HG_HEADS = 4
HG_HEAD_DIM = 128
WIN_ROWS = 8
WIN_COLS = 16
LN_EPS = 1e-5
RMS_EPS = 1e-6
N_SLABS = 9
SLAB = 512

LANES = 128
F32_SUBLANES = 8
VMEM_LIMIT_BYTES = 56 * 1024 * 1024
PROJ_TM = 512
OUT_PROJ_TM = 1024
OUT_PROJ_ROWS = 256
ATT_ROWS_PER_STEP = 16
ATT_UNROLL = 4
ATT_GROUP = 4
ATT_BIAS_UNROLL = 8
HG_CHUNK = 64
HG_GROUP = F32_SUBLANES
HG_NORM_ROWS = 256
HG_SCAN_UNROLL = 16
MASK_VALUE = -1e30
LOG2_E = 1.4426950408889634
SLAB_SCALES = {1: ATT_HEAD_DIM ** -0.5 * LOG2_E, 3: 0.5, 8: 0.5}

BF16 = jnp.bfloat16
F32 = jnp.float32


def _silu_from_half(half_g):
    return half_g * (1.0 + jnp.tanh(half_g))


def _dot(a, b):
    return jnp.dot(a, b, preferred_element_type=F32)


def _dot_nt(a, b):
    return lax.dot_general(a, b, (((1,), (1,)), ((), ())), preferred_element_type=F32)


def _cast_weight_once(w_ref, w_scr):
    @pl.when(pl.program_id(0) == 0)
    def _():
        for c0 in range(0, w_ref.shape[1], SLAB):
            w_scr[:, c0:c0 + SLAB] = w_ref[:, c0:c0 + SLAB].astype(w_scr.dtype)


def _in_proj_kernel(x_ref, w_ref, b_ref, *refs):
    out_refs, w_scr = refs[:-1], refs[-1]
    _cast_weight_once(w_ref, w_scr)
    xb = x_ref[...].astype(BF16)
    for j, o_ref in enumerate(out_refs):
        cols = slice(j * SLAB, (j + 1) * SLAB)
        h = _dot(xb, w_scr[:, cols]) + b_ref[:, cols]
        if j in SLAB_SCALES:
            h = h * SLAB_SCALES[j]
        o_ref[...] = h.astype(o_ref.dtype)


def _in_proj(x2d, w, b_row):
    n_tok, d_model = x2d.shape
    d_in = w.shape[1]
    assert d_in == N_SLABS * SLAB and n_tok % PROJ_TM == 0
    return pl.pallas_call(
        _in_proj_kernel,
        grid=(n_tok // PROJ_TM,),
        in_specs=[
            pl.BlockSpec((PROJ_TM, d_model), lambda i: (i, 0)),
            pl.BlockSpec((d_model, d_in), lambda i: (0, 0), pipeline_mode=pl.Buffered(1)),
            pl.BlockSpec((1, d_in), lambda i: (0, 0)),
        ],
        out_specs=[pl.BlockSpec((PROJ_TM, SLAB), lambda i: (i, 0))] * N_SLABS,
        out_shape=[jax.ShapeDtypeStruct((n_tok, SLAB), BF16)] * N_SLABS,
        scratch_shapes=[pltpu.VMEM((d_model, d_in), BF16)],
        compiler_params=pltpu.CompilerParams(
            dimension_semantics=("arbitrary",), vmem_limit_bytes=VMEM_LIMIT_BYTES),
        name="in_proj",
    )(x2d, w, b_row)


ATT_REL_ROWS = 2 * WIN_ROWS - 1
ATT_BIAS_TILES = (ATT_REL_ROWS - 1) // 2
ATT_GROUPS = ATT_HEADS // ATT_GROUP


def _attention_bias_rows(rpb):
    n_rel_cols = 2 * WIN_COLS - 1
    scaled = rpb.astype(F32) * LOG2_E
    gap = jnp.zeros((ATT_HEADS, ATT_BIAS_TILES, (LANES - 2 * n_rel_cols) // 2), F32)
    per_parity = []
    for p in range(2):
        first = scaled[:, p:p + 2 * ATT_BIAS_TILES - 1:2]
        second = scaled[:, p + 1:p + 2 * ATT_BIAS_TILES:2]
        per_parity.append(jnp.concatenate(
            [first[..., WIN_COLS - 1:], gap, second, gap, first[..., :WIN_COLS - 1]], axis=-1))
    gen = jnp.stack(per_parity)
    gen = gen.reshape(2, ATT_GROUPS, ATT_GROUP, ATT_BIAS_TILES, LANES).transpose(0, 1, 3, 2, 4)
    return gen.reshape(2 * ATT_GROUPS * ATT_BIAS_TILES * ATT_GROUP, 1, LANES)


def _attention_window_mask():
    qc = np.arange(GRID_W)[:, None]
    kc = np.arange(LANES)[None, :] % GRID_W
    col_start = np.clip(qc - WIN_COLS // 2, 0, GRID_W - WIN_COLS)
    return ((kc >= col_start) & (kc < col_start + WIN_COLS)).astype(np.float32)


def _attention_kernel(q_ref, k_ref, v_ref, g_ref, gen_ref, win_ref, o_ref, bias_scr, *, rows, kr):
    step = pl.program_id(1)
    gw = ATT_GROUP * ATT_HEAD_DIM
    row_head = lax.broadcasted_iota(jnp.int32, (ATT_GROUP * GRID_W, gw), 0) // GRID_W
    lane_head = lax.broadcasted_iota(jnp.int32, (ATT_GROUP * GRID_W, gw), 1) // ATT_HEAD_DIM
    own_head = row_head == lane_head
    out_lane_head = lax.broadcasted_iota(jnp.int32, (GRID_W, gw), 1) // ATT_HEAD_DIM

    @pl.when((pl.program_id(0) == 0) & (step == 0))
    def _build_bias():
        in_window = win_ref[...] > 0.0

        def one_block(n, carry):
            gen = jnp.broadcast_to(gen_ref[n], (GRID_W, LANES))
            toeplitz = pltpu.roll(gen, 0, 1, stride=1, stride_axis=0)
            head_rows = pl.ds(pl.multiple_of((n % ATT_GROUP) * GRID_W, GRID_W), GRID_W)
            bias_scr[n // ATT_GROUP, head_rows, :] = jnp.where(in_window, toeplitz, MASK_VALUE)
            return carry

        lax.fori_loop(0, gen_ref.shape[0], one_block, 0, unroll=ATT_BIAS_UNROLL)

    def one_row(j, carry):
        r = step * ATT_ROWS_PER_STEP + j
        row_start = jnp.clip(r - kr // 2, 0, rows - kr)
        variant = row_start - r + (WIN_ROWS - 1)
        q_tok = pl.multiple_of(j * GRID_W, GRID_W)
        k_tok = pl.multiple_of(row_start * GRID_W, GRID_W)
        for grp in range(ATT_GROUPS):
            lanes = slice(grp * gw, (grp + 1) * gw)
            q = q_ref[0, pl.ds(q_tok, GRID_W), lanes]
            q_bd = jnp.where(own_head, jnp.concatenate([q] * ATT_GROUP, axis=0), 0)
            keys = k_ref[0, pl.ds(k_tok, kr * GRID_W), lanes]
            vals = v_ref[0, pl.ds(k_tok, kr * GRID_W), lanes]
            tile0 = ((variant % 2) * ATT_GROUPS + grp) * ATT_BIAS_TILES + variant // 2
            bias = jnp.concatenate([bias_scr[tile0 + t] for t in range(kr // 2)], axis=1)
            s = _dot_nt(q_bd, keys) + bias
            m = jnp.max(s, axis=-1, keepdims=True)
            p = jnp.exp2(s - m)
            denom = jnp.sum(p, axis=-1, keepdims=True)
            pv = _dot(p.astype(BF16), vals) / denom
            o = jnp.zeros((GRID_W, gw), F32)
            for h in range(ATT_GROUP):
                o = o + jnp.where(out_lane_head == h, pv[h * GRID_W:(h + 1) * GRID_W], 0.0)
            gate = g_ref[0, pl.ds(q_tok, GRID_W), lanes].astype(F32)
            o = o * _silu_from_half(gate)
            o_ref[0, pl.ds(q_tok, GRID_W), lanes] = o.astype(o_ref.dtype)
        return carry

    lax.fori_loop(0, ATT_ROWS_PER_STEP, one_row, 0, unroll=ATT_UNROLL)


def _attention(q, k, v, g, rpb):
    batch, seq, width = q.shape
    rows = seq // GRID_W
    kr = min(WIN_ROWS, rows)
    assert rows % ATT_ROWS_PER_STEP == 0 and ATT_ROWS_PER_STEP % ATT_UNROLL == 0
    assert kr == WIN_ROWS and 2 * GRID_W == LANES
    gen = _attention_bias_rows(rpb)
    win = jnp.asarray(_attention_window_mask())
    blk = ATT_ROWS_PER_STEP * GRID_W
    tile = pl.BlockSpec((1, blk, width), lambda b, i: (b, i, 0))
    whole = pl.BlockSpec((1, seq, width), lambda b, i: (b, 0, 0))
    return pl.pallas_call(
        functools.partial(_attention_kernel, rows=rows, kr=kr),
        grid=(batch, rows // ATT_ROWS_PER_STEP),
        in_specs=[tile, whole, whole, tile,
                  pl.BlockSpec(gen.shape, lambda b, i: (0, 0, 0)),
                  pl.BlockSpec(win.shape, lambda b, i: (0, 0))],
        out_specs=tile,
        out_shape=jax.ShapeDtypeStruct((batch, seq, width), BF16),
        scratch_shapes=[pltpu.VMEM((2 * ATT_GROUPS * ATT_BIAS_TILES, ATT_GROUP * GRID_W, 2 * GRID_W), F32)],
        compiler_params=pltpu.CompilerParams(
            dimension_semantics=("arbitrary", "arbitrary"), vmem_limit_bytes=VMEM_LIMIT_BYTES),
        name="nbr_attention",
    )(q, k, v, g, gen, win)


def _hgrn_fine_levels():
    sizes = []
    c = HG_GROUP
    while c >= 2:
        sizes.append(c)
        c //= 2
    return sizes


def _hgrn_group_pairs(direction, chunk):
    n = chunk // HG_GROUP
    if direction == 0:
        return [(t, g) for t in range(1, n) for g in range(t)]
    return [(t, g) for t in range(n - 2, -1, -1) for g in range(t + 1, n)]


def _hgrn_constants(chunk):
    t = np.arange(chunk)[:, None]
    u = np.arange(chunk)[None, :]
    blocks = [(u <= t)]
    masks = []
    for c in _hgrn_fine_levels():
        half = c // 2
        mid = (t // c) * c + half
        upper = (t % c) >= half
        if 2 < c < HG_GROUP:
            blocks.append(np.where(upper, (u >= mid) & (u <= t), (u > t) & (u <= mid - 1)))
        masks.append((t // c == u // c) & upper & ((u % c) < half))
    fwd_sums = np.stack(blocks).astype(np.float32)
    fwd_masks = np.stack(masks).astype(np.float32)
    twice = lambda m: np.concatenate([m.reshape(-1, chunk)] * 2, axis=1)
    sums = np.stack([twice(fwd_sums), twice(fwd_sums[:, ::-1, ::-1])])
    masks = np.concatenate([fwd_masks, fwd_masks[:, ::-1, ::-1]], axis=2)
    return sums, masks


def _block_diag(a, b):
    zero = jnp.zeros_like(a)
    return jnp.concatenate([jnp.concatenate([a, zero], axis=1),
                            jnp.concatenate([zero, b], axis=1)], axis=0)


def _hgrn_dec_rows(chunk):
    n_var = len(_hgrn_group_pairs(0, chunk)) * HG_GROUP
    n_mm = sum(c > 2 for c in _hgrn_fine_levels())
    edge, var, fine = 2 * chunk, 3 * chunk, 3 * chunk + n_var
    return edge, var, fine, fine + n_mm * chunk


def _hgrn_kernel(q_ref, zf_ref, zb_ref, i_ref, g_ref, lbf_ref, lbb_ref, gain_ref,
                 sums_ref, masks_ref, o_ref, of_scr, ob_scr, sf_scr, sb_scr,
                 k_scr0, k_scr1, dec_scr0, dec_scr1, *, layer, seq):
    chunk = HG_CHUNK
    pair = 2 * chunk
    n_pairs = seq // pair
    dh = HG_HEAD_DIM
    grp = HG_GROUP
    n_groups = chunk // grp
    fine = _hgrn_fine_levels()
    group_pairs = [_hgrn_group_pairs(direction, chunk) for direction in (0, 1)]
    r_edge, r_var, r_fine, _ = _hgrn_dec_rows(chunk)
    k_slots = (k_scr0, k_scr1)
    dec_slots = (dec_scr0, dec_scr1)
    row_in_group = lax.broadcasted_iota(jnp.int32, (grp, dh), 0)
    lane_group = lax.broadcasted_iota(jnp.int32, (grp, 2 * chunk), 1) // grp

    def lower_bound(logit_ref):
        logits = logit_ref[...].astype(F32)
        e = jnp.exp(logits - jnp.max(logits, axis=0, keepdims=True))
        return jnp.sum(e[:layer + 1], axis=0, keepdims=True) / jnp.sum(e, axis=0, keepdims=True)

    def gates(z, lb):
        f = lb + (1.0 - lb) * jax.nn.sigmoid(z.astype(F32))
        return jnp.log2(f), 1.0 - f

    def store_decays(slot, direction, g2):
        hi = g2.astype(BF16)
        lo = (g2 - hi.astype(F32)).astype(BF16)
        stacked = [jnp.concatenate([hi[c * chunk:(c + 1) * chunk], lo[c * chunk:(c + 1) * chunk]], axis=0)
                   for c in range(2)]
        raw = _dot(sums_ref[direction], jnp.concatenate(stacked, axis=1))
        dec = dec_slots[slot].at[direction]
        cum = raw[0:chunk]
        whole = cum[chunk - 1:chunk] if direction == 0 else cum[0:1]
        dec[0:chunk, :] = jnp.exp2(cum)
        dec[chunk:2 * chunk, :] = jnp.exp2(whole - cum)
        by_group = cum.reshape(n_groups, grp, 2 * dh)
        mid_row = grp // 2 - 1 if direction == 0 else grp // 2
        to_mid = -jnp.abs(by_group - by_group[:, mid_row:mid_row + 1, :])
        dec[r_fine:r_fine + chunk, :] = jnp.exp2(to_mid.reshape(chunk, 2 * dh))
        dec[r_fine + chunk:, :] = jnp.exp2(raw[chunk:])
        edge = grp - 1 if direction == 0 else 0
        edges = [cum[g * grp + edge:g * grp + edge + 1] for g in range(n_groups)]
        group = lambda g: cum[g * grp:(g + 1) * grp]
        dec[r_edge:r_var, :] = jnp.exp2(jnp.concatenate([edges[g] - group(g) for g in range(n_groups)], axis=0))
        dec[r_var:r_fine, :] = jnp.exp2(jnp.concatenate([group(t) - edges[g]
                                                         for t, g in group_pairs[direction]], axis=0))

    def fine_operand(direction, lvl, q, k, dec):
        c = fine[lvl]
        pos = row_in_group % c
        is_upper = (pos >= c // 2) if direction == 0 else (pos < c // 2)
        pick = lambda x, y: jnp.where(is_upper[None], x.reshape(n_groups, grp, dh),
                                      y.reshape(n_groups, grp, dh)).reshape(chunk, dh)
        if c == 2:
            return pick(q * (1.0 - k), k).astype(BF16)
        return (pick(q, k) * dec(r_fine + lvl * chunk, r_fine + (lvl + 1) * chunk)).astype(BF16)

    def chunk_part(slot, direction, c, q, v, base):
        rows = slice(c * chunk, (c + 1) * chunk)
        lanes = slice(c * dh, (c + 1) * dh)
        return dict(q=q[rows], k=k_slots[slot][direction, rows, :], v=v[rows],
                    dec=lambda r0, r1: dec_slots[slot][direction, r0:r1, lanes],
                    rows=pl.ds(pl.multiple_of(base + c * chunk, chunk), chunk))

    def pair_step(f, b):
        lhs = jnp.concatenate(
            [(jnp.concatenate([p["q"][t * grp:(t + 1) * grp] for t, _ in group_pairs[d]], axis=0)
              * p["dec"](r_var, r_fine)).astype(BF16) for d, p in enumerate((f, b))], axis=1)
        keys = [(p["k"] * p["dec"](r_edge, r_var)).astype(BF16) for p in (f, b)]
        r = _dot_nt(lhs, _block_diag(*keys))
        rows_of = [jnp.zeros((grp, 2 * chunk), F32)] * n_groups
        for i, ((tf, gf), (tb, gb)) in enumerate(zip(*group_pairs)):
            block = r[i * grp:(i + 1) * grp]
            rows_of[tf] = jnp.where(lane_group == gf, block, rows_of[tf])
            rows_of[tb] = jnp.where(lane_group == n_groups + gb, block, rows_of[tb])
        a = jnp.concatenate(rows_of, axis=0)
        for lvl in range(len(fine)):
            xf = fine_operand(0, lvl, f["q"], f["k"], f["dec"])
            xb = fine_operand(1, lvl, b["q"], b["k"], b["dec"])
            a = a + masks_ref[lvl] * _dot_nt(jnp.concatenate([xf, xb], axis=1), _block_diag(xf, xb))
        o = _dot(a.astype(BF16), _block_diag(f["v"], b["v"]))
        sf = sf_scr[...]
        sb = sb_scr[...]
        q_dec = jnp.concatenate([(f["q"] * f["dec"](0, chunk)).astype(BF16),
                                 (b["q"] * b["dec"](0, chunk)).astype(BF16)], axis=1)
        o = o + _dot_nt(q_dec, _block_diag(sf.astype(BF16), sb.astype(BF16)))
        diag_f = jnp.sum(f["q"] * f["k"], axis=-1, keepdims=True)
        diag_b = jnp.sum(b["q"] * b["k"], axis=-1, keepdims=True)
        of_scr[f["rows"], :] = o[:, :dh] + diag_f * f["v"].astype(F32)
        ob_scr[b["rows"], :] = o[:, dh:] + diag_b * b["v"].astype(F32)
        k_dec = _block_diag((f["k"] * f["dec"](chunk, 2 * chunk)).astype(BF16),
                            (b["k"] * b["dec"](chunk, 2 * chunk)).astype(BF16))
        v_t = jnp.concatenate([f["v"], b["v"]], axis=0).astype(F32).T.astype(BF16)
        u = _dot(v_t, k_dec)
        sf_scr[...] = sf * f["dec"](chunk - 1, chunk) + u[:, :dh]
        sb_scr[...] = sb * b["dec"](0, 1) + u[:, dh:]

    lb_f = lower_bound(lbf_ref)
    lb_b = lower_bound(lbb_ref)
    sf_scr[...] = jnp.zeros_like(sf_scr)
    sb_scr[...] = jnp.zeros_like(sb_scr)

    def bases(n):
        return (pl.multiple_of(n * pair, pair), pl.multiple_of((n_pairs - 1 - n) * pair, pair))

    def prepare(n, slot):
        for direction, base, z_ref, lb in zip((0, 1), bases(n), (zf_ref, zb_ref), (lb_f, lb_b)):
            g2, k = gates(z_ref[0, pl.ds(base, pair), :], lb)
            k_slots[slot][direction] = k
            store_decays(slot, direction, g2)

    def consume(n, slot):
        parts = []
        for direction, base in zip((0, 1), bases(n)):
            rows = pl.ds(base, pair)
            q = q_ref[0, rows, :].astype(F32)
            v = i_ref[0, rows, :]
            parts.append([chunk_part(slot, direction, c, q, v, base) for c in range(2)])
        pair_step(parts[0][0], parts[1][1])
        pair_step(parts[0][1], parts[1][0])

    def scan_body(m, carry):
        prepare(2 * m + 1, 1)
        consume(2 * m, 0)
        prepare(jnp.minimum(2 * m + 2, n_pairs - 1), 0)
        consume(2 * m + 1, 1)
        return carry

    prepare(0, 0)
    lax.fori_loop(0, n_pairs // 2, scan_body, 0, unroll=HG_SCAN_UNROLL)

    gain = gain_ref[...].astype(F32)

    def norm_body(n, carry):
        rows = pl.ds(pl.multiple_of(n * HG_NORM_ROWS, HG_NORM_ROWS), HG_NORM_ROWS)
        o = of_scr[rows, :] + ob_scr[rows, :]
        o = o * lax.rsqrt(jnp.mean(jnp.square(o), axis=-1, keepdims=True) + RMS_EPS) * gain
        gate = g_ref[0, rows, :].astype(F32)
        o_ref[0, rows, :] = (o * _silu_from_half(gate)).astype(o_ref.dtype)
        return carry

    lax.fori_loop(0, seq // HG_NORM_ROWS, norm_body, 0, unroll=4)


def _hgrn(q, zf, zb, i, g, lb_fwd_logits, lb_bwd_logits, gain_row, layer):
    batch, seq, width = q.shape
    assert width == HG_HEADS * HG_HEAD_DIM and seq % (4 * HG_NORM_ROWS) == 0
    assert seq % (4 * HG_CHUNK * HG_SCAN_UNROLL) == 0
    sums, masks = _hgrn_constants(HG_CHUNK)
    sums = jnp.asarray(sums, BF16)
    masks = jnp.asarray(masks, F32)
    n_layers = lb_fwd_logits.shape[0]
    n_dec_rows = _hgrn_dec_rows(HG_CHUNK)[-1]
    head = pl.BlockSpec((1, seq, HG_HEAD_DIM), lambda b, h: (b, 0, h))
    per_head_row = lambda n: pl.BlockSpec((n, HG_HEAD_DIM), lambda b, h: (0, h))
    whole = lambda a: pl.BlockSpec(a.shape, lambda b, h: (0,) * a.ndim)
    return pl.pallas_call(
        functools.partial(_hgrn_kernel, layer=layer, seq=seq),
        grid=(batch, HG_HEADS),
        in_specs=[head, head, head, head, head,
                  per_head_row(n_layers), per_head_row(n_layers), per_head_row(1),
                  whole(sums), whole(masks)],
        out_specs=head,
        out_shape=jax.ShapeDtypeStruct((batch, seq, width), BF16),
        scratch_shapes=[pltpu.VMEM((seq, HG_HEAD_DIM), F32), pltpu.VMEM((seq, HG_HEAD_DIM), F32),
                        pltpu.VMEM((HG_HEAD_DIM, HG_HEAD_DIM), F32),
                        pltpu.VMEM((HG_HEAD_DIM, HG_HEAD_DIM), F32),
                        pltpu.VMEM((2, 2 * HG_CHUNK, HG_HEAD_DIM), F32),
                        pltpu.VMEM((2, 2 * HG_CHUNK, HG_HEAD_DIM), F32),
                        pltpu.VMEM((2, n_dec_rows, 2 * HG_HEAD_DIM), F32),
                        pltpu.VMEM((2, n_dec_rows, 2 * HG_HEAD_DIM), F32)],
        compiler_params=pltpu.CompilerParams(
            dimension_semantics=("arbitrary", "arbitrary"), vmem_limit_bytes=VMEM_LIMIT_BYTES),
        name="hgrn2_scan",
    )(q, zf, zb, i, g, lb_fwd_logits, lb_bwd_logits, gain_row, sums, masks)


def _out_proj_kernel(oa_ref, oh_ref, x_ref, w_ref, b_ref, gain_ref, bias_ref, o_ref, w_scr, *, alpha):
    d_att = oa_ref.shape[1]
    _cast_weight_once(w_ref, w_scr)
    for r0 in range(0, o_ref.shape[0], OUT_PROJ_ROWS):
        rows = slice(r0, r0 + OUT_PROJ_ROWS)
        y = _dot(oa_ref[rows, :], w_scr[:d_att, :]) + _dot(oh_ref[rows, :], w_scr[d_att:, :]) + b_ref[...]
        r = alpha * x_ref[rows, :] + y
        mu = jnp.mean(r, axis=-1, keepdims=True)
        c = r - mu
        var = jnp.mean(jnp.square(c), axis=-1, keepdims=True)
        o_ref[rows, :] = (c * lax.rsqrt(var + LN_EPS) * gain_ref[...] + bias_ref[...]).astype(o_ref.dtype)


def _out_proj(o_a, o_h, x2d, w, b_row, gain_row, bias_row, alpha):
    n_tok, d_model = x2d.shape
    d_att, d_hg = o_a.shape[1], o_h.shape[1]
    row = pl.BlockSpec((1, d_model), lambda i: (0, 0))
    return pl.pallas_call(
        functools.partial(_out_proj_kernel, alpha=alpha),
        grid=(n_tok // OUT_PROJ_TM,),
        in_specs=[pl.BlockSpec((OUT_PROJ_TM, d_att), lambda i: (i, 0)),
                  pl.BlockSpec((OUT_PROJ_TM, d_hg), lambda i: (i, 0)),
                  pl.BlockSpec((OUT_PROJ_TM, d_model), lambda i: (i, 0)),
                  pl.BlockSpec((d_att + d_hg, d_model), lambda i: (0, 0), pipeline_mode=pl.Buffered(1)),
                  row, row, row],
        out_specs=pl.BlockSpec((OUT_PROJ_TM, d_model), lambda i: (i, 0)),
        out_shape=jax.ShapeDtypeStruct((n_tok, d_model), x2d.dtype),
        scratch_shapes=[pltpu.VMEM((d_att + d_hg, d_model), BF16)],
        compiler_params=pltpu.CompilerParams(
            dimension_semantics=("arbitrary",), vmem_limit_bytes=VMEM_LIMIT_BYTES),
        name="out_proj_layernorm",
    )(o_a, o_h, x2d, w, b_row, gain_row, bias_row)


def _layer(x, layer, depth, w_in, b_in, rpb, lb_fwd_logits, lb_bwd_logits, hg_norm_gain,
           w_out, b_out, ln_gain, ln_bias):
    batch, seq, d_model = x.shape
    x2d = x.reshape(batch * seq, d_model)
    slabs = _in_proj(x2d, w_in, b_in.reshape(1, -1))
    q_a, k_a, v_a, g_a, q_h, z_f, z_b, i_h, g_h = [s.reshape(batch, seq, SLAB) for s in slabs]
    o_a = _attention(q_a, k_a, v_a, g_a, rpb)
    o_h = _hgrn(q_h, z_f, z_b, i_h, g_h, lb_fwd_logits, lb_bwd_logits,
                hg_norm_gain.reshape(1, -1), layer)
    alpha = (2.0 * depth) ** 0.25
    out = _out_proj(o_a.reshape(batch * seq, SLAB), o_h.reshape(batch * seq, SLAB), x2d,
                    w_out, b_out.reshape(1, -1), ln_gain.reshape(1, -1),
                    ln_bias.reshape(1, -1), alpha)
    return out.reshape(batch, seq, d_model)


def kernel(x, w_in, b_in, rpb, lb_fwd_logits, lb_bwd_logits, hg_norm_gain, w_out, b_out, ln_gain, ln_bias):
    depth = w_in.shape[0]
    for layer in range(depth):
        x = _layer(x, layer, depth, w_in[layer], b_in[layer], rpb[layer], lb_fwd_logits,
                   lb_bwd_logits, hg_norm_gain[layer], w_out[layer], b_out[layer],
                   ln_gain[layer], ln_bias[layer])
    return x
```

```python
import functools

import numpy as np
import jax
import jax.numpy as jnp
from jax import lax
from jax.experimental import pallas as pl
from jax.experimental.pallas import tpu as pltpu

GRID_W = 64
ATT_HEADS = 8
ATT_HEAD_DIM = 64
HG_HEADS = 4
HG_HEAD_DIM = 128
WIN_ROWS = 8
WIN_COLS = 16
LN_EPS = 1e-5
RMS_EPS = 1e-6
N_SLABS = 9
SLAB = 512

LANES = 128
F32_SUBLANES = 8
VMEM_LIMIT_BYTES = 56 * 1024 * 1024
PROJ_TM = 512
OUT_PROJ_TM = 1024
OUT_PROJ_ROWS = 256
ATT_ROWS_PER_STEP = 16
ATT_UNROLL = 4
ATT_GROUP = 4
ATT_BIAS_UNROLL = 8
HG_CHUNK = 64
HG_GROUP = F32_SUBLANES
HG_NORM_ROWS = 256
HG_SCAN_UNROLL = 32
MASK_VALUE = -1e30
LOG2_E = 1.4426950408889634
SLAB_SCALES = {1: ATT_HEAD_DIM ** -0.5 * LOG2_E, 3: 0.5, 8: 0.5}

BF16 = jnp.bfloat16
F32 = jnp.float32


def _silu_from_half(half_g):
    return half_g * (1.0 + jnp.tanh(half_g))


def _dot(a, b):
    return jnp.dot(a, b, preferred_element_type=F32)


def _dot_nt(a, b):
    return lax.dot_general(a, b, (((1,), (1,)), ((), ())), preferred_element_type=F32)


def _cast_weight_once(w_ref, w_scr):
    @pl.when(pl.program_id(0) == 0)
    def _():
        for c0 in range(0, w_ref.shape[1], SLAB):
            w_scr[:, c0:c0 + SLAB] = w_ref[:, c0:c0 + SLAB].astype(w_scr.dtype)


def _in_proj_kernel(x_ref, w_ref, b_ref, *refs):
    out_refs, w_scr = refs[:-1], refs[-1]
    _cast_weight_once(w_ref, w_scr)
    xb = x_ref[...].astype(BF16)
    for j, o_ref in enumerate(out_refs):
        cols = slice(j * SLAB, (j + 1) * SLAB)
        h = _dot(xb, w_scr[:, cols]) + b_ref[:, cols]
        if j in SLAB_SCALES:
            h = h * SLAB_SCALES[j]
        o_ref[...] = h.astype(o_ref.dtype)


def _in_proj(x2d, w, b_row):
    n_tok, d_model = x2d.shape
    d_in = w.shape[1]
    assert d_in == N_SLABS * SLAB and n_tok % PROJ_TM == 0
    return pl.pallas_call(
        _in_proj_kernel,
        grid=(n_tok // PROJ_TM,),
        in_specs=[
            pl.BlockSpec((PROJ_TM, d_model), lambda i: (i, 0)),
            pl.BlockSpec((d_model, d_in), lambda i: (0, 0), pipeline_mode=pl.Buffered(1)),
            pl.BlockSpec((1, d_in), lambda i: (0, 0)),
        ],
        out_specs=[pl.BlockSpec((PROJ_TM, SLAB), lambda i: (i, 0))] * N_SLABS,
        out_shape=[jax.ShapeDtypeStruct((n_tok, SLAB), BF16)] * N_SLABS,
        scratch_shapes=[pltpu.VMEM((d_model, d_in), BF16)],
        compiler_params=pltpu.CompilerParams(
            dimension_semantics=("arbitrary",), vmem_limit_bytes=VMEM_LIMIT_BYTES),
        name="in_proj",
    )(x2d, w, b_row)


ATT_REL_ROWS = 2 * WIN_ROWS - 1
ATT_BIAS_TILES = (ATT_REL_ROWS - 1) // 2
ATT_GROUPS = ATT_HEADS // ATT_GROUP


def _attention_bias_rows(rpb):
    n_rel_cols = 2 * WIN_COLS - 1
    scaled = rpb.astype(F32) * LOG2_E
    gap = jnp.zeros((ATT_HEADS, ATT_BIAS_TILES, (LANES - 2 * n_rel_cols) // 2), F32)
    per_parity = []
    for p in range(2):
        first = scaled[:, p:p + 2 * ATT_BIAS_TILES - 1:2]
        second = scaled[:, p + 1:p + 2 * ATT_BIAS_TILES:2]
        per_parity.append(jnp.concatenate(
            [first[..., WIN_COLS - 1:], gap, second, gap, first[..., :WIN_COLS - 1]], axis=-1))
    gen = jnp.stack(per_parity)
    gen = gen.reshape(2, ATT_GROUPS, ATT_GROUP, ATT_BIAS_TILES, LANES).transpose(0, 1, 3, 2, 4)
    return gen.reshape(2 * ATT_GROUPS * ATT_BIAS_TILES * ATT_GROUP, 1, LANES)


def _attention_window_mask():
    qc = np.arange(GRID_W)[:, None]
    kc = np.arange(LANES)[None, :] % GRID_W
    col_start = np.clip(qc - WIN_COLS // 2, 0, GRID_W - WIN_COLS)
    return ((kc >= col_start) & (kc < col_start + WIN_COLS)).astype(np.float32)


def _attention_kernel(q_ref, k_ref, v_ref, g_ref, gen_ref, win_ref, o_ref, bias_scr, *, rows, kr):
    step = pl.program_id(1)
    gw = ATT_GROUP * ATT_HEAD_DIM
    row_head = lax.broadcasted_iota(jnp.int32, (ATT_GROUP * GRID_W, gw), 0) // GRID_W
    lane_head = lax.broadcasted_iota(jnp.int32, (ATT_GROUP * GRID_W, gw), 1) // ATT_HEAD_DIM
    own_head = row_head == lane_head
    out_lane_head = lax.broadcasted_iota(jnp.int32, (GRID_W, gw), 1) // ATT_HEAD_DIM

    @pl.when((pl.program_id(0) == 0) & (step == 0))
    def _build_bias():
        in_window = win_ref[...] > 0.0

        def one_block(n, carry):
            gen = jnp.broadcast_to(gen_ref[n], (GRID_W, LANES))
            toeplitz = pltpu.roll(gen, 0, 1, stride=1, stride_axis=0)
            head_rows = pl.ds(pl.multiple_of((n % ATT_GROUP) * GRID_W, GRID_W), GRID_W)
            bias_scr[n // ATT_GROUP, head_rows, :] = jnp.where(in_window, toeplitz, MASK_VALUE)
            return carry

        lax.fori_loop(0, gen_ref.shape[0], one_block, 0, unroll=ATT_BIAS_UNROLL)

    def one_row(j, carry):
        r = step * ATT_ROWS_PER_STEP + j
        row_start = jnp.clip(r - kr // 2, 0, rows - kr)
        variant = row_start - r + (WIN_ROWS - 1)
        q_tok = pl.multiple_of(j * GRID_W, GRID_W)
        k_tok = pl.multiple_of(row_start * GRID_W, GRID_W)
        for grp in range(ATT_GROUPS):
            lanes = slice(grp * gw, (grp + 1) * gw)
            q = q_ref[0, pl.ds(q_tok, GRID_W), lanes]
            q_bd = jnp.where(own_head, jnp.concatenate([q] * ATT_GROUP, axis=0), 0)
            keys = k_ref[0, pl.ds(k_tok, kr * GRID_W), lanes]
            vals = v_ref[0, pl.ds(k_tok, kr * GRID_W), lanes]
            tile0 = ((variant % 2) * ATT_GROUPS + grp) * ATT_BIAS_TILES + variant // 2
            bias = jnp.concatenate([bias_scr[tile0 + t] for t in range(kr // 2)], axis=1)
            s = _dot_nt(q_bd, keys) + bias
            m = jnp.max(s, axis=-1, keepdims=True)
            p = jnp.exp2(s - m)
            denom = jnp.sum(p, axis=-1, keepdims=True)
            pv = _dot(p.astype(BF16), vals) / denom
            o = jnp.zeros((GRID_W, gw), F32)
            for h in range(ATT_GROUP):
                o = o + jnp.where(out_lane_head == h, pv[h * GRID_W:(h + 1) * GRID_W], 0.0)
            gate = g_ref[0, pl.ds(q_tok, GRID_W), lanes].astype(F32)
            o = o * _silu_from_half(gate)
            o_ref[0, pl.ds(q_tok, GRID_W), lanes] = o.astype(o_ref.dtype)
        return carry

    lax.fori_loop(0, ATT_ROWS_PER_STEP, one_row, 0, unroll=ATT_UNROLL)


def _attention(q, k, v, g, rpb):
    batch, seq, width = q.shape
    rows = seq // GRID_W
    kr = min(WIN_ROWS, rows)
    assert rows % ATT_ROWS_PER_STEP == 0 and ATT_ROWS_PER_STEP % ATT_UNROLL == 0
    assert kr == WIN_ROWS and 2 * GRID_W == LANES
    gen = _attention_bias_rows(rpb)
    win = jnp.asarray(_attention_window_mask())
    blk = ATT_ROWS_PER_STEP * GRID_W
    tile = pl.BlockSpec((1, blk, width), lambda b, i: (b, i, 0))
    whole = pl.BlockSpec((1, seq, width), lambda b, i: (b, 0, 0))
    return pl.pallas_call(
        functools.partial(_attention_kernel, rows=rows, kr=kr),
        grid=(batch, rows // ATT_ROWS_PER_STEP),
        in_specs=[tile, whole, whole, tile,
                  pl.BlockSpec(gen.shape, lambda b, i: (0, 0, 0)),
                  pl.BlockSpec(win.shape, lambda b, i: (0, 0))],
        out_specs=tile,
        out_shape=jax.ShapeDtypeStruct((batch, seq, width), BF16),
        scratch_shapes=[pltpu.VMEM((2 * ATT_GROUPS * ATT_BIAS_TILES, ATT_GROUP * GRID_W, 2 * GRID_W), F32)],
        compiler_params=pltpu.CompilerParams(
            dimension_semantics=("arbitrary", "arbitrary"), vmem_limit_bytes=VMEM_LIMIT_BYTES),
        name="nbr_attention",
    )(q, k, v, g, gen, win)


def _hgrn_fine_levels():
    sizes = []
    c = HG_GROUP
    while c >= 2:
        sizes.append(c)
        c //= 2
    return sizes


def _hgrn_group_pairs(direction, chunk):
    n = chunk // HG_GROUP
    if direction == 0:
        return [(t, g) for t in range(1, n) for g in range(t)]
    return [(t, g) for t in range(n - 2, -1, -1) for g in range(t + 1, n)]


def _hgrn_constants(chunk):
    t = np.arange(chunk)[:, None]
    u = np.arange(chunk)[None, :]
    blocks = [(u <= t)]
    masks = []
    for c in _hgrn_fine_levels():
        half = c // 2
        mid = (t // c) * c + half
        upper = (t % c) >= half
        if 2 < c < HG_GROUP:
            blocks.append(np.where(upper, (u >= mid) & (u <= t), (u > t) & (u <= mid - 1)))
        masks.append((t // c == u // c) & upper & ((u % c) < half))
    fwd_sums = np.stack(blocks).astype(np.float32)
    fwd_masks = np.stack(masks).astype(np.float32)
    twice = lambda m: np.concatenate([m.reshape(-1, chunk)] * 2, axis=1)
    sums = np.stack([twice(fwd_sums), twice(fwd_sums[:, ::-1, ::-1])])
    masks = np.concatenate([fwd_masks, fwd_masks[:, ::-1, ::-1]], axis=2)
    return sums, masks


def _block_diag(a, b):
    zero = jnp.zeros_like(a)
    return jnp.concatenate([jnp.concatenate([a, zero], axis=1),
                            jnp.concatenate([zero, b], axis=1)], axis=0)


def _hgrn_dec_rows(chunk):
    n_var = len(_hgrn_group_pairs(0, chunk)) * HG_GROUP
    n_mm = sum(c > 2 for c in _hgrn_fine_levels())
    edge, var, fine = 2 * chunk, 3 * chunk, 3 * chunk + n_var
    return edge, var, fine, fine + n_mm * chunk


def _hgrn_kernel(q_ref, zf_ref, zb_ref, i_ref, g_ref, lbf_ref, lbb_ref, gain_ref,
                 sums_ref, masks_ref, o_ref, of_scr, ob_scr, sf_scr, sb_scr,
                 k_scr0, k_scr1, dec_scr0, dec_scr1, *, layer, seq):
    chunk = HG_CHUNK
    pair = 2 * chunk
    n_pairs = seq // pair
    dh = HG_HEAD_DIM
    grp = HG_GROUP
    n_groups = chunk // grp
    fine = _hgrn_fine_levels()
    group_pairs = [_hgrn_group_pairs(direction, chunk) for direction in (0, 1)]
    r_edge, r_var, r_fine, _ = _hgrn_dec_rows(chunk)
    k_slots = (k_scr0, k_scr1)
    dec_slots = (dec_scr0, dec_scr1)
    row_in_group = lax.broadcasted_iota(jnp.int32, (grp, dh), 0)
    lane_group = lax.broadcasted_iota(jnp.int32, (grp, 2 * chunk), 1) // grp

    def lower_bound(logit_ref):
        logits = logit_ref[...].astype(F32)
        e = jnp.exp(logits - jnp.max(logits, axis=0, keepdims=True))
        return jnp.sum(e[:layer + 1], axis=0, keepdims=True) / jnp.sum(e, axis=0, keepdims=True)

    def gates(z, lb):
        f = lb + (1.0 - lb) * jax.nn.sigmoid(z.astype(F32))
        return jnp.log2(f), 1.0 - f

    def store_decays(slot, direction, g2):
        hi = g2.astype(BF16)
        lo = (g2 - hi.astype(F32)).astype(BF16)
        stacked = [jnp.concatenate([hi[c * chunk:(c + 1) * chunk], lo[c * chunk:(c + 1) * chunk]], axis=0)
                   for c in range(2)]
        raw = _dot(sums_ref[direction], jnp.concatenate(stacked, axis=1))
        dec = dec_slots[slot].at[direction]
        cum = raw[0:chunk]
        whole = cum[chunk - 1:chunk] if direction == 0 else cum[0:1]
        dec[0:chunk, :] = jnp.exp2(cum)
        dec[chunk:2 * chunk, :] = jnp.exp2(whole - cum)
        by_group = cum.reshape(n_groups, grp, 2 * dh)
        mid_row = grp // 2 - 1 if direction == 0 else grp // 2
        to_mid = -jnp.abs(by_group - by_group[:, mid_row:mid_row + 1, :])
        dec[r_fine:r_fine + chunk, :] = jnp.exp2(to_mid.reshape(chunk, 2 * dh))
        dec[r_fine + chunk:, :] = jnp.exp2(raw[chunk:])
        edge = grp - 1 if direction == 0 else 0
        edges = [cum[g * grp + edge:g * grp + edge + 1] for g in range(n_groups)]
        group = lambda g: cum[g * grp:(g + 1) * grp]
        dec[r_edge:r_var, :] = jnp.exp2(jnp.concatenate([edges[g] - group(g) for g in range(n_groups)], axis=0))
        dec[r_var:r_fine, :] = jnp.exp2(jnp.concatenate([group(t) - edges[g]
                                                         for t, g in group_pairs[direction]], axis=0))

    def fine_operand(direction, lvl, q, k, dec):
        c = fine[lvl]
        pos = row_in_group % c
        is_upper = (pos >= c // 2) if direction == 0 else (pos < c // 2)
        pick = lambda x, y: jnp.where(is_upper[None], x.reshape(n_groups, grp, dh),
                                      y.reshape(n_groups, grp, dh)).reshape(chunk, dh)
        if c == 2:
            return pick(q * (1.0 - k), k).astype(BF16)
        return (pick(q, k) * dec(r_fine + lvl * chunk, r_fine + (lvl + 1) * chunk)).astype(BF16)

    def chunk_part(slot, direction, c, q, v, base):
        rows = slice(c * chunk, (c + 1) * chunk)
        lanes = slice(c * dh, (c + 1) * dh)
        return dict(q=q[rows], k=k_slots[slot][direction, rows, :], v=v[rows],
                    dec=lambda r0, r1: dec_slots[slot][direction, r0:r1, lanes],
                    rows=pl.ds(pl.multiple_of(base + c * chunk, chunk), chunk))

    def pair_step(f, b):
        lhs = jnp.concatenate(
            [(jnp.concatenate([p["q"][t * grp:(t + 1) * grp] for t, _ in group_pairs[d]], axis=0)
              * p["dec"](r_var, r_fine)).astype(BF16) for d, p in enumerate((f, b))], axis=1)
        keys = [(p["k"] * p["dec"](r_edge, r_var)).astype(BF16) for p in (f, b)]
        r = _dot_nt(lhs, _block_diag(*keys))
        rows_of = [jnp.zeros((grp, 2 * chunk), F32)] * n_groups
        for i, ((tf, gf), (tb, gb)) in enumerate(zip(*group_pairs)):
            block = r[i * grp:(i + 1) * grp]
            rows_of[tf] = jnp.where(lane_group == gf, block, rows_of[tf])
            rows_of[tb] = jnp.where(lane_group == n_groups + gb, block, rows_of[tb])
        a = jnp.concatenate(rows_of, axis=0)
        for lvl in range(len(fine)):
            xf = fine_operand(0, lvl, f["q"], f["k"], f["dec"])
            xb = fine_operand(1, lvl, b["q"], b["k"], b["dec"])
            a = a + masks_ref[lvl] * _dot_nt(jnp.concatenate([xf, xb], axis=1), _block_diag(xf, xb))
        o = _dot(a.astype(BF16), _block_diag(f["v"], b["v"]))
        sf = sf_scr[...]
        sb = sb_scr[...]
        q_dec = jnp.concatenate([(f["q"] * f["dec"](0, chunk)).astype(BF16),
                                 (b["q"] * b["dec"](0, chunk)).astype(BF16)], axis=1)
        o = o + _dot_nt(q_dec, _block_diag(sf.astype(BF16), sb.astype(BF16)))
        diag_f = jnp.sum(f["q"] * f["k"], axis=-1, keepdims=True)
        diag_b = jnp.sum(b["q"] * b["k"], axis=-1, keepdims=True)
        of_scr[f["rows"], :] = o[:, :dh] + diag_f * f["v"].astype(F32)
        ob_scr[b["rows"], :] = o[:, dh:] + diag_b * b["v"].astype(F32)
        k_dec = _block_diag((f["k"] * f["dec"](chunk, 2 * chunk)).astype(BF16),
                            (b["k"] * b["dec"](chunk, 2 * chunk)).astype(BF16))
        v_t = jnp.concatenate([f["v"], b["v"]], axis=0).astype(F32).T.astype(BF16)
        u = _dot(v_t, k_dec)
        sf_scr[...] = sf * f["dec"](chunk - 1, chunk) + u[:, :dh]
        sb_scr[...] = sb * b["dec"](0, 1) + u[:, dh:]

    lb_f = lower_bound(lbf_ref)
    lb_b = lower_bound(lbb_ref)
    sf_scr[...] = jnp.zeros_like(sf_scr)
    sb_scr[...] = jnp.zeros_like(sb_scr)

    def bases(n):
        return (pl.multiple_of(n * pair, pair), pl.multiple_of((n_pairs - 1 - n) * pair, pair))

    def prepare(n, slot):
        for direction, base, z_ref, lb in zip((0, 1), bases(n), (zf_ref, zb_ref), (lb_f, lb_b)):
            g2, k = gates(z_ref[0, pl.ds(base, pair), :], lb)
            k_slots[slot][direction] = k
            store_decays(slot, direction, g2)

    def consume(n, slot):
        parts = []
        for direction, base in zip((0, 1), bases(n)):
            rows = pl.ds(base, pair)
            q = q_ref[0, rows, :].astype(F32)
            v = i_ref[0, rows, :]
            parts.append([chunk_part(slot, direction, c, q, v, base) for c in range(2)])
        pair_step(parts[0][0], parts[1][1])
        pair_step(parts[0][1], parts[1][0])

    def scan_body(m, carry):
        prepare(2 * m + 1, 1)
        consume(2 * m, 0)
        prepare(jnp.minimum(2 * m + 2, n_pairs - 1), 0)
        consume(2 * m + 1, 1)
        return carry

    prepare(0, 0)
    lax.fori_loop(0, n_pairs // 2, scan_body, 0, unroll=HG_SCAN_UNROLL)

    gain = gain_ref[...].astype(F32)

    def norm_body(n, carry):
        rows = pl.ds(pl.multiple_of(n * HG_NORM_ROWS, HG_NORM_ROWS), HG_NORM_ROWS)
        o = of_scr[rows, :] + ob_scr[rows, :]
        o = o * lax.rsqrt(jnp.mean(jnp.square(o), axis=-1, keepdims=True) + RMS_EPS) * gain
        gate = g_ref[0, rows, :].astype(F32)
        o_ref[0, rows, :] = (o * _silu_from_half(gate)).astype(o_ref.dtype)
        return carry

    lax.fori_loop(0, seq // HG_NORM_ROWS, norm_body, 0, unroll=4)


def _hgrn(q, zf, zb, i, g, lb_fwd_logits, lb_bwd_logits, gain_row, layer):
    batch, seq, width = q.shape
    assert width == HG_HEADS * HG_HEAD_DIM and seq % (4 * HG_NORM_ROWS) == 0
    assert seq % (4 * HG_CHUNK * HG_SCAN_UNROLL) == 0
    sums, masks = _hgrn_constants(HG_CHUNK)
    sums = jnp.asarray(sums, BF16)
    masks = jnp.asarray(masks, F32)
    n_layers = lb_fwd_logits.shape[0]
    n_dec_rows = _hgrn_dec_rows(HG_CHUNK)[-1]
    head = pl.BlockSpec((1, seq, HG_HEAD_DIM), lambda b, h: (b, 0, h))
    per_head_row = lambda n: pl.BlockSpec((n, HG_HEAD_DIM), lambda b, h: (0, h))
    whole = lambda a: pl.BlockSpec(a.shape, lambda b, h: (0,) * a.ndim)
    return pl.pallas_call(
        functools.partial(_hgrn_kernel, layer=layer, seq=seq),
        grid=(batch, HG_HEADS),
        in_specs=[head, head, head, head, head,
                  per_head_row(n_layers), per_head_row(n_layers), per_head_row(1),
                  whole(sums), whole(masks)],
        out_specs=head,
        out_shape=jax.ShapeDtypeStruct((batch, seq, width), BF16),
        scratch_shapes=[pltpu.VMEM((seq, HG_HEAD_DIM), F32), pltpu.VMEM((seq, HG_HEAD_DIM), F32),
                        pltpu.VMEM((HG_HEAD_DIM, HG_HEAD_DIM), F32),
                        pltpu.VMEM((HG_HEAD_DIM, HG_HEAD_DIM), F32),
                        pltpu.VMEM((2, 2 * HG_CHUNK, HG_HEAD_DIM), F32),
                        pltpu.VMEM((2, 2 * HG_CHUNK, HG_HEAD_DIM), F32),
                        pltpu.VMEM((2, n_dec_rows, 2 * HG_HEAD_DIM), F32),
                        pltpu.VMEM((2, n_dec_rows, 2 * HG_HEAD_DIM), F32)],
        compiler_params=pltpu.CompilerParams(
            dimension_semantics=("arbitrary", "arbitrary"), vmem_limit_bytes=VMEM_LIMIT_BYTES),
        name="hgrn2_scan",
    )(q, zf, zb, i, g, lb_fwd_logits, lb_bwd_logits, gain_row, sums, masks)


def _out_proj_kernel(oa_ref, oh_ref, x_ref, w_ref, b_ref, gain_ref, bias_ref, o_ref, w_scr, *, alpha):
    d_att = oa_ref.shape[1]
    _cast_weight_once(w_ref, w_scr)
    for r0 in range(0, o_ref.shape[0], OUT_PROJ_ROWS):
        rows = slice(r0, r0 + OUT_PROJ_ROWS)
        y = _dot(oa_ref[rows, :], w_scr[:d_att, :]) + _dot(oh_ref[rows, :], w_scr[d_att:, :]) + b_ref[...]
        r = alpha * x_ref[rows, :] + y
        mu = jnp.mean(r, axis=-1, keepdims=True)
        c = r - mu
        var = jnp.mean(jnp.square(c), axis=-1, keepdims=True)
        o_ref[rows, :] = (c * lax.rsqrt(var + LN_EPS) * gain_ref[...] + bias_ref[...]).astype(o_ref.dtype)


def _out_proj(o_a, o_h, x2d, w, b_row, gain_row, bias_row, alpha):
    n_tok, d_model = x2d.shape
    d_att, d_hg = o_a.shape[1], o_h.shape[1]
    row = pl.BlockSpec((1, d_model), lambda i: (0, 0))
    return pl.pallas_call(
        functools.partial(_out_proj_kernel, alpha=alpha),
        grid=(n_tok // OUT_PROJ_TM,),
        in_specs=[pl.BlockSpec((OUT_PROJ_TM, d_att), lambda i: (i, 0)),
                  pl.BlockSpec((OUT_PROJ_TM, d_hg), lambda i: (i, 0)),
                  pl.BlockSpec((OUT_PROJ_TM, d_model), lambda i: (i, 0)),
                  pl.BlockSpec((d_att + d_hg, d_model), lambda i: (0, 0), pipeline_mode=pl.Buffered(1)),
                  row, row, row],
        out_specs=pl.BlockSpec((OUT_PROJ_TM, d_model), lambda i: (i, 0)),
        out_shape=jax.ShapeDtypeStruct((n_tok, d_model), x2d.dtype),
        scratch_shapes=[pltpu.VMEM((d_att + d_hg, d_model), BF16)],
        compiler_params=pltpu.CompilerParams(
            dimension_semantics=("arbitrary",), vmem_limit_bytes=VMEM_LIMIT_BYTES),
        name="out_proj_layernorm",
    )(o_a, o_h, x2d, w, b_row, gain_row, bias_row)


def _layer(x, layer, depth, w_in, b_in, rpb, lb_fwd_logits, lb_bwd_logits, hg_norm_gain,
           w_out, b_out, ln_gain, ln_bias):
    batch, seq, d_model = x.shape
    x2d = x.reshape(batch * seq, d_model)
    slabs = _in_proj(x2d, w_in, b_in.reshape(1, -1))
    q_a, k_a, v_a, g_a, q_h, z_f, z_b, i_h, g_h = [s.reshape(batch, seq, SLAB) for s in slabs]
    o_a = _attention(q_a, k_a, v_a, g_a, rpb)
    o_h = _hgrn(q_h, z_f, z_b, i_h, g_h, lb_fwd_logits, lb_bwd_logits,
                hg_norm_gain.reshape(1, -1), layer)
    alpha = (2.0 * depth) ** 0.25
    out = _out_proj(o_a.reshape(batch * seq, SLAB), o_h.reshape(batch * seq, SLAB), x2d,
                    w_out, b_out.reshape(1, -1), ln_gain.reshape(1, -1),
                    ln_bias.reshape(1, -1), alpha)
    return out.reshape(batch, seq, d_model)


def kernel(x, w_in, b_in, rpb, lb_fwd_logits, lb_bwd_logits, hg_norm_gain, w_out, b_out, ln_gain, ln_bias):
    depth = w_in.shape[0]
    for layer in range(depth):
        x = _layer(x, layer, depth, w_in[layer], b_in[layer], rpb[layer], lb_fwd_logits,
                   lb_bwd_logits, hg_norm_gain[layer], w_out[layer], b_out[layer],
                   ln_gain[layer], ln_bias[layer])
    return x
```

```python
import functools

import numpy as np
import jax
import jax.numpy as jnp
from jax import lax
from jax.experimental import pallas as pl
from jax.experimental.pallas import tpu as pltpu

GRID_W = 64
ATT_HEADS = 8
ATT_HEAD_DIM = 64
HG_HEADS = 4
HG_HEAD_DIM = 128
WIN_ROWS = 8
WIN_COLS = 16
LN_EPS = 1e-5
RMS_EPS = 1e-6
N_SLABS = 9
SLAB = 512

LANES = 128
F32_SUBLANES = 8
VMEM_LIMIT_BYTES = 56 * 1024 * 1024
PROJ_TM = 512
OUT_PROJ_TM = 1024
OUT_PROJ_ROWS = 256
ATT_ROWS_PER_STEP = 32
ATT_UNROLL = 4
ATT_GROUP = 4
ATT_BIAS_UNROLL = 8
HG_CHUNK = 64
HG_GROUP = F32_SUBLANES
HG_NORM_ROWS = 256
HG_SCAN_UNROLL = 16
MASK_VALUE = -1e30
LOG2_E = 1.4426950408889634
SLAB_SCALES = {1: ATT_HEAD_DIM ** -0.5 * LOG2_E, 3: 0.5, 8: 0.5}

BF16 = jnp.bfloat16
F32 = jnp.float32


def _silu_from_half(half_g):
    return half_g * (1.0 + jnp.tanh(half_g))


def _dot(a, b):
    return jnp.dot(a, b, preferred_element_type=F32)


def _dot_nt(a, b):
    return lax.dot_general(a, b, (((1,), (1,)), ((), ())), preferred_element_type=F32)


def _cast_weight_once(w_ref, w_scr):
    @pl.when(pl.program_id(0) == 0)
    def _():
        for c0 in range(0, w_ref.shape[1], SLAB):
            w_scr[:, c0:c0 + SLAB] = w_ref[:, c0:c0 + SLAB].astype(w_scr.dtype)


def _in_proj_kernel(x_ref, w_ref, b_ref, *refs):
    out_refs, w_scr = refs[:-1], refs[-1]
    _cast_weight_once(w_ref, w_scr)
    xb = x_ref[...].astype(BF16)
    for j, o_ref in enumerate(out_refs):
        cols = slice(j * SLAB, (j + 1) * SLAB)
        h = _dot(xb, w_scr[:, cols]) + b_ref[:, cols]
        if j in SLAB_SCALES:
            h = h * SLAB_SCALES[j]
        o_ref[...] = h.astype(o_ref.dtype)


def _in_proj(x2d, w, b_row):
    n_tok, d_model = x2d.shape
    d_in = w.shape[1]
    assert d_in == N_SLABS * SLAB and n_tok % PROJ_TM == 0
    return pl.pallas_call(
        _in_proj_kernel,
        grid=(n_tok // PROJ_TM,),
        in_specs=[
            pl.BlockSpec((PROJ_TM, d_model), lambda i: (i, 0)),
            pl.BlockSpec((d_model, d_in), lambda i: (0, 0), pipeline_mode=pl.Buffered(1)),
            pl.BlockSpec((1, d_in), lambda i: (0, 0)),
        ],
        out_specs=[pl.BlockSpec((PROJ_TM, SLAB), lambda i: (i, 0))] * N_SLABS,
        out_shape=[jax.ShapeDtypeStruct((n_tok, SLAB), BF16)] * N_SLABS,
        scratch_shapes=[pltpu.VMEM((d_model, d_in), BF16)],
        compiler_params=pltpu.CompilerParams(
            dimension_semantics=("arbitrary",), vmem_limit_bytes=VMEM_LIMIT_BYTES),
        name="in_proj",
    )(x2d, w, b_row)


ATT_REL_ROWS = 2 * WIN_ROWS - 1
ATT_BIAS_TILES = (ATT_REL_ROWS - 1) // 2
ATT_GROUPS = ATT_HEADS // ATT_GROUP


def _attention_bias_rows(rpb):
    n_rel_cols = 2 * WIN_COLS - 1
    scaled = rpb.astype(F32) * LOG2_E
    gap = jnp.zeros((ATT_HEADS, ATT_BIAS_TILES, (LANES - 2 * n_rel_cols) // 2), F32)
    per_parity = []
    for p in range(2):
        first = scaled[:, p:p + 2 * ATT_BIAS_TILES - 1:2]
        second = scaled[:, p + 1:p + 2 * ATT_BIAS_TILES:2]
        per_parity.append(jnp.concatenate(
            [first[..., WIN_COLS - 1:], gap, second, gap, first[..., :WIN_COLS - 1]], axis=-1))
    gen = jnp.stack(per_parity)
    gen = gen.reshape(2, ATT_GROUPS, ATT_GROUP, ATT_BIAS_TILES, LANES).transpose(0, 1, 3, 2, 4)
    return gen.reshape(2 * ATT_GROUPS * ATT_BIAS_TILES * ATT_GROUP, 1, LANES)


def _attention_window_mask():
    qc = np.arange(GRID_W)[:, None]
    kc = np.arange(LANES)[None, :] % GRID_W
    col_start = np.clip(qc - WIN_COLS // 2, 0, GRID_W - WIN_COLS)
    return ((kc >= col_start) & (kc < col_start + WIN_COLS)).astype(np.float32)


def _attention_kernel(q_ref, k_ref, v_ref, g_ref, gen_ref, win_ref, o_ref, bias_scr, *, rows, kr):
    step = pl.program_id(1)
    gw = ATT_GROUP * ATT_HEAD_DIM
    row_head = lax.broadcasted_iota(jnp.int32, (ATT_GROUP * GRID_W, gw), 0) // GRID_W
    lane_head = lax.broadcasted_iota(jnp.int32, (ATT_GROUP * GRID_W, gw), 1) // ATT_HEAD_DIM
    own_head = row_head == lane_head
    out_lane_head = lax.broadcasted_iota(jnp.int32, (GRID_W, gw), 1) // ATT_HEAD_DIM

    @pl.when((pl.program_id(0) == 0) & (step == 0))
    def _build_bias():
        in_window = win_ref[...] > 0.0

        def one_block(n, carry):
            gen = jnp.broadcast_to(gen_ref[n], (GRID_W, LANES))
            toeplitz = pltpu.roll(gen, 0, 1, stride=1, stride_axis=0)
            head_rows = pl.ds(pl.multiple_of((n % ATT_GROUP) * GRID_W, GRID_W), GRID_W)
            bias_scr[n // ATT_GROUP, head_rows, :] = jnp.where(in_window, toeplitz, MASK_VALUE)
            return carry

        lax.fori_loop(0, gen_ref.shape[0], one_block, 0, unroll=ATT_BIAS_UNROLL)

    def one_row(j, carry):
        r = step * ATT_ROWS_PER_STEP + j
        row_start = jnp.clip(r - kr // 2, 0, rows - kr)
        variant = row_start - r + (WIN_ROWS - 1)
        q_tok = pl.multiple_of(j * GRID_W, GRID_W)
        k_tok = pl.multiple_of(row_start * GRID_W, GRID_W)
        for grp in range(ATT_GROUPS):
            lanes = slice(grp * gw, (grp + 1) * gw)
            q = q_ref[0, pl.ds(q_tok, GRID_W), lanes]
            q_bd = jnp.where(own_head, jnp.concatenate([q] * ATT_GROUP, axis=0), 0)
            keys = k_ref[0, pl.ds(k_tok, kr * GRID_W), lanes]
            vals = v_ref[0, pl.ds(k_tok, kr * GRID_W), lanes]
            tile0 = ((variant % 2) * ATT_GROUPS + grp) * ATT_BIAS_TILES + variant // 2
            bias = jnp.concatenate([bias_scr[tile0 + t] for t in range(kr // 2)], axis=1)
            s = _dot_nt(q_bd, keys) + bias
            m = jnp.max(s, axis=-1, keepdims=True)
            p = jnp.exp2(s - m)
            denom = jnp.sum(p, axis=-1, keepdims=True)
            pv = _dot(p.astype(BF16), vals) / denom
            o = jnp.zeros((GRID_W, gw), F32)
            for h in range(ATT_GROUP):
                o = o + jnp.where(out_lane_head == h, pv[h * GRID_W:(h + 1) * GRID_W], 0.0)
            gate = g_ref[0, pl.ds(q_tok, GRID_W), lanes].astype(F32)
            o = o * _silu_from_half(gate)
            o_ref[0, pl.ds(q_tok, GRID_W), lanes] = o.astype(o_ref.dtype)
        return carry

    lax.fori_loop(0, ATT_ROWS_PER_STEP, one_row, 0, unroll=ATT_UNROLL)


def _attention(q, k, v, g, rpb):
    batch, seq, width = q.shape
    rows = seq // GRID_W
    kr = min(WIN_ROWS, rows)
    assert rows % ATT_ROWS_PER_STEP == 0 and ATT_ROWS_PER_STEP % ATT_UNROLL == 0
    assert kr == WIN_ROWS and 2 * GRID_W == LANES
    gen = _attention_bias_rows(rpb)
    win = jnp.asarray(_attention_window_mask())
    blk = ATT_ROWS_PER_STEP * GRID_W
    tile = pl.BlockSpec((1, blk, width), lambda b, i: (b, i, 0))
    whole = pl.BlockSpec((1, seq, width), lambda b, i: (b, 0, 0))
    return pl.pallas_call(
        functools.partial(_attention_kernel, rows=rows, kr=kr),
        grid=(batch, rows // ATT_ROWS_PER_STEP),
        in_specs=[tile, whole, whole, tile,
                  pl.BlockSpec(gen.shape, lambda b, i: (0, 0, 0)),
                  pl.BlockSpec(win.shape, lambda b, i: (0, 0))],
        out_specs=tile,
        out_shape=jax.ShapeDtypeStruct((batch, seq, width), BF16),
        scratch_shapes=[pltpu.VMEM((2 * ATT_GROUPS * ATT_BIAS_TILES, ATT_GROUP * GRID_W, 2 * GRID_W), F32)],
        compiler_params=pltpu.CompilerParams(
            dimension_semantics=("arbitrary", "arbitrary"), vmem_limit_bytes=VMEM_LIMIT_BYTES),
        name="nbr_attention",
    )(q, k, v, g, gen, win)


def _hgrn_fine_levels():
    sizes = []
    c = HG_GROUP
    while c >= 2:
        sizes.append(c)
        c //= 2
    return sizes


def _hgrn_group_pairs(direction, chunk):
    n = chunk // HG_GROUP
    if direction == 0:
        return [(t, g) for t in range(1, n) for g in range(t)]
    return [(t, g) for t in range(n - 2, -1, -1) for g in range(t + 1, n)]


def _hgrn_constants(chunk):
    t = np.arange(chunk)[:, None]
    u = np.arange(chunk)[None, :]
    blocks = [(u <= t)]
    masks = []
    for c in _hgrn_fine_levels():
        half = c // 2
        mid = (t // c) * c + half
        upper = (t % c) >= half
        if 2 < c < HG_GROUP:
            blocks.append(np.where(upper, (u >= mid) & (u <= t), (u > t) & (u <= mid - 1)))
        masks.append((t // c == u // c) & upper & ((u % c) < half))
    fwd_sums = np.stack(blocks).astype(np.float32)
    fwd_masks = np.stack(masks).astype(np.float32)
    twice = lambda m: np.concatenate([m.reshape(-1, chunk)] * 2, axis=1)
    sums = np.stack([twice(fwd_sums), twice(fwd_sums[:, ::-1, ::-1])])
    masks = np.concatenate([fwd_masks, fwd_masks[:, ::-1, ::-1]], axis=2)
    return sums, masks


def _block_diag(a, b):
    zero = jnp.zeros_like(a)
    return jnp.concatenate([jnp.concatenate([a, zero], axis=1),
                            jnp.concatenate([zero, b], axis=1)], axis=0)


def _hgrn_dec_rows(chunk):
    n_var = len(_hgrn_group_pairs(0, chunk)) * HG_GROUP
    n_mm = sum(c > 2 for c in _hgrn_fine_levels())
    edge, var, fine = 2 * chunk, 3 * chunk, 3 * chunk + n_var
    return edge, var, fine, fine + n_mm * chunk


def _hgrn_kernel(q_ref, zf_ref, zb_ref, i_ref, g_ref, lbf_ref, lbb_ref, gain_ref,
                 sums_ref, masks_ref, o_ref, of_scr, ob_scr, sf_scr, sb_scr,
                 k_scr0, k_scr1, dec_scr0, dec_scr1, *, layer, seq):
    chunk = HG_CHUNK
    pair = 2 * chunk
    n_pairs = seq // pair
    dh = HG_HEAD_DIM
    grp = HG_GROUP
    n_groups = chunk // grp
    fine = _hgrn_fine_levels()
    group_pairs = [_hgrn_group_pairs(direction, chunk) for direction in (0, 1)]
    r_edge, r_var, r_fine, _ = _hgrn_dec_rows(chunk)
    k_slots = (k_scr0, k_scr1)
    dec_slots = (dec_scr0, dec_scr1)
    row_in_group = lax.broadcasted_iota(jnp.int32, (grp, dh), 0)
    lane_group = lax.broadcasted_iota(jnp.int32, (grp, 2 * chunk), 1) // grp

    def lower_bound(logit_ref):
        logits = logit_ref[...].astype(F32)
        e = jnp.exp(logits - jnp.max(logits, axis=0, keepdims=True))
        return jnp.sum(e[:layer + 1], axis=0, keepdims=True) / jnp.sum(e, axis=0, keepdims=True)

    def gates(z, lb):
        f = lb + (1.0 - lb) * jax.nn.sigmoid(z.astype(F32))
        return jnp.log2(f), 1.0 - f

    def store_decays(slot, direction, g2):
        hi = g2.astype(BF16)
        lo = (g2 - hi.astype(F32)).astype(BF16)
        stacked = [jnp.concatenate([hi[c * chunk:(c + 1) * chunk], lo[c * chunk:(c + 1) * chunk]], axis=0)
                   for c in range(2)]
        raw = _dot(sums_ref[direction], jnp.concatenate(stacked, axis=1))
        dec = dec_slots[slot].at[direction]
        cum = raw[0:chunk]
        whole = cum[chunk - 1:chunk] if direction == 0 else cum[0:1]
        dec[0:chunk, :] = jnp.exp2(cum)
        dec[chunk:2 * chunk, :] = jnp.exp2(whole - cum)
        by_group = cum.reshape(n_groups, grp, 2 * dh)
        mid_row = grp // 2 - 1 if direction == 0 else grp // 2
        to_mid = -jnp.abs(by_group - by_group[:, mid_row:mid_row + 1, :])
        dec[r_fine:r_fine + chunk, :] = jnp.exp2(to_mid.reshape(chunk, 2 * dh))
        dec[r_fine + chunk:, :] = jnp.exp2(raw[chunk:])
        edge = grp - 1 if direction == 0 else 0
        edges = [cum[g * grp + edge:g * grp + edge + 1] for g in range(n_groups)]
        group = lambda g: cum[g * grp:(g + 1) * grp]
        dec[r_edge:r_var, :] = jnp.exp2(jnp.concatenate([edges[g] - group(g) for g in range(n_groups)], axis=0))
        dec[r_var:r_fine, :] = jnp.exp2(jnp.concatenate([group(t) - edges[g]
                                                         for t, g in group_pairs[direction]], axis=0))

    def fine_operand(direction, lvl, q, k, dec):
        c = fine[lvl]
        pos = row_in_group % c
        is_upper = (pos >= c // 2) if direction == 0 else (pos < c // 2)
        pick = lambda x, y: jnp.where(is_upper[None], x.reshape(n_groups, grp, dh),
                                      y.reshape(n_groups, grp, dh)).reshape(chunk, dh)
        if c == 2:
            return pick(q * (1.0 - k), k).astype(BF16)
        return (pick(q, k) * dec(r_fine + lvl * chunk, r_fine + (lvl + 1) * chunk)).astype(BF16)

    def chunk_part(slot, direction, c, q, v, base):
        rows = slice(c * chunk, (c + 1) * chunk)
        lanes = slice(c * dh, (c + 1) * dh)
        return dict(q=q[rows], k=k_slots[slot][direction, rows, :], v=v[rows],
                    dec=lambda r0, r1: dec_slots[slot][direction, r0:r1, lanes],
                    rows=pl.ds(pl.multiple_of(base + c * chunk, chunk), chunk))

    def pair_step(f, b):
        lhs = jnp.concatenate(
            [(jnp.concatenate([p["q"][t * grp:(t + 1) * grp] for t, _ in group_pairs[d]], axis=0)
              * p["dec"](r_var, r_fine)).astype(BF16) for d, p in enumerate((f, b))], axis=1)
        keys = [(p["k"] * p["dec"](r_edge, r_var)).astype(BF16) for p in (f, b)]
        r = _dot_nt(lhs, _block_diag(*keys))
        rows_of = [jnp.zeros((grp, 2 * chunk), F32)] * n_groups
        for i, ((tf, gf), (tb, gb)) in enumerate(zip(*group_pairs)):
            block = r[i * grp:(i + 1) * grp]
            rows_of[tf] = jnp.where(lane_group == gf, block, rows_of[tf])
            rows_of[tb] = jnp.where(lane_group == n_groups + gb, block, rows_of[tb])
        a = jnp.concatenate(rows_of, axis=0)
        for lvl in range(len(fine)):
            xf = fine_operand(0, lvl, f["q"], f["k"], f["dec"])
            xb = fine_operand(1, lvl, b["q"], b["k"], b["dec"])
            a = a + masks_ref[lvl] * _dot_nt(jnp.concatenate([xf, xb], axis=1), _block_diag(xf, xb))
        o = _dot(a.astype(BF16), _block_diag(f["v"], b["v"]))
        sf = sf_scr[...]
        sb = sb_scr[...]
        q_dec = jnp.concatenate([(f["q"] * f["dec"](0, chunk)).astype(BF16),
                                 (b["q"] * b["dec"](0, chunk)).astype(BF16)], axis=1)
        o = o + _dot_nt(q_dec, _block_diag(sf.astype(BF16), sb.astype(BF16)))
        diag_f = jnp.sum(f["q"] * f["k"], axis=-1, keepdims=True)
        diag_b = jnp.sum(b["q"] * b["k"], axis=-1, keepdims=True)
        of_scr[f["rows"], :] = o[:, :dh] + diag_f * f["v"].astype(F32)
        ob_scr[b["rows"], :] = o[:, dh:] + diag_b * b["v"].astype(F32)
        k_dec = _block_diag((f["k"] * f["dec"](chunk, 2 * chunk)).astype(BF16),
                            (b["k"] * b["dec"](chunk, 2 * chunk)).astype(BF16))
        v_t = jnp.concatenate([f["v"], b["v"]], axis=0).astype(F32).T.astype(BF16)
        u = _dot(v_t, k_dec)
        sf_scr[...] = sf * f["dec"](chunk - 1, chunk) + u[:, :dh]
        sb_scr[...] = sb * b["dec"](0, 1) + u[:, dh:]

    lb_f = lower_bound(lbf_ref)
    lb_b = lower_bound(lbb_ref)
    sf_scr[...] = jnp.zeros_like(sf_scr)
    sb_scr[...] = jnp.zeros_like(sb_scr)

    def bases(n):
        return (pl.multiple_of(n * pair, pair), pl.multiple_of((n_pairs - 1 - n) * pair, pair))

    def prepare(n, slot):
        for direction, base, z_ref, lb in zip((0, 1), bases(n), (zf_ref, zb_ref), (lb_f, lb_b)):
            g2, k = gates(z_ref[0, pl.ds(base, pair), :], lb)
            k_slots[slot][direction] = k
            store_decays(slot, direction, g2)

    def consume(n, slot):
        parts = []
        for direction, base in zip((0, 1), bases(n)):
            rows = pl.ds(base, pair)
            q = q_ref[0, rows, :].astype(F32)
            v = i_ref[0, rows, :]
            parts.append([chunk_part(slot, direction, c, q, v, base) for c in range(2)])
        pair_step(parts[0][0], parts[1][1])
        pair_step(parts[0][1], parts[1][0])

    def scan_body(m, carry):
        prepare(2 * m + 1, 1)
        consume(2 * m, 0)
        prepare(jnp.minimum(2 * m + 2, n_pairs - 1), 0)
        consume(2 * m + 1, 1)
        return carry

    prepare(0, 0)
    lax.fori_loop(0, n_pairs // 2, scan_body, 0, unroll=HG_SCAN_UNROLL)

    gain = gain_ref[...].astype(F32)

    def norm_body(n, carry):
        rows = pl.ds(pl.multiple_of(n * HG_NORM_ROWS, HG_NORM_ROWS), HG_NORM_ROWS)
        o = of_scr[rows, :] + ob_scr[rows, :]
        o = o * lax.rsqrt(jnp.mean(jnp.square(o), axis=-1, keepdims=True) + RMS_EPS) * gain
        gate = g_ref[0, rows, :].astype(F32)
        o_ref[0, rows, :] = (o * _silu_from_half(gate)).astype(o_ref.dtype)
        return carry

    lax.fori_loop(0, seq // HG_NORM_ROWS, norm_body, 0, unroll=4)


def _hgrn(q, zf, zb, i, g, lb_fwd_logits, lb_bwd_logits, gain_row, layer):
    batch, seq, width = q.shape
    assert width == HG_HEADS * HG_HEAD_DIM and seq % (4 * HG_NORM_ROWS) == 0
    assert seq % (4 * HG_CHUNK * HG_SCAN_UNROLL) == 0
    sums, masks = _hgrn_constants(HG_CHUNK)
    sums = jnp.asarray(sums, BF16)
    masks = jnp.asarray(masks, F32)
    n_layers = lb_fwd_logits.shape[0]
    n_dec_rows = _hgrn_dec_rows(HG_CHUNK)[-1]
    head = pl.BlockSpec((1, seq, HG_HEAD_DIM), lambda b, h: (b, 0, h))
    per_head_row = lambda n: pl.BlockSpec((n, HG_HEAD_DIM), lambda b, h: (0, h))
    whole = lambda a: pl.BlockSpec(a.shape, lambda b, h: (0,) * a.ndim)
    return pl.pallas_call(
        functools.partial(_hgrn_kernel, layer=layer, seq=seq),
        grid=(batch, HG_HEADS),
        in_specs=[head, head, head, head, head,
                  per_head_row(n_layers), per_head_row(n_layers), per_head_row(1),
                  whole(sums), whole(masks)],
        out_specs=head,
        out_shape=jax.ShapeDtypeStruct((batch, seq, width), BF16),
        scratch_shapes=[pltpu.VMEM((seq, HG_HEAD_DIM), F32), pltpu.VMEM((seq, HG_HEAD_DIM), F32),
                        pltpu.VMEM((HG_HEAD_DIM, HG_HEAD_DIM), F32),
                        pltpu.VMEM((HG_HEAD_DIM, HG_HEAD_DIM), F32),
                        pltpu.VMEM((2, 2 * HG_CHUNK, HG_HEAD_DIM), F32),
                        pltpu.VMEM((2, 2 * HG_CHUNK, HG_HEAD_DIM), F32),
                        pltpu.VMEM((2, n_dec_rows, 2 * HG_HEAD_DIM), F32),
                        pltpu.VMEM((2, n_dec_rows, 2 * HG_HEAD_DIM), F32)],
        compiler_params=pltpu.CompilerParams(
            dimension_semantics=("arbitrary", "arbitrary"), vmem_limit_bytes=VMEM_LIMIT_BYTES),
        name="hgrn2_scan",
    )(q, zf, zb, i, g, lb_fwd_logits, lb_bwd_logits, gain_row, sums, masks)


def _out_proj_kernel(oa_ref, oh_ref, x_ref, w_ref, b_ref, gain_ref, bias_ref, o_ref, w_scr, *, alpha):
    d_att = oa_ref.shape[1]
    _cast_weight_once(w_ref, w_scr)
    for r0 in range(0, o_ref.shape[0], OUT_PROJ_ROWS):
        rows = slice(r0, r0 + OUT_PROJ_ROWS)
        y = _dot(oa_ref[rows, :], w_scr[:d_att, :]) + _dot(oh_ref[rows, :], w_scr[d_att:, :]) + b_ref[...]
        r = alpha * x_ref[rows, :] + y
        mu = jnp.mean(r, axis=-1, keepdims=True)
        c = r - mu
        var = jnp.mean(jnp.square(c), axis=-1, keepdims=True)
        o_ref[rows, :] = (c * lax.rsqrt(var + LN_EPS) * gain_ref[...] + bias_ref[...]).astype(o_ref.dtype)


def _out_proj(o_a, o_h, x2d, w, b_row, gain_row, bias_row, alpha):
    n_tok, d_model = x2d.shape
    d_att, d_hg = o_a.shape[1], o_h.shape[1]
    row = pl.BlockSpec((1, d_model), lambda i: (0, 0))
    return pl.pallas_call(
        functools.partial(_out_proj_kernel, alpha=alpha),
        grid=(n_tok // OUT_PROJ_TM,),
        in_specs=[pl.BlockSpec((OUT_PROJ_TM, d_att), lambda i: (i, 0)),
                  pl.BlockSpec((OUT_PROJ_TM, d_hg), lambda i: (i, 0)),
                  pl.BlockSpec((OUT_PROJ_TM, d_model), lambda i: (i, 0)),
                  pl.BlockSpec((d_att + d_hg, d_model), lambda i: (0, 0), pipeline_mode=pl.Buffered(1)),
                  row, row, row],
        out_specs=pl.BlockSpec((OUT_PROJ_TM, d_model), lambda i: (i, 0)),
        out_shape=jax.ShapeDtypeStruct((n_tok, d_model), x2d.dtype),
        scratch_shapes=[pltpu.VMEM((d_att + d_hg, d_model), BF16)],
        compiler_params=pltpu.CompilerParams(
            dimension_semantics=("arbitrary",), vmem_limit_bytes=VMEM_LIMIT_BYTES),
        name="out_proj_layernorm",
    )(o_a, o_h, x2d, w, b_row, gain_row, bias_row)


def _layer(x, layer, depth, w_in, b_in, rpb, lb_fwd_logits, lb_bwd_logits, hg_norm_gain,
           w_out, b_out, ln_gain, ln_bias):
    batch, seq, d_model = x.shape
    x2d = x.reshape(batch * seq, d_model)
    slabs = _in_proj(x2d, w_in, b_in.reshape(1, -1))
    q_a, k_a, v_a, g_a, q_h, z_f, z_b, i_h, g_h = [s.reshape(batch, seq, SLAB) for s in slabs]
    o_a = _attention(q_a, k_a, v_a, g_a, rpb)
    o_h = _hgrn(q_h, z_f, z_b, i_h, g_h, lb_fwd_logits, lb_bwd_logits,
                hg_norm_gain.reshape(1, -1), layer)
    alpha = (2.0 * depth) ** 0.25
    out = _out_proj(o_a.reshape(batch * seq, SLAB), o_h.reshape(batch * seq, SLAB), x2d,
                    w_out, b_out.reshape(1, -1), ln_gain.reshape(1, -1),
                    ln_bias.reshape(1, -1), alpha)
    return out.reshape(batch, seq, d_model)


def kernel(x, w_in, b_in, rpb, lb_fwd_logits, lb_bwd_logits, hg_norm_gain, w_out, b_out, ln_gain, ln_bias):
    depth = w_in.shape[0]
    for layer in range(depth):
        x = _layer(x, layer, depth, w_in[layer], b_in[layer], rpb[layer], lb_fwd_logits,
                   lb_bwd_logits, hg_norm_gain[layer], w_out[layer], b_out[layer],
                   ln_gain[layer], ln_bias[layer])
    return x
```

```python
import functools

import numpy as np
import jax
import jax.numpy as jnp
from jax import lax
from jax.experimental import pallas as pl
from jax.experimental.pallas import tpu as pltpu

GRID_W = 64
ATT_HEADS = 8
ATT_HEAD_DIM = 64
HG_HEADS = 4
HG_HEAD_DIM = 128
WIN_ROWS = 8
WIN_COLS = 16
LN_EPS = 1e-5
RMS_EPS = 1e-6
N_SLABS = 9
SLAB = 512

LANES = 128
F32_SUBLANES = 8
VMEM_LIMIT_BYTES = 56 * 1024 * 1024
PROJ_TM = 512
OUT_PROJ_TM = 2048
OUT_PROJ_ROWS = 256
ATT_ROWS_PER_STEP = 32
ATT_UNROLL = 4
ATT_GROUP = 4
ATT_BIAS_UNROLL = 8
HG_CHUNK = 64
HG_GROUP = F32_SUBLANES
HG_NORM_ROWS = 256
HG_SCAN_UNROLL = 16
MASK_VALUE = -1e30
LOG2_E = 1.4426950408889634
SLAB_SCALES = {1: ATT_HEAD_DIM ** -0.5 * LOG2_E, 3: 0.5, 8: 0.5}

BF16 = jnp.bfloat16
F32 = jnp.float32


def _silu_from_half(half_g):
    return half_g * (1.0 + jnp.tanh(half_g))


def _dot(a, b):
    return jnp.dot(a, b, preferred_element_type=F32)


def _dot_nt(a, b):
    return lax.dot_general(a, b, (((1,), (1,)), ((), ())), preferred_element_type=F32)


def _cast_weight_once(w_ref, w_scr):
    @pl.when(pl.program_id(0) == 0)
    def _():
        for c0 in range(0, w_ref.shape[1], SLAB):
            w_scr[:, c0:c0 + SLAB] = w_ref[:, c0:c0 + SLAB].astype(w_scr.dtype)


def _in_proj_kernel(x_ref, w_ref, b_ref, *refs):
    out_refs, w_scr = refs[:-1], refs[-1]
    _cast_weight_once(w_ref, w_scr)
    xb = x_ref[...].astype(BF16)
    for j, o_ref in enumerate(out_refs):
        cols = slice(j * SLAB, (j + 1) * SLAB)
        h = _dot(xb, w_scr[:, cols]) + b_ref[:, cols]
        if j in SLAB_SCALES:
            h = h * SLAB_SCALES[j]
        o_ref[...] = h.astype(o_ref.dtype)


def _in_proj(x2d, w, b_row):
    n_tok, d_model = x2d.shape
    d_in = w.shape[1]
    assert d_in == N_SLABS * SLAB and n_tok % PROJ_TM == 0
    return pl.pallas_call(
        _in_proj_kernel,
        grid=(n_tok // PROJ_TM,),
        in_specs=[
            pl.BlockSpec((PROJ_TM, d_model), lambda i: (i, 0)),
            pl.BlockSpec((d_model, d_in), lambda i: (0, 0), pipeline_mode=pl.Buffered(1)),
            pl.BlockSpec((1, d_in), lambda i: (0, 0)),
        ],
        out_specs=[pl.BlockSpec((PROJ_TM, SLAB), lambda i: (i, 0))] * N_SLABS,
        out_shape=[jax.ShapeDtypeStruct((n_tok, SLAB), BF16)] * N_SLABS,
        scratch_shapes=[pltpu.VMEM((d_model, d_in), BF16)],
        compiler_params=pltpu.CompilerParams(
            dimension_semantics=("arbitrary",), vmem_limit_bytes=VMEM_LIMIT_BYTES),
        name="in_proj",
    )(x2d, w, b_row)


ATT_REL_ROWS = 2 * WIN_ROWS - 1
ATT_BIAS_TILES = (ATT_REL_ROWS - 1) // 2
ATT_GROUPS = ATT_HEADS // ATT_GROUP


def _attention_bias_rows(rpb):
    n_rel_cols = 2 * WIN_COLS - 1
    scaled = rpb.astype(F32) * LOG2_E
    gap = jnp.zeros((ATT_HEADS, ATT_BIAS_TILES, (LANES - 2 * n_rel_cols) // 2), F32)
    per_parity = []
    for p in range(2):
        first = scaled[:, p:p + 2 * ATT_BIAS_TILES - 1:2]
        second = scaled[:, p + 1:p + 2 * ATT_BIAS_TILES:2]
        per_parity.append(jnp.concatenate(
            [first[..., WIN_COLS - 1:], gap, second, gap, first[..., :WIN_COLS - 1]], axis=-1))
    gen = jnp.stack(per_parity)
    gen = gen.reshape(2, ATT_GROUPS, ATT_GROUP, ATT_BIAS_TILES, LANES).transpose(0, 1, 3, 2, 4)
    return gen.reshape(2 * ATT_GROUPS * ATT_BIAS_TILES * ATT_GROUP, 1, LANES)


def _attention_window_mask():
    qc = np.arange(GRID_W)[:, None]
    kc = np.arange(LANES)[None, :] % GRID_W
    col_start = np.clip(qc - WIN_COLS // 2, 0, GRID_W - WIN_COLS)
    return ((kc >= col_start) & (kc < col_start + WIN_COLS)).astype(np.float32)


def _attention_kernel(q_ref, k_ref, v_ref, g_ref, gen_ref, win_ref, o_ref, bias_scr, *, rows, kr):
    step = pl.program_id(1)
    gw = ATT_GROUP * ATT_HEAD_DIM
    row_head = lax.broadcasted_iota(jnp.int32, (ATT_GROUP * GRID_W, gw), 0) // GRID_W
    lane_head = lax.broadcasted_iota(jnp.int32, (ATT_GROUP * GRID_W, gw), 1) // ATT_HEAD_DIM
    own_head = row_head == lane_head
    out_lane_head = lax.broadcasted_iota(jnp.int32, (GRID_W, gw), 1) // ATT_HEAD_DIM

    @pl.when((pl.program_id(0) == 0) & (step == 0))
    def _build_bias():
        in_window = win_ref[...] > 0.0

        def one_block(n, carry):
            gen = jnp.broadcast_to(gen_ref[n], (GRID_W, LANES))
            toeplitz = pltpu.roll(gen, 0, 1, stride=1, stride_axis=0)
            head_rows = pl.ds(pl.multiple_of((n % ATT_GROUP) * GRID_W, GRID_W), GRID_W)
            bias_scr[n // ATT_GROUP, head_rows, :] = jnp.where(in_window, toeplitz, MASK_VALUE)
            return carry

        lax.fori_loop(0, gen_ref.shape[0], one_block, 0, unroll=ATT_BIAS_UNROLL)

    def one_row(j, carry):
        r = step * ATT_ROWS_PER_STEP + j
        row_start = jnp.clip(r - kr // 2, 0, rows - kr)
        variant = row_start - r + (WIN_ROWS - 1)
        q_tok = pl.multiple_of(j * GRID_W, GRID_W)
        k_tok = pl.multiple_of(row_start * GRID_W, GRID_W)
        for grp in range(ATT_GROUPS):
            lanes = slice(grp * gw, (grp + 1) * gw)
            q = q_ref[0, pl.ds(q_tok, GRID_W), lanes]
            q_bd = jnp.where(own_head, jnp.concatenate([q] * ATT_GROUP, axis=0), 0)
            keys = k_ref[0, pl.ds(k_tok, kr * GRID_W), lanes]
            vals = v_ref[0, pl.ds(k_tok, kr * GRID_W), lanes]
            tile0 = ((variant % 2) * ATT_GROUPS + grp) * ATT_BIAS_TILES + variant // 2
            bias = jnp.concatenate([bias_scr[tile0 + t] for t in range(kr // 2)], axis=1)
            s = _dot_nt(q_bd, keys) + bias
            m = jnp.max(s, axis=-1, keepdims=True)
            p = jnp.exp2(s - m)
            denom = jnp.sum(p, axis=-1, keepdims=True)
            pv = _dot(p.astype(BF16), vals) / denom
            o = jnp.zeros((GRID_W, gw), F32)
            for h in range(ATT_GROUP):
                o = o + jnp.where(out_lane_head == h, pv[h * GRID_W:(h + 1) * GRID_W], 0.0)
            gate = g_ref[0, pl.ds(q_tok, GRID_W), lanes].astype(F32)
            o = o * _silu_from_half(gate)
            o_ref[0, pl.ds(q_tok, GRID_W), lanes] = o.astype(o_ref.dtype)
        return carry

    lax.fori_loop(0, ATT_ROWS_PER_STEP, one_row, 0, unroll=ATT_UNROLL)


def _attention(q, k, v, g, rpb):
    batch, seq, width = q.shape
    rows = seq // GRID_W
    kr = min(WIN_ROWS, rows)
    assert rows % ATT_ROWS_PER_STEP == 0 and ATT_ROWS_PER_STEP % ATT_UNROLL == 0
    assert kr == WIN_ROWS and 2 * GRID_W == LANES
    gen = _attention_bias_rows(rpb)
    win = jnp.asarray(_attention_window_mask())
    blk = ATT_ROWS_PER_STEP * GRID_W
    tile = pl.BlockSpec((1, blk, width), lambda b, i: (b, i, 0))
    whole = pl.BlockSpec((1, seq, width), lambda b, i: (b, 0, 0))
    return pl.pallas_call(
        functools.partial(_attention_kernel, rows=rows, kr=kr),
        grid=(batch, rows // ATT_ROWS_PER_STEP),
        in_specs=[tile, whole, whole, tile,
                  pl.BlockSpec(gen.shape, lambda b, i: (0, 0, 0)),
                  pl.BlockSpec(win.shape, lambda b, i: (0, 0))],
        out_specs=tile,
        out_shape=jax.ShapeDtypeStruct((batch, seq, width), BF16),
        scratch_shapes=[pltpu.VMEM((2 * ATT_GROUPS * ATT_BIAS_TILES, ATT_GROUP * GRID_W, 2 * GRID_W), F32)],
        compiler_params=pltpu.CompilerParams(
            dimension_semantics=("arbitrary", "arbitrary"), vmem_limit_bytes=VMEM_LIMIT_BYTES),
        name="nbr_attention",
    )(q, k, v, g, gen, win)


def _hgrn_fine_levels():
    sizes = []
    c = HG_GROUP
    while c >= 2:
        sizes.append(c)
        c //= 2
    return sizes


def _hgrn_group_pairs(direction, chunk):
    n = chunk // HG_GROUP
    if direction == 0:
        return [(t, g) for t in range(1, n) for g in range(t)]
    return [(t, g) for t in range(n - 2, -1, -1) for g in range(t + 1, n)]


def _hgrn_constants(chunk):
    t = np.arange(chunk)[:, None]
    u = np.arange(chunk)[None, :]
    blocks = [(u <= t)]
    masks = []
    for c in _hgrn_fine_levels():
        half = c // 2
        mid = (t // c) * c + half
        upper = (t % c) >= half
        if 2 < c < HG_GROUP:
            blocks.append(np.where(upper, (u >= mid) & (u <= t), (u > t) & (u <= mid - 1)))
        masks.append((t // c == u // c) & upper & ((u % c) < half))
    fwd_sums = np.stack(blocks).astype(np.float32)
    fwd_masks = np.stack(masks).astype(np.float32)
    twice = lambda m: np.concatenate([m.reshape(-1, chunk)] * 2, axis=1)
    sums = np.stack([twice(fwd_sums), twice(fwd_sums[:, ::-1, ::-1])])
    masks = np.concatenate([fwd_masks, fwd_masks[:, ::-1, ::-1]], axis=2)
    return sums, masks


def _block_diag(a, b):
    zero = jnp.zeros_like(a)
    return jnp.concatenate([jnp.concatenate([a, zero], axis=1),
                            jnp.concatenate([zero, b], axis=1)], axis=0)


def _hgrn_dec_rows(chunk):
    n_var = len(_hgrn_group_pairs(0, chunk)) * HG_GROUP
    n_mm = sum(c > 2 for c in _hgrn_fine_levels())
    edge, var, fine = 2 * chunk, 3 * chunk, 3 * chunk + n_var
    return edge, var, fine, fine + n_mm * chunk


def _hgrn_kernel(q_ref, zf_ref, zb_ref, i_ref, g_ref, lbf_ref, lbb_ref, gain_ref,
                 sums_ref, masks_ref, o_ref, of_scr, ob_scr, sf_scr, sb_scr,
                 k_scr0, k_scr1, dec_scr0, dec_scr1, *, layer, seq):
    chunk = HG_CHUNK
    pair = 2 * chunk
    n_pairs = seq // pair
    dh = HG_HEAD_DIM
    grp = HG_GROUP
    n_groups = chunk // grp
    fine = _hgrn_fine_levels()
    group_pairs = [_hgrn_group_pairs(direction, chunk) for direction in (0, 1)]
    r_edge, r_var, r_fine, _ = _hgrn_dec_rows(chunk)
    k_slots = (k_scr0, k_scr1)
    dec_slots = (dec_scr0, dec_scr1)
    row_in_group = lax.broadcasted_iota(jnp.int32, (grp, dh), 0)
    lane_group = lax.broadcasted_iota(jnp.int32, (grp, 2 * chunk), 1) // grp

    def lower_bound(logit_ref):
        logits = logit_ref[...].astype(F32)
        e = jnp.exp(logits - jnp.max(logits, axis=0, keepdims=True))
        return jnp.sum(e[:layer + 1], axis=0, keepdims=True) / jnp.sum(e, axis=0, keepdims=True)

    def gates(z, lb):
        f = lb + (1.0 - lb) * jax.nn.sigmoid(z.astype(F32))
        return jnp.log2(f), 1.0 - f

    def store_decays(slot, direction, g2):
        hi = g2.astype(BF16)
        lo = (g2 - hi.astype(F32)).astype(BF16)
        stacked = [jnp.concatenate([hi[c * chunk:(c + 1) * chunk], lo[c * chunk:(c + 1) * chunk]], axis=0)
                   for c in range(2)]
        raw = _dot(sums_ref[direction], jnp.concatenate(stacked, axis=1))
        dec = dec_slots[slot].at[direction]
        cum = raw[0:chunk]
        whole = cum[chunk - 1:chunk] if direction == 0 else cum[0:1]
        dec[0:chunk, :] = jnp.exp2(cum)
        dec[chunk:2 * chunk, :] = jnp.exp2(whole - cum)
        by_group = cum.reshape(n_groups, grp, 2 * dh)
        mid_row = grp // 2 - 1 if direction == 0 else grp // 2
        to_mid = -jnp.abs(by_group - by_group[:, mid_row:mid_row + 1, :])
        dec[r_fine:r_fine + chunk, :] = jnp.exp2(to_mid.reshape(chunk, 2 * dh))
        dec[r_fine + chunk:, :] = jnp.exp2(raw[chunk:])
        edge = grp - 1 if direction == 0 else 0
        edges = [cum[g * grp + edge:g * grp + edge + 1] for g in range(n_groups)]
        group = lambda g: cum[g * grp:(g + 1) * grp]
        dec[r_edge:r_var, :] = jnp.exp2(jnp.concatenate([edges[g] - group(g) for g in range(n_groups)], axis=0))
        dec[r_var:r_fine, :] = jnp.exp2(jnp.concatenate([group(t) - edges[g]
                                                         for t, g in group_pairs[direction]], axis=0))

    def fine_operand(direction, lvl, q, k, dec):
        c = fine[lvl]
        pos = row_in_group % c
        is_upper = (pos >= c // 2) if direction == 0 else (pos < c // 2)
        pick = lambda x, y: jnp.where(is_upper[None], x.reshape(n_groups, grp, dh),
                                      y.reshape(n_groups, grp, dh)).reshape(chunk, dh)
        if c == 2:
            return pick(q * (1.0 - k), k).astype(BF16)
        return (pick(q, k) * dec(r_fine + lvl * chunk, r_fine + (lvl + 1) * chunk)).astype(BF16)

    def chunk_part(slot, direction, c, q, v, base):
        rows = slice(c * chunk, (c + 1) * chunk)
        lanes = slice(c * dh, (c + 1) * dh)
        return dict(q=q[rows], k=k_slots[slot][direction, rows, :], v=v[rows],
                    dec=lambda r0, r1: dec_slots[slot][direction, r0:r1, lanes],
                    rows=pl.ds(pl.multiple_of(base + c * chunk, chunk), chunk))

    def pair_step(f, b):
        lhs = jnp.concatenate(
            [(jnp.concatenate([p["q"][t * grp:(t + 1) * grp] for t, _ in group_pairs[d]], axis=0)
              * p["dec"](r_var, r_fine)).astype(BF16) for d, p in enumerate((f, b))], axis=1)
        keys = [(p["k"] * p["dec"](r_edge, r_var)).astype(BF16) for p in (f, b)]
        r = _dot_nt(lhs, _block_diag(*keys))
        rows_of = [jnp.zeros((grp, 2 * chunk), F32)] * n_groups
        for i, ((tf, gf), (tb, gb)) in enumerate(zip(*group_pairs)):
            block = r[i * grp:(i + 1) * grp]
            rows_of[tf] = jnp.where(lane_group == gf, block, rows_of[tf])
            rows_of[tb] = jnp.where(lane_group == n_groups + gb, block, rows_of[tb])
        a = jnp.concatenate(rows_of, axis=0)
        for lvl in range(len(fine)):
            xf = fine_operand(0, lvl, f["q"], f["k"], f["dec"])
            xb = fine_operand(1, lvl, b["q"], b["k"], b["dec"])
            a = a + masks_ref[lvl] * _dot_nt(jnp.concatenate([xf, xb], axis=1), _block_diag(xf, xb))
        o = _dot(a.astype(BF16), _block_diag(f["v"], b["v"]))
        sf = sf_scr[...]
        sb = sb_scr[...]
        q_dec = jnp.concatenate([(f["q"] * f["dec"](0, chunk)).astype(BF16),
                                 (b["q"] * b["dec"](0, chunk)).astype(BF16)], axis=1)
        o = o + _dot_nt(q_dec, _block_diag(sf.astype(BF16), sb.astype(BF16)))
        diag_f = jnp.sum(f["q"] * f["k"], axis=-1, keepdims=True)
        diag_b = jnp.sum(b["q"] * b["k"], axis=-1, keepdims=True)
        of_scr[f["rows"], :] = o[:, :dh] + diag_f * f["v"].astype(F32)
        ob_scr[b["rows"], :] = o[:, dh:] + diag_b * b["v"].astype(F32)
        k_dec = _block_diag((f["k"] * f["dec"](chunk, 2 * chunk)).astype(BF16),
                            (b["k"] * b["dec"](chunk, 2 * chunk)).astype(BF16))
        v_t = jnp.concatenate([f["v"], b["v"]], axis=0).astype(F32).T.astype(BF16)
        u = _dot(v_t, k_dec)
        sf_scr[...] = sf * f["dec"](chunk - 1, chunk) + u[:, :dh]
        sb_scr[...] = sb * b["dec"](0, 1) + u[:, dh:]

    lb_f = lower_bound(lbf_ref)
    lb_b = lower_bound(lbb_ref)
    sf_scr[...] = jnp.zeros_like(sf_scr)
    sb_scr[...] = jnp.zeros_like(sb_scr)

    def bases(n):
        return (pl.multiple_of(n * pair, pair), pl.multiple_of((n_pairs - 1 - n) * pair, pair))

    def prepare(n, slot):
        for direction, base, z_ref, lb in zip((0, 1), bases(n), (zf_ref, zb_ref), (lb_f, lb_b)):
            g2, k = gates(z_ref[0, pl.ds(base, pair), :], lb)
            k_slots[slot][direction] = k
            store_decays(slot, direction, g2)

    def consume(n, slot):
        parts = []
        for direction, base in zip((0, 1), bases(n)):
            rows = pl.ds(base, pair)
            q = q_ref[0, rows, :].astype(F32)
            v = i_ref[0, rows, :]
            parts.append([chunk_part(slot, direction, c, q, v, base) for c in range(2)])
        pair_step(parts[0][0], parts[1][1])
        pair_step(parts[0][1], parts[1][0])

    def scan_body(m, carry):
        prepare(2 * m + 1, 1)
        consume(2 * m, 0)
        prepare(jnp.minimum(2 * m + 2, n_pairs - 1), 0)
        consume(2 * m + 1, 1)
        return carry

    prepare(0, 0)
    lax.fori_loop(0, n_pairs // 2, scan_body, 0, unroll=HG_SCAN_UNROLL)

    gain = gain_ref[...].astype(F32)

    def norm_body(n, carry):
        rows = pl.ds(pl.multiple_of(n * HG_NORM_ROWS, HG_NORM_ROWS), HG_NORM_ROWS)
        o = of_scr[rows, :] + ob_scr[rows, :]
        o = o * lax.rsqrt(jnp.mean(jnp.square(o), axis=-1, keepdims=True) + RMS_EPS) * gain
        gate = g_ref[0, rows, :].astype(F32)
        o_ref[0, rows, :] = (o * _silu_from_half(gate)).astype(o_ref.dtype)
        return carry

    lax.fori_loop(0, seq // HG_NORM_ROWS, norm_body, 0, unroll=4)


def _hgrn(q, zf, zb, i, g, lb_fwd_logits, lb_bwd_logits, gain_row, layer):
    batch, seq, width = q.shape
    assert width == HG_HEADS * HG_HEAD_DIM and seq % (4 * HG_NORM_ROWS) == 0
    assert seq % (4 * HG_CHUNK * HG_SCAN_UNROLL) == 0
    sums, masks = _hgrn_constants(HG_CHUNK)
    sums = jnp.asarray(sums, BF16)
    masks = jnp.asarray(masks, F32)
    n_layers = lb_fwd_logits.shape[0]
    n_dec_rows = _hgrn_dec_rows(HG_CHUNK)[-1]
    head = pl.BlockSpec((1, seq, HG_HEAD_DIM), lambda b, h: (b, 0, h))
    per_head_row = lambda n: pl.BlockSpec((n, HG_HEAD_DIM), lambda b, h: (0, h))
    whole = lambda a: pl.BlockSpec(a.shape, lambda b, h: (0,) * a.ndim)
    return pl.pallas_call(
        functools.partial(_hgrn_kernel, layer=layer, seq=seq),
        grid=(batch, HG_HEADS),
        in_specs=[head, head, head, head, head,
                  per_head_row(n_layers), per_head_row(n_layers), per_head_row(1),
                  whole(sums), whole(masks)],
        out_specs=head,
        out_shape=jax.ShapeDtypeStruct((batch, seq, width), BF16),
        scratch_shapes=[pltpu.VMEM((seq, HG_HEAD_DIM), F32), pltpu.VMEM((seq, HG_HEAD_DIM), F32),
                        pltpu.VMEM((HG_HEAD_DIM, HG_HEAD_DIM), F32),
                        pltpu.VMEM((HG_HEAD_DIM, HG_HEAD_DIM), F32),
                        pltpu.VMEM((2, 2 * HG_CHUNK, HG_HEAD_DIM), F32),
                        pltpu.VMEM((2, 2 * HG_CHUNK, HG_HEAD_DIM), F32),
                        pltpu.VMEM((2, n_dec_rows, 2 * HG_HEAD_DIM), F32),
                        pltpu.VMEM((2, n_dec_rows, 2 * HG_HEAD_DIM), F32)],
        compiler_params=pltpu.CompilerParams(
            dimension_semantics=("arbitrary", "arbitrary"), vmem_limit_bytes=VMEM_LIMIT_BYTES),
        name="hgrn2_scan",
    )(q, zf, zb, i, g, lb_fwd_logits, lb_bwd_logits, gain_row, sums, masks)


def _out_proj_kernel(oa_ref, oh_ref, x_ref, w_ref, b_ref, gain_ref, bias_ref, o_ref, w_scr, *, alpha):
    d_att = oa_ref.shape[1]
    _cast_weight_once(w_ref, w_scr)
    for r0 in range(0, o_ref.shape[0], OUT_PROJ_ROWS):
        rows = slice(r0, r0 + OUT_PROJ_ROWS)
        y = _dot(oa_ref[rows, :], w_scr[:d_att, :]) + _dot(oh_ref[rows, :], w_scr[d_att:, :]) + b_ref[...]
        r = alpha * x_ref[rows, :] + y
        mu = jnp.mean(r, axis=-1, keepdims=True)
        c = r - mu
        var = jnp.mean(jnp.square(c), axis=-1, keepdims=True)
        o_ref[rows, :] = (c * lax.rsqrt(var + LN_EPS) * gain_ref[...] + bias_ref[...]).astype(o_ref.dtype)


def _out_proj(o_a, o_h, x2d, w, b_row, gain_row, bias_row, alpha):
    n_tok, d_model = x2d.shape
    d_att, d_hg = o_a.shape[1], o_h.shape[1]
    row = pl.BlockSpec((1, d_model), lambda i: (0, 0))
    return pl.pallas_call(
        functools.partial(_out_proj_kernel, alpha=alpha),
        grid=(n_tok // OUT_PROJ_TM,),
        in_specs=[pl.BlockSpec((OUT_PROJ_TM, d_att), lambda i: (i, 0)),
                  pl.BlockSpec((OUT_PROJ_TM, d_hg), lambda i: (i, 0)),
                  pl.BlockSpec((OUT_PROJ_TM, d_model), lambda i: (i, 0)),
                  pl.BlockSpec((d_att + d_hg, d_model), lambda i: (0, 0), pipeline_mode=pl.Buffered(1)),
                  row, row, row],
        out_specs=pl.BlockSpec((OUT_PROJ_TM, d_model), lambda i: (i, 0)),
        out_shape=jax.ShapeDtypeStruct((n_tok, d_model), x2d.dtype),
        scratch_shapes=[pltpu.VMEM((d_att + d_hg, d_model), BF16)],
        compiler_params=pltpu.CompilerParams(
            dimension_semantics=("arbitrary",), vmem_limit_bytes=VMEM_LIMIT_BYTES),
        name="out_proj_layernorm",
    )(o_a, o_h, x2d, w, b_row, gain_row, bias_row)


def _layer(x, layer, depth, w_in, b_in, rpb, lb_fwd_logits, lb_bwd_logits, hg_norm_gain,
           w_out, b_out, ln_gain, ln_bias):
    batch, seq, d_model = x.shape
    x2d = x.reshape(batch * seq, d_model)
    slabs = _in_proj(x2d, w_in, b_in.reshape(1, -1))
    q_a, k_a, v_a, g_a, q_h, z_f, z_b, i_h, g_h = [s.reshape(batch, seq, SLAB) for s in slabs]
    o_a = _attention(q_a, k_a, v_a, g_a, rpb)
    o_h = _hgrn(q_h, z_f, z_b, i_h, g_h, lb_fwd_logits, lb_bwd_logits,
                hg_norm_gain.reshape(1, -1), layer)
    alpha = (2.0 * depth) ** 0.25
    out = _out_proj(o_a.reshape(batch * seq, SLAB), o_h.reshape(batch * seq, SLAB), x2d,
                    w_out, b_out.reshape(1, -1), ln_gain.reshape(1, -1),
                    ln_bias.reshape(1, -1), alpha)
    return out.reshape(batch, seq, d_model)


def kernel(x, w_in, b_in, rpb, lb_fwd_logits, lb_bwd_logits, hg_norm_gain, w_out, b_out, ln_gain, ln_bias):
    depth = w_in.shape[0]
    for layer in range(depth):
        x = _layer(x, layer, depth, w_in[layer], b_in[layer], rpb[layer], lb_fwd_logits,
                   lb_bwd_logits, hg_norm_gain[layer], w_out[layer], b_out[layer],
                   ln_gain[layer], ln_bias[layer])
    return x
```

```python
import functools

import numpy as np
import jax
import jax.numpy as jnp
from jax import lax
from jax.experimental import pallas as pl
from jax.experimental.pallas import tpu as pltpu

GRID_W = 64
ATT_HEADS = 8
ATT_HEAD_DIM = 64
HG_HEADS = 4
HG_HEAD_DIM = 128
WIN_ROWS = 8
WIN_COLS = 16
LN_EPS = 1e-5
RMS_EPS = 1e-6
N_SLABS = 9
SLAB = 512

LANES = 128
F32_SUBLANES = 8
VMEM_LIMIT_BYTES = 56 * 1024 * 1024
PROJ_TM = 512
OUT_PROJ_TM = 2048
OUT_PROJ_ROWS = 256
ATT_ROWS_PER_STEP = 32
ATT_UNROLL = 4
ATT_GROUP = 4
ATT_BIAS_UNROLL = 8
HG_CHUNK = 64
HG_GROUP = F32_SUBLANES
HG_NORM_ROWS = 256
HG_SCAN_UNROLL = 16
MASK_VALUE = -1e30
LOG2_E = 1.4426950408889634
SLAB_SCALES = {1: ATT_HEAD_DIM ** -0.5 * LOG2_E, 3: 0.5, 8: 0.5}

BF16 = jnp.bfloat16
F32 = jnp.float32


def _silu_from_half(half_g):
    return half_g * (1.0 + jnp.tanh(half_g))


def _dot(a, b):
    return jnp.dot(a, b, preferred_element_type=F32)


def _dot_nt(a, b):
    return lax.dot_general(a, b, (((1,), (1,)), ((), ())), preferred_element_type=F32)


def _cast_weight_once(w_ref, w_scr):
    @pl.when(pl.program_id(0) == 0)
    def _():
        for c0 in range(0, w_ref.shape[1], SLAB):
            w_scr[:, c0:c0 + SLAB] = w_ref[:, c0:c0 + SLAB].astype(w_scr.dtype)


def _in_proj_kernel(x_ref, w_ref, b_ref, *refs):
    out_refs, w_scr = refs[:-1], refs[-1]
    _cast_weight_once(w_ref, w_scr)
    xb = x_ref[...].astype(BF16)
    for j, o_ref in enumerate(out_refs):
        cols = slice(j * SLAB, (j + 1) * SLAB)
        h = _dot(xb, w_scr[:, cols]) + b_ref[:, cols]
        if j in SLAB_SCALES:
            h = h * SLAB_SCALES[j]
        o_ref[...] = h.astype(o_ref.dtype)


def _in_proj(x2d, w, b_row):
    n_tok, d_model = x2d.shape
    d_in = w.shape[1]
    assert d_in == N_SLABS * SLAB and n_tok % PROJ_TM == 0
    return pl.pallas_call(
        _in_proj_kernel,
        grid=(n_tok // PROJ_TM,),
        in_specs=[
            pl.BlockSpec((PROJ_TM, d_model), lambda i: (i, 0)),
            pl.BlockSpec((d_model, d_in), lambda i: (0, 0), pipeline_mode=pl.Buffered(1)),
            pl.BlockSpec((1, d_in), lambda i: (0, 0)),
        ],
        out_specs=[pl.BlockSpec((PROJ_TM, SLAB), lambda i: (i, 0))] * N_SLABS,
        out_shape=[jax.ShapeDtypeStruct((n_tok, SLAB), BF16)] * N_SLABS,
        scratch_shapes=[pltpu.VMEM((d_model, d_in), BF16)],
        compiler_params=pltpu.CompilerParams(
            dimension_semantics=("arbitrary",), vmem_limit_bytes=VMEM_LIMIT_BYTES),
        name="in_proj",
    )(x2d, w, b_row)


ATT_REL_ROWS = 2 * WIN_ROWS - 1
ATT_BIAS_TILES = (ATT_REL_ROWS - 1) // 2
ATT_GROUPS = ATT_HEADS // ATT_GROUP


def _attention_bias_rows(rpb):
    n_rel_cols = 2 * WIN_COLS - 1
    scaled = rpb.astype(F32) * LOG2_E
    gap = jnp.zeros((ATT_HEADS, ATT_BIAS_TILES, (LANES - 2 * n_rel_cols) // 2), F32)
    per_parity = []
    for p in range(2):
        first = scaled[:, p:p + 2 * ATT_BIAS_TILES - 1:2]
        second = scaled[:, p + 1:p + 2 * ATT_BIAS_TILES:2]
        per_parity.append(jnp.concatenate(
            [first[..., WIN_COLS - 1:], gap, second, gap, first[..., :WIN_COLS - 1]], axis=-1))
    gen = jnp.stack(per_parity)
    gen = gen.reshape(2, ATT_GROUPS, ATT_GROUP, ATT_BIAS_TILES, LANES).transpose(0, 1, 3, 2, 4)
    return gen.reshape(2 * ATT_GROUPS * ATT_BIAS_TILES * ATT_GROUP, 1, LANES)


def _attention_window_mask():
    qc = np.arange(GRID_W)[:, None]
    kc = np.arange(LANES)[None, :] % GRID_W
    col_start = np.clip(qc - WIN_COLS // 2, 0, GRID_W - WIN_COLS)
    return ((kc >= col_start) & (kc < col_start + WIN_COLS)).astype(np.float32)


def _attention_kernel(q_ref, k_ref, v_ref, g_ref, gen_ref, win_ref, o_ref, bias_scr, *, rows, kr):
    step = pl.program_id(1)
    gw = ATT_GROUP * ATT_HEAD_DIM
    row_head = lax.broadcasted_iota(jnp.int32, (ATT_GROUP * GRID_W, gw), 0) // GRID_W
    lane_head = lax.broadcasted_iota(jnp.int32, (ATT_GROUP * GRID_W, gw), 1) // ATT_HEAD_DIM
    own_head = row_head == lane_head
    out_lane_head = lax.broadcasted_iota(jnp.int32, (GRID_W, gw), 1) // ATT_HEAD_DIM

    @pl.when((pl.program_id(0) == 0) & (step == 0))
    def _build_bias():
        in_window = win_ref[...] > 0.0

        def one_block(n, carry):
            gen = jnp.broadcast_to(gen_ref[n], (GRID_W, LANES))
            toeplitz = pltpu.roll(gen, 0, 1, stride=1, stride_axis=0)
            head_rows = pl.ds(pl.multiple_of((n % ATT_GROUP) * GRID_W, GRID_W), GRID_W)
            bias_scr[n // ATT_GROUP, head_rows, :] = jnp.where(in_window, toeplitz, MASK_VALUE)
            return carry

        lax.fori_loop(0, gen_ref.shape[0], one_block, 0, unroll=ATT_BIAS_UNROLL)

    def one_row(j, carry):
        r = step * ATT_ROWS_PER_STEP + j
        row_start = jnp.clip(r - kr // 2, 0, rows - kr)
        variant = row_start - r + (WIN_ROWS - 1)
        q_tok = pl.multiple_of(j * GRID_W, GRID_W)
        k_tok = pl.multiple_of(row_start * GRID_W, GRID_W)
        for grp in range(ATT_GROUPS):
            lanes = slice(grp * gw, (grp + 1) * gw)
            q = q_ref[0, pl.ds(q_tok, GRID_W), lanes]
            q_bd = jnp.where(own_head, jnp.concatenate([q] * ATT_GROUP, axis=0), 0)
            keys = k_ref[0, pl.ds(k_tok, kr * GRID_W), lanes]
            vals = v_ref[0, pl.ds(k_tok, kr * GRID_W), lanes]
            tile0 = ((variant % 2) * ATT_GROUPS + grp) * ATT_BIAS_TILES + variant // 2
            bias = jnp.concatenate([bias_scr[tile0 + t] for t in range(kr // 2)], axis=1)
            s = _dot_nt(q_bd, keys) + bias
            m = jnp.max(s, axis=-1, keepdims=True)
            p = jnp.exp2(s - m)
            denom = jnp.sum(p, axis=-1, keepdims=True)
            pv = _dot(p.astype(BF16), vals) / denom
            o = jnp.zeros((GRID_W, gw), F32)
            for h in range(ATT_GROUP):
                o = o + jnp.where(out_lane_head == h, pv[h * GRID_W:(h + 1) * GRID_W], 0.0)
            gate = g_ref[0, pl.ds(q_tok, GRID_W), lanes].astype(F32)
            o = o * _silu_from_half(gate)
            o_ref[0, pl.ds(q_tok, GRID_W), lanes] = o.astype(o_ref.dtype)
        return carry

    lax.fori_loop(0, ATT_ROWS_PER_STEP, one_row, 0, unroll=ATT_UNROLL)


def _attention(q, k, v, g, rpb):
    batch, seq, width = q.shape
    rows = seq // GRID_W
    kr = min(WIN_ROWS, rows)
    assert rows % ATT_ROWS_PER_STEP == 0 and ATT_ROWS_PER_STEP % ATT_UNROLL == 0
    assert kr == WIN_ROWS and 2 * GRID_W == LANES
    gen = _attention_bias_rows(rpb)
    win = jnp.asarray(_attention_window_mask())
    blk = ATT_ROWS_PER_STEP * GRID_W
    tile = pl.BlockSpec((1, blk, width), lambda b, i: (b, i, 0))
    whole = pl.BlockSpec((1, seq, width), lambda b, i: (b, 0, 0))
    return pl.pallas_call(
        functools.partial(_attention_kernel, rows=rows, kr=kr),
        grid=(batch, rows // ATT_ROWS_PER_STEP),
        in_specs=[tile, whole, whole, tile,
                  pl.BlockSpec(gen.shape, lambda b, i: (0, 0, 0)),
                  pl.BlockSpec(win.shape, lambda b, i: (0, 0))],
        out_specs=tile,
        out_shape=jax.ShapeDtypeStruct((batch, seq, width), BF16),
        scratch_shapes=[pltpu.VMEM((2 * ATT_GROUPS * ATT_BIAS_TILES, ATT_GROUP * GRID_W, 2 * GRID_W), F32)],
        compiler_params=pltpu.CompilerParams(
            dimension_semantics=("arbitrary", "arbitrary"), vmem_limit_bytes=VMEM_LIMIT_BYTES),
        name="nbr_attention",
    )(q, k, v, g, gen, win)


def _hgrn_fine_levels():
    sizes = []
    c = HG_GROUP
    while c >= 2:
        sizes.append(c)
        c //= 2
    return sizes


def _hgrn_group_pairs(direction, chunk):
    n = chunk // HG_GROUP
    if direction == 0:
        return [(t, g) for t in range(1, n) for g in range(t)]
    return [(t, g) for t in range(n - 2, -1, -1) for g in range(t + 1, n)]


def _hgrn_constants(chunk):
    t = np.arange(chunk)[:, None]
    u = np.arange(chunk)[None, :]
    blocks = [(u <= t)]
    masks = []
    for c in _hgrn_fine_levels():
        half = c // 2
        mid = (t // c) * c + half
        upper = (t % c) >= half
        if 2 < c < HG_GROUP:
            blocks.append(np.where(upper, (u >= mid) & (u <= t), (u > t) & (u <= mid - 1)))
        masks.append((t // c == u // c) & upper & ((u % c) < half))
    fwd_sums = np.stack(blocks).astype(np.float32)
    fwd_masks = np.stack(masks).astype(np.float32)
    twice = lambda m: np.concatenate([m.reshape(-1, chunk)] * 2, axis=1)
    sums = np.stack([twice(fwd_sums), twice(fwd_sums[:, ::-1, ::-1])])
    masks = np.concatenate([fwd_masks, fwd_masks[:, ::-1, ::-1]], axis=2)
    return sums, masks


def _block_diag(a, b):
    zero = jnp.zeros_like(a)
    return jnp.concatenate([jnp.concatenate([a, zero], axis=1),
                            jnp.concatenate([zero, b], axis=1)], axis=0)


def _hgrn_dec_rows(chunk):
    n_var = len(_hgrn_group_pairs(0, chunk)) * HG_GROUP
    n_mm = sum(c > 2 for c in _hgrn_fine_levels())
    edge, var, fine = 2 * chunk, 3 * chunk, 3 * chunk + n_var
    return edge, var, fine, fine + n_mm * chunk


def _hgrn_kernel(q_ref, zf_ref, zb_ref, i_ref, g_ref, lbf_ref, lbb_ref, gain_ref,
                 sums_ref, masks_ref, o_ref, of_scr, ob_scr, sf_scr, sb_scr,
                 k_scr0, k_scr1, dec_scr0, dec_scr1, *, layer, seq):
    chunk = HG_CHUNK
    pair = 2 * chunk
    n_pairs = seq // pair
    dh = HG_HEAD_DIM
    grp = HG_GROUP
    n_groups = chunk // grp
    fine = _hgrn_fine_levels()
    group_pairs = [_hgrn_group_pairs(direction, chunk) for direction in (0, 1)]
    r_edge, r_var, r_fine, _ = _hgrn_dec_rows(chunk)
    k_slots = (k_scr0, k_scr1)
    dec_slots = (dec_scr0, dec_scr1)
    row_in_group = lax.broadcasted_iota(jnp.int32, (grp, dh), 0)
    lane_group = lax.broadcasted_iota(jnp.int32, (grp, 2 * chunk), 1) // grp

    def lower_bound(logit_ref):
        logits = logit_ref[...].astype(F32)
        e = jnp.exp(logits - jnp.max(logits, axis=0, keepdims=True))
        return jnp.sum(e[:layer + 1], axis=0, keepdims=True) / jnp.sum(e, axis=0, keepdims=True)

    def gates(z, lb):
        f = lb + (1.0 - lb) * jax.nn.sigmoid(z.astype(F32))
        return jnp.log2(f), 1.0 - f

    def store_decays(slot, direction, g2):
        hi = g2.astype(BF16)
        lo = (g2 - hi.astype(F32)).astype(BF16)
        stacked = [jnp.concatenate([hi[c * chunk:(c + 1) * chunk], lo[c * chunk:(c + 1) * chunk]], axis=0)
                   for c in range(2)]
        raw = _dot(sums_ref[direction], jnp.concatenate(stacked, axis=1))
        dec = dec_slots[slot].at[direction]
        cum = raw[0:chunk]
        whole = cum[chunk - 1:chunk] if direction == 0 else cum[0:1]
        dec[0:chunk, :] = jnp.exp2(cum)
        dec[chunk:2 * chunk, :] = jnp.exp2(whole - cum)
        by_group = cum.reshape(n_groups, grp, 2 * dh)
        mid_row = grp // 2 - 1 if direction == 0 else grp // 2
        to_mid = -jnp.abs(by_group - by_group[:, mid_row:mid_row + 1, :])
        dec[r_fine:r_fine + chunk, :] = jnp.exp2(to_mid.reshape(chunk, 2 * dh))
        dec[r_fine + chunk:, :] = jnp.exp2(raw[chunk:])
        edge = grp - 1 if direction == 0 else 0
        edges = [cum[g * grp + edge:g * grp + edge + 1] for g in range(n_groups)]
        group = lambda g: cum[g * grp:(g + 1) * grp]
        dec[r_edge:r_var, :] = jnp.exp2(jnp.concatenate([edges[g] - group(g) for g in range(n_groups)], axis=0))
        dec[r_var:r_fine, :] = jnp.exp2(jnp.concatenate([group(t) - edges[g]
                                                         for t, g in group_pairs[direction]], axis=0))

    def fine_operand(direction, lvl, q, k, dec):
        c = fine[lvl]
        pos = row_in_group % c
        is_upper = (pos >= c // 2) if direction == 0 else (pos < c // 2)
        pick = lambda x, y: jnp.where(is_upper[None], x.reshape(n_groups, grp, dh),
                                      y.reshape(n_groups, grp, dh)).reshape(chunk, dh)
        if c == 2:
            return pick(q * (1.0 - k), k).astype(BF16)
        return (pick(q, k) * dec(r_fine + lvl * chunk, r_fine + (lvl + 1) * chunk)).astype(BF16)

    def chunk_part(slot, direction, c, q, v, base):
        rows = slice(c * chunk, (c + 1) * chunk)
        lanes = slice(c * dh, (c + 1) * dh)
        return dict(q=q[rows], k=k_slots[slot][direction, rows, :], v=v[rows],
                    dec=lambda r0, r1: dec_slots[slot][direction, r0:r1, lanes],
                    rows=pl.ds(pl.multiple_of(base + c * chunk, chunk), chunk))

    def pair_step(f, b):
        lhs = jnp.concatenate(
            [(jnp.concatenate([p["q"][t * grp:(t + 1) * grp] for t, _ in group_pairs[d]], axis=0)
              * p["dec"](r_var, r_fine)).astype(BF16) for d, p in enumerate((f, b))], axis=1)
        keys = [(p["k"] * p["dec"](r_edge, r_var)).astype(BF16) for p in (f, b)]
        r = _dot_nt(lhs, _block_diag(*keys))
        rows_of = [jnp.zeros((grp, 2 * chunk), F32)] * n_groups
        for i, ((tf, gf), (tb, gb)) in enumerate(zip(*group_pairs)):
            block = r[i * grp:(i + 1) * grp]
            rows_of[tf] = jnp.where(lane_group == gf, block, rows_of[tf])
            rows_of[tb] = jnp.where(lane_group == n_groups + gb, block, rows_of[tb])
        a = jnp.concatenate(rows_of, axis=0)
        for lvl in range(len(fine)):
            xf = fine_operand(0, lvl, f["q"], f["k"], f["dec"])
            xb = fine_operand(1, lvl, b["q"], b["k"], b["dec"])
            a = a + masks_ref[lvl] * _dot_nt(jnp.concatenate([xf, xb], axis=1), _block_diag(xf, xb))
        o = _dot(a.astype(BF16), _block_diag(f["v"], b["v"]))
        sf = sf_scr[...]
        sb = sb_scr[...]
        q_dec = jnp.concatenate([(f["q"] * f["dec"](0, chunk)).astype(BF16),
                                 (b["q"] * b["dec"](0, chunk)).astype(BF16)], axis=1)
        o = o + _dot_nt(q_dec, _block_diag(sf.astype(BF16), sb.astype(BF16)))
        diag_f = jnp.sum(f["q"] * f["k"], axis=-1, keepdims=True)
        diag_b = jnp.sum(b["q"] * b["k"], axis=-1, keepdims=True)
        of_scr[f["rows"], :] = o[:, :dh] + diag_f * f["v"].astype(F32)
        ob_scr[b["rows"], :] = o[:, dh:] + diag_b * b["v"].astype(F32)
        k_dec = _block_diag((f["k"] * f["dec"](chunk, 2 * chunk)).astype(BF16),
                            (b["k"] * b["dec"](chunk, 2 * chunk)).astype(BF16))
        v_cat = jnp.concatenate([f["v"], b["v"]], axis=0)
        u = lax.dot_general(v_cat, k_dec, (((0,), (0,)), ((), ())),
                            preferred_element_type=F32)
        sf_scr[...] = sf * f["dec"](chunk - 1, chunk) + u[:, :dh]
        sb_scr[...] = sb * b["dec"](0, 1) + u[:, dh:]

    lb_f = lower_bound(lbf_ref)
    lb_b = lower_bound(lbb_ref)
    sf_scr[...] = jnp.zeros_like(sf_scr)
    sb_scr[...] = jnp.zeros_like(sb_scr)

    def bases(n):
        return (pl.multiple_of(n * pair, pair), pl.multiple_of((n_pairs - 1 - n) * pair, pair))

    def prepare(n, slot):
        for direction, base, z_ref, lb in zip((0, 1), bases(n), (zf_ref, zb_ref), (lb_f, lb_b)):
            g2, k = gates(z_ref[0, pl.ds(base, pair), :], lb)
            k_slots[slot][direction] = k
            store_decays(slot, direction, g2)

    def consume(n, slot):
        parts = []
        for direction, base in zip((0, 1), bases(n)):
            rows = pl.ds(base, pair)
            q = q_ref[0, rows, :].astype(F32)
            v = i_ref[0, rows, :]
            parts.append([chunk_part(slot, direction, c, q, v, base) for c in range(2)])
        pair_step(parts[0][0], parts[1][1])
        pair_step(parts[0][1], parts[1][0])

    def scan_body(m, carry):
        prepare(2 * m + 1, 1)
        consume(2 * m, 0)
        prepare(jnp.minimum(2 * m + 2, n_pairs - 1), 0)
        consume(2 * m + 1, 1)
        return carry

    prepare(0, 0)
    lax.fori_loop(0, n_pairs // 2, scan_body, 0, unroll=HG_SCAN_UNROLL)

    gain = gain_ref[...].astype(F32)

    def norm_body(n, carry):
        rows = pl.ds(pl.multiple_of(n * HG_NORM_ROWS, HG_NORM_ROWS), HG_NORM_ROWS)
        o = of_scr[rows, :] + ob_scr[rows, :]
        o = o * lax.rsqrt(jnp.mean(jnp.square(o), axis=-1, keepdims=True) + RMS_EPS) * gain
        gate = g_ref[0, rows, :].astype(F32)
        o_ref[0, rows, :] = (o * _silu_from_half(gate)).astype(o_ref.dtype)
        return carry

    lax.fori_loop(0, seq // HG_NORM_ROWS, norm_body, 0, unroll=4)


def _hgrn(q, zf, zb, i, g, lb_fwd_logits, lb_bwd_logits, gain_row, layer):
    batch, seq, width = q.shape
    assert width == HG_HEADS * HG_HEAD_DIM and seq % (4 * HG_NORM_ROWS) == 0
    assert seq % (4 * HG_CHUNK * HG_SCAN_UNROLL) == 0
    sums, masks = _hgrn_constants(HG_CHUNK)
    sums = jnp.asarray(sums, BF16)
    masks = jnp.asarray(masks, F32)
    n_layers = lb_fwd_logits.shape[0]
    n_dec_rows = _hgrn_dec_rows(HG_CHUNK)[-1]
    head = pl.BlockSpec((1, seq, HG_HEAD_DIM), lambda b, h: (b, 0, h))
    per_head_row = lambda n: pl.BlockSpec((n, HG_HEAD_DIM), lambda b, h: (0, h))
    whole = lambda a: pl.BlockSpec(a.shape, lambda b, h: (0,) * a.ndim)
    return pl.pallas_call(
        functools.partial(_hgrn_kernel, layer=layer, seq=seq),
        grid=(batch, HG_HEADS),
        in_specs=[head, head, head, head, head,
                  per_head_row(n_layers), per_head_row(n_layers), per_head_row(1),
                  whole(sums), whole(masks)],
        out_specs=head,
        out_shape=jax.ShapeDtypeStruct((batch, seq, width), BF16),
        scratch_shapes=[pltpu.VMEM((seq, HG_HEAD_DIM), F32), pltpu.VMEM((seq, HG_HEAD_DIM), F32),
                        pltpu.VMEM((HG_HEAD_DIM, HG_HEAD_DIM), F32),
                        pltpu.VMEM((HG_HEAD_DIM, HG_HEAD_DIM), F32),
                        pltpu.VMEM((2, 2 * HG_CHUNK, HG_HEAD_DIM), F32),
                        pltpu.VMEM((2, 2 * HG_CHUNK, HG_HEAD_DIM), F32),
                        pltpu.VMEM((2, n_dec_rows, 2 * HG_HEAD_DIM), F32),
                        pltpu.VMEM((2, n_dec_rows, 2 * HG_HEAD_DIM), F32)],
        compiler_params=pltpu.CompilerParams(
            dimension_semantics=("arbitrary", "arbitrary"), vmem_limit_bytes=VMEM_LIMIT_BYTES),
        name="hgrn2_scan",
    )(q, zf, zb, i, g, lb_fwd_logits, lb_bwd_logits, gain_row, sums, masks)


def _out_proj_kernel(oa_ref, oh_ref, x_ref, w_ref, b_ref, gain_ref, bias_ref, o_ref, w_scr, *, alpha):
    d_att = oa_ref.shape[1]
    _cast_weight_once(w_ref, w_scr)
    for r0 in range(0, o_ref.shape[0], OUT_PROJ_ROWS):
        rows = slice(r0, r0 + OUT_PROJ_ROWS)
        y = _dot(oa_ref[rows, :], w_scr[:d_att, :]) + _dot(oh_ref[rows, :], w_scr[d_att:, :]) + b_ref[...]
        r = alpha * x_ref[rows, :] + y
        mu = jnp.mean(r, axis=-1, keepdims=True)
        c = r - mu
        var = jnp.mean(jnp.square(c), axis=-1, keepdims=True)
        o_ref[rows, :] = (c * lax.rsqrt(var + LN_EPS) * gain_ref[...] + bias_ref[...]).astype(o_ref.dtype)


def _out_proj(o_a, o_h, x2d, w, b_row, gain_row, bias_row, alpha):
    n_tok, d_model = x2d.shape
    d_att, d_hg = o_a.shape[1], o_h.shape[1]
    row = pl.BlockSpec((1, d_model), lambda i: (0, 0))
    return pl.pallas_call(
        functools.partial(_out_proj_kernel, alpha=alpha),
        grid=(n_tok // OUT_PROJ_TM,),
        in_specs=[pl.BlockSpec((OUT_PROJ_TM, d_att), lambda i: (i, 0)),
                  pl.BlockSpec((OUT_PROJ_TM, d_hg), lambda i: (i, 0)),
                  pl.BlockSpec((OUT_PROJ_TM, d_model), lambda i: (i, 0)),
                  pl.BlockSpec((d_att + d_hg, d_model), lambda i: (0, 0), pipeline_mode=pl.Buffered(1)),
                  row, row, row],
        out_specs=pl.BlockSpec((OUT_PROJ_TM, d_model), lambda i: (i, 0)),
        out_shape=jax.ShapeDtypeStruct((n_tok, d_model), x2d.dtype),
        scratch_shapes=[pltpu.VMEM((d_att + d_hg, d_model), BF16)],
        compiler_params=pltpu.CompilerParams(
            dimension_semantics=("arbitrary",), vmem_limit_bytes=VMEM_LIMIT_BYTES),
        name="out_proj_layernorm",
    )(o_a, o_h, x2d, w, b_row, gain_row, bias_row)


def _layer(x, layer, depth, w_in, b_in, rpb, lb_fwd_logits, lb_bwd_logits, hg_norm_gain,
           w_out, b_out, ln_gain, ln_bias):
    batch, seq, d_model = x.shape
    x2d = x.reshape(batch * seq, d_model)
    slabs = _in_proj(x2d, w_in, b_in.reshape(1, -1))
    q_a, k_a, v_a, g_a, q_h, z_f, z_b, i_h, g_h = [s.reshape(batch, seq, SLAB) for s in slabs]
    o_a = _attention(q_a, k_a, v_a, g_a, rpb)
    o_h = _hgrn(q_h, z_f, z_b, i_h, g_h, lb_fwd_logits, lb_bwd_logits,
                hg_norm_gain.reshape(1, -1), layer)
    alpha = (2.0 * depth) ** 0.25
    out = _out_proj(o_a.reshape(batch * seq, SLAB), o_h.reshape(batch * seq, SLAB), x2d,
                    w_out, b_out.reshape(1, -1), ln_gain.reshape(1, -1),
                    ln_bias.reshape(1, -1), alpha)
    return out.reshape(batch, seq, d_model)


def kernel(x, w_in, b_in, rpb, lb_fwd_logits, lb_bwd_logits, hg_norm_gain, w_out, b_out, ln_gain, ln_bias):
    depth = w_in.shape[0]
    for layer in range(depth):
        x = _layer(x, layer, depth, w_in[layer], b_in[layer], rpb[layer], lb_fwd_logits,
                   lb_bwd_logits, hg_norm_gain[layer], w_out[layer], b_out[layer],
                   ln_gain[layer], ln_bias[layer])
    return x
```

```python
import functools

import numpy as np
import jax
import jax.numpy as jnp
from jax import lax
from jax.experimental import pallas as pl
from jax.experimental.pallas import tpu as pltpu

GRID_W = 64
ATT_HEADS = 8
ATT_HEAD_DIM = 64
HG_HEADS = 4
HG_HEAD_DIM = 128
WIN_ROWS = 8
WIN_COLS = 16
LN_EPS = 1e-5
RMS_EPS = 1e-6
N_SLABS = 9
SLAB = 512

LANES = 128
F32_SUBLANES = 8
VMEM_LIMIT_BYTES = 56 * 1024 * 1024
PROJ_TM = 512
OUT_PROJ_TM = 2048
OUT_PROJ_ROWS = 256
ATT_ROWS_PER_STEP = 32
ATT_UNROLL = 4
ATT_GROUP = 4
ATT_BIAS_UNROLL = 8
HG_CHUNK = 64
HG_GROUP = F32_SUBLANES
HG_NORM_ROWS = 256
HG_SCAN_UNROLL = 16
MASK_VALUE = -1e30
LOG2_E = 1.4426950408889634
SLAB_SCALES = {1: ATT_HEAD_DIM ** -0.5 * LOG2_E, 3: 0.5, 8: 0.5}

BF16 = jnp.bfloat16
F32 = jnp.float32


def _silu_from_half(half_g):
    return half_g * (1.0 + jnp.tanh(half_g))


def _dot(a, b):
    return jnp.dot(a, b, preferred_element_type=F32)


def _dot_nt(a, b):
    return lax.dot_general(a, b, (((1,), (1,)), ((), ())), preferred_element_type=F32)


def _cast_weight_once(w_ref, w_scr):
    @pl.when(pl.program_id(0) == 0)
    def _():
        for c0 in range(0, w_ref.shape[1], SLAB):
            w_scr[:, c0:c0 + SLAB] = w_ref[:, c0:c0 + SLAB].astype(w_scr.dtype)


def _in_proj_kernel(x_ref, w_ref, b_ref, *refs):
    out_refs, w_scr = refs[:-1], refs[-1]
    _cast_weight_once(w_ref, w_scr)
    xb = x_ref[...].astype(BF16)
    for j, o_ref in enumerate(out_refs):
        cols = slice(j * SLAB, (j + 1) * SLAB)
        h = _dot(xb, w_scr[:, cols]) + b_ref[:, cols]
        if j in SLAB_SCALES:
            h = h * SLAB_SCALES[j]
        o_ref[...] = h.astype(o_ref.dtype)


def _in_proj(x2d, w, b_row):
    n_tok, d_model = x2d.shape
    d_in = w.shape[1]
    assert d_in == N_SLABS * SLAB and n_tok % PROJ_TM == 0
    return pl.pallas_call(
        _in_proj_kernel,
        grid=(n_tok // PROJ_TM,),
        in_specs=[
            pl.BlockSpec((PROJ_TM, d_model), lambda i: (i, 0)),
            pl.BlockSpec((d_model, d_in), lambda i: (0, 0), pipeline_mode=pl.Buffered(1)),
            pl.BlockSpec((1, d_in), lambda i: (0, 0)),
        ],
        out_specs=[pl.BlockSpec((PROJ_TM, SLAB), lambda i: (i, 0))] * N_SLABS,
        out_shape=[jax.ShapeDtypeStruct((n_tok, SLAB), BF16)] * N_SLABS,
        scratch_shapes=[pltpu.VMEM((d_model, d_in), BF16)],
        compiler_params=pltpu.CompilerParams(
            dimension_semantics=("arbitrary",), vmem_limit_bytes=VMEM_LIMIT_BYTES),
        name="in_proj",
    )(x2d, w, b_row)


ATT_REL_ROWS = 2 * WIN_ROWS - 1
ATT_BIAS_TILES = (ATT_REL_ROWS - 1) // 2
ATT_GROUPS = ATT_HEADS // ATT_GROUP


def _attention_bias_rows(rpb):
    n_rel_cols = 2 * WIN_COLS - 1
    scaled = rpb.astype(F32) * LOG2_E
    gap = jnp.zeros((ATT_HEADS, ATT_BIAS_TILES, (LANES - 2 * n_rel_cols) // 2), F32)
    per_parity = []
    for p in range(2):
        first = scaled[:, p:p + 2 * ATT_BIAS_TILES - 1:2]
        second = scaled[:, p + 1:p + 2 * ATT_BIAS_TILES:2]
        per_parity.append(jnp.concatenate(
            [first[..., WIN_COLS - 1:], gap, second, gap, first[..., :WIN_COLS - 1]], axis=-1))
    gen = jnp.stack(per_parity)
    gen = gen.reshape(2, ATT_GROUPS, ATT_GROUP, ATT_BIAS_TILES, LANES).transpose(0, 1, 3, 2, 4)
    return gen.reshape(2 * ATT_GROUPS * ATT_BIAS_TILES * ATT_GROUP, 1, LANES)


def _attention_window_mask():
    qc = np.arange(GRID_W)[:, None]
    kc = np.arange(LANES)[None, :] % GRID_W
    col_start = np.clip(qc - WIN_COLS // 2, 0, GRID_W - WIN_COLS)
    return ((kc >= col_start) & (kc < col_start + WIN_COLS)).astype(np.float32)


def _attention_first_key_row(step, rows, kr):
    return jnp.clip(step * ATT_ROWS_PER_STEP - kr // 2, 0, rows - ATT_ROWS_PER_STEP - kr)


def _attention_kernel(q_ref, k_ref, v_ref, g_ref, gen_ref, win_ref, o_ref, bias_scr, *, rows, kr):
    step = pl.program_id(1)
    gw = ATT_GROUP * ATT_HEAD_DIM
    row_head = lax.broadcasted_iota(jnp.int32, (ATT_GROUP * GRID_W, gw), 0) // GRID_W
    lane_head = lax.broadcasted_iota(jnp.int32, (ATT_GROUP * GRID_W, gw), 1) // ATT_HEAD_DIM
    own_head = row_head == lane_head
    out_lane_head = lax.broadcasted_iota(jnp.int32, (GRID_W, gw), 1) // ATT_HEAD_DIM

    @pl.when((pl.program_id(0) == 0) & (step == 0))
    def _build_bias():
        in_window = win_ref[...] > 0.0

        def one_block(n, carry):
            gen = jnp.broadcast_to(gen_ref[n], (GRID_W, LANES))
            toeplitz = pltpu.roll(gen, 0, 1, stride=1, stride_axis=0)
            head_rows = pl.ds(pl.multiple_of((n % ATT_GROUP) * GRID_W, GRID_W), GRID_W)
            bias_scr[n // ATT_GROUP, head_rows, :] = jnp.where(in_window, toeplitz, MASK_VALUE)
            return carry

        lax.fori_loop(0, gen_ref.shape[0], one_block, 0, unroll=ATT_BIAS_UNROLL)

    def one_row(j, carry):
        r = step * ATT_ROWS_PER_STEP + j
        row_start = jnp.clip(r - kr // 2, 0, rows - kr)
        variant = row_start - r + (WIN_ROWS - 1)
        q_tok = pl.multiple_of(j * GRID_W, GRID_W)
        k_tok = pl.multiple_of((row_start - _attention_first_key_row(step, rows, kr)) * GRID_W, GRID_W)
        for grp in range(ATT_GROUPS):
            lanes = slice(grp * gw, (grp + 1) * gw)
            q = q_ref[0, pl.ds(q_tok, GRID_W), lanes]
            q_bd = jnp.where(own_head, jnp.concatenate([q] * ATT_GROUP, axis=0), 0)
            keys = k_ref[0, pl.ds(k_tok, kr * GRID_W), lanes]
            vals = v_ref[0, pl.ds(k_tok, kr * GRID_W), lanes]
            tile0 = ((variant % 2) * ATT_GROUPS + grp) * ATT_BIAS_TILES + variant // 2
            bias = jnp.concatenate([bias_scr[tile0 + t] for t in range(kr // 2)], axis=1)
            s = _dot_nt(q_bd, keys) + bias
            m = jnp.max(s, axis=-1, keepdims=True)
            p = jnp.exp2(s - m)
            denom = jnp.sum(p, axis=-1, keepdims=True)
            pv = _dot(p.astype(BF16), vals) / denom
            o = jnp.zeros((GRID_W, gw), F32)
            for h in range(ATT_GROUP):
                o = o + jnp.where(out_lane_head == h, pv[h * GRID_W:(h + 1) * GRID_W], 0.0)
            gate = g_ref[0, pl.ds(q_tok, GRID_W), lanes].astype(F32)
            o = o * _silu_from_half(gate)
            o_ref[0, pl.ds(q_tok, GRID_W), lanes] = o.astype(o_ref.dtype)
        return carry

    lax.fori_loop(0, ATT_ROWS_PER_STEP, one_row, 0, unroll=ATT_UNROLL)


def _attention(q, k, v, g, rpb):
    batch, seq, width = q.shape
    rows = seq // GRID_W
    kr = min(WIN_ROWS, rows)
    assert rows % ATT_ROWS_PER_STEP == 0 and ATT_ROWS_PER_STEP % ATT_UNROLL == 0
    assert kr == WIN_ROWS and 2 * GRID_W == LANES
    gen = _attention_bias_rows(rpb)
    win = jnp.asarray(_attention_window_mask())
    blk = ATT_ROWS_PER_STEP * GRID_W
    tile = pl.BlockSpec((1, blk, width), lambda b, i: (b, i, 0))
    assert rows >= ATT_ROWS_PER_STEP + kr
    halo = pl.BlockSpec((pl.Element(1), pl.Element((ATT_ROWS_PER_STEP + kr) * GRID_W), pl.Element(width)),
                        lambda b, i: (b, _attention_first_key_row(i, rows, kr) * GRID_W, 0))
    return pl.pallas_call(
        functools.partial(_attention_kernel, rows=rows, kr=kr),
        grid=(batch, rows // ATT_ROWS_PER_STEP),
        in_specs=[tile, halo, halo, tile,
                  pl.BlockSpec(gen.shape, lambda b, i: (0, 0, 0)),
                  pl.BlockSpec(win.shape, lambda b, i: (0, 0))],
        out_specs=tile,
        out_shape=jax.ShapeDtypeStruct((batch, seq, width), BF16),
        scratch_shapes=[pltpu.VMEM((2 * ATT_GROUPS * ATT_BIAS_TILES, ATT_GROUP * GRID_W, 2 * GRID_W), F32)],
        compiler_params=pltpu.CompilerParams(
            dimension_semantics=("arbitrary", "arbitrary"), vmem_limit_bytes=VMEM_LIMIT_BYTES),
        name="nbr_attention",
    )(q, k, v, g, gen, win)


def _hgrn_fine_levels():
    sizes = []
    c = HG_GROUP
    while c >= 2:
        sizes.append(c)
        c //= 2
    return sizes


def _hgrn_group_pairs(direction, chunk):
    n = chunk // HG_GROUP
    if direction == 0:
        return [(t, g) for t in range(1, n) for g in range(t)]
    return [(t, g) for t in range(n - 2, -1, -1) for g in range(t + 1, n)]


def _hgrn_constants(chunk):
    t = np.arange(chunk)[:, None]
    u = np.arange(chunk)[None, :]
    blocks = [(u <= t)]
    masks = []
    for c in _hgrn_fine_levels():
        half = c // 2
        mid = (t // c) * c + half
        upper = (t % c) >= half
        if 2 < c < HG_GROUP:
            blocks.append(np.where(upper, (u >= mid) & (u <= t), (u > t) & (u <= mid - 1)))
        masks.append((t // c == u // c) & upper & ((u % c) < half))
    fwd_sums = np.stack(blocks).astype(np.float32)
    fwd_masks = np.stack(masks).astype(np.float32)
    twice = lambda m: np.concatenate([m.reshape(-1, chunk)] * 2, axis=1)
    sums = np.stack([twice(fwd_sums), twice(fwd_sums[:, ::-1, ::-1])])
    masks = np.concatenate([fwd_masks, fwd_masks[:, ::-1, ::-1]], axis=2)
    return sums, masks


def _block_diag(a, b):
    zero = jnp.zeros_like(a)
    return jnp.concatenate([jnp.concatenate([a, zero], axis=1),
                            jnp.concatenate([zero, b], axis=1)], axis=0)


def _hgrn_dec_rows(chunk):
    n_var = len(_hgrn_group_pairs(0, chunk)) * HG_GROUP
    n_mm = sum(c > 2 for c in _hgrn_fine_levels())
    edge, var, fine = 2 * chunk, 3 * chunk, 3 * chunk + n_var
    return edge, var, fine, fine + n_mm * chunk


def _hgrn_kernel(q_ref, zf_ref, zb_ref, i_ref, g_ref, lbf_ref, lbb_ref, gain_ref,
                 sums_ref, masks_ref, o_ref, of_scr, ob_scr, sf_scr, sb_scr,
                 k_scr0, k_scr1, dec_scr0, dec_scr1, *, layer, seq):
    chunk = HG_CHUNK
    pair = 2 * chunk
    n_pairs = seq // pair
    dh = HG_HEAD_DIM
    grp = HG_GROUP
    n_groups = chunk // grp
    fine = _hgrn_fine_levels()
    group_pairs = [_hgrn_group_pairs(direction, chunk) for direction in (0, 1)]
    r_edge, r_var, r_fine, _ = _hgrn_dec_rows(chunk)
    k_slots = (k_scr0, k_scr1)
    dec_slots = (dec_scr0, dec_scr1)
    row_in_group = lax.broadcasted_iota(jnp.int32, (grp, dh), 0)
    lane_group = lax.broadcasted_iota(jnp.int32, (grp, 2 * chunk), 1) // grp

    def lower_bound(logit_ref):
        logits = logit_ref[...].astype(F32)
        e = jnp.exp(logits - jnp.max(logits, axis=0, keepdims=True))
        return jnp.sum(e[:layer + 1], axis=0, keepdims=True) / jnp.sum(e, axis=0, keepdims=True)

    def gates(z, lb):
        f = lb + (1.0 - lb) * jax.nn.sigmoid(z.astype(F32))
        return jnp.log2(f), 1.0 - f

    def store_decays(slot, direction, g2):
        hi = g2.astype(BF16)
        lo = (g2 - hi.astype(F32)).astype(BF16)
        stacked = [jnp.concatenate([hi[c * chunk:(c + 1) * chunk], lo[c * chunk:(c + 1) * chunk]], axis=0)
                   for c in range(2)]
        raw = _dot(sums_ref[direction], jnp.concatenate(stacked, axis=1))
        dec = dec_slots[slot].at[direction]
        cum = raw[0:chunk]
        whole = cum[chunk - 1:chunk] if direction == 0 else cum[0:1]
        dec[0:chunk, :] = jnp.exp2(cum)
        dec[chunk:2 * chunk, :] = jnp.exp2(whole - cum)
        by_group = cum.reshape(n_groups, grp, 2 * dh)
        mid_row = grp // 2 - 1 if direction == 0 else grp // 2
        to_mid = -jnp.abs(by_group - by_group[:, mid_row:mid_row + 1, :])
        dec[r_fine:r_fine + chunk, :] = jnp.exp2(to_mid.reshape(chunk, 2 * dh))
        dec[r_fine + chunk:, :] = jnp.exp2(raw[chunk:])
        edge = grp - 1 if direction == 0 else 0
        edges = [cum[g * grp + edge:g * grp + edge + 1] for g in range(n_groups)]
        group = lambda g: cum[g * grp:(g + 1) * grp]
        dec[r_edge:r_var, :] = jnp.exp2(jnp.concatenate([edges[g] - group(g) for g in range(n_groups)], axis=0))
        dec[r_var:r_fine, :] = jnp.exp2(jnp.concatenate([group(t) - edges[g]
                                                         for t, g in group_pairs[direction]], axis=0))

    def fine_operand(direction, lvl, q, k, dec):
        c = fine[lvl]
        pos = row_in_group % c
        is_upper = (pos >= c // 2) if direction == 0 else (pos < c // 2)
        pick = lambda x, y: jnp.where(is_upper[None], x.reshape(n_groups, grp, dh),
                                      y.reshape(n_groups, grp, dh)).reshape(chunk, dh)
        if c == 2:
            return pick(q * (1.0 - k), k).astype(BF16)
        return (pick(q, k) * dec(r_fine + lvl * chunk, r_fine + (lvl + 1) * chunk)).astype(BF16)

    def chunk_part(slot, direction, c, q, v, base):
        rows = slice(c * chunk, (c + 1) * chunk)
        lanes = slice(c * dh, (c + 1) * dh)
        return dict(q=q[rows], k=k_slots[slot][direction, rows, :], v=v[rows],
                    dec=lambda r0, r1: dec_slots[slot][direction, r0:r1, lanes],
                    rows=pl.ds(pl.multiple_of(base + c * chunk, chunk), chunk))

    def pair_step(f, b):
        lhs = jnp.concatenate(
            [(jnp.concatenate([p["q"][t * grp:(t + 1) * grp] for t, _ in group_pairs[d]], axis=0)
              * p["dec"](r_var, r_fine)).astype(BF16) for d, p in enumerate((f, b))], axis=1)
        keys = [(p["k"] * p["dec"](r_edge, r_var)).astype(BF16) for p in (f, b)]
        r = _dot_nt(lhs, _block_diag(*keys))
        rows_of = [jnp.zeros((grp, 2 * chunk), F32)] * n_groups
        for i, ((tf, gf), (tb, gb)) in enumerate(zip(*group_pairs)):
            block = r[i * grp:(i + 1) * grp]
            rows_of[tf] = jnp.where(lane_group == gf, block, rows_of[tf])
            rows_of[tb] = jnp.where(lane_group == n_groups + gb, block, rows_of[tb])
        a = jnp.concatenate(rows_of, axis=0)
        for lvl in range(len(fine)):
            xf = fine_operand(0, lvl, f["q"], f["k"], f["dec"])
            xb = fine_operand(1, lvl, b["q"], b["k"], b["dec"])
            a = a + masks_ref[lvl] * _dot_nt(jnp.concatenate([xf, xb], axis=1), _block_diag(xf, xb))
        o = _dot(a.astype(BF16), _block_diag(f["v"], b["v"]))
        sf = sf_scr[...]
        sb = sb_scr[...]
        q_dec = jnp.concatenate([(f["q"] * f["dec"](0, chunk)).astype(BF16),
                                 (b["q"] * b["dec"](0, chunk)).astype(BF16)], axis=1)
        o = o + _dot_nt(q_dec, _block_diag(sf.astype(BF16), sb.astype(BF16)))
        diag_f = jnp.sum(f["q"] * f["k"], axis=-1, keepdims=True)
        diag_b = jnp.sum(b["q"] * b["k"], axis=-1, keepdims=True)
        of_scr[f["rows"], :] = o[:, :dh] + diag_f * f["v"].astype(F32)
        ob_scr[b["rows"], :] = o[:, dh:] + diag_b * b["v"].astype(F32)
        k_dec = _block_diag((f["k"] * f["dec"](chunk, 2 * chunk)).astype(BF16),
                            (b["k"] * b["dec"](chunk, 2 * chunk)).astype(BF16))
        v_t = jnp.concatenate([f["v"], b["v"]], axis=0).astype(F32).T.astype(BF16)
        u = _dot(v_t, k_dec)
        sf_scr[...] = sf * f["dec"](chunk - 1, chunk) + u[:, :dh]
        sb_scr[...] = sb * b["dec"](0, 1) + u[:, dh:]

    lb_f = lower_bound(lbf_ref)
    lb_b = lower_bound(lbb_ref)
    sf_scr[...] = jnp.zeros_like(sf_scr)
    sb_scr[...] = jnp.zeros_like(sb_scr)

    def bases(n):
        return (pl.multiple_of(n * pair, pair), pl.multiple_of((n_pairs - 1 - n) * pair, pair))

    def prepare(n, slot):
        for direction, base, z_ref, lb in zip((0, 1), bases(n), (zf_ref, zb_ref), (lb_f, lb_b)):
            g2, k = gates(z_ref[0, pl.ds(base, pair), :], lb)
            k_slots[slot][direction] = k
            store_decays(slot, direction, g2)

    def consume(n, slot):
        parts = []
        for direction, base in zip((0, 1), bases(n)):
            rows = pl.ds(base, pair)
            q = q_ref[0, rows, :].astype(F32)
            v = i_ref[0, rows, :]
            parts.append([chunk_part(slot, direction, c, q, v, base) for c in range(2)])
        pair_step(parts[0][0], parts[1][1])
        pair_step(parts[0][1], parts[1][0])

    def scan_body(m, carry):
        prepare(2 * m + 1, 1)
        consume(2 * m, 0)
        prepare(jnp.minimum(2 * m + 2, n_pairs - 1), 0)
        consume(2 * m + 1, 1)
        return carry

    prepare(0, 0)
    lax.fori_loop(0, n_pairs // 2, scan_body, 0, unroll=HG_SCAN_UNROLL)

    gain = gain_ref[...].astype(F32)

    def norm_body(n, carry):
        rows = pl.ds(pl.multiple_of(n * HG_NORM_ROWS, HG_NORM_ROWS), HG_NORM_ROWS)
        o = of_scr[rows, :] + ob_scr[rows, :]
        o = o * lax.rsqrt(jnp.mean(jnp.square(o), axis=-1, keepdims=True) + RMS_EPS) * gain
        gate = g_ref[0, rows, :].astype(F32)
        o_ref[0, rows, :] = (o * _silu_from_half(gate)).astype(o_ref.dtype)
        return carry

    lax.fori_loop(0, seq // HG_NORM_ROWS, norm_body, 0, unroll=4)


def _hgrn(q, zf, zb, i, g, lb_fwd_logits, lb_bwd_logits, gain_row, layer):
    batch, seq, width = q.shape
    assert width == HG_HEADS * HG_HEAD_DIM and seq % (4 * HG_NORM_ROWS) == 0
    assert seq % (4 * HG_CHUNK * HG_SCAN_UNROLL) == 0
    sums, masks = _hgrn_constants(HG_CHUNK)
    sums = jnp.asarray(sums, BF16)
    masks = jnp.asarray(masks, F32)
    n_layers = lb_fwd_logits.shape[0]
    n_dec_rows = _hgrn_dec_rows(HG_CHUNK)[-1]
    head = pl.BlockSpec((1, seq, HG_HEAD_DIM), lambda b, h: (b, 0, h))
    per_head_row = lambda n: pl.BlockSpec((n, HG_HEAD_DIM), lambda b, h: (0, h))
    whole = lambda a: pl.BlockSpec(a.shape, lambda b, h: (0,) * a.ndim)
    return pl.pallas_call(
        functools.partial(_hgrn_kernel, layer=layer, seq=seq),
        grid=(batch, HG_HEADS),
        in_specs=[head, head, head, head, head,
                  per_head_row(n_layers), per_head_row(n_layers), per_head_row(1),
                  whole(sums), whole(masks)],
        out_specs=head,
        out_shape=jax.ShapeDtypeStruct((batch, seq, width), BF16),
        scratch_shapes=[pltpu.VMEM((seq, HG_HEAD_DIM), F32), pltpu.VMEM((seq, HG_HEAD_DIM), F32),
                        pltpu.VMEM((HG_HEAD_DIM, HG_HEAD_DIM), F32),
                        pltpu.VMEM((HG_HEAD_DIM, HG_HEAD_DIM), F32),
                        pltpu.VMEM((2, 2 * HG_CHUNK, HG_HEAD_DIM), F32),
                        pltpu.VMEM((2, 2 * HG_CHUNK, HG_HEAD_DIM), F32),
                        pltpu.VMEM((2, n_dec_rows, 2 * HG_HEAD_DIM), F32),
                        pltpu.VMEM((2, n_dec_rows, 2 * HG_HEAD_DIM), F32)],
        compiler_params=pltpu.CompilerParams(
            dimension_semantics=("arbitrary", "arbitrary"), vmem_limit_bytes=VMEM_LIMIT_BYTES),
        name="hgrn2_scan",
    )(q, zf, zb, i, g, lb_fwd_logits, lb_bwd_logits, gain_row, sums, masks)


def _out_proj_kernel(oa_ref, oh_ref, x_ref, w_ref, b_ref, gain_ref, bias_ref, o_ref, w_scr, *, alpha):
    d_att = oa_ref.shape[1]
    _cast_weight_once(w_ref, w_scr)
    for r0 in range(0, o_ref.shape[0], OUT_PROJ_ROWS):
        rows = slice(r0, r0 + OUT_PROJ_ROWS)
        y = _dot(oa_ref[rows, :], w_scr[:d_att, :]) + _dot(oh_ref[rows, :], w_scr[d_att:, :]) + b_ref[...]
        r = alpha * x_ref[rows, :] + y
        mu = jnp.mean(r, axis=-1, keepdims=True)
        c = r - mu
        var = jnp.mean(jnp.square(c), axis=-1, keepdims=True)
        o_ref[rows, :] = (c * lax.rsqrt(var + LN_EPS) * gain_ref[...] + bias_ref[...]).astype(o_ref.dtype)


def _out_proj(o_a, o_h, x2d, w, b_row, gain_row, bias_row, alpha):
    n_tok, d_model = x2d.shape
    d_att, d_hg = o_a.shape[1], o_h.shape[1]
    row = pl.BlockSpec((1, d_model), lambda i: (0, 0))
    return pl.pallas_call(
        functools.partial(_out_proj_kernel, alpha=alpha),
        grid=(n_tok // OUT_PROJ_TM,),
        in_specs=[pl.BlockSpec((OUT_PROJ_TM, d_att), lambda i: (i, 0)),
                  pl.BlockSpec((OUT_PROJ_TM, d_hg), lambda i: (i, 0)),
                  pl.BlockSpec((OUT_PROJ_TM, d_model), lambda i: (i, 0)),
                  pl.BlockSpec((d_att + d_hg, d_model), lambda i: (0, 0), pipeline_mode=pl.Buffered(1)),
                  row, row, row],
        out_specs=pl.BlockSpec((OUT_PROJ_TM, d_model), lambda i: (i, 0)),
        out_shape=jax.ShapeDtypeStruct((n_tok, d_model), x2d.dtype),
        scratch_shapes=[pltpu.VMEM((d_att + d_hg, d_model), BF16)],
        compiler_params=pltpu.CompilerParams(
            dimension_semantics=("arbitrary",), vmem_limit_bytes=VMEM_LIMIT_BYTES),
        name="out_proj_layernorm",
    )(o_a, o_h, x2d, w, b_row, gain_row, bias_row)


def _layer(x, layer, depth, w_in, b_in, rpb, lb_fwd_logits, lb_bwd_logits, hg_norm_gain,
           w_out, b_out, ln_gain, ln_bias):
    batch, seq, d_model = x.shape
    x2d = x.reshape(batch * seq, d_model)
    slabs = _in_proj(x2d, w_in, b_in.reshape(1, -1))
    q_a, k_a, v_a, g_a, q_h, z_f, z_b, i_h, g_h = [s.reshape(batch, seq, SLAB) for s in slabs]
    o_a = _attention(q_a, k_a, v_a, g_a, rpb)
    o_h = _hgrn(q_h, z_f, z_b, i_h, g_h, lb_fwd_logits, lb_bwd_logits,
                hg_norm_gain.reshape(1, -1), layer)
    alpha = (2.0 * depth) ** 0.25
    out = _out_proj(o_a.reshape(batch * seq, SLAB), o_h.reshape(batch * seq, SLAB), x2d,
                    w_out, b_out.reshape(1, -1), ln_gain.reshape(1, -1),
                    ln_bias.reshape(1, -1), alpha)
    return out.reshape(batch, seq, d_model)


def kernel(x, w_in, b_in, rpb, lb_fwd_logits, lb_bwd_logits, hg_norm_gain, w_out, b_out, ln_gain, ln_bias):
    depth = w_in.shape[0]
    for layer in range(depth):
        x = _layer(x, layer, depth, w_in[layer], b_in[layer], rpb[layer], lb_fwd_logits,
                   lb_bwd_logits, hg_norm_gain[layer], w_out[layer], b_out[layer],
                   ln_gain[layer], ln_bias[layer])
    return x
```

```python
import functools

import numpy as np
import jax
import jax.numpy as jnp
from jax import lax
from jax.experimental import pallas as pl
from jax.experimental.pallas import tpu as pltpu

GRID_W = 64
ATT_HEADS = 8
ATT_HEAD_DIM = 64
HG_HEADS = 4
HG_HEAD_DIM = 128
WIN_ROWS = 8
WIN_COLS = 16
LN_EPS = 1e-5
RMS_EPS = 1e-6
N_SLABS = 9
SLAB = 512

LANES = 128
F32_SUBLANES = 8
VMEM_LIMIT_BYTES = 56 * 1024 * 1024
PROJ_TM = 512
OUT_PROJ_TM = 2048
OUT_PROJ_ROWS = 256
ATT_ROWS_PER_STEP = 32
ATT_UNROLL = 4
ATT_GROUP = 4
ATT_BIAS_UNROLL = 8
HG_CHUNK = 64
HG_GROUP = F32_SUBLANES
HG_NORM_ROWS = 256
HG_SCAN_UNROLL = 16
MASK_VALUE = -1e30
LOG2_E = 1.4426950408889634
SLAB_SCALES = {1: ATT_HEAD_DIM ** -0.5 * LOG2_E, 3: 0.5, 8: 0.5}

BF16 = jnp.bfloat16
F32 = jnp.float32


def _silu_from_half(half_g):
    return half_g * (1.0 + jnp.tanh(half_g))


def _dot(a, b):
    return jnp.dot(a, b, preferred_element_type=F32)


def _dot_nt(a, b):
    return lax.dot_general(a, b, (((1,), (1,)), ((), ())), preferred_element_type=F32)


def _cast_weight_once(w_ref, w_scr):
    @pl.when(pl.program_id(0) == 0)
    def _():
        for c0 in range(0, w_ref.shape[1], SLAB):
            w_scr[:, c0:c0 + SLAB] = w_ref[:, c0:c0 + SLAB].astype(w_scr.dtype)


def _in_proj_kernel(x_ref, w_ref, b_ref, *refs):
    out_refs, w_scr = refs[:-1], refs[-1]
    step = pl.program_id(0)
    xb = x_ref[...].astype(BF16)

    def project(j):
        cols = slice(j * SLAB, (j + 1) * SLAB)
        h = _dot(xb, w_scr[:, cols]) + b_ref[:, cols]
        if j in SLAB_SCALES:
            h = h * SLAB_SCALES[j]
        out_refs[j][...] = h.astype(out_refs[j].dtype)

    for j in range(N_SLABS):
        @pl.when(step == j)
        def _(j=j):
            w_scr[:, j * SLAB:(j + 1) * SLAB] = w_ref[...].astype(w_scr.dtype)
            project(j)

    @pl.when(step >= N_SLABS)
    def _():
        for j in range(N_SLABS):
            project(j)


def _in_proj(x2d, w, b_row):
    n_tok, d_model = x2d.shape
    d_in = w.shape[1]
    assert d_in == N_SLABS * SLAB and n_tok % PROJ_TM == 0
    lead = N_SLABS - 1
    tile = lambda i: (jnp.maximum(i - lead, 0), 0)
    return pl.pallas_call(
        _in_proj_kernel,
        grid=(n_tok // PROJ_TM + lead,),
        in_specs=[
            pl.BlockSpec((PROJ_TM, d_model), tile),
            pl.BlockSpec((d_model, SLAB), lambda i: (0, jnp.minimum(i, lead))),
            pl.BlockSpec((1, d_in), lambda i: (0, 0)),
        ],
        out_specs=[pl.BlockSpec((PROJ_TM, SLAB), tile)] * N_SLABS,
        out_shape=[jax.ShapeDtypeStruct((n_tok, SLAB), BF16)] * N_SLABS,
        scratch_shapes=[pltpu.VMEM((d_model, d_in), BF16)],
        compiler_params=pltpu.CompilerParams(
            dimension_semantics=("arbitrary",), vmem_limit_bytes=VMEM_LIMIT_BYTES),
        name="in_proj",
    )(x2d, w, b_row)


ATT_REL_ROWS = 2 * WIN_ROWS - 1
ATT_BIAS_TILES = (ATT_REL_ROWS - 1) // 2
ATT_GROUPS = ATT_HEADS // ATT_GROUP


def _attention_bias_rows(rpb):
    n_rel_cols = 2 * WIN_COLS - 1
    scaled = rpb.astype(F32) * LOG2_E
    gap = jnp.zeros((ATT_HEADS, ATT_BIAS_TILES, (LANES - 2 * n_rel_cols) // 2), F32)
    per_parity = []
    for p in range(2):
        first = scaled[:, p:p + 2 * ATT_BIAS_TILES - 1:2]
        second = scaled[:, p + 1:p + 2 * ATT_BIAS_TILES:2]
        per_parity.append(jnp.concatenate(
            [first[..., WIN_COLS - 1:], gap, second, gap, first[..., :WIN_COLS - 1]], axis=-1))
    gen = jnp.stack(per_parity)
    gen = gen.reshape(2, ATT_GROUPS, ATT_GROUP, ATT_BIAS_TILES, LANES).transpose(0, 1, 3, 2, 4)
    return gen.reshape(2 * ATT_GROUPS * ATT_BIAS_TILES * ATT_GROUP, 1, LANES)


def _attention_window_mask():
    qc = np.arange(GRID_W)[:, None]
    kc = np.arange(LANES)[None, :] % GRID_W
    col_start = np.clip(qc - WIN_COLS // 2, 0, GRID_W - WIN_COLS)
    return ((kc >= col_start) & (kc < col_start + WIN_COLS)).astype(np.float32)


def _attention_first_key_row(step, rows, kr):
    return jnp.clip(step * ATT_ROWS_PER_STEP - kr // 2, 0, rows - ATT_ROWS_PER_STEP - kr)


def _attention_kernel(q_ref, k_ref, v_ref, g_ref, gen_ref, win_ref, o_ref, bias_scr, *, rows, kr):
    step = pl.program_id(1)
    gw = ATT_GROUP * ATT_HEAD_DIM
    row_head = lax.broadcasted_iota(jnp.int32, (ATT_GROUP * GRID_W, gw), 0) // GRID_W
    lane_head = lax.broadcasted_iota(jnp.int32, (ATT_GROUP * GRID_W, gw), 1) // ATT_HEAD_DIM
    own_head = row_head == lane_head
    out_lane_head = lax.broadcasted_iota(jnp.int32, (GRID_W, gw), 1) // ATT_HEAD_DIM

    @pl.when((pl.program_id(0) == 0) & (step == 0))
    def _build_bias():
        in_window = win_ref[...] > 0.0

        def one_block(n, carry):
            gen = jnp.broadcast_to(gen_ref[n], (GRID_W, LANES))
            toeplitz = pltpu.roll(gen, 0, 1, stride=1, stride_axis=0)
            head_rows = pl.ds(pl.multiple_of((n % ATT_GROUP) * GRID_W, GRID_W), GRID_W)
            bias_scr[n // ATT_GROUP, head_rows, :] = jnp.where(in_window, toeplitz, MASK_VALUE)
            return carry

        lax.fori_loop(0, gen_ref.shape[0], one_block, 0, unroll=ATT_BIAS_UNROLL)

    def one_row(j, carry):
        r = step * ATT_ROWS_PER_STEP + j
        row_start = jnp.clip(r - kr // 2, 0, rows - kr)
        variant = row_start - r + (WIN_ROWS - 1)
        q_tok = pl.multiple_of(j * GRID_W, GRID_W)
        k_tok = pl.multiple_of((row_start - _attention_first_key_row(step, rows, kr)) * GRID_W, GRID_W)
        for grp in range(ATT_GROUPS):
            lanes = slice(grp * gw, (grp + 1) * gw)
            q = q_ref[0, pl.ds(q_tok, GRID_W), lanes]
            q_bd = jnp.where(own_head, jnp.concatenate([q] * ATT_GROUP, axis=0), 0)
            keys = k_ref[0, pl.ds(k_tok, kr * GRID_W), lanes]
            vals = v_ref[0, pl.ds(k_tok, kr * GRID_W), lanes]
            tile0 = ((variant % 2) * ATT_GROUPS + grp) * ATT_BIAS_TILES + variant // 2
            bias = jnp.concatenate([bias_scr[tile0 + t] for t in range(kr // 2)], axis=1)
            s = _dot_nt(q_bd, keys) + bias
            m = jnp.max(s, axis=-1, keepdims=True)
            p = jnp.exp2(s - m)
            denom = jnp.sum(p, axis=-1, keepdims=True)
            pv = _dot(p.astype(BF16), vals) / denom
            o = jnp.zeros((GRID_W, gw), F32)
            for h in range(ATT_GROUP):
                o = o + jnp.where(out_lane_head == h, pv[h * GRID_W:(h + 1) * GRID_W], 0.0)
            gate = g_ref[0, pl.ds(q_tok, GRID_W), lanes].astype(F32)
            o = o * _silu_from_half(gate)
            o_ref[0, pl.ds(q_tok, GRID_W), lanes] = o.astype(o_ref.dtype)
        return carry

    lax.fori_loop(0, ATT_ROWS_PER_STEP, one_row, 0, unroll=ATT_UNROLL)


def _attention(q, k, v, g, rpb):
    batch, seq, width = q.shape
    rows = seq // GRID_W
    kr = min(WIN_ROWS, rows)
    assert rows % ATT_ROWS_PER_STEP == 0 and ATT_ROWS_PER_STEP % ATT_UNROLL == 0
    assert kr == WIN_ROWS and 2 * GRID_W == LANES
    gen = _attention_bias_rows(rpb)
    win = jnp.asarray(_attention_window_mask())
    blk = ATT_ROWS_PER_STEP * GRID_W
    tile = pl.BlockSpec((1, blk, width), lambda b, i: (b, i, 0))
    assert rows >= ATT_ROWS_PER_STEP + kr
    halo = pl.BlockSpec((pl.Element(1), pl.Element((ATT_ROWS_PER_STEP + kr) * GRID_W), pl.Element(width)),
                        lambda b, i: (b, _attention_first_key_row(i, rows, kr) * GRID_W, 0))
    return pl.pallas_call(
        functools.partial(_attention_kernel, rows=rows, kr=kr),
        grid=(batch, rows // ATT_ROWS_PER_STEP),
        in_specs=[tile, halo, halo, tile,
                  pl.BlockSpec(gen.shape, lambda b, i: (0, 0, 0)),
                  pl.BlockSpec(win.shape, lambda b, i: (0, 0))],
        out_specs=tile,
        out_shape=jax.ShapeDtypeStruct((batch, seq, width), BF16),
        scratch_shapes=[pltpu.VMEM((2 * ATT_GROUPS * ATT_BIAS_TILES, ATT_GROUP * GRID_W, 2 * GRID_W), F32)],
        compiler_params=pltpu.CompilerParams(
            dimension_semantics=("arbitrary", "arbitrary"), vmem_limit_bytes=VMEM_LIMIT_BYTES),
        name="nbr_attention",
    )(q, k, v, g, gen, win)


def _hgrn_fine_levels():
    sizes = []
    c = HG_GROUP
    while c >= 2:
        sizes.append(c)
        c //= 2
    return sizes


def _hgrn_group_pairs(direction, chunk):
    n = chunk // HG_GROUP
    if direction == 0:
        return [(t, g) for t in range(1, n) for g in range(t)]
    return [(t, g) for t in range(n - 2, -1, -1) for g in range(t + 1, n)]


def _hgrn_constants(chunk):
    t = np.arange(chunk)[:, None]
    u = np.arange(chunk)[None, :]
    blocks = [(u <= t)]
    masks = []
    for c in _hgrn_fine_levels():
        half = c // 2
        mid = (t // c) * c + half
        upper = (t % c) >= half
        if 2 < c < HG_GROUP:
            blocks.append(np.where(upper, (u >= mid) & (u <= t), (u > t) & (u <= mid - 1)))
        masks.append((t // c == u // c) & upper & ((u % c) < half))
    fwd_sums = np.stack(blocks).astype(np.float32)
    fwd_masks = np.stack(masks).astype(np.float32)
    twice = lambda m: np.concatenate([m.reshape(-1, chunk)] * 2, axis=1)
    sums = np.stack([twice(fwd_sums), twice(fwd_sums[:, ::-1, ::-1])])
    masks = np.concatenate([fwd_masks, fwd_masks[:, ::-1, ::-1]], axis=2)
    return sums, masks


def _block_diag(a, b):
    zero = jnp.zeros_like(a)
    return jnp.concatenate([jnp.concatenate([a, zero], axis=1),
                            jnp.concatenate([zero, b], axis=1)], axis=0)


def _hgrn_dec_rows(chunk):
    n_var = len(_hgrn_group_pairs(0, chunk)) * HG_GROUP
    n_mm = sum(c > 2 for c in _hgrn_fine_levels())
    edge, var, fine = 2 * chunk, 3 * chunk, 3 * chunk + n_var
    return edge, var, fine, fine + n_mm * chunk


def _hgrn_kernel(q_ref, zf_ref, zb_ref, i_ref, g_ref, lbf_ref, lbb_ref, gain_ref,
                 sums_ref, masks_ref, o_ref, of_scr, ob_scr, sf_scr, sb_scr,
                 k_scr0, k_scr1, dec_scr0, dec_scr1, *, layer, seq):
    chunk = HG_CHUNK
    pair = 2 * chunk
    n_pairs = seq // pair
    dh = HG_HEAD_DIM
    grp = HG_GROUP
    n_groups = chunk // grp
    fine = _hgrn_fine_levels()
    group_pairs = [_hgrn_group_pairs(direction, chunk) for direction in (0, 1)]
    r_edge, r_var, r_fine, _ = _hgrn_dec_rows(chunk)
    k_slots = (k_scr0, k_scr1)
    dec_slots = (dec_scr0, dec_scr1)
    row_in_group = lax.broadcasted_iota(jnp.int32, (grp, dh), 0)
    lane_group = lax.broadcasted_iota(jnp.int32, (grp, 2 * chunk), 1) // grp

    def lower_bound(logit_ref):
        logits = logit_ref[...].astype(F32)
        e = jnp.exp(logits - jnp.max(logits, axis=0, keepdims=True))
        return jnp.sum(e[:layer + 1], axis=0, keepdims=True) / jnp.sum(e, axis=0, keepdims=True)

    def gates(z, lb):
        f = lb + (1.0 - lb) * jax.nn.sigmoid(z.astype(F32))
        return jnp.log2(f), 1.0 - f

    def store_decays(slot, direction, g2):
        hi = g2.astype(BF16)
        lo = (g2 - hi.astype(F32)).astype(BF16)
        stacked = [jnp.concatenate([hi[c * chunk:(c + 1) * chunk], lo[c * chunk:(c + 1) * chunk]], axis=0)
                   for c in range(2)]
        raw = _dot(sums_ref[direction], jnp.concatenate(stacked, axis=1))
        dec = dec_slots[slot].at[direction]
        cum = raw[0:chunk]
        whole = cum[chunk - 1:chunk] if direction == 0 else cum[0:1]
        dec[0:chunk, :] = jnp.exp2(cum)
        dec[chunk:2 * chunk, :] = jnp.exp2(whole - cum)
        by_group = cum.reshape(n_groups, grp, 2 * dh)
        mid_row = grp // 2 - 1 if direction == 0 else grp // 2
        to_mid = -jnp.abs(by_group - by_group[:, mid_row:mid_row + 1, :])
        dec[r_fine:r_fine + chunk, :] = jnp.exp2(to_mid.reshape(chunk, 2 * dh))
        dec[r_fine + chunk:, :] = jnp.exp2(raw[chunk:])
        edge = grp - 1 if direction == 0 else 0
        edges = [cum[g * grp + edge:g * grp + edge + 1] for g in range(n_groups)]
        group = lambda g: cum[g * grp:(g + 1) * grp]
        dec[r_edge:r_var, :] = jnp.exp2(jnp.concatenate([edges[g] - group(g) for g in range(n_groups)], axis=0))
        dec[r_var:r_fine, :] = jnp.exp2(jnp.concatenate([group(t) - edges[g]
                                                         for t, g in group_pairs[direction]], axis=0))

    def fine_operand(direction, lvl, q, k, dec):
        c = fine[lvl]
        pos = row_in_group % c
        is_upper = (pos >= c // 2) if direction == 0 else (pos < c // 2)
        pick = lambda x, y: jnp.where(is_upper[None], x.reshape(n_groups, grp, dh),
                                      y.reshape(n_groups, grp, dh)).reshape(chunk, dh)
        if c == 2:
            return pick(q * (1.0 - k), k).astype(BF16)
        return (pick(q, k) * dec(r_fine + lvl * chunk, r_fine + (lvl + 1) * chunk)).astype(BF16)

    def chunk_part(slot, direction, c, q, v, base):
        rows = slice(c * chunk, (c + 1) * chunk)
        lanes = slice(c * dh, (c + 1) * dh)
        return dict(q=q[rows], k=k_slots[slot][direction, rows, :], v=v[rows],
                    dec=lambda r0, r1: dec_slots[slot][direction, r0:r1, lanes],
                    rows=pl.ds(pl.multiple_of(base + c * chunk, chunk), chunk))

    def pair_step(f, b):
        lhs = jnp.concatenate(
            [(jnp.concatenate([p["q"][t * grp:(t + 1) * grp] for t, _ in group_pairs[d]], axis=0)
              * p["dec"](r_var, r_fine)).astype(BF16) for d, p in enumerate((f, b))], axis=1)
        keys = [(p["k"] * p["dec"](r_edge, r_var)).astype(BF16) for p in (f, b)]
        r = _dot_nt(lhs, _block_diag(*keys))
        rows_of = [jnp.zeros((grp, 2 * chunk), F32)] * n_groups
        for i, ((tf, gf), (tb, gb)) in enumerate(zip(*group_pairs)):
            block = r[i * grp:(i + 1) * grp]
            rows_of[tf] = jnp.where(lane_group == gf, block, rows_of[tf])
            rows_of[tb] = jnp.where(lane_group == n_groups + gb, block, rows_of[tb])
        a = jnp.concatenate(rows_of, axis=0)
        for lvl in range(len(fine)):
            xf = fine_operand(0, lvl, f["q"], f["k"], f["dec"])
            xb = fine_operand(1, lvl, b["q"], b["k"], b["dec"])
            a = a + masks_ref[lvl] * _dot_nt(jnp.concatenate([xf, xb], axis=1), _block_diag(xf, xb))
        o = _dot(a.astype(BF16), _block_diag(f["v"], b["v"]))
        sf = sf_scr[...]
        sb = sb_scr[...]
        q_dec = jnp.concatenate([(f["q"] * f["dec"](0, chunk)).astype(BF16),
                                 (b["q"] * b["dec"](0, chunk)).astype(BF16)], axis=1)
        o = o + _dot_nt(q_dec, _block_diag(sf.astype(BF16), sb.astype(BF16)))
        diag_f = jnp.sum(f["q"] * f["k"], axis=-1, keepdims=True)
        diag_b = jnp.sum(b["q"] * b["k"], axis=-1, keepdims=True)
        of_scr[f["rows"], :] = o[:, :dh] + diag_f * f["v"].astype(F32)
        ob_scr[b["rows"], :] = o[:, dh:] + diag_b * b["v"].astype(F32)
        k_dec = _block_diag((f["k"] * f["dec"](chunk, 2 * chunk)).astype(BF16),
                            (b["k"] * b["dec"](chunk, 2 * chunk)).astype(BF16))
        v_t = jnp.concatenate([f["v"], b["v"]], axis=0).astype(F32).T.astype(BF16)
        u = _dot(v_t, k_dec)
        sf_scr[...] = sf * f["dec"](chunk - 1, chunk) + u[:, :dh]
        sb_scr[...] = sb * b["dec"](0, 1) + u[:, dh:]

    lb_f = lower_bound(lbf_ref)
    lb_b = lower_bound(lbb_ref)
    sf_scr[...] = jnp.zeros_like(sf_scr)
    sb_scr[...] = jnp.zeros_like(sb_scr)

    def bases(n):
        return (pl.multiple_of(n * pair, pair), pl.multiple_of((n_pairs - 1 - n) * pair, pair))

    def prepare(n, slot):
        for direction, base, z_ref, lb in zip((0, 1), bases(n), (zf_ref, zb_ref), (lb_f, lb_b)):
            g2, k = gates(z_ref[0, pl.ds(base, pair), :], lb)
            k_slots[slot][direction] = k
            store_decays(slot, direction, g2)

    def consume(n, slot):
        parts = []
        for direction, base in zip((0, 1), bases(n)):
            rows = pl.ds(base, pair)
            q = q_ref[0, rows, :].astype(F32)
            v = i_ref[0, rows, :]
            parts.append([chunk_part(slot, direction, c, q, v, base) for c in range(2)])
        pair_step(parts[0][0], parts[1][1])
        pair_step(parts[0][1], parts[1][0])

    def scan_body(m, carry):
        prepare(2 * m + 1, 1)
        consume(2 * m, 0)
        prepare(jnp.minimum(2 * m + 2, n_pairs - 1), 0)
        consume(2 * m + 1, 1)
        return carry

    prepare(0, 0)
    lax.fori_loop(0, n_pairs // 2, scan_body, 0, unroll=HG_SCAN_UNROLL)

    gain = gain_ref[...].astype(F32)

    def norm_body(n, carry):
        rows = pl.ds(pl.multiple_of(n * HG_NORM_ROWS, HG_NORM_ROWS), HG_NORM_ROWS)
        o = of_scr[rows, :] + ob_scr[rows, :]
        o = o * lax.rsqrt(jnp.mean(jnp.square(o), axis=-1, keepdims=True) + RMS_EPS) * gain
        gate = g_ref[0, rows, :].astype(F32)
        o_ref[0, rows, :] = (o * _silu_from_half(gate)).astype(o_ref.dtype)
        return carry

    lax.fori_loop(0, seq // HG_NORM_ROWS, norm_body, 0, unroll=4)


def _hgrn(q, zf, zb, i, g, lb_fwd_logits, lb_bwd_logits, gain_row, layer):
    batch, seq, width = q.shape
    assert width == HG_HEADS * HG_HEAD_DIM and seq % (4 * HG_NORM_ROWS) == 0
    assert seq % (4 * HG_CHUNK * HG_SCAN_UNROLL) == 0
    sums, masks = _hgrn_constants(HG_CHUNK)
    sums = jnp.asarray(sums, BF16)
    masks = jnp.asarray(masks, F32)
    n_layers = lb_fwd_logits.shape[0]
    n_dec_rows = _hgrn_dec_rows(HG_CHUNK)[-1]
    head = pl.BlockSpec((1, seq, HG_HEAD_DIM), lambda b, h: (b, 0, h))
    per_head_row = lambda n: pl.BlockSpec((n, HG_HEAD_DIM), lambda b, h: (0, h))
    whole = lambda a: pl.BlockSpec(a.shape, lambda b, h: (0,) * a.ndim)
    return pl.pallas_call(
        functools.partial(_hgrn_kernel, layer=layer, seq=seq),
        grid=(batch, HG_HEADS),
        in_specs=[head, head, head, head, head,
                  per_head_row(n_layers), per_head_row(n_layers), per_head_row(1),
                  whole(sums), whole(masks)],
        out_specs=head,
        out_shape=jax.ShapeDtypeStruct((batch, seq, width), BF16),
        scratch_shapes=[pltpu.VMEM((seq, HG_HEAD_DIM), F32), pltpu.VMEM((seq, HG_HEAD_DIM), F32),
                        pltpu.VMEM((HG_HEAD_DIM, HG_HEAD_DIM), F32),
                        pltpu.VMEM((HG_HEAD_DIM, HG_HEAD_DIM), F32),
                        pltpu.VMEM((2, 2 * HG_CHUNK, HG_HEAD_DIM), F32),
                        pltpu.VMEM((2, 2 * HG_CHUNK, HG_HEAD_DIM), F32),
                        pltpu.VMEM((2, n_dec_rows, 2 * HG_HEAD_DIM), F32),
                        pltpu.VMEM((2, n_dec_rows, 2 * HG_HEAD_DIM), F32)],
        compiler_params=pltpu.CompilerParams(
            dimension_semantics=("arbitrary", "arbitrary"), vmem_limit_bytes=VMEM_LIMIT_BYTES),
        name="hgrn2_scan",
    )(q, zf, zb, i, g, lb_fwd_logits, lb_bwd_logits, gain_row, sums, masks)


def _out_proj_kernel(oa_ref, oh_ref, x_ref, w_ref, b_ref, gain_ref, bias_ref, o_ref, w_scr, *, alpha):
    d_att = oa_ref.shape[1]
    _cast_weight_once(w_ref, w_scr)
    for r0 in range(0, o_ref.shape[0], OUT_PROJ_ROWS):
        rows = slice(r0, r0 + OUT_PROJ_ROWS)
        y = _dot(oa_ref[rows, :], w_scr[:d_att, :]) + _dot(oh_ref[rows, :], w_scr[d_att:, :]) + b_ref[...]
        r = alpha * x_ref[rows, :] + y
        mu = jnp.mean(r, axis=-1, keepdims=True)
        c = r - mu
        var = jnp.mean(jnp.square(c), axis=-1, keepdims=True)
        o_ref[rows, :] = (c * lax.rsqrt(var + LN_EPS) * gain_ref[...] + bias_ref[...]).astype(o_ref.dtype)


def _out_proj(o_a, o_h, x2d, w, b_row, gain_row, bias_row, alpha):
    n_tok, d_model = x2d.shape
    d_att, d_hg = o_a.shape[1], o_h.shape[1]
    row = pl.BlockSpec((1, d_model), lambda i: (0, 0))
    return pl.pallas_call(
        functools.partial(_out_proj_kernel, alpha=alpha),
        grid=(n_tok // OUT_PROJ_TM,),
        in_specs=[pl.BlockSpec((OUT_PROJ_TM, d_att), lambda i: (i, 0)),
                  pl.BlockSpec((OUT_PROJ_TM, d_hg), lambda i: (i, 0)),
                  pl.BlockSpec((OUT_PROJ_TM, d_model), lambda i: (i, 0)),
                  pl.BlockSpec((d_att + d_hg, d_model), lambda i: (0, 0), pipeline_mode=pl.Buffered(1)),
                  row, row, row],
        out_specs=pl.BlockSpec((OUT_PROJ_TM, d_model), lambda i: (i, 0)),
        out_shape=jax.ShapeDtypeStruct((n_tok, d_model), x2d.dtype),
        scratch_shapes=[pltpu.VMEM((d_att + d_hg, d_model), BF16)],
        compiler_params=pltpu.CompilerParams(
            dimension_semantics=("arbitrary",), vmem_limit_bytes=VMEM_LIMIT_BYTES),
        name="out_proj_layernorm",
    )(o_a, o_h, x2d, w, b_row, gain_row, bias_row)


def _layer(x, layer, depth, w_in, b_in, rpb, lb_fwd_logits, lb_bwd_logits, hg_norm_gain,
           w_out, b_out, ln_gain, ln_bias):
    batch, seq, d_model = x.shape
    x2d = x.reshape(batch * seq, d_model)
    slabs = _in_proj(x2d, w_in, b_in.reshape(1, -1))
    q_a, k_a, v_a, g_a, q_h, z_f, z_b, i_h, g_h = [s.reshape(batch, seq, SLAB) for s in slabs]
    o_a = _attention(q_a, k_a, v_a, g_a, rpb)
    o_h = _hgrn(q_h, z_f, z_b, i_h, g_h, lb_fwd_logits, lb_bwd_logits,
                hg_norm_gain.reshape(1, -1), layer)
    alpha = (2.0 * depth) ** 0.25
    out = _out_proj(o_a.reshape(batch * seq, SLAB), o_h.reshape(batch * seq, SLAB), x2d,
                    w_out, b_out.reshape(1, -1), ln_gain.reshape(1, -1),
                    ln_bias.reshape(1, -1), alpha)
    return out.reshape(batch, seq, d_model)


def kernel(x, w_in, b_in, rpb, lb_fwd_logits, lb_bwd_logits, hg_norm_gain, w_out, b_out, ln_gain, ln_bias):
    depth = w_in.shape[0]
    for layer in range(depth):
        x = _layer(x, layer, depth, w_in[layer], b_in[layer], rpb[layer], lb_fwd_logits,
                   lb_bwd_logits, hg_norm_gain[layer], w_out[layer], b_out[layer],
                   ln_gain[layer], ln_bias[layer])
    return x
```

```python
import functools

import numpy as np
import jax
import jax.numpy as jnp
from jax import lax
from jax.experimental import pallas as pl
from jax.experimental.pallas import tpu as pltpu

GRID_W = 64
ATT_HEADS = 8
ATT_HEAD_DIM = 64
HG_HEADS = 4
HG_HEAD_DIM = 128
WIN_ROWS = 8
WIN_COLS = 16
LN_EPS = 1e-5
RMS_EPS = 1e-6
N_SLABS = 9
SLAB = 512

LANES = 128
F32_SUBLANES = 8
VMEM_LIMIT_BYTES = 56 * 1024 * 1024
PROJ_TM = 512
OUT_PROJ_TM = 2048
OUT_PROJ_ROWS = 256
ATT_ROWS_PER_STEP = 16
ATT_UNROLL = 4
ATT_GROUP = 4
ATT_BIAS_UNROLL = 8
HG_CHUNK = 64
HG_GROUP = F32_SUBLANES
HG_NORM_ROWS = 256
HG_SCAN_UNROLL = 16
MASK_VALUE = -1e30
LOG2_E = 1.4426950408889634
SLAB_SCALES = {1: ATT_HEAD_DIM ** -0.5 * LOG2_E, 3: 0.5, 8: 0.5}

BF16 = jnp.bfloat16
F32 = jnp.float32


def _silu_from_half(half_g):
    return half_g * (1.0 + jnp.tanh(half_g))


def _dot(a, b):
    return jnp.dot(a, b, preferred_element_type=F32)


def _dot_nt(a, b):
    return lax.dot_general(a, b, (((1,), (1,)), ((), ())), preferred_element_type=F32)


def _cast_weight_once(w_ref, w_scr):
    @pl.when(pl.program_id(0) == 0)
    def _():
        for c0 in range(0, w_ref.shape[1], SLAB):
            w_scr[:, c0:c0 + SLAB] = w_ref[:, c0:c0 + SLAB].astype(w_scr.dtype)


def _in_proj_kernel(x_ref, w_ref, b_ref, *refs):
    out_refs, w_scr = refs[:-1], refs[-1]
    _cast_weight_once(w_ref, w_scr)
    xb = x_ref[...].astype(BF16)
    for j, o_ref in enumerate(out_refs):
        cols = slice(j * SLAB, (j + 1) * SLAB)
        h = _dot(xb, w_scr[:, cols]) + b_ref[:, cols]
        if j in SLAB_SCALES:
            h = h * SLAB_SCALES[j]
        o_ref[...] = h.astype(o_ref.dtype)


def _in_proj(x2d, w, b_row):
    n_tok, d_model = x2d.shape
    d_in = w.shape[1]
    assert d_in == N_SLABS * SLAB and n_tok % PROJ_TM == 0
    return pl.pallas_call(
        _in_proj_kernel,
        grid=(n_tok // PROJ_TM,),
        in_specs=[
            pl.BlockSpec((PROJ_TM, d_model), lambda i: (i, 0)),
            pl.BlockSpec((d_model, d_in), lambda i: (0, 0), pipeline_mode=pl.Buffered(1)),
            pl.BlockSpec((1, d_in), lambda i: (0, 0)),
        ],
        out_specs=[pl.BlockSpec((PROJ_TM, SLAB), lambda i: (i, 0))] * N_SLABS,
        out_shape=[jax.ShapeDtypeStruct((n_tok, SLAB), BF16)] * N_SLABS,
        scratch_shapes=[pltpu.VMEM((d_model, d_in), BF16)],
        compiler_params=pltpu.CompilerParams(
            dimension_semantics=("arbitrary",), vmem_limit_bytes=VMEM_LIMIT_BYTES),
        name="in_proj",
    )(x2d, w, b_row)


ATT_REL_ROWS = 2 * WIN_ROWS - 1
ATT_BIAS_TILES = (ATT_REL_ROWS - 1) // 2
ATT_GROUPS = ATT_HEADS // ATT_GROUP


def _attention_bias_rows(rpb):
    n_rel_cols = 2 * WIN_COLS - 1
    scaled = rpb.astype(F32) * LOG2_E
    gap = jnp.zeros((ATT_HEADS, ATT_BIAS_TILES, (LANES - 2 * n_rel_cols) // 2), F32)
    per_parity = []
    for p in range(2):
        first = scaled[:, p:p + 2 * ATT_BIAS_TILES - 1:2]
        second = scaled[:, p + 1:p + 2 * ATT_BIAS_TILES:2]
        per_parity.append(jnp.concatenate(
            [first[..., WIN_COLS - 1:], gap, second, gap, first[..., :WIN_COLS - 1]], axis=-1))
    gen = jnp.stack(per_parity)
    gen = gen.reshape(2, ATT_GROUPS, ATT_GROUP, ATT_BIAS_TILES, LANES).transpose(0, 1, 3, 2, 4)
    return gen.reshape(2 * ATT_GROUPS * ATT_BIAS_TILES * ATT_GROUP, 1, LANES)


def _attention_window_mask():
    qc = np.arange(GRID_W)[:, None]
    kc = np.arange(LANES)[None, :] % GRID_W
    col_start = np.clip(qc - WIN_COLS // 2, 0, GRID_W - WIN_COLS)
    return ((kc >= col_start) & (kc < col_start + WIN_COLS)).astype(np.float32)


def _attention_first_key_row(step, rows, kr):
    return jnp.clip(step * ATT_ROWS_PER_STEP - kr // 2, 0, rows - ATT_ROWS_PER_STEP - kr)


def _attention_kernel(q_ref, k_ref, v_ref, g_ref, gen_ref, win_ref, o_ref, bias_scr, *, rows, kr):
    step = pl.program_id(1)
    gw = ATT_GROUP * ATT_HEAD_DIM
    row_head = lax.broadcasted_iota(jnp.int32, (ATT_GROUP * GRID_W, gw), 0) // GRID_W
    lane_head = lax.broadcasted_iota(jnp.int32, (ATT_GROUP * GRID_W, gw), 1) // ATT_HEAD_DIM
    own_head = row_head == lane_head
    out_lane_head = lax.broadcasted_iota(jnp.int32, (GRID_W, gw), 1) // ATT_HEAD_DIM

    @pl.when((pl.program_id(0) == 0) & (step == 0))
    def _build_bias():
        in_window = win_ref[...] > 0.0

        def one_block(n, carry):
            gen = jnp.broadcast_to(gen_ref[n], (GRID_W, LANES))
            toeplitz = pltpu.roll(gen, 0, 1, stride=1, stride_axis=0)
            head_rows = pl.ds(pl.multiple_of((n % ATT_GROUP) * GRID_W, GRID_W), GRID_W)
            bias_scr[n // ATT_GROUP, head_rows, :] = jnp.where(in_window, toeplitz, MASK_VALUE)
            return carry

        lax.fori_loop(0, gen_ref.shape[0], one_block, 0, unroll=ATT_BIAS_UNROLL)

    def one_row(j, carry):
        r = step * ATT_ROWS_PER_STEP + j
        row_start = jnp.clip(r - kr // 2, 0, rows - kr)
        variant = row_start - r + (WIN_ROWS - 1)
        q_tok = pl.multiple_of(j * GRID_W, GRID_W)
        k_tok = pl.multiple_of((row_start - _attention_first_key_row(step, rows, kr)) * GRID_W, GRID_W)
        for grp in range(ATT_GROUPS):
            lanes = slice(grp * gw, (grp + 1) * gw)
            q = q_ref[0, pl.ds(q_tok, GRID_W), lanes]
            q_bd = jnp.where(own_head, jnp.concatenate([q] * ATT_GROUP, axis=0), 0)
            keys = k_ref[0, pl.ds(k_tok, kr * GRID_W), lanes]
            vals = v_ref[0, pl.ds(k_tok, kr * GRID_W), lanes]
            tile0 = ((variant % 2) * ATT_GROUPS + grp) * ATT_BIAS_TILES + variant // 2
            bias = jnp.concatenate([bias_scr[tile0 + t] for t in range(kr // 2)], axis=1)
            s = _dot_nt(q_bd, keys) + bias
            m = jnp.max(s, axis=-1, keepdims=True)
            p = jnp.exp2(s - m)
            denom = jnp.sum(p, axis=-1, keepdims=True)
            pv = _dot(p.astype(BF16), vals) / denom
            o = jnp.zeros((GRID_W, gw), F32)
            for h in range(ATT_GROUP):
                o = o + jnp.where(out_lane_head == h, pv[h * GRID_W:(h + 1) * GRID_W], 0.0)
            gate = g_ref[0, pl.ds(q_tok, GRID_W), lanes].astype(F32)
            o = o * _silu_from_half(gate)
            o_ref[0, pl.ds(q_tok, GRID_W), lanes] = o.astype(o_ref.dtype)
        return carry

    lax.fori_loop(0, ATT_ROWS_PER_STEP, one_row, 0, unroll=ATT_UNROLL)


def _attention(q, k, v, g, rpb):
    batch, seq, width = q.shape
    rows = seq // GRID_W
    kr = min(WIN_ROWS, rows)
    assert rows % ATT_ROWS_PER_STEP == 0 and ATT_ROWS_PER_STEP % ATT_UNROLL == 0
    assert kr == WIN_ROWS and 2 * GRID_W == LANES
    gen = _attention_bias_rows(rpb)
    win = jnp.asarray(_attention_window_mask())
    blk = ATT_ROWS_PER_STEP * GRID_W
    tile = pl.BlockSpec((1, blk, width), lambda b, i: (b, i, 0))
    assert rows >= ATT_ROWS_PER_STEP + kr
    halo = pl.BlockSpec((pl.Element(1), pl.Element((ATT_ROWS_PER_STEP + kr) * GRID_W), pl.Element(width)),
                        lambda b, i: (b, _attention_first_key_row(i, rows, kr) * GRID_W, 0))
    return pl.pallas_call(
        functools.partial(_attention_kernel, rows=rows, kr=kr),
        grid=(batch, rows // ATT_ROWS_PER_STEP),
        in_specs=[tile, halo, halo, tile,
                  pl.BlockSpec(gen.shape, lambda b, i: (0, 0, 0)),
                  pl.BlockSpec(win.shape, lambda b, i: (0, 0))],
        out_specs=tile,
        out_shape=jax.ShapeDtypeStruct((batch, seq, width), BF16),
        scratch_shapes=[pltpu.VMEM((2 * ATT_GROUPS * ATT_BIAS_TILES, ATT_GROUP * GRID_W, 2 * GRID_W), F32)],
        compiler_params=pltpu.CompilerParams(
            dimension_semantics=("arbitrary", "arbitrary"), vmem_limit_bytes=VMEM_LIMIT_BYTES),
        name="nbr_attention",
    )(q, k, v, g, gen, win)


def _hgrn_fine_levels():
    sizes = []
    c = HG_GROUP
    while c >= 2:
        sizes.append(c)
        c //= 2
    return sizes


def _hgrn_group_pairs(direction, chunk):
    n = chunk // HG_GROUP
    if direction == 0:
        return [(t, g) for t in range(1, n) for g in range(t)]
    return [(t, g) for t in range(n - 2, -1, -1) for g in range(t + 1, n)]


def _hgrn_constants(chunk):
    t = np.arange(chunk)[:, None]
    u = np.arange(chunk)[None, :]
    blocks = [(u <= t)]
    masks = []
    for c in _hgrn_fine_levels():
        half = c // 2
        mid = (t // c) * c + half
        upper = (t % c) >= half
        if 2 < c < HG_GROUP:
            blocks.append(np.where(upper, (u >= mid) & (u <= t), (u > t) & (u <= mid - 1)))
        masks.append((t // c == u // c) & upper & ((u % c) < half))
    fwd_sums = np.stack(blocks).astype(np.float32)
    fwd_masks = np.stack(masks).astype(np.float32)
    twice = lambda m: np.concatenate([m.reshape(-1, chunk)] * 2, axis=1)
    sums = np.stack([twice(fwd_sums), twice(fwd_sums[:, ::-1, ::-1])])
    masks = np.concatenate([fwd_masks, fwd_masks[:, ::-1, ::-1]], axis=2)
    return sums, masks


def _block_diag(a, b):
    zero = jnp.zeros_like(a)
    return jnp.concatenate([jnp.concatenate([a, zero], axis=1),
                            jnp.concatenate([zero, b], axis=1)], axis=0)


def _hgrn_dec_rows(chunk):
    n_var = len(_hgrn_group_pairs(0, chunk)) * HG_GROUP
    n_mm = sum(c > 2 for c in _hgrn_fine_levels())
    edge, var, fine = 2 * chunk, 3 * chunk, 3 * chunk + n_var
    return edge, var, fine, fine + n_mm * chunk


def _hgrn_kernel(q_ref, zf_ref, zb_ref, i_ref, g_ref, lbf_ref, lbb_ref, gain_ref,
                 sums_ref, masks_ref, o_ref, of_scr, ob_scr, sf_scr, sb_scr,
                 k_scr0, k_scr1, dec_scr0, dec_scr1, *, layer, seq):
    chunk = HG_CHUNK
    pair = 2 * chunk
    n_pairs = seq // pair
    dh = HG_HEAD_DIM
    grp = HG_GROUP
    n_groups = chunk // grp
    fine = _hgrn_fine_levels()
    group_pairs = [_hgrn_group_pairs(direction, chunk) for direction in (0, 1)]
    r_edge, r_var, r_fine, _ = _hgrn_dec_rows(chunk)
    k_slots = (k_scr0, k_scr1)
    dec_slots = (dec_scr0, dec_scr1)
    row_in_group = lax.broadcasted_iota(jnp.int32, (grp, dh), 0)
    lane_group = lax.broadcasted_iota(jnp.int32, (grp, 2 * chunk), 1) // grp

    def lower_bound(logit_ref):
        logits = logit_ref[...].astype(F32)
        e = jnp.exp(logits - jnp.max(logits, axis=0, keepdims=True))
        return jnp.sum(e[:layer + 1], axis=0, keepdims=True) / jnp.sum(e, axis=0, keepdims=True)

    def gates(z, lb):
        f = lb + (1.0 - lb) * jax.nn.sigmoid(z.astype(F32))
        return jnp.log2(f), 1.0 - f

    def store_decays(slot, direction, g2):
        hi = g2.astype(BF16)
        lo = (g2 - hi.astype(F32)).astype(BF16)
        stacked = [jnp.concatenate([hi[c * chunk:(c + 1) * chunk], lo[c * chunk:(c + 1) * chunk]], axis=0)
                   for c in range(2)]
        raw = _dot(sums_ref[direction], jnp.concatenate(stacked, axis=1))
        dec = dec_slots[slot].at[direction]
        cum = raw[0:chunk]
        whole = cum[chunk - 1:chunk] if direction == 0 else cum[0:1]
        dec[0:chunk, :] = jnp.exp2(cum)
        dec[chunk:2 * chunk, :] = jnp.exp2(whole - cum)
        by_group = cum.reshape(n_groups, grp, 2 * dh)
        mid_row = grp // 2 - 1 if direction == 0 else grp // 2
        to_mid = -jnp.abs(by_group - by_group[:, mid_row:mid_row + 1, :])
        dec[r_fine:r_fine + chunk, :] = jnp.exp2(to_mid.reshape(chunk, 2 * dh))
        dec[r_fine + chunk:, :] = jnp.exp2(raw[chunk:])
        edge = grp - 1 if direction == 0 else 0
        edges = [cum[g * grp + edge:g * grp + edge + 1] for g in range(n_groups)]
        group = lambda g: cum[g * grp:(g + 1) * grp]
        dec[r_edge:r_var, :] = jnp.exp2(jnp.concatenate([edges[g] - group(g) for g in range(n_groups)], axis=0))
        dec[r_var:r_fine, :] = jnp.exp2(jnp.concatenate([group(t) - edges[g]
                                                         for t, g in group_pairs[direction]], axis=0))

    def fine_operand(direction, lvl, q, k, dec):
        c = fine[lvl]
        pos = row_in_group % c
        is_upper = (pos >= c // 2) if direction == 0 else (pos < c // 2)
        pick = lambda x, y: jnp.where(is_upper[None], x.reshape(n_groups, grp, dh),
                                      y.reshape(n_groups, grp, dh)).reshape(chunk, dh)
        if c == 2:
            return pick(q * (1.0 - k), k).astype(BF16)
        return (pick(q, k) * dec(r_fine + lvl * chunk, r_fine + (lvl + 1) * chunk)).astype(BF16)

    def chunk_part(slot, direction, c, q, v, base):
        rows = slice(c * chunk, (c + 1) * chunk)
        lanes = slice(c * dh, (c + 1) * dh)
        return dict(q=q[rows], k=k_slots[slot][direction, rows, :], v=v[rows],
                    dec=lambda r0, r1: dec_slots[slot][direction, r0:r1, lanes],
                    rows=pl.ds(pl.multiple_of(base + c * chunk, chunk), chunk))

    def pair_step(f, b):
        lhs = jnp.concatenate(
            [(jnp.concatenate([p["q"][t * grp:(t + 1) * grp] for t, _ in group_pairs[d]], axis=0)
              * p["dec"](r_var, r_fine)).astype(BF16) for d, p in enumerate((f, b))], axis=1)
        keys = [(p["k"] * p["dec"](r_edge, r_var)).astype(BF16) for p in (f, b)]
        r = _dot_nt(lhs, _block_diag(*keys))
        rows_of = [jnp.zeros((grp, 2 * chunk), F32)] * n_groups
        for i, ((tf, gf), (tb, gb)) in enumerate(zip(*group_pairs)):
            block = r[i * grp:(i + 1) * grp]
            rows_of[tf] = jnp.where(lane_group == gf, block, rows_of[tf])
            rows_of[tb] = jnp.where(lane_group == n_groups + gb, block, rows_of[tb])
        a = jnp.concatenate(rows_of, axis=0)
        for lvl in range(len(fine)):
            xf = fine_operand(0, lvl, f["q"], f["k"], f["dec"])
            xb = fine_operand(1, lvl, b["q"], b["k"], b["dec"])
            a = a + masks_ref[lvl] * _dot_nt(jnp.concatenate([xf, xb], axis=1), _block_diag(xf, xb))
        o = _dot(a.astype(BF16), _block_diag(f["v"], b["v"]))
        sf = sf_scr[...]
        sb = sb_scr[...]
        q_dec = jnp.concatenate([(f["q"] * f["dec"](0, chunk)).astype(BF16),
                                 (b["q"] * b["dec"](0, chunk)).astype(BF16)], axis=1)
        o = o + _dot_nt(q_dec, _block_diag(sf.astype(BF16), sb.astype(BF16)))
        diag_f = jnp.sum(f["q"] * f["k"], axis=-1, keepdims=True)
        diag_b = jnp.sum(b["q"] * b["k"], axis=-1, keepdims=True)
        of_scr[f["rows"], :] = o[:, :dh] + diag_f * f["v"].astype(F32)
        ob_scr[b["rows"], :] = o[:, dh:] + diag_b * b["v"].astype(F32)
        k_dec = _block_diag((f["k"] * f["dec"](chunk, 2 * chunk)).astype(BF16),
                            (b["k"] * b["dec"](chunk, 2 * chunk)).astype(BF16))
        v_t = jnp.concatenate([f["v"], b["v"]], axis=0).astype(F32).T.astype(BF16)
        u = _dot(v_t, k_dec)
        sf_scr[...] = sf * f["dec"](chunk - 1, chunk) + u[:, :dh]
        sb_scr[...] = sb * b["dec"](0, 1) + u[:, dh:]

    lb_f = lower_bound(lbf_ref)
    lb_b = lower_bound(lbb_ref)
    sf_scr[...] = jnp.zeros_like(sf_scr)
    sb_scr[...] = jnp.zeros_like(sb_scr)

    def bases(n):
        return (pl.multiple_of(n * pair, pair), pl.multiple_of((n_pairs - 1 - n) * pair, pair))

    def prepare(n, slot):
        for direction, base, z_ref, lb in zip((0, 1), bases(n), (zf_ref, zb_ref), (lb_f, lb_b)):
            g2, k = gates(z_ref[0, pl.ds(base, pair), :], lb)
            k_slots[slot][direction] = k
            store_decays(slot, direction, g2)

    def consume(n, slot):
        parts = []
        for direction, base in zip((0, 1), bases(n)):
            rows = pl.ds(base, pair)
            q = q_ref[0, rows, :].astype(F32)
            v = i_ref[0, rows, :]
            parts.append([chunk_part(slot, direction, c, q, v, base) for c in range(2)])
        pair_step(parts[0][0], parts[1][1])
        pair_step(parts[0][1], parts[1][0])

    def scan_body(m, carry):
        prepare(2 * m + 1, 1)
        consume(2 * m, 0)
        prepare(jnp.minimum(2 * m + 2, n_pairs - 1), 0)
        consume(2 * m + 1, 1)
        return carry

    prepare(0, 0)
    lax.fori_loop(0, n_pairs // 2, scan_body, 0, unroll=HG_SCAN_UNROLL)

    gain = gain_ref[...].astype(F32)

    def norm_body(n, carry):
        rows = pl.ds(pl.multiple_of(n * HG_NORM_ROWS, HG_NORM_ROWS), HG_NORM_ROWS)
        o = of_scr[rows, :] + ob_scr[rows, :]
        o = o * lax.rsqrt(jnp.mean(jnp.square(o), axis=-1, keepdims=True) + RMS_EPS) * gain
        gate = g_ref[0, rows, :].astype(F32)
        o_ref[0, rows, :] = (o * _silu_from_half(gate)).astype(o_ref.dtype)
        return carry

    lax.fori_loop(0, seq // HG_NORM_ROWS, norm_body, 0, unroll=4)


def _hgrn(q, zf, zb, i, g, lb_fwd_logits, lb_bwd_logits, gain_row, layer):
    batch, seq, width = q.shape
    assert width == HG_HEADS * HG_HEAD_DIM and seq % (4 * HG_NORM_ROWS) == 0
    assert seq % (4 * HG_CHUNK * HG_SCAN_UNROLL) == 0
    sums, masks = _hgrn_constants(HG_CHUNK)
    sums = jnp.asarray(sums, BF16)
    masks = jnp.asarray(masks, F32)
    n_layers = lb_fwd_logits.shape[0]
    n_dec_rows = _hgrn_dec_rows(HG_CHUNK)[-1]
    head = pl.BlockSpec((1, seq, HG_HEAD_DIM), lambda b, h: (b, 0, h))
    per_head_row = lambda n: pl.BlockSpec((n, HG_HEAD_DIM), lambda b, h: (0, h))
    whole = lambda a: pl.BlockSpec(a.shape, lambda b, h: (0,) * a.ndim)
    return pl.pallas_call(
        functools.partial(_hgrn_kernel, layer=layer, seq=seq),
        grid=(batch, HG_HEADS),
        in_specs=[head, head, head, head, head,
                  per_head_row(n_layers), per_head_row(n_layers), per_head_row(1),
                  whole(sums), whole(masks)],
        out_specs=head,
        out_shape=jax.ShapeDtypeStruct((batch, seq, width), BF16),
        scratch_shapes=[pltpu.VMEM((seq, HG_HEAD_DIM), F32), pltpu.VMEM((seq, HG_HEAD_DIM), F32),
                        pltpu.VMEM((HG_HEAD_DIM, HG_HEAD_DIM), F32),
                        pltpu.VMEM((HG_HEAD_DIM, HG_HEAD_DIM), F32),
                        pltpu.VMEM((2, 2 * HG_CHUNK, HG_HEAD_DIM), F32),
                        pltpu.VMEM((2, 2 * HG_CHUNK, HG_HEAD_DIM), F32),
                        pltpu.VMEM((2, n_dec_rows, 2 * HG_HEAD_DIM), F32),
                        pltpu.VMEM((2, n_dec_rows, 2 * HG_HEAD_DIM), F32)],
        compiler_params=pltpu.CompilerParams(
            dimension_semantics=("arbitrary", "arbitrary"), vmem_limit_bytes=VMEM_LIMIT_BYTES),
        name="hgrn2_scan",
    )(q, zf, zb, i, g, lb_fwd_logits, lb_bwd_logits, gain_row, sums, masks)


def _out_proj_kernel(oa_ref, oh_ref, x_ref, w_ref, b_ref, gain_ref, bias_ref, o_ref, w_scr, *, alpha):
    d_att = oa_ref.shape[1]
    _cast_weight_once(w_ref, w_scr)
    for r0 in range(0, o_ref.shape[0], OUT_PROJ_ROWS):
        rows = slice(r0, r0 + OUT_PROJ_ROWS)
        y = _dot(oa_ref[rows, :], w_scr[:d_att, :]) + _dot(oh_ref[rows, :], w_scr[d_att:, :]) + b_ref[...]
        r = alpha * x_ref[rows, :] + y
        mu = jnp.mean(r, axis=-1, keepdims=True)
        c = r - mu
        var = jnp.mean(jnp.square(c), axis=-1, keepdims=True)
        o_ref[rows, :] = (c * lax.rsqrt(var + LN_EPS) * gain_ref[...] + bias_ref[...]).astype(o_ref.dtype)


def _out_proj(o_a, o_h, x2d, w, b_row, gain_row, bias_row, alpha):
    n_tok, d_model = x2d.shape
    d_att, d_hg = o_a.shape[1], o_h.shape[1]
    row = pl.BlockSpec((1, d_model), lambda i: (0, 0))
    return pl.pallas_call(
        functools.partial(_out_proj_kernel, alpha=alpha),
        grid=(n_tok // OUT_PROJ_TM,),
        in_specs=[pl.BlockSpec((OUT_PROJ_TM, d_att), lambda i: (i, 0)),
                  pl.BlockSpec((OUT_PROJ_TM, d_hg), lambda i: (i, 0)),
                  pl.BlockSpec((OUT_PROJ_TM, d_model), lambda i: (i, 0)),
                  pl.BlockSpec((d_att + d_hg, d_model), lambda i: (0, 0), pipeline_mode=pl.Buffered(1)),
                  row, row, row],
        out_specs=pl.BlockSpec((OUT_PROJ_TM, d_model), lambda i: (i, 0)),
        out_shape=jax.ShapeDtypeStruct((n_tok, d_model), x2d.dtype),
        scratch_shapes=[pltpu.VMEM((d_att + d_hg, d_model), BF16)],
        compiler_params=pltpu.CompilerParams(
            dimension_semantics=("arbitrary",), vmem_limit_bytes=VMEM_LIMIT_BYTES),
        name="out_proj_layernorm",
    )(o_a, o_h, x2d, w, b_row, gain_row, bias_row)


def _layer(x, layer, depth, w_in, b_in, rpb, lb_fwd_logits, lb_bwd_logits, hg_norm_gain,
           w_out, b_out, ln_gain, ln_bias):
    batch, seq, d_model = x.shape
    x2d = x.reshape(batch * seq, d_model)
    slabs = _in_proj(x2d, w_in, b_in.reshape(1, -1))
    q_a, k_a, v_a, g_a, q_h, z_f, z_b, i_h, g_h = [s.reshape(batch, seq, SLAB) for s in slabs]
    o_a = _attention(q_a, k_a, v_a, g_a, rpb)
    o_h = _hgrn(q_h, z_f, z_b, i_h, g_h, lb_fwd_logits, lb_bwd_logits,
                hg_norm_gain.reshape(1, -1), layer)
    alpha = (2.0 * depth) ** 0.25
    out = _out_proj(o_a.reshape(batch * seq, SLAB), o_h.reshape(batch * seq, SLAB), x2d,
                    w_out, b_out.reshape(1, -1), ln_gain.reshape(1, -1),
                    ln_bias.reshape(1, -1), alpha)
    return out.reshape(batch, seq, d_model)


def kernel(x, w_in, b_in, rpb, lb_fwd_logits, lb_bwd_logits, hg_norm_gain, w_out, b_out, ln_gain, ln_bias):
    depth = w_in.shape[0]
    for layer in range(depth):
        x = _layer(x, layer, depth, w_in[layer], b_in[layer], rpb[layer], lb_fwd_logits,
                   lb_bwd_logits, hg_norm_gain[layer], w_out[layer], b_out[layer],
                   ln_gain[layer], ln_bias[layer])
    return x
```

```python
import functools

import numpy as np
import jax
import jax.numpy as jnp
from jax import lax
from jax.experimental import pallas as pl
from jax.experimental.pallas import tpu as pltpu

GRID_W = 64
ATT_HEADS = 8
ATT_HEAD_DIM = 64
HG_HEADS = 4
HG_HEAD_DIM = 128
WIN_ROWS = 8
WIN_COLS = 16
LN_EPS = 1e-5
RMS_EPS = 1e-6
N_SLABS = 9
SLAB = 512

LANES = 128
F32_SUBLANES = 8
VMEM_LIMIT_BYTES = 56 * 1024 * 1024
PROJ_TM = 512
OUT_PROJ_TM = 1024
OUT_PROJ_RING = 3
OUT_PROJ_ROWS = 256
ATT_ROWS_PER_STEP = 16
ATT_UNROLL = 4
ATT_GROUP = 4
ATT_BIAS_UNROLL = 8
HG_CHUNK = 64
HG_GROUP = F32_SUBLANES
HG_NORM_ROWS = 256
HG_SCAN_UNROLL = 16
MASK_VALUE = -1e30
LOG2_E = 1.4426950408889634
SLAB_SCALES = {1: ATT_HEAD_DIM ** -0.5 * LOG2_E, 3: 0.5, 8: 0.5}

BF16 = jnp.bfloat16
F32 = jnp.float32


def _silu_from_half(half_g):
    return half_g * (1.0 + jnp.tanh(half_g))


def _dot(a, b):
    return jnp.dot(a, b, preferred_element_type=F32)


def _dot_nt(a, b):
    return lax.dot_general(a, b, (((1,), (1,)), ((), ())), preferred_element_type=F32)


def _cast_weight_once(w_ref, w_scr):
    @pl.when(pl.program_id(0) == 0)
    def _():
        for c0 in range(0, w_ref.shape[1], SLAB):
            w_scr[:, c0:c0 + SLAB] = w_ref[:, c0:c0 + SLAB].astype(w_scr.dtype)


def _in_proj_kernel(x_ref, w_ref, b_ref, *refs):
    out_refs, w_scr = refs[:-1], refs[-1]
    _cast_weight_once(w_ref, w_scr)
    xb = x_ref[...].astype(BF16)
    for j, o_ref in enumerate(out_refs):
        cols = slice(j * SLAB, (j + 1) * SLAB)
        h = _dot(xb, w_scr[:, cols]) + b_ref[:, cols]
        if j in SLAB_SCALES:
            h = h * SLAB_SCALES[j]
        o_ref[...] = h.astype(o_ref.dtype)


def _in_proj(x2d, w, b_row):
    n_tok, d_model = x2d.shape
    d_in = w.shape[1]
    assert d_in == N_SLABS * SLAB and n_tok % PROJ_TM == 0
    return pl.pallas_call(
        _in_proj_kernel,
        grid=(n_tok // PROJ_TM,),
        in_specs=[
            pl.BlockSpec((PROJ_TM, d_model), lambda i: (i, 0)),
            pl.BlockSpec((d_model, d_in), lambda i: (0, 0), pipeline_mode=pl.Buffered(1)),
            pl.BlockSpec((1, d_in), lambda i: (0, 0)),
        ],
        out_specs=[pl.BlockSpec((PROJ_TM, SLAB), lambda i: (i, 0))] * N_SLABS,
        out_shape=[jax.ShapeDtypeStruct((n_tok, SLAB), BF16)] * N_SLABS,
        scratch_shapes=[pltpu.VMEM((d_model, d_in), BF16)],
        compiler_params=pltpu.CompilerParams(
            dimension_semantics=("arbitrary",), vmem_limit_bytes=VMEM_LIMIT_BYTES),
        name="in_proj",
    )(x2d, w, b_row)


ATT_REL_ROWS = 2 * WIN_ROWS - 1
ATT_BIAS_TILES = (ATT_REL_ROWS - 1) // 2
ATT_GROUPS = ATT_HEADS // ATT_GROUP


def _attention_bias_rows(rpb):
    n_rel_cols = 2 * WIN_COLS - 1
    scaled = rpb.astype(F32) * LOG2_E
    gap = jnp.zeros((ATT_HEADS, ATT_BIAS_TILES, (LANES - 2 * n_rel_cols) // 2), F32)
    per_parity = []
    for p in range(2):
        first = scaled[:, p:p + 2 * ATT_BIAS_TILES - 1:2]
        second = scaled[:, p + 1:p + 2 * ATT_BIAS_TILES:2]
        per_parity.append(jnp.concatenate(
            [first[..., WIN_COLS - 1:], gap, second, gap, first[..., :WIN_COLS - 1]], axis=-1))
    gen = jnp.stack(per_parity)
    gen = gen.reshape(2, ATT_GROUPS, ATT_GROUP, ATT_BIAS_TILES, LANES).transpose(0, 1, 3, 2, 4)
    return gen.reshape(2 * ATT_GROUPS * ATT_BIAS_TILES * ATT_GROUP, 1, LANES)


def _attention_window_mask():
    qc = np.arange(GRID_W)[:, None]
    kc = np.arange(LANES)[None, :] % GRID_W
    col_start = np.clip(qc - WIN_COLS // 2, 0, GRID_W - WIN_COLS)
    return ((kc >= col_start) & (kc < col_start + WIN_COLS)).astype(np.float32)


def _attention_first_key_row(step, rows, kr):
    return jnp.clip(step * ATT_ROWS_PER_STEP - kr // 2, 0, rows - ATT_ROWS_PER_STEP - kr)


def _attention_kernel(q_ref, k_ref, v_ref, g_ref, gen_ref, win_ref, o_ref, bias_scr, *, rows, kr):
    step = pl.program_id(1)
    gw = ATT_GROUP * ATT_HEAD_DIM
    row_head = lax.broadcasted_iota(jnp.int32, (ATT_GROUP * GRID_W, gw), 0) // GRID_W
    lane_head = lax.broadcasted_iota(jnp.int32, (ATT_GROUP * GRID_W, gw), 1) // ATT_HEAD_DIM
    own_head = row_head == lane_head
    out_lane_head = lax.broadcasted_iota(jnp.int32, (GRID_W, gw), 1) // ATT_HEAD_DIM

    @pl.when((pl.program_id(0) == 0) & (step == 0))
    def _build_bias():
        in_window = win_ref[...] > 0.0

        def one_block(n, carry):
            gen = jnp.broadcast_to(gen_ref[n], (GRID_W, LANES))
            toeplitz = pltpu.roll(gen, 0, 1, stride=1, stride_axis=0)
            head_rows = pl.ds(pl.multiple_of((n % ATT_GROUP) * GRID_W, GRID_W), GRID_W)
            bias_scr[n // ATT_GROUP, head_rows, :] = jnp.where(in_window, toeplitz, MASK_VALUE)
            return carry

        lax.fori_loop(0, gen_ref.shape[0], one_block, 0, unroll=ATT_BIAS_UNROLL)

    def one_row(j, carry):
        r = step * ATT_ROWS_PER_STEP + j
        row_start = jnp.clip(r - kr // 2, 0, rows - kr)
        variant = row_start - r + (WIN_ROWS - 1)
        q_tok = pl.multiple_of(j * GRID_W, GRID_W)
        k_tok = pl.multiple_of((row_start - _attention_first_key_row(step, rows, kr)) * GRID_W, GRID_W)
        for grp in range(ATT_GROUPS):
            lanes = slice(grp * gw, (grp + 1) * gw)
            q = q_ref[0, pl.ds(q_tok, GRID_W), lanes]
            q_bd = jnp.where(own_head, jnp.concatenate([q] * ATT_GROUP, axis=0), 0)
            keys = k_ref[0, pl.ds(k_tok, kr * GRID_W), lanes]
            vals = v_ref[0, pl.ds(k_tok, kr * GRID_W), lanes]
            tile0 = ((variant % 2) * ATT_GROUPS + grp) * ATT_BIAS_TILES + variant // 2
            bias = jnp.concatenate([bias_scr[tile0 + t] for t in range(kr // 2)], axis=1)
            s = _dot_nt(q_bd, keys) + bias
            m = jnp.max(s, axis=-1, keepdims=True)
            p = jnp.exp2(s - m)
            denom = jnp.sum(p, axis=-1, keepdims=True)
            pv = _dot(p.astype(BF16), vals) / denom
            o = jnp.zeros((GRID_W, gw), F32)
            for h in range(ATT_GROUP):
                o = o + jnp.where(out_lane_head == h, pv[h * GRID_W:(h + 1) * GRID_W], 0.0)
            gate = g_ref[0, pl.ds(q_tok, GRID_W), lanes].astype(F32)
            o = o * _silu_from_half(gate)
            o_ref[0, pl.ds(q_tok, GRID_W), lanes] = o.astype(o_ref.dtype)
        return carry

    lax.fori_loop(0, ATT_ROWS_PER_STEP, one_row, 0, unroll=ATT_UNROLL)


def _attention(q, k, v, g, rpb):
    batch, seq, width = q.shape
    rows = seq // GRID_W
    kr = min(WIN_ROWS, rows)
    assert rows % ATT_ROWS_PER_STEP == 0 and ATT_ROWS_PER_STEP % ATT_UNROLL == 0
    assert kr == WIN_ROWS and 2 * GRID_W == LANES
    gen = _attention_bias_rows(rpb)
    win = jnp.asarray(_attention_window_mask())
    blk = ATT_ROWS_PER_STEP * GRID_W
    tile = pl.BlockSpec((1, blk, width), lambda b, i: (b, i, 0))
    assert rows >= ATT_ROWS_PER_STEP + kr
    halo = pl.BlockSpec((pl.Element(1), pl.Element((ATT_ROWS_PER_STEP + kr) * GRID_W), pl.Element(width)),
                        lambda b, i: (b, _attention_first_key_row(i, rows, kr) * GRID_W, 0))
    return pl.pallas_call(
        functools.partial(_attention_kernel, rows=rows, kr=kr),
        grid=(batch, rows // ATT_ROWS_PER_STEP),
        in_specs=[tile, halo, halo, tile,
                  pl.BlockSpec(gen.shape, lambda b, i: (0, 0, 0)),
                  pl.BlockSpec(win.shape, lambda b, i: (0, 0))],
        out_specs=tile,
        out_shape=jax.ShapeDtypeStruct((batch, seq, width), BF16),
        scratch_shapes=[pltpu.VMEM((2 * ATT_GROUPS * ATT_BIAS_TILES, ATT_GROUP * GRID_W, 2 * GRID_W), F32)],
        compiler_params=pltpu.CompilerParams(
            dimension_semantics=("arbitrary", "arbitrary"), vmem_limit_bytes=VMEM_LIMIT_BYTES),
        name="nbr_attention",
    )(q, k, v, g, gen, win)


def _hgrn_fine_levels():
    sizes = []
    c = HG_GROUP
    while c >= 2:
        sizes.append(c)
        c //= 2
    return sizes


def _hgrn_group_pairs(direction, chunk):
    n = chunk // HG_GROUP
    if direction == 0:
        return [(t, g) for t in range(1, n) for g in range(t)]
    return [(t, g) for t in range(n - 2, -1, -1) for g in range(t + 1, n)]


def _hgrn_constants(chunk):
    t = np.arange(chunk)[:, None]
    u = np.arange(chunk)[None, :]
    blocks = [(u <= t)]
    masks = []
    for c in _hgrn_fine_levels():
        half = c // 2
        mid = (t // c) * c + half
        upper = (t % c) >= half
        if 2 < c < HG_GROUP:
            blocks.append(np.where(upper, (u >= mid) & (u <= t), (u > t) & (u <= mid - 1)))
        masks.append((t // c == u // c) & upper & ((u % c) < half))
    fwd_sums = np.stack(blocks).astype(np.float32)
    fwd_masks = np.stack(masks).astype(np.float32)
    twice = lambda m: np.concatenate([m.reshape(-1, chunk)] * 2, axis=1)
    sums = np.stack([twice(fwd_sums), twice(fwd_sums[:, ::-1, ::-1])])
    masks = np.concatenate([fwd_masks, fwd_masks[:, ::-1, ::-1]], axis=2)
    return sums, masks


def _block_diag(a, b):
    zero = jnp.zeros_like(a)
    return jnp.concatenate([jnp.concatenate([a, zero], axis=1),
                            jnp.concatenate([zero, b], axis=1)], axis=0)


def _hgrn_dec_rows(chunk):
    n_var = len(_hgrn_group_pairs(0, chunk)) * HG_GROUP
    n_mm = sum(c > 2 for c in _hgrn_fine_levels())
    edge, var, fine = 2 * chunk, 3 * chunk, 3 * chunk + n_var
    return edge, var, fine, fine + n_mm * chunk


def _hgrn_kernel(q_ref, zf_ref, zb_ref, i_ref, g_ref, lbf_ref, lbb_ref, gain_ref,
                 sums_ref, masks_ref, o_ref, of_scr, ob_scr, sf_scr, sb_scr,
                 k_scr0, k_scr1, dec_scr0, dec_scr1, *, layer, seq):
    chunk = HG_CHUNK
    pair = 2 * chunk
    n_pairs = seq // pair
    dh = HG_HEAD_DIM
    grp = HG_GROUP
    n_groups = chunk // grp
    fine = _hgrn_fine_levels()
    group_pairs = [_hgrn_group_pairs(direction, chunk) for direction in (0, 1)]
    r_edge, r_var, r_fine, _ = _hgrn_dec_rows(chunk)
    k_slots = (k_scr0, k_scr1)
    dec_slots = (dec_scr0, dec_scr1)
    row_in_group = lax.broadcasted_iota(jnp.int32, (grp, dh), 0)
    lane_group = lax.broadcasted_iota(jnp.int32, (grp, 2 * chunk), 1) // grp

    def lower_bound(logit_ref):
        logits = logit_ref[...].astype(F32)
        e = jnp.exp(logits - jnp.max(logits, axis=0, keepdims=True))
        return jnp.sum(e[:layer + 1], axis=0, keepdims=True) / jnp.sum(e, axis=0, keepdims=True)

    def gates(z, lb):
        f = lb + (1.0 - lb) * jax.nn.sigmoid(z.astype(F32))
        return jnp.log2(f), 1.0 - f

    def store_decays(slot, direction, g2):
        hi = g2.astype(BF16)
        lo = (g2 - hi.astype(F32)).astype(BF16)
        stacked = [jnp.concatenate([hi[c * chunk:(c + 1) * chunk], lo[c * chunk:(c + 1) * chunk]], axis=0)
                   for c in range(2)]
        raw = _dot(sums_ref[direction], jnp.concatenate(stacked, axis=1))
        dec = dec_slots[slot].at[direction]
        cum = raw[0:chunk]
        whole = cum[chunk - 1:chunk] if direction == 0 else cum[0:1]
        dec[0:chunk, :] = jnp.exp2(cum)
        dec[chunk:2 * chunk, :] = jnp.exp2(whole - cum)
        by_group = cum.reshape(n_groups, grp, 2 * dh)
        mid_row = grp // 2 - 1 if direction == 0 else grp // 2
        to_mid = -jnp.abs(by_group - by_group[:, mid_row:mid_row + 1, :])
        dec[r_fine:r_fine + chunk, :] = jnp.exp2(to_mid.reshape(chunk, 2 * dh))
        dec[r_fine + chunk:, :] = jnp.exp2(raw[chunk:])
        edge = grp - 1 if direction == 0 else 0
        edges = [cum[g * grp + edge:g * grp + edge + 1] for g in range(n_groups)]
        group = lambda g: cum[g * grp:(g + 1) * grp]
        dec[r_edge:r_var, :] = jnp.exp2(jnp.concatenate([edges[g] - group(g) for g in range(n_groups)], axis=0))
        dec[r_var:r_fine, :] = jnp.exp2(jnp.concatenate([group(t) - edges[g]
                                                         for t, g in group_pairs[direction]], axis=0))

    def fine_operand(direction, lvl, q, k, dec):
        c = fine[lvl]
        pos = row_in_group % c
        is_upper = (pos >= c // 2) if direction == 0 else (pos < c // 2)
        pick = lambda x, y: jnp.where(is_upper[None], x.reshape(n_groups, grp, dh),
                                      y.reshape(n_groups, grp, dh)).reshape(chunk, dh)
        if c == 2:
            return pick(q * (1.0 - k), k).astype(BF16)
        return (pick(q, k) * dec(r_fine + lvl * chunk, r_fine + (lvl + 1) * chunk)).astype(BF16)

    def chunk_part(slot, direction, c, q, v, base):
        rows = slice(c * chunk, (c + 1) * chunk)
        lanes = slice(c * dh, (c + 1) * dh)
        return dict(q=q[rows], k=k_slots[slot][direction, rows, :], v=v[rows],
                    dec=lambda r0, r1: dec_slots[slot][direction, r0:r1, lanes],
                    rows=pl.ds(pl.multiple_of(base + c * chunk, chunk), chunk))

    def pair_step(f, b):
        lhs = jnp.concatenate(
            [(jnp.concatenate([p["q"][t * grp:(t + 1) * grp] for t, _ in group_pairs[d]], axis=0)
              * p["dec"](r_var, r_fine)).astype(BF16) for d, p in enumerate((f, b))], axis=1)
        keys = [(p["k"] * p["dec"](r_edge, r_var)).astype(BF16) for p in (f, b)]
        r = _dot_nt(lhs, _block_diag(*keys))
        rows_of = [jnp.zeros((grp, 2 * chunk), F32)] * n_groups
        for i, ((tf, gf), (tb, gb)) in enumerate(zip(*group_pairs)):
            block = r[i * grp:(i + 1) * grp]
            rows_of[tf] = jnp.where(lane_group == gf, block, rows_of[tf])
            rows_of[tb] = jnp.where(lane_group == n_groups + gb, block, rows_of[tb])
        a = jnp.concatenate(rows_of, axis=0)
        for lvl in range(len(fine)):
            xf = fine_operand(0, lvl, f["q"], f["k"], f["dec"])
            xb = fine_operand(1, lvl, b["q"], b["k"], b["dec"])
            a = a + masks_ref[lvl] * _dot_nt(jnp.concatenate([xf, xb], axis=1), _block_diag(xf, xb))
        o = _dot(a.astype(BF16), _block_diag(f["v"], b["v"]))
        sf = sf_scr[...]
        sb = sb_scr[...]
        q_dec = jnp.concatenate([(f["q"] * f["dec"](0, chunk)).astype(BF16),
                                 (b["q"] * b["dec"](0, chunk)).astype(BF16)], axis=1)
        o = o + _dot_nt(q_dec, _block_diag(sf.astype(BF16), sb.astype(BF16)))
        diag_f = jnp.sum(f["q"] * f["k"], axis=-1, keepdims=True)
        diag_b = jnp.sum(b["q"] * b["k"], axis=-1, keepdims=True)
        of_scr[f["rows"], :] = o[:, :dh] + diag_f * f["v"].astype(F32)
        ob_scr[b["rows"], :] = o[:, dh:] + diag_b * b["v"].astype(F32)
        k_dec = _block_diag((f["k"] * f["dec"](chunk, 2 * chunk)).astype(BF16),
                            (b["k"] * b["dec"](chunk, 2 * chunk)).astype(BF16))
        v_t = jnp.concatenate([f["v"], b["v"]], axis=0).astype(F32).T.astype(BF16)
        u = _dot(v_t, k_dec)
        sf_scr[...] = sf * f["dec"](chunk - 1, chunk) + u[:, :dh]
        sb_scr[...] = sb * b["dec"](0, 1) + u[:, dh:]

    lb_f = lower_bound(lbf_ref)
    lb_b = lower_bound(lbb_ref)
    sf_scr[...] = jnp.zeros_like(sf_scr)
    sb_scr[...] = jnp.zeros_like(sb_scr)

    def bases(n):
        return (pl.multiple_of(n * pair, pair), pl.multiple_of((n_pairs - 1 - n) * pair, pair))

    def prepare(n, slot):
        for direction, base, z_ref, lb in zip((0, 1), bases(n), (zf_ref, zb_ref), (lb_f, lb_b)):
            g2, k = gates(z_ref[0, pl.ds(base, pair), :], lb)
            k_slots[slot][direction] = k
            store_decays(slot, direction, g2)

    def consume(n, slot):
        parts = []
        for direction, base in zip((0, 1), bases(n)):
            rows = pl.ds(base, pair)
            q = q_ref[0, rows, :].astype(F32)
            v = i_ref[0, rows, :]
            parts.append([chunk_part(slot, direction, c, q, v, base) for c in range(2)])
        pair_step(parts[0][0], parts[1][1])
        pair_step(parts[0][1], parts[1][0])

    def scan_body(m, carry):
        prepare(2 * m + 1, 1)
        consume(2 * m, 0)
        prepare(jnp.minimum(2 * m + 2, n_pairs - 1), 0)
        consume(2 * m + 1, 1)
        return carry

    prepare(0, 0)
    lax.fori_loop(0, n_pairs // 2, scan_body, 0, unroll=HG_SCAN_UNROLL)

    gain = gain_ref[...].astype(F32)

    def norm_body(n, carry):
        rows = pl.ds(pl.multiple_of(n * HG_NORM_ROWS, HG_NORM_ROWS), HG_NORM_ROWS)
        o = of_scr[rows, :] + ob_scr[rows, :]
        o = o * lax.rsqrt(jnp.mean(jnp.square(o), axis=-1, keepdims=True) + RMS_EPS) * gain
        gate = g_ref[0, rows, :].astype(F32)
        o_ref[0, rows, :] = (o * _silu_from_half(gate)).astype(o_ref.dtype)
        return carry

    lax.fori_loop(0, seq // HG_NORM_ROWS, norm_body, 0, unroll=4)


def _hgrn(q, zf, zb, i, g, lb_fwd_logits, lb_bwd_logits, gain_row, layer):
    batch, seq, width = q.shape
    assert width == HG_HEADS * HG_HEAD_DIM and seq % (4 * HG_NORM_ROWS) == 0
    assert seq % (4 * HG_CHUNK * HG_SCAN_UNROLL) == 0
    sums, masks = _hgrn_constants(HG_CHUNK)
    sums = jnp.asarray(sums, BF16)
    masks = jnp.asarray(masks, F32)
    n_layers = lb_fwd_logits.shape[0]
    n_dec_rows = _hgrn_dec_rows(HG_CHUNK)[-1]
    head = pl.BlockSpec((1, seq, HG_HEAD_DIM), lambda b, h: (b, 0, h))
    per_head_row = lambda n: pl.BlockSpec((n, HG_HEAD_DIM), lambda b, h: (0, h))
    whole = lambda a: pl.BlockSpec(a.shape, lambda b, h: (0,) * a.ndim)
    return pl.pallas_call(
        functools.partial(_hgrn_kernel, layer=layer, seq=seq),
        grid=(batch, HG_HEADS),
        in_specs=[head, head, head, head, head,
                  per_head_row(n_layers), per_head_row(n_layers), per_head_row(1),
                  whole(sums), whole(masks)],
        out_specs=head,
        out_shape=jax.ShapeDtypeStruct((batch, seq, width), BF16),
        scratch_shapes=[pltpu.VMEM((seq, HG_HEAD_DIM), F32), pltpu.VMEM((seq, HG_HEAD_DIM), F32),
                        pltpu.VMEM((HG_HEAD_DIM, HG_HEAD_DIM), F32),
                        pltpu.VMEM((HG_HEAD_DIM, HG_HEAD_DIM), F32),
                        pltpu.VMEM((2, 2 * HG_CHUNK, HG_HEAD_DIM), F32),
                        pltpu.VMEM((2, 2 * HG_CHUNK, HG_HEAD_DIM), F32),
                        pltpu.VMEM((2, n_dec_rows, 2 * HG_HEAD_DIM), F32),
                        pltpu.VMEM((2, n_dec_rows, 2 * HG_HEAD_DIM), F32)],
        compiler_params=pltpu.CompilerParams(
            dimension_semantics=("arbitrary", "arbitrary"), vmem_limit_bytes=VMEM_LIMIT_BYTES),
        name="hgrn2_scan",
    )(q, zf, zb, i, g, lb_fwd_logits, lb_bwd_logits, gain_row, sums, masks)


def _out_proj_kernel(oa_ref, oh_ref, x_hbm, w_ref, b_ref, gain_ref, bias_ref, o_ref, w_scr, x_ring, x_sem,
                     *, alpha):
    d_att = oa_ref.shape[1]
    tm = o_ref.shape[0]
    step = pl.program_id(0)
    n_steps = pl.num_programs(0)

    def fetch(s):
        slot = s % OUT_PROJ_RING
        return pltpu.make_async_copy(x_hbm.at[pl.ds(pl.multiple_of(s * tm, tm), tm)],
                                     x_ring.at[slot], x_sem.at[slot])

    @pl.when(step == 0)
    def _():
        for s in range(OUT_PROJ_RING - 1):
            fetch(s).start()

    @pl.when(step + (OUT_PROJ_RING - 1) < n_steps)
    def _():
        fetch(step + (OUT_PROJ_RING - 1)).start()

    _cast_weight_once(w_ref, w_scr)
    fetch(step).wait()
    x_ref = x_ring.at[step % OUT_PROJ_RING]
    for r0 in range(0, o_ref.shape[0], OUT_PROJ_ROWS):
        rows = slice(r0, r0 + OUT_PROJ_ROWS)
        y = _dot(oa_ref[rows, :], w_scr[:d_att, :]) + _dot(oh_ref[rows, :], w_scr[d_att:, :]) + b_ref[...]
        r = alpha * x_ref[rows, :] + y
        mu = jnp.mean(r, axis=-1, keepdims=True)
        c = r - mu
        var = jnp.mean(jnp.square(c), axis=-1, keepdims=True)
        o_ref[rows, :] = (c * lax.rsqrt(var + LN_EPS) * gain_ref[...] + bias_ref[...]).astype(o_ref.dtype)


def _out_proj(o_a, o_h, x2d, w, b_row, gain_row, bias_row, alpha):
    n_tok, d_model = x2d.shape
    d_att, d_hg = o_a.shape[1], o_h.shape[1]
    row = pl.BlockSpec((1, d_model), lambda i: (0, 0))
    return pl.pallas_call(
        functools.partial(_out_proj_kernel, alpha=alpha),
        grid=(n_tok // OUT_PROJ_TM,),
        in_specs=[pl.BlockSpec((OUT_PROJ_TM, d_att), lambda i: (i, 0)),
                  pl.BlockSpec((OUT_PROJ_TM, d_hg), lambda i: (i, 0)),
                  pl.BlockSpec(memory_space=pl.ANY),
                  pl.BlockSpec((d_att + d_hg, d_model), lambda i: (0, 0), pipeline_mode=pl.Buffered(1)),
                  row, row, row],
        out_specs=pl.BlockSpec((OUT_PROJ_TM, d_model), lambda i: (i, 0)),
        out_shape=jax.ShapeDtypeStruct((n_tok, d_model), x2d.dtype),
        scratch_shapes=[pltpu.VMEM((d_att + d_hg, d_model), BF16),
                        pltpu.VMEM((OUT_PROJ_RING, OUT_PROJ_TM, d_model), x2d.dtype),
                        pltpu.SemaphoreType.DMA((OUT_PROJ_RING,))],
        compiler_params=pltpu.CompilerParams(
            dimension_semantics=("arbitrary",), vmem_limit_bytes=VMEM_LIMIT_BYTES),
        name="out_proj_layernorm",
    )(o_a, o_h, x2d, w, b_row, gain_row, bias_row)


def _layer(x, layer, depth, w_in, b_in, rpb, lb_fwd_logits, lb_bwd_logits, hg_norm_gain,
           w_out, b_out, ln_gain, ln_bias):
    batch, seq, d_model = x.shape
    x2d = x.reshape(batch * seq, d_model)
    slabs = _in_proj(x2d, w_in, b_in.reshape(1, -1))
    q_a, k_a, v_a, g_a, q_h, z_f, z_b, i_h, g_h = [s.reshape(batch, seq, SLAB) for s in slabs]
    o_a = _attention(q_a, k_a, v_a, g_a, rpb)
    o_h = _hgrn(q_h, z_f, z_b, i_h, g_h, lb_fwd_logits, lb_bwd_logits,
                hg_norm_gain.reshape(1, -1), layer)
    alpha = (2.0 * depth) ** 0.25
    out = _out_proj(o_a.reshape(batch * seq, SLAB), o_h.reshape(batch * seq, SLAB), x2d,
                    w_out, b_out.reshape(1, -1), ln_gain.reshape(1, -1),
                    ln_bias.reshape(1, -1), alpha)
    return out.reshape(batch, seq, d_model)


def kernel(x, w_in, b_in, rpb, lb_fwd_logits, lb_bwd_logits, hg_norm_gain, w_out, b_out, ln_gain, ln_bias):
    depth = w_in.shape[0]
    for layer in range(depth):
        x = _layer(x, layer, depth, w_in[layer], b_in[layer], rpb[layer], lb_fwd_logits,
                   lb_bwd_logits, hg_norm_gain[layer], w_out[layer], b_out[layer],
                   ln_gain[layer], ln_bias[layer])
    return x
```

```python
import functools

import numpy as np
import jax
import jax.numpy as jnp
from jax import lax
from jax.experimental import pallas as pl
from jax.experimental.pallas import tpu as pltpu

GRID_W = 64
ATT_HEADS = 8
ATT_HEAD_DIM = 64
HG_HEADS = 4
HG_HEAD_DIM = 128
WIN_ROWS = 8
WIN_COLS = 16
LN_EPS = 1e-5
RMS_EPS = 1e-6
N_SLABS = 9
SLAB = 512

LANES = 128
F32_SUBLANES = 8
VMEM_LIMIT_BYTES = 56 * 1024 * 1024
PROJ_TM = 512
OUT_PROJ_TM = 1024
OUT_PROJ_RING = 3
OUT_PROJ_ROWS = 256
ATT_ROWS_PER_STEP = 16
ATT_UNROLL = 4
ATT_GROUP = 4
ATT_BIAS_UNROLL = 8
HG_CHUNK = 64
HG_GROUP = F32_SUBLANES
HG_NORM_ROWS = 256
HG_SCAN_UNROLL = 16
MASK_VALUE = -1e30
LOG2_E = 1.4426950408889634
SLAB_SCALES = {1: ATT_HEAD_DIM ** -0.5 * LOG2_E, 3: 0.5, 8: 0.5}

BF16 = jnp.bfloat16
F32 = jnp.float32


def _silu_from_half(half_g):
    return half_g * (1.0 + jnp.tanh(half_g))


def _dot(a, b):
    return jnp.dot(a, b, preferred_element_type=F32)


def _dot_nt(a, b):
    return lax.dot_general(a, b, (((1,), (1,)), ((), ())), preferred_element_type=F32)


def _cast_weight_once(w_ref, w_scr):
    @pl.when(pl.program_id(0) == 0)
    def _():
        for c0 in range(0, w_ref.shape[1], SLAB):
            w_scr[:, c0:c0 + SLAB] = w_ref[:, c0:c0 + SLAB].astype(w_scr.dtype)


def _in_proj_kernel(x_ref, w_ref, b_ref, *refs):
    out_refs, w_scr = refs[:-1], refs[-1]
    _cast_weight_once(w_ref, w_scr)
    xb = x_ref[...].astype(BF16)
    for j, o_ref in enumerate(out_refs):
        cols = slice(j * SLAB, (j + 1) * SLAB)
        h = _dot(xb, w_scr[:, cols]) + b_ref[:, cols]
        if j in SLAB_SCALES:
            h = h * SLAB_SCALES[j]
        o_ref[...] = h.astype(o_ref.dtype)


def _in_proj(x2d, w, b_row):
    n_tok, d_model = x2d.shape
    d_in = w.shape[1]
    assert d_in == N_SLABS * SLAB and n_tok % PROJ_TM == 0
    return pl.pallas_call(
        _in_proj_kernel,
        grid=(n_tok // PROJ_TM,),
        in_specs=[
            pl.BlockSpec((PROJ_TM, d_model), lambda i: (i, 0)),
            pl.BlockSpec((d_model, d_in), lambda i: (0, 0), pipeline_mode=pl.Buffered(1)),
            pl.BlockSpec((1, d_in), lambda i: (0, 0)),
        ],
        out_specs=[pl.BlockSpec((PROJ_TM, SLAB), lambda i: (i, 0))] * N_SLABS,
        out_shape=[jax.ShapeDtypeStruct((n_tok, SLAB), BF16)] * N_SLABS,
        scratch_shapes=[pltpu.VMEM((d_model, d_in), BF16)],
        compiler_params=pltpu.CompilerParams(
            dimension_semantics=("arbitrary",), vmem_limit_bytes=VMEM_LIMIT_BYTES),
        name="in_proj",
    )(x2d, w, b_row)


ATT_REL_ROWS = 2 * WIN_ROWS - 1
ATT_BIAS_TILES = (ATT_REL_ROWS - 1) // 2
ATT_GROUPS = ATT_HEADS // ATT_GROUP


def _attention_bias_rows(rpb):
    n_rel_cols = 2 * WIN_COLS - 1
    scaled = rpb.astype(F32) * LOG2_E
    gap = jnp.zeros((ATT_HEADS, ATT_BIAS_TILES, (LANES - 2 * n_rel_cols) // 2), F32)
    per_parity = []
    for p in range(2):
        first = scaled[:, p:p + 2 * ATT_BIAS_TILES - 1:2]
        second = scaled[:, p + 1:p + 2 * ATT_BIAS_TILES:2]
        per_parity.append(jnp.concatenate(
            [first[..., WIN_COLS - 1:], gap, second, gap, first[..., :WIN_COLS - 1]], axis=-1))
    gen = jnp.stack(per_parity)
    gen = gen.reshape(2, ATT_GROUPS, ATT_GROUP, ATT_BIAS_TILES, LANES).transpose(0, 1, 3, 2, 4)
    return gen.reshape(2 * ATT_GROUPS * ATT_BIAS_TILES * ATT_GROUP, 1, LANES)


def _attention_window_mask():
    qc = np.arange(GRID_W)[:, None]
    kc = np.arange(LANES)[None, :] % GRID_W
    col_start = np.clip(qc - WIN_COLS // 2, 0, GRID_W - WIN_COLS)
    return ((kc >= col_start) & (kc < col_start + WIN_COLS)).astype(np.float32)


def _attention_first_key_row(step, rows, kr):
    return jnp.clip(step * ATT_ROWS_PER_STEP - kr // 2, 0, rows - ATT_ROWS_PER_STEP - kr)


def _attention_kernel(q_ref, k_ref, v_ref, g_ref, gen_ref, win_ref, o_ref, bias_scr, *, rows, kr):
    step = pl.program_id(1)
    gw = ATT_GROUP * ATT_HEAD_DIM
    row_head = lax.broadcasted_iota(jnp.int32, (ATT_GROUP * GRID_W, gw), 0) // GRID_W
    lane_head = lax.broadcasted_iota(jnp.int32, (ATT_GROUP * GRID_W, gw), 1) // ATT_HEAD_DIM
    own_head = row_head == lane_head
    out_lane_head = lax.broadcasted_iota(jnp.int32, (GRID_W, gw), 1) // ATT_HEAD_DIM

    @pl.when((pl.program_id(0) == 0) & (step == 0))
    def _build_bias():
        in_window = win_ref[...] > 0.0

        def one_block(n, carry):
            gen = jnp.broadcast_to(gen_ref[n], (GRID_W, LANES))
            toeplitz = pltpu.roll(gen, 0, 1, stride=1, stride_axis=0)
            head_rows = pl.ds(pl.multiple_of((n % ATT_GROUP) * GRID_W, GRID_W), GRID_W)
            bias_scr[n // ATT_GROUP, head_rows, :] = jnp.where(in_window, toeplitz, MASK_VALUE)
            return carry

        lax.fori_loop(0, gen_ref.shape[0], one_block, 0, unroll=ATT_BIAS_UNROLL)

    def one_row(j, carry):
        r = step * ATT_ROWS_PER_STEP + j
        row_start = jnp.clip(r - kr // 2, 0, rows - kr)
        variant = row_start - r + (WIN_ROWS - 1)
        q_tok = pl.multiple_of(j * GRID_W, GRID_W)
        k_tok = pl.multiple_of((row_start - _attention_first_key_row(step, rows, kr)) * GRID_W, GRID_W)
        for grp in range(ATT_GROUPS):
            lanes = slice(grp * gw, (grp + 1) * gw)
            q = q_ref[0, pl.ds(q_tok, GRID_W), lanes]
            q_bd = jnp.where(own_head, jnp.concatenate([q] * ATT_GROUP, axis=0), 0)
            keys = k_ref[0, pl.ds(k_tok, kr * GRID_W), lanes]
            vals = v_ref[0, pl.ds(k_tok, kr * GRID_W), lanes]
            tile0 = ((variant % 2) * ATT_GROUPS + grp) * ATT_BIAS_TILES + variant // 2
            bias = jnp.concatenate([bias_scr[tile0 + t] for t in range(kr // 2)], axis=1)
            s = _dot_nt(q_bd, keys) + bias
            m = jnp.max(s, axis=-1, keepdims=True)
            p = jnp.exp2(s - m)
            denom = jnp.sum(p, axis=-1, keepdims=True)
            pv = _dot(p.astype(BF16), vals) / denom
            o = jnp.zeros((GRID_W, gw), F32)
            for h in range(ATT_GROUP):
                o = o + jnp.where(out_lane_head == h, pv[h * GRID_W:(h + 1) * GRID_W], 0.0)
            gate = g_ref[0, pl.ds(q_tok, GRID_W), lanes].astype(F32)
            o = o * _silu_from_half(gate)
            o_ref[0, pl.ds(q_tok, GRID_W), lanes] = o.astype(o_ref.dtype)
        return carry

    lax.fori_loop(0, ATT_ROWS_PER_STEP, one_row, 0, unroll=ATT_UNROLL)


def _attention(q, k, v, g, rpb):
    batch, seq, width = q.shape
    rows = seq // GRID_W
    kr = min(WIN_ROWS, rows)
    assert rows % ATT_ROWS_PER_STEP == 0 and ATT_ROWS_PER_STEP % ATT_UNROLL == 0
    assert kr == WIN_ROWS and 2 * GRID_W == LANES
    gen = _attention_bias_rows(rpb)
    win = jnp.asarray(_attention_window_mask())
    blk = ATT_ROWS_PER_STEP * GRID_W
    tile = pl.BlockSpec((1, blk, width), lambda b, i: (b, i, 0))
    assert rows >= ATT_ROWS_PER_STEP + kr
    halo = pl.BlockSpec((pl.Element(1), pl.Element((ATT_ROWS_PER_STEP + kr) * GRID_W), pl.Element(width)),
                        lambda b, i: (b, _attention_first_key_row(i, rows, kr) * GRID_W, 0))
    return pl.pallas_call(
        functools.partial(_attention_kernel, rows=rows, kr=kr),
        grid=(batch, rows // ATT_ROWS_PER_STEP),
        in_specs=[tile, halo, halo, tile,
                  pl.BlockSpec(gen.shape, lambda b, i: (0, 0, 0)),
                  pl.BlockSpec(win.shape, lambda b, i: (0, 0))],
        out_specs=tile,
        out_shape=jax.ShapeDtypeStruct((batch, seq, width), BF16),
        scratch_shapes=[pltpu.VMEM((2 * ATT_GROUPS * ATT_BIAS_TILES, ATT_GROUP * GRID_W, 2 * GRID_W), F32)],
        compiler_params=pltpu.CompilerParams(
            dimension_semantics=("arbitrary", "arbitrary"), vmem_limit_bytes=VMEM_LIMIT_BYTES),
        name="nbr_attention",
    )(q, k, v, g, gen, win)


def _hgrn_fine_levels():
    sizes = []
    c = HG_GROUP
    while c >= 2:
        sizes.append(c)
        c //= 2
    return sizes


def _hgrn_group_pairs(direction, chunk):
    n = chunk // HG_GROUP
    if direction == 0:
        return [(t, g) for t in range(1, n) for g in range(t)]
    return [(t, g) for t in range(n - 2, -1, -1) for g in range(t + 1, n)]


def _hgrn_constants(chunk):
    t = np.arange(chunk)[:, None]
    u = np.arange(chunk)[None, :]
    blocks = [(u <= t)]
    masks = []
    for c in _hgrn_fine_levels():
        half = c // 2
        mid = (t // c) * c + half
        upper = (t % c) >= half
        if 2 < c < HG_GROUP:
            blocks.append(np.where(upper, (u >= mid) & (u <= t), (u > t) & (u <= mid - 1)))
        masks.append((t // c == u // c) & upper & ((u % c) < half))
    fwd_sums = np.stack(blocks).astype(np.float32)
    fwd_masks = np.stack(masks).astype(np.float32)
    twice = lambda m: np.concatenate([m.reshape(-1, chunk)] * 2, axis=1)
    sums = np.stack([twice(fwd_sums), twice(fwd_sums[:, ::-1, ::-1])])
    masks = np.concatenate([fwd_masks, fwd_masks[:, ::-1, ::-1]], axis=2)
    return sums, masks


def _block_diag(a, b):
    zero = jnp.zeros_like(a)
    return jnp.concatenate([jnp.concatenate([a, zero], axis=1),
                            jnp.concatenate([zero, b], axis=1)], axis=0)


def _hgrn_dec_rows(chunk):
    n_var = len(_hgrn_group_pairs(0, chunk)) * HG_GROUP
    n_mm = sum(c > 2 for c in _hgrn_fine_levels())
    edge, var, fine = 2 * chunk, 3 * chunk, 3 * chunk + n_var
    return edge, var, fine, fine + n_mm * chunk


def _hgrn_kernel(q_ref, zf_ref, zb_ref, i_ref, g_ref, lbf_ref, lbb_ref, gain_ref,
                 sums_ref, masks_ref, o_ref, of_scr, ob_scr, sf_scr, sb_scr,
                 k_scr0, k_scr1, dec_scr0, dec_scr1, *, layer, seq):
    chunk = HG_CHUNK
    pair = 2 * chunk
    n_pairs = seq // pair
    dh = HG_HEAD_DIM
    grp = HG_GROUP
    n_groups = chunk // grp
    fine = _hgrn_fine_levels()
    group_pairs = [_hgrn_group_pairs(direction, chunk) for direction in (0, 1)]
    r_edge, r_var, r_fine, _ = _hgrn_dec_rows(chunk)
    k_slots = (k_scr0, k_scr1)
    dec_slots = (dec_scr0, dec_scr1)
    row_in_group = lax.broadcasted_iota(jnp.int32, (grp, dh), 0)
    lane_group = lax.broadcasted_iota(jnp.int32, (grp, 2 * chunk), 1) // grp

    def lower_bound(logit_ref):
        logits = logit_ref[...].astype(F32)
        e = jnp.exp(logits - jnp.max(logits, axis=0, keepdims=True))
        return jnp.sum(e[:layer + 1], axis=0, keepdims=True) / jnp.sum(e, axis=0, keepdims=True)

    def gates(z, lb):
        f = lb + (1.0 - lb) * jax.nn.sigmoid(z.astype(F32))
        return jnp.log2(f), 1.0 - f

    def store_decays(slot, direction, g2):
        hi = g2.astype(BF16)
        lo = (g2 - hi.astype(F32)).astype(BF16)
        stacked = [jnp.concatenate([hi[c * chunk:(c + 1) * chunk], lo[c * chunk:(c + 1) * chunk]], axis=0)
                   for c in range(2)]
        raw = _dot(sums_ref[direction], jnp.concatenate(stacked, axis=1))
        dec = dec_slots[slot].at[direction]
        cum = raw[0:chunk]
        whole = cum[chunk - 1:chunk] if direction == 0 else cum[0:1]
        dec[0:chunk, :] = jnp.exp2(cum)
        dec[chunk:2 * chunk, :] = jnp.exp2(whole - cum)
        by_group = cum.reshape(n_groups, grp, 2 * dh)
        mid_row = grp // 2 - 1 if direction == 0 else grp // 2
        to_mid = -jnp.abs(by_group - by_group[:, mid_row:mid_row + 1, :])
        dec[r_fine:r_fine + chunk, :] = jnp.exp2(to_mid.reshape(chunk, 2 * dh))
        dec[r_fine + chunk:, :] = jnp.exp2(raw[chunk:])
        edge = grp - 1 if direction == 0 else 0
        edges = [cum[g * grp + edge:g * grp + edge + 1] for g in range(n_groups)]
        group = lambda g: cum[g * grp:(g + 1) * grp]
        dec[r_edge:r_var, :] = jnp.exp2(jnp.concatenate([edges[g] - group(g) for g in range(n_groups)], axis=0))
        dec[r_var:r_fine, :] = jnp.exp2(jnp.concatenate([group(t) - edges[g]
                                                         for t, g in group_pairs[direction]], axis=0))

    def fine_operand(direction, lvl, q, k, dec):
        c = fine[lvl]
        pos = row_in_group % c
        is_upper = (pos >= c // 2) if direction == 0 else (pos < c // 2)
        pick = lambda x, y: jnp.where(is_upper[None], x.reshape(n_groups, grp, dh),
                                      y.reshape(n_groups, grp, dh)).reshape(chunk, dh)
        if c == 2:
            return pick(q * (1.0 - k), k).astype(BF16)
        return (pick(q, k) * dec(r_fine + lvl * chunk, r_fine + (lvl + 1) * chunk)).astype(BF16)

    def chunk_part(slot, direction, c, q, v, base):
        rows = slice(c * chunk, (c + 1) * chunk)
        lanes = slice(c * dh, (c + 1) * dh)
        return dict(q=q[rows], k=k_slots[slot][direction, rows, :], v=v[rows],
                    dec=lambda r0, r1: dec_slots[slot][direction, r0:r1, lanes],
                    rows=pl.ds(pl.multiple_of(base + c * chunk, chunk), chunk))

    def pair_step(f, b):
        lhs = jnp.concatenate(
            [(jnp.concatenate([p["q"][t * grp:(t + 1) * grp] for t, _ in group_pairs[d]], axis=0)
              * p["dec"](r_var, r_fine)).astype(BF16) for d, p in enumerate((f, b))], axis=1)
        keys = [(p["k"] * p["dec"](r_edge, r_var)).astype(BF16) for p in (f, b)]
        r = _dot_nt(lhs, _block_diag(*keys))
        rows_of = [jnp.zeros((grp, 2 * chunk), F32)] * n_groups
        for i, ((tf, gf), (tb, gb)) in enumerate(zip(*group_pairs)):
            block = r[i * grp:(i + 1) * grp]
            rows_of[tf] = jnp.where(lane_group == gf, block, rows_of[tf])
            rows_of[tb] = jnp.where(lane_group == n_groups + gb, block, rows_of[tb])
        a = jnp.concatenate(rows_of, axis=0)
        for lvl in range(len(fine)):
            xf = fine_operand(0, lvl, f["q"], f["k"], f["dec"])
            xb = fine_operand(1, lvl, b["q"], b["k"], b["dec"])
            a = a + masks_ref[lvl] * _dot_nt(jnp.concatenate([xf, xb], axis=1), _block_diag(xf, xb))
        o = _dot(a.astype(BF16), _block_diag(f["v"], b["v"]))
        sf = sf_scr[...]
        sb = sb_scr[...]
        q_dec = jnp.concatenate([(f["q"] * f["dec"](0, chunk)).astype(BF16),
                                 (b["q"] * b["dec"](0, chunk)).astype(BF16)], axis=1)
        o = o + _dot_nt(q_dec, _block_diag(sf.astype(BF16), sb.astype(BF16)))
        diag_f = jnp.sum(f["q"] * f["k"], axis=-1, keepdims=True)
        diag_b = jnp.sum(b["q"] * b["k"], axis=-1, keepdims=True)
        of_scr[f["rows"], :] = o[:, :dh] + diag_f * f["v"].astype(F32)
        ob_scr[b["rows"], :] = o[:, dh:] + diag_b * b["v"].astype(F32)
        k_dec = _block_diag((f["k"] * f["dec"](chunk, 2 * chunk)).astype(BF16),
                            (b["k"] * b["dec"](chunk, 2 * chunk)).astype(BF16))
        v_t = jnp.concatenate([f["v"], b["v"]], axis=0).astype(F32).T.astype(BF16)
        u = _dot(v_t, k_dec)
        sf_scr[...] = sf * f["dec"](chunk - 1, chunk) + u[:, :dh]
        sb_scr[...] = sb * b["dec"](0, 1) + u[:, dh:]

    lb_f = lower_bound(lbf_ref)
    lb_b = lower_bound(lbb_ref)
    sf_scr[...] = jnp.zeros_like(sf_scr)
    sb_scr[...] = jnp.zeros_like(sb_scr)

    def bases(n):
        return (pl.multiple_of(n * pair, pair), pl.multiple_of((n_pairs - 1 - n) * pair, pair))

    def prepare(n, slot):
        for direction, base, z_ref, lb in zip((0, 1), bases(n), (zf_ref, zb_ref), (lb_f, lb_b)):
            g2, k = gates(z_ref[0, pl.ds(base, pair), :], lb)
            k_slots[slot][direction] = k
            store_decays(slot, direction, g2)

    def consume(n, slot):
        parts = []
        for direction, base in zip((0, 1), bases(n)):
            rows = pl.ds(base, pair)
            q = q_ref[0, rows, :].astype(F32)
            v = i_ref[0, rows, :]
            parts.append([chunk_part(slot, direction, c, q, v, base) for c in range(2)])
        pair_step(parts[0][0], parts[1][1])
        pair_step(parts[0][1], parts[1][0])

    def scan_body(m, carry):
        prepare(2 * m + 1, 1)
        consume(2 * m, 0)
        prepare(jnp.minimum(2 * m + 2, n_pairs - 1), 0)
        consume(2 * m + 1, 1)
        return carry

    prepare(0, 0)
    lax.fori_loop(0, n_pairs // 2, scan_body, 0, unroll=HG_SCAN_UNROLL)

    gain = gain_ref[...].astype(F32)

    def norm_body(n, carry):
        rows = pl.ds(pl.multiple_of(n * HG_NORM_ROWS, HG_NORM_ROWS), HG_NORM_ROWS)
        o = of_scr[rows, :] + ob_scr[rows, :]
        o = o * lax.rsqrt(jnp.mean(jnp.square(o), axis=-1, keepdims=True) + RMS_EPS) * gain
        gate = g_ref[0, rows, :].astype(F32)
        o_ref[0, rows, :] = (o * _silu_from_half(gate)).astype(o_ref.dtype)
        return carry

    lax.fori_loop(0, seq // HG_NORM_ROWS, norm_body, 0, unroll=4)


def _hgrn(q, zf, zb, i, g, lb_fwd_logits, lb_bwd_logits, gain_row, layer):
    batch, seq, width = q.shape
    assert width == HG_HEADS * HG_HEAD_DIM and seq % (4 * HG_NORM_ROWS) == 0
    assert seq % (4 * HG_CHUNK * HG_SCAN_UNROLL) == 0
    sums, masks = _hgrn_constants(HG_CHUNK)
    sums = jnp.asarray(sums, BF16)
    masks = jnp.asarray(masks, F32)
    n_layers = lb_fwd_logits.shape[0]
    n_dec_rows = _hgrn_dec_rows(HG_CHUNK)[-1]
    head = pl.BlockSpec((1, seq, HG_HEAD_DIM), lambda b, h: (b, 0, h))
    per_head_row = lambda n: pl.BlockSpec((n, HG_HEAD_DIM), lambda b, h: (0, h))
    whole = lambda a: pl.BlockSpec(a.shape, lambda b, h: (0,) * a.ndim)
    return pl.pallas_call(
        functools.partial(_hgrn_kernel, layer=layer, seq=seq),
        grid=(batch, HG_HEADS),
        in_specs=[head, head, head, head, head,
                  per_head_row(n_layers), per_head_row(n_layers), per_head_row(1),
                  whole(sums), whole(masks)],
        out_specs=head,
        out_shape=jax.ShapeDtypeStruct((batch, seq, width), BF16),
        scratch_shapes=[pltpu.VMEM((seq, HG_HEAD_DIM), F32), pltpu.VMEM((seq, HG_HEAD_DIM), F32),
                        pltpu.VMEM((HG_HEAD_DIM, HG_HEAD_DIM), F32),
                        pltpu.VMEM((HG_HEAD_DIM, HG_HEAD_DIM), F32),
                        pltpu.VMEM((2, 2 * HG_CHUNK, HG_HEAD_DIM), F32),
                        pltpu.VMEM((2, 2 * HG_CHUNK, HG_HEAD_DIM), F32),
                        pltpu.VMEM((2, n_dec_rows, 2 * HG_HEAD_DIM), F32),
                        pltpu.VMEM((2, n_dec_rows, 2 * HG_HEAD_DIM), F32)],
        compiler_params=pltpu.CompilerParams(
            dimension_semantics=("arbitrary", "arbitrary"), vmem_limit_bytes=VMEM_LIMIT_BYTES),
        name="hgrn2_scan",
    )(q, zf, zb, i, g, lb_fwd_logits, lb_bwd_logits, gain_row, sums, masks)


def _out_proj_kernel(oa_hbm, oh_hbm, x_hbm, w_ref, b_ref, gain_ref, bias_ref, o_ref, w_scr,
                     oa_ring, oh_ring, x_ring, sems, *, alpha):
    d_att = oa_hbm.shape[1]
    tm = o_ref.shape[0]
    step = pl.program_id(0)
    n_steps = pl.num_programs(0)
    streams = ((oa_hbm, oa_ring), (oh_hbm, oh_ring), (x_hbm, x_ring))

    def fetch(s):
        slot = s % OUT_PROJ_RING
        return [pltpu.make_async_copy(hbm.at[pl.ds(pl.multiple_of(s * tm, tm), tm)],
                                      ring.at[slot], sems.at[n, slot])
                for n, (hbm, ring) in enumerate(streams)]

    @pl.when(step == 0)
    def _():
        for s in range(OUT_PROJ_RING - 1):
            for copy in fetch(s):
                copy.start()

    @pl.when(step + (OUT_PROJ_RING - 1) < n_steps)
    def _():
        for copy in fetch(step + (OUT_PROJ_RING - 1)):
            copy.start()

    _cast_weight_once(w_ref, w_scr)
    for copy in fetch(step):
        copy.wait()
    oa_ref, oh_ref, x_ref = [ring.at[step % OUT_PROJ_RING] for _, ring in streams]
    for r0 in range(0, o_ref.shape[0], OUT_PROJ_ROWS):
        rows = slice(r0, r0 + OUT_PROJ_ROWS)
        y = _dot(oa_ref[rows, :], w_scr[:d_att, :]) + _dot(oh_ref[rows, :], w_scr[d_att:, :]) + b_ref[...]
        r = alpha * x_ref[rows, :] + y
        mu = jnp.mean(r, axis=-1, keepdims=True)
        c = r - mu
        var = jnp.mean(jnp.square(c), axis=-1, keepdims=True)
        o_ref[rows, :] = (c * lax.rsqrt(var + LN_EPS) * gain_ref[...] + bias_ref[...]).astype(o_ref.dtype)


def _out_proj(o_a, o_h, x2d, w, b_row, gain_row, bias_row, alpha):
    n_tok, d_model = x2d.shape
    d_att, d_hg = o_a.shape[1], o_h.shape[1]
    row = pl.BlockSpec((1, d_model), lambda i: (0, 0))
    return pl.pallas_call(
        functools.partial(_out_proj_kernel, alpha=alpha),
        grid=(n_tok // OUT_PROJ_TM,),
        in_specs=[pl.BlockSpec(memory_space=pl.ANY),
                  pl.BlockSpec(memory_space=pl.ANY),
                  pl.BlockSpec(memory_space=pl.ANY),
                  pl.BlockSpec((d_att + d_hg, d_model), lambda i: (0, 0), pipeline_mode=pl.Buffered(1)),
                  row, row, row],
        out_specs=pl.BlockSpec((OUT_PROJ_TM, d_model), lambda i: (i, 0)),
        out_shape=jax.ShapeDtypeStruct((n_tok, d_model), x2d.dtype),
        scratch_shapes=[pltpu.VMEM((d_att + d_hg, d_model), BF16),
                        pltpu.VMEM((OUT_PROJ_RING, OUT_PROJ_TM, d_att), o_a.dtype),
                        pltpu.VMEM((OUT_PROJ_RING, OUT_PROJ_TM, d_hg), o_h.dtype),
                        pltpu.VMEM((OUT_PROJ_RING, OUT_PROJ_TM, d_model), x2d.dtype),
                        pltpu.SemaphoreType.DMA((3, OUT_PROJ_RING))],
        compiler_params=pltpu.CompilerParams(
            dimension_semantics=("arbitrary",), vmem_limit_bytes=VMEM_LIMIT_BYTES),
        name="out_proj_layernorm",
    )(o_a, o_h, x2d, w, b_row, gain_row, bias_row)


def _layer(x, layer, depth, w_in, b_in, rpb, lb_fwd_logits, lb_bwd_logits, hg_norm_gain,
           w_out, b_out, ln_gain, ln_bias):
    batch, seq, d_model = x.shape
    x2d = x.reshape(batch * seq, d_model)
    slabs = _in_proj(x2d, w_in, b_in.reshape(1, -1))
    q_a, k_a, v_a, g_a, q_h, z_f, z_b, i_h, g_h = [s.reshape(batch, seq, SLAB) for s in slabs]
    o_a = _attention(q_a, k_a, v_a, g_a, rpb)
    o_h = _hgrn(q_h, z_f, z_b, i_h, g_h, lb_fwd_logits, lb_bwd_logits,
                hg_norm_gain.reshape(1, -1), layer)
    alpha = (2.0 * depth) ** 0.25
    out = _out_proj(o_a.reshape(batch * seq, SLAB), o_h.reshape(batch * seq, SLAB), x2d,
                    w_out, b_out.reshape(1, -1), ln_gain.reshape(1, -1),
                    ln_bias.reshape(1, -1), alpha)
    return out.reshape(batch, seq, d_model)


def kernel(x, w_in, b_in, rpb, lb_fwd_logits, lb_bwd_logits, hg_norm_gain, w_out, b_out, ln_gain, ln_bias):
    depth = w_in.shape[0]
    for layer in range(depth):
        x = _layer(x, layer, depth, w_in[layer], b_in[layer], rpb[layer], lb_fwd_logits,
                   lb_bwd_logits, hg_norm_gain[layer], w_out[layer], b_out[layer],
                   ln_gain[layer], ln_bias[layer])
    return x
```
